```python
import math
import jax, jax.numpy as jnp
from jax import lax
import numpy as np

D_MODEL = 1024
BATCH = 4
SEQ = 4096
DEPTH = 1

N_META = 16
EPS = 1e-6
D_FF = 2816
MLA_HEADS = 8
MLA_Q_RANK = 256
MLA_KV_RANK = 128
MLA_NOPE = 64
MLA_ROPE = 32
MLA_V = 64
ROPE_THETA = 10000.0
Q_BLOCK = 128
GDN_HEADS = 8
GDN_DK = 64
GDN_DV = 64
CONV_K = 4
CHUNK = 64
SPLITS = (MLA_Q_RANK, MLA_KV_RANK, MLA_ROPE,
          GDN_HEADS * GDN_DK, GDN_HEADS * GDN_DK, GDN_HEADS * GDN_DV,
          GDN_HEADS, GDN_HEADS, GDN_HEADS * GDN_DV,
          D_MODEL, D_MODEL)
D_IN = sum(SPLITS)
GDN_CONV_CH = 2 * GDN_HEADS * GDN_DK + GDN_HEADS * GDN_DV

kernel_name = 'hybrid_mla_gdn_macaron_block'


def rmsnorm(x, w):
    x32 = x.astype(jnp.float32)
    y = x32 * lax.rsqrt(jnp.mean(x32 * x32, axis=-1, keepdims=True) + EPS)
    return (y * w.astype(jnp.float32)).astype(x.dtype)


def swiglu(x, w_gate, w_up, w_down):
    return (jax.nn.silu(x @ w_gate) * (x @ w_up)) @ w_down


def rope_tables(length):
    pos = jnp.arange(length, dtype=jnp.float32)
    inv = ROPE_THETA ** (-jnp.arange(0, MLA_ROPE, 2, dtype=jnp.float32) / MLA_ROPE)
    ang = pos[:, None] * inv[None, :]
    return jnp.cos(ang), jnp.sin(ang)


def apply_rope(x, cos, sin):
    half = x.shape[-1] // 2
    x1 = x[..., :half].astype(jnp.float32)
    x2 = x[..., half:].astype(jnp.float32)
    out = jnp.concatenate([x1 * cos - x2 * sin, x2 * cos + x1 * sin], axis=-1)
    return out.astype(x.dtype)


def mla(c_q_raw, c_kv_raw, k_rope_raw, q_norm, w_uq, kv_norm, w_ukv):
    b, length, _ = c_q_raw.shape
    q = (rmsnorm(c_q_raw, q_norm) @ w_uq).reshape(b, length, MLA_HEADS, MLA_NOPE + MLA_ROPE)
    kv = (rmsnorm(c_kv_raw, kv_norm) @ w_ukv).reshape(b, length, MLA_HEADS, MLA_NOPE + MLA_V)
    q_nope, q_rope = q[..., :MLA_NOPE], q[..., MLA_NOPE:]
    k_nope, v = kv[..., :MLA_NOPE], kv[..., MLA_NOPE:]
    cos, sin = rope_tables(length)
    q_rope = apply_rope(q_rope, cos[:, None, :], sin[:, None, :])
    k_rope = apply_rope(k_rope_raw, cos, sin)
    scale = (MLA_NOPE + MLA_ROPE) ** -0.5
    n_blk = -(-length // Q_BLOCK)
    l_pad = n_blk * Q_BLOCK

    def to_blocks(t):
        t = jnp.pad(t, ((0, 0), (0, l_pad - length), (0, 0), (0, 0)))
        return t.reshape(b, n_blk, Q_BLOCK, MLA_HEADS, t.shape[-1]).transpose(1, 0, 2, 3, 4)

    qn_blocks = to_blocks(q_nope)
    qr_blocks = to_blocks(q_rope)
    starts = jnp.arange(n_blk, dtype=jnp.int32) * Q_BLOCK
    k_pos = jnp.arange(length, dtype=jnp.int32)

    def attend_block(args):
        qn_b, qr_b, start = args
        s = (jnp.einsum('bqhd,bkhd->bhqk', qn_b, k_nope)
             + jnp.einsum('bqhd,bkd->bhqk', qr_b, k_rope)).astype(jnp.float32) * scale
        q_pos = start + jnp.arange(Q_BLOCK, dtype=jnp.int32)
        causal = k_pos[None, :] <= q_pos[:, None]
        s = jnp.where(causal, s, -jnp.inf)
        p = jax.nn.softmax(s, axis=-1).astype(v.dtype)
        return jnp.einsum('bhqk,bkhd->bqhd', p, v)

    o = lax.map(attend_block, (qn_blocks, qr_blocks, starts))
    o = o.transpose(1, 0, 2, 3, 4).reshape(b, l_pad, MLA_HEADS * MLA_V)
    return o[:, :length]


def short_conv(x, w):
    ch = x.shape[-1]
    return lax.conv_general_dilated(
        x, w[:, None, :].astype(x.dtype), window_strides=(1,),
        padding=[(CONV_K - 1, 0)], dimension_numbers=('NWC', 'WIO', 'NWC'),
        feature_group_count=ch)


def l2norm(x):
    return x * lax.rsqrt(jnp.sum(x * x, axis=-1, keepdims=True) + EPS)


def gated_deltanet(q_raw, k_raw, v_raw, b_raw, a_raw, z_raw, conv_w, a_log, dt_bias, gdn_norm):
    b, length, _ = q_raw.shape
    dtype = q_raw.dtype
    qkv = jax.nn.silu(short_conv(jnp.concatenate([q_raw, k_raw, v_raw], axis=-1), conv_w))
    nq = GDN_HEADS * GDN_DK
    q = qkv[..., :nq].reshape(b, length, GDN_HEADS, GDN_DK).astype(jnp.float32)
    k = qkv[..., nq:2 * nq].reshape(b, length, GDN_HEADS, GDN_DK).astype(jnp.float32)
    v = qkv[..., 2 * nq:].reshape(b, length, GDN_HEADS, GDN_DV).astype(jnp.float32)
    q = l2norm(q) * (GDN_DK ** -0.5)
    k = l2norm(k)
    beta = jax.nn.sigmoid(b_raw.astype(jnp.float32))
    g = -jnp.exp(a_log.astype(jnp.float32)) * jax.nn.softplus(
        a_raw.astype(jnp.float32) + dt_bias.astype(jnp.float32))

    pad_front = (-N_META) % CHUNK
    lc = pad_front + length
    n_chunk = lc // CHUNK

    def to_chunks(t):
        t = jnp.pad(t, ((0, 0), (pad_front, 0)) + ((0, 0),) * (t.ndim - 2))
        t = t.reshape((b, n_chunk, CHUNK) + t.shape[2:])
        return jnp.moveaxis(t, 3, 1)

    qc, kc, vc = to_chunks(q), to_chunks(k), to_chunks(v)
    bc = to_chunks(beta)
    gc = jnp.cumsum(to_chunks(g), axis=-1)
    tri_incl = jnp.tril(jnp.ones((CHUNK, CHUNK), dtype=bool))
    tri_strict = jnp.tril(jnp.ones((CHUNK, CHUNK), dtype=bool), -1)
    diff = gc[..., :, None] - gc[..., None, :]
    decay = jnp.exp(jnp.where(tri_incl, diff, -jnp.inf))
    kb = kc * bc[..., None]
    lmat = jnp.where(tri_strict, jnp.einsum('bhncd,bhnsd->bhncs', kb, kc) * decay, 0.0)
    eye = jnp.eye(CHUNK, dtype=jnp.float32)
    tmat = lax.linalg.triangular_solve(eye + lmat, jnp.broadcast_to(eye, lmat.shape),
                                       left_side=True, lower=True)
    u_c = tmat @ (vc * bc[..., None])
    w_c = tmat @ (kb * jnp.exp(gc)[..., None])
    intra = jnp.einsum('bhncd,bhnsd->bhncs', qc, kc) * decay

    def step(state, inp):
        q_i, k_i, u_i, w_i, g_i, a_i = inp
        v_new = u_i - w_i @ state
        o_i = (q_i * jnp.exp(g_i)[..., None]) @ state + a_i @ v_new
        g_last = g_i[..., -1]
        k_dec = k_i * jnp.exp(g_last[..., None] - g_i)[..., None]
        state = state * jnp.exp(g_last)[..., None, None] + jnp.einsum('bhcd,bhce->bhde', k_dec, v_new)
        return state, o_i

    xs = tuple(jnp.moveaxis(t, 2, 0) for t in (qc, kc, u_c, w_c, gc, intra))
    s0 = jnp.zeros((b, GDN_HEADS, GDN_DK, GDN_DV), jnp.float32)
    _, o = lax.scan(step, s0, xs)
    o = o.transpose(1, 0, 3, 2, 4).reshape(b, lc, GDN_HEADS, GDN_DV)[:, pad_front:]
    z = z_raw.reshape(b, length, GDN_HEADS, GDN_DV).astype(jnp.float32)
    o = rmsnorm(o, gdn_norm) * jax.nn.silu(z)
    return o.reshape(b, length, GDN_HEADS * GDN_DV).astype(dtype)


def _w(k, shape, fan_in):
    return jax.random.normal(k, shape, jnp.float32) * (fan_in ** -0.5)


def _gain(k, shape):
    return 1.0 + 0.02 * jax.random.normal(k, shape, jnp.float32)


def setup_inputs(seed: int = 0) -> dict:
    key = jax.random.key(seed)
    ks = jax.random.split(key, 32)
    d = D_MODEL
    dt = jnp.exp(jax.random.uniform(ks[15], (DEPTH, GDN_HEADS), jnp.float32,
                                    minval=math.log(1e-3), maxval=math.log(1e-1)))
    return {
        'x': jax.random.normal(ks[0], (BATCH, SEQ, d), jnp.float32),
        'meta_tokens': jax.random.normal(ks[1], (N_META, d), jnp.float32),
        'ffn1_norm': _gain(ks[2], (DEPTH, d)),
        'ffn1_w_gate': _w(ks[3], (DEPTH, d, D_FF), d),
        'ffn1_w_up': _w(ks[4], (DEPTH, d, D_FF), d),
        'ffn1_w_down': _w(ks[5], (DEPTH, D_FF, d), D_FF),
        'mix_norm': _gain(ks[6], (DEPTH, d)),
        'w_in': _w(ks[7], (DEPTH, d, D_IN), d),
        'q_norm': _gain(ks[8], (DEPTH, MLA_Q_RANK)),
        'w_uq': _w(ks[9], (DEPTH, MLA_Q_RANK, MLA_HEADS * (MLA_NOPE + MLA_ROPE)), MLA_Q_RANK),
        'kv_norm': _gain(ks[10], (DEPTH, MLA_KV_RANK)),
        'w_ukv': _w(ks[11], (DEPTH, MLA_KV_RANK, MLA_HEADS * (MLA_NOPE + MLA_V)), MLA_KV_RANK),
        'w_mla_o': _w(ks[12], (DEPTH, MLA_HEADS * MLA_V, d), MLA_HEADS * MLA_V),
        'conv_w': _w(ks[13], (DEPTH, CONV_K, GDN_CONV_CH), CONV_K),
        'a_log': jnp.log(jax.random.uniform(ks[14], (DEPTH, GDN_HEADS), jnp.float32, minval=1.0, maxval=16.0)),
        'dt_bias': dt + jnp.log(-jnp.expm1(-dt)),
        'gdn_norm': _gain(ks[16], (DEPTH, GDN_DV)),
        'w_gdn_o': _w(ks[17], (DEPTH, GDN_HEADS * GDN_DV, d), GDN_HEADS * GDN_DV),
        'w_out': _w(ks[18], (DEPTH, d, d), d),
        'ffn2_norm': _gain(ks[19], (DEPTH, d)),
        'ffn2_w_gate': _w(ks[20], (DEPTH, d, D_FF), d),
        'ffn2_w_up': _w(ks[21], (DEPTH, d, D_FF), d),
        'ffn2_w_down': _w(ks[22], (DEPTH, D_FF, d), D_FF),
        'final_norm': _gain(ks[23], (d,)),
    }


def reference(x, meta_tokens, ffn1_norm, ffn1_w_gate, ffn1_w_up, ffn1_w_down, mix_norm, w_in,
              q_norm, w_uq, kv_norm, w_ukv, w_mla_o, conv_w, a_log, dt_bias, gdn_norm, w_gdn_o,
              w_out, ffn2_norm, ffn2_w_gate, ffn2_w_up, ffn2_w_down, final_norm):
    b = x.shape[0]
    meta = jnp.broadcast_to(meta_tokens[None].astype(x.dtype), (b, N_META, D_MODEL))
    h = jnp.concatenate([meta, x], axis=1)
    split_points = [int(p) for p in np.cumsum(SPLITS)[:-1]]
    for l in range(DEPTH):
        h = h + 0.5 * swiglu(rmsnorm(h, ffn1_norm[l]), ffn1_w_gate[l], ffn1_w_up[l], ffn1_w_down[l])
        u = rmsnorm(h, mix_norm[l])
        (c_q, c_kv, k_rope, g_q, g_k, g_v, g_b, g_a, g_z,
         gate_mla, gate_gdn) = jnp.split(u @ w_in[l], split_points, axis=-1)
        y_mla = mla(c_q, c_kv, k_rope, q_norm[l], w_uq[l], kv_norm[l], w_ukv[l]) @ w_mla_o[l]
        y_gdn = gated_deltanet(g_q, g_k, g_v, g_b, g_a, g_z, conv_w[l], a_log[l], dt_bias[l],
                               gdn_norm[l]) @ w_gdn_o[l]
        merged = jax.nn.sigmoid(gate_mla) * y_mla + jax.nn.sigmoid(gate_gdn) * y_gdn
        h = h + merged @ w_out[l]
        h = h + 0.5 * swiglu(rmsnorm(h, ffn2_norm[l]), ffn2_w_gate[l], ffn2_w_up[l], ffn2_w_down[l])
    return rmsnorm(h, final_norm)[:, N_META:]
```

```python
import functools
import math

import jax
import jax.numpy as jnp
import numpy as np
from jax import lax
from jax.experimental import pallas as pl
from jax.experimental.pallas import tpu as pltpu

F32 = jnp.float32
BF16 = jnp.bfloat16

D_MODEL = 1024
N_META = 16
EPS = 1e-6
D_FF = 2816
MLA_HEADS = 8
MLA_Q_RANK = 256
MLA_KV_RANK = 128
MLA_NOPE = 64
MLA_ROPE = 32
MLA_V = 64
ROPE_THETA = 10000.0
GDN_HEADS = 8
GDN_DK = 64
GDN_DV = 64
CONV_K = 4
CHUNK = 64

LANES = 128
HEAD_PAD = 128
QK_W = MLA_HEADS * HEAD_PAD
V_W = MLA_HEADS * MLA_V
GDN_W = GDN_HEADS * GDN_DK
GROUP_HEADS = 4
GROUP_W = GROUP_HEADS * GDN_DK
N_GROUPS = GDN_HEADS // GROUP_HEADS
PROJ_W = MLA_Q_RANK + MLA_KV_RANK + LANES + 3 * GDN_W + LANES + GDN_W
VMEM_LIMIT = 56 * 1024 * 1024
NEG_BIG = -1e30


def _const_spec(shape):
    zeros = (0,) * len(shape)
    return pl.BlockSpec(shape, lambda *_: zeros, pipeline_mode=pl.Buffered(1))


def _rms(x, w):
    return x * lax.rsqrt(jnp.mean(x * x, axis=-1, keepdims=True) + EPS) * w


def _dot(a, b):
    return jnp.dot(a, b, preferred_element_type=F32)


def _dot_nt(a, b):
    return lax.dot_general(a, b, (((1,), (1,)), ((), ())), preferred_element_type=F32)


def _silu(x):
    return x * jax.nn.sigmoid(x)


def _split(x, n):
    pieces = []
    for _ in range(n - 1):
        hi = x.astype(BF16)
        pieces.append(hi)
        x = x - hi.astype(F32)
    pieces.append(x.astype(BF16))
    return pieces


def _dot_exact_rhs(x, rhs, n):
    return sum(_dot(p, rhs) for p in _split(x, n))


def _dot_exact_lhs(lhs, x, n):
    return sum(_dot(lhs, p) for p in _split(x, n))


def _token_proj_kernel(x_ref, tc_ref, ts_ref, tk_ref, n1_ref, wgu_ref, wd_ref, nm_ref, win_ref,
                       qn_ref, wqa_ref, wqb_ref, kvn_ref, wk_ref, wv_ref, e_ref,
                       h1_ref, q_ref, k_ref, v_ref, gqkv_ref, gba_ref, z_ref):
    x = x_ref[...]
    xn = _rms(x, n1_ref[...]).astype(BF16)
    gu = _dot(xn, wgu_ref[...])
    act = (_silu(gu[:, :D_FF]) * gu[:, D_FF:]).astype(BF16)
    h1 = x + 0.5 * _dot(act, wd_ref[...])
    h1_ref[...] = h1

    un = _rms(h1, nm_ref[...]).astype(BF16)
    p = _dot(un, win_ref[...])
    o = 0
    cq = p[:, o:o + MLA_Q_RANK]; o += MLA_Q_RANK
    ckv = p[:, o:o + MLA_KV_RANK]; o += MLA_KV_RANK
    kr = p[:, o:o + LANES]; o += LANES
    gqkv_ref[...] = p[:, o:o + 3 * GDN_W]; o += 3 * GDN_W
    gba_ref[...] = p[:, o:o + LANES]; o += LANES
    z_ref[...] = p[:, o:o + GDN_W]

    cqn = _rms(cq, qn_ref[...]).astype(BF16)
    tc = jnp.concatenate([tc_ref[...]] * MLA_HEADS, axis=1)
    ts = jnp.concatenate([ts_ref[...]] * MLA_HEADS, axis=1)
    q = _dot(cqn, wqa_ref[...]) * tc + _dot(cqn, wqb_ref[...]) * ts
    q_ref[...] = q.astype(BF16)

    ckvn = _rms(ckv, kvn_ref[...]).astype(BF16)
    k = _dot(ckvn, wk_ref[...]) + _dot((kr * tk_ref[...]).astype(BF16), e_ref[...])
    k_ref[...] = k.astype(BF16)
    v_ref[...] = _dot(ckvn, wv_ref[...]).astype(BF16)


def _token_proj(x2d, tabs, weights, tm, tab_blocks):
    n = x2d.shape[0]
    assert n % tm == 0
    row = lambda w: pl.BlockSpec((tm, w), lambda i: (i, 0))
    tab = pl.BlockSpec((tm, LANES), lambda i: (i % tab_blocks, 0))
    out_widths = (D_MODEL, QK_W, QK_W, V_W, 3 * GDN_W, LANES, GDN_W)
    out_dtypes = (F32, BF16, BF16, BF16, F32, F32, F32)
    return pl.pallas_call(
        _token_proj_kernel,
        grid=(n // tm,),
        in_specs=[row(D_MODEL), tab, tab, tab] + [_const_spec(w.shape) for w in weights],
        out_specs=[row(w) for w in out_widths],
        out_shape=[jax.ShapeDtypeStruct((n, w), d) for w, d in zip(out_widths, out_dtypes)],
        compiler_params=pltpu.CompilerParams(dimension_semantics=("arbitrary",),
                                             vmem_limit_bytes=VMEM_LIMIT),
        name="token_proj",
    )(x2d, *tabs, *weights)


def _mla_kernel(q_ref, k_ref, v_ref, km_ref, vm_ref, o_ref, *, tq, tk):
    qi = pl.program_id(2)
    meta_rows = km_ref.shape[0]
    meta_valid = lax.broadcasted_iota(jnp.int32, (tq, meta_rows), 1) < N_META
    causal = (lax.broadcasted_iota(jnp.int32, (tq, tk), 0)
              >= lax.broadcasted_iota(jnp.int32, (tq, tk), 1))
    outs = []
    for hh in range(2):
        hs = slice(hh * HEAD_PAD, (hh + 1) * HEAD_PAD)
        q = q_ref[:, hs]

        s = jnp.where(meta_valid, _dot_nt(q, km_ref[:, hs]), NEG_BIG)
        m = jnp.max(s, axis=-1, keepdims=True)
        p = jnp.exp(s - m)
        l = jnp.sum(p, axis=-1, keepdims=True)
        acc = _dot(p.astype(BF16), vm_ref[...])

        def block(ki, carry, masked):
            m, l, acc = carry
            rows = pl.ds(pl.multiple_of(ki * tk, tk), tk)
            s = _dot_nt(q, k_ref[rows, hs])
            if masked:
                s = jnp.where(causal, s, NEG_BIG)
            m_new = jnp.maximum(m, jnp.max(s, axis=-1, keepdims=True))
            alpha = jnp.exp(m - m_new)
            p = jnp.exp(s - m_new)
            l = alpha * l + jnp.sum(p, axis=-1, keepdims=True)
            acc = alpha * acc + _dot(p.astype(BF16), v_ref[rows, :])
            return m_new, l, acc

        carry = lax.fori_loop(0, qi, functools.partial(block, masked=False), (m, l, acc))
        m, l, acc = block(qi, carry, masked=True)
        outs.append(acc / l)
    lane = lax.broadcasted_iota(jnp.int32, outs[0].shape, 1)
    o_ref[...] = jnp.where(lane < MLA_V, outs[0], outs[1]).astype(BF16)


def _mla_attn(q, k, v, k_meta, v_meta, batch, seq, tq):
    nq = seq // tq
    kern = functools.partial(_mla_kernel, tq=tq, tk=tq)
    return pl.pallas_call(
        kern,
        grid=(batch, MLA_HEADS // 2, nq),
        in_specs=[
            pl.BlockSpec((tq, 2 * HEAD_PAD), lambda b, hp, i: (b * nq + i, hp)),
            pl.BlockSpec((seq, 2 * HEAD_PAD), lambda b, hp, i: (b, hp)),
            pl.BlockSpec((seq, 2 * MLA_V), lambda b, hp, i: (b, hp)),
            pl.BlockSpec((k_meta.shape[0], 2 * HEAD_PAD), lambda b, hp, i: (0, hp)),
            pl.BlockSpec((v_meta.shape[0], 2 * MLA_V), lambda b, hp, i: (0, hp)),
        ],
        out_specs=pl.BlockSpec((tq, 2 * MLA_V), lambda b, hp, i: (b * nq + i, hp)),
        out_shape=jax.ShapeDtypeStruct((batch * seq, V_W), BF16),
        compiler_params=pltpu.CompilerParams(
            dimension_semantics=("arbitrary", "arbitrary", "arbitrary"),
            vmem_limit_bytes=VMEM_LIMIT),
        name="mla_attn",
    )(q, k, v, k_meta, v_meta)


SOLVE_PIECES = 2
LEVELS = (1, 2, 4, 8, 16, 32)


def _gdn_constants():
    i = np.arange(CHUNK)[:, None]
    lane = np.arange(GDN_W)[None, :]
    j = lane % GDN_DK
    c = {}
    c["bd_ones"] = (np.arange(GDN_W)[:, None] // GDN_DK == lane // GDN_DK).astype(np.float32)
    r = np.arange(LANES)[:, None]
    c["expand_b"] = (r == lane // GDN_DK).astype(np.float32)
    c["expand_a"] = (r == GDN_HEADS + lane // GDN_DK).astype(np.float32)
    c["ltri"] = (i >= np.arange(CHUNK)[None, :]).astype(np.float32)
    c["eye_t"] = (i == j).astype(np.float32)
    c["tril_t"] = (i >= j).astype(np.float32)
    c["stril_t"] = (i > j).astype(np.float32)
    jg = j[:, :GROUP_W]
    c["level_masks"] = np.stack([
        ((i // (2 * s) == jg // (2 * s)) & ((i // s) % 2 == 1) & ((jg // s) % 2 == 0))
        for s in LEVELS]).astype(np.float32)
    g = np.arange(GROUP_W)
    c["bd_mask"] = (g[:, None] // GDN_DK == g[None, :] // GDN_DK).astype(np.float32)
    c["head_masks"] = np.stack([(g[None, :] // GDN_DK == h) for h in range(GROUP_HEADS)]).astype(np.float32)
    bf = ("bd_ones", "expand_b", "expand_a", "ltri", "bd_mask")
    return {k: jnp.asarray(v, BF16 if k in bf else F32) for k, v in c.items()}


def _block_diag(y, bd_mask):
    return jnp.concatenate([y] * GROUP_HEADS, axis=0) * bd_mask


def _head_matmul(x, y, bd_mask, pieces):
    xs = _split(x, pieces)
    ys = [_block_diag(p, bd_mask) for p in _split(y, pieces)]
    out = None
    for a in range(pieces):
        for b in range(pieces - a):
            t = _dot(xs[a], ys[b])
            out = t if out is None else out + t
    return out


def _gdn_kernel(gx_ref, gm_ref, bax_ref, bam_ref, z_ref, convw_ref, arate_ref, dtb_ref, gnorm_ref,
                bd_ones_ref, expand_b_ref, expand_a_ref, ltri_ref, eye_t_ref, tril_t_ref, stril_t_ref,
                level_masks_ref, bd_mask_ref, head_masks_ref,
                o_ref, state_ref, carry_ref):
    c = pl.program_id(1)
    first = c == 0

    @pl.when(first)
    def _():
        state_ref[...] = jnp.zeros_like(state_ref)
        carry_ref[...] = jnp.zeros_like(carry_ref)

    raw = jnp.where(first, gm_ref[...], gx_ref[...])
    xe = jnp.concatenate([carry_ref[...], raw], axis=0)
    w = convw_ref[...]
    y = sum(w[t:t + 1] * xe[8 - (CONV_K - 1) + t:8 - (CONV_K - 1) + t + CHUNK] for t in range(CONV_K))
    carry_ref[...] = raw[CHUNK - 8:]
    qkv = _silu(y)
    q, k, v = qkv[:, :GDN_W], qkv[:, GDN_W:2 * GDN_W], qkv[:, 2 * GDN_W:]

    bd_ones = bd_ones_ref[...]
    ss = _dot_exact_rhs(jnp.concatenate([q * q, k * k], axis=0), bd_ones, 2)
    qn = q * lax.rsqrt(ss[:CHUNK] + EPS) * (GDN_DK ** -0.5)
    kn = k * lax.rsqrt(ss[CHUNK:] + EPS)

    ba = jnp.where(first, bam_ref[...], bax_ref[...])
    row = lax.broadcasted_iota(jnp.int32, (CHUNK, 1), 0)
    valid = jnp.logical_or(jnp.logical_not(first), row >= CHUNK - N_META)
    beta = jnp.where(valid, jax.nn.sigmoid(ba), 0.0)
    sp_in = ba + dtb_ref[...]
    softplus = jnp.maximum(sp_in, 0.0) + jnp.log1p(jnp.exp(-jnp.abs(sp_in)))
    g = jnp.where(valid, arate_ref[...] * softplus, 0.0)
    beta_e = _dot_exact_rhs(beta, expand_b_ref[...], 2)
    gc = _dot_exact_lhs(ltri_ref[...], g, 3)
    gc_e = _dot_exact_rhs(gc, expand_a_ref[...], 3)
    gc_t = jnp.sum(gc_e * eye_t_ref[...], axis=0, keepdims=True)
    decay = jnp.exp(jnp.where(tril_t_ref[...] > 0.5, gc_e - gc_t, -jnp.inf))
    g_last = gc_e[CHUNK - 1:CHUNK]
    kb = kn * beta_e
    vb = v * beta_e
    kbg = kb * jnp.exp(gc_e)
    qg = qn * jnp.exp(gc_e)
    kdec = kn * jnp.exp(g_last - gc_e)
    s_decay = jnp.exp(g_last)
    stril = stril_t_ref[...]
    eye = eye_t_ref[...]
    bd_mask = bd_mask_ref[...]
    bd_mask_f = bd_mask.astype(F32)

    outs = []
    for gi in range(N_GROUPS):
        gs = slice(gi * GROUP_W, (gi + 1) * GROUP_W)
        k4 = kn[:, gs].astype(BF16)
        kstack = jnp.concatenate(
            [k4 * head_masks_ref[h].astype(BF16) for h in range(GROUP_HEADS)], axis=0)
        lhs = jnp.concatenate([kb[:, gs], qn[:, gs]], axis=0).astype(BF16)
        sc = _dot_nt(lhs, kstack)
        lmat = sc[:CHUNK] * decay[:, gs] * stril[:, gs]
        amat = sc[CHUNK:] * decay[:, gs]

        xinv = eye[:, gs] - lmat * level_masks_ref[0]
        for li in range(1, len(LEVELS)):
            off = lmat * level_masks_ref[li]
            xo = _head_matmul(xinv, off, bd_mask, SOLVE_PIECES)
            xinv = xinv - _head_matmul(xo, xinv, bd_mask, SOLVE_PIECES)
        tmat = xinv.astype(BF16)

        u = _dot(tmat, _block_diag(vb[:, gs].astype(BF16), bd_mask))
        wmat = _dot(tmat, _block_diag(kbg[:, gs].astype(BF16), bd_mask))

        state = state_ref[gi]
        state_bd = _block_diag(state.astype(BF16), bd_mask)
        v_new = u - _dot(wmat.astype(BF16), state_bd)
        o = (_dot(qg[:, gs].astype(BF16), state_bd)
             + _dot(amat.astype(BF16), _block_diag(v_new.astype(BF16), bd_mask)))
        outs.append(o)
        kv = _dot(kdec[:, gs].T.astype(BF16), v_new.astype(BF16))
        upd = sum(kv[h * GDN_DK:(h + 1) * GDN_DK] * head_masks_ref[h] for h in range(GROUP_HEADS))
        state_ref[gi] = state * s_decay[:, gs] + upd

    o = jnp.concatenate(outs, axis=1)
    ms = _dot_exact_rhs(o * o, bd_ones, 2) * (1.0 / GDN_DV)
    o = o * lax.rsqrt(ms + EPS) * gnorm_ref[...] * _silu(z_ref[...])
    o_ref[...] = o.astype(BF16)


def _gdn(gqkv, gqkv_meta, gba, gba_meta, z, convw, arate, dtb, gnorm, batch, seq):
    nc = seq // CHUNK
    consts = _gdn_constants()
    names = ("bd_ones", "expand_b", "expand_a", "ltri", "eye_t", "tril_t", "stril_t",
             "level_masks", "bd_mask", "head_masks")
    cvals = [consts[n] for n in names]
    xrow = lambda w: pl.BlockSpec((CHUNK, w), lambda b, c: (b * nc + jnp.maximum(c - 1, 0), 0))
    return pl.pallas_call(
        _gdn_kernel,
        grid=(batch, nc + 1),
        in_specs=[xrow(3 * GDN_W), _const_spec(gqkv_meta.shape), xrow(LANES), _const_spec(gba_meta.shape),
                  xrow(GDN_W)]
                 + [_const_spec(a.shape) for a in (convw, arate, dtb, gnorm)]
                 + [_const_spec(a.shape) for a in cvals],
        out_specs=xrow(GDN_W),
        out_shape=jax.ShapeDtypeStruct((batch * seq, GDN_W), BF16),
        scratch_shapes=[pltpu.VMEM((N_GROUPS, GDN_DK, GROUP_W), F32),
                        pltpu.VMEM((8, 3 * GDN_W), F32)],
        compiler_params=pltpu.CompilerParams(dimension_semantics=("arbitrary", "arbitrary"),
                                             vmem_limit_bytes=VMEM_LIMIT),
        name="gdn_chunk",
    )(gqkv, gqkv_meta, gba, gba_meta, z, convw, arate, dtb, gnorm, *cvals)


def _merge_ffn_kernel(h1_ref, om_ref, og_ref, nm_ref, wing_ref, wmo_ref, wgo_ref, wout_ref,
                      n2_ref, wgu_ref, wd_ref, nf_ref, out_ref):
    h1 = h1_ref[...]
    un = _rms(h1, nm_ref[...]).astype(BF16)
    gates = jax.nn.sigmoid(_dot(un, wing_ref[...]))
    merged = (gates[:, :D_MODEL] * _dot(om_ref[...], wmo_ref[...])
              + gates[:, D_MODEL:] * _dot(og_ref[...], wgo_ref[...]))
    h2 = h1 + _dot(merged.astype(BF16), wout_ref[...])
    xn = _rms(h2, n2_ref[...]).astype(BF16)
    gu = _dot(xn, wgu_ref[...])
    act = (_silu(gu[:, :D_FF]) * gu[:, D_FF:]).astype(BF16)
    h3 = h2 + 0.5 * _dot(act, wd_ref[...])
    out_ref[...] = _rms(h3, nf_ref[...])


def _merge_ffn(h1, o_mla, o_gdn, weights, tm):
    n = h1.shape[0]
    row = lambda w: pl.BlockSpec((tm, w), lambda i: (i, 0))
    return pl.pallas_call(
        _merge_ffn_kernel,
        grid=(n // tm,),
        in_specs=[row(D_MODEL), row(V_W), row(GDN_W)] + [_const_spec(w.shape) for w in weights],
        out_specs=row(D_MODEL),
        out_shape=jax.ShapeDtypeStruct((n, D_MODEL), F32),
        compiler_params=pltpu.CompilerParams(dimension_semantics=("arbitrary",),
                                             vmem_limit_bytes=VMEM_LIMIT),
        name="merge_ffn",
    )(h1, o_mla, o_gdn, *weights)


def _rope_tables(pos):
    inv = ROPE_THETA ** (-jnp.arange(0, MLA_ROPE, 2, dtype=F32) / MLA_ROPE)
    ang = pos.astype(F32)[:, None] * inv[None, :]
    cos, sin = jnp.cos(ang), jnp.sin(ang)
    cos2 = jnp.concatenate([cos, cos], axis=1)
    sin2 = jnp.concatenate([sin, sin], axis=1)
    n = pos.shape[0]
    scale = (MLA_NOPE + MLA_ROPE) ** -0.5
    pad = jnp.zeros((n, HEAD_PAD - MLA_NOPE - MLA_ROPE), F32)
    tab_c = jnp.concatenate([jnp.ones((n, MLA_NOPE), F32), cos2, pad], axis=1) * scale
    tab_s = jnp.concatenate([jnp.zeros((n, MLA_NOPE), F32), sin2, pad], axis=1) * scale
    tab_k = jnp.concatenate([cos2, sin2, jnp.zeros((n, LANES - 2 * MLA_ROPE), F32)], axis=1)
    return tab_c, tab_s, tab_k


def _rot(w):
    half = MLA_ROPE // 2
    return jnp.concatenate([-w[..., half:], w[..., :half]], axis=-1)


def kernel(x, meta_tokens, ffn1_norm, ffn1_w_gate, ffn1_w_up, ffn1_w_down, mix_norm, w_in, q_norm, w_uq,
           kv_norm, w_ukv, w_mla_o, conv_w, a_log, dt_bias, gdn_norm, w_gdn_o, w_out, ffn2_norm,
           ffn2_w_gate, ffn2_w_up, ffn2_w_down, final_norm):
    assert ffn1_norm.shape[0] == 1, "single-layer block"
    batch, seq, d = x.shape
    assert d == D_MODEL and seq % CHUNK == 0
    tm = min(256, seq)
    tq = min(512, seq)
    assert seq % tm == 0 and seq % tq == 0

    wi = w_in[0]
    sizes = (MLA_Q_RANK, MLA_KV_RANK, MLA_ROPE, GDN_W, GDN_W, GDN_W, GDN_HEADS, GDN_HEADS, GDN_W,
             D_MODEL, D_MODEL)
    offs = np.concatenate([[0], np.cumsum(sizes)])
    col = lambda a, b: wi[:, int(offs[a]):int(offs[b])]
    zcols = lambda n: jnp.zeros((D_MODEL, n), F32)
    w_kr = col(2, 3)
    win_a = jnp.concatenate([col(0, 2), w_kr, _rot(w_kr), zcols(LANES - 2 * MLA_ROPE), col(3, 6),
                             col(6, 8), zcols(LANES - 2 * GDN_HEADS), col(8, 9)], axis=1).astype(BF16)
    assert win_a.shape[1] == PROJ_W
    win_g = col(9, 11).astype(BF16)

    wq = w_uq[0].reshape(MLA_Q_RANK, MLA_HEADS, MLA_NOPE + MLA_ROPE)
    wq_nope, wq_rope = wq[..., :MLA_NOPE], wq[..., MLA_NOPE:]
    zq = lambda n: jnp.zeros((MLA_Q_RANK, MLA_HEADS, n), F32)
    tail = HEAD_PAD - MLA_NOPE - MLA_ROPE
    wqa = jnp.concatenate([wq_nope, wq_rope, zq(tail)], axis=-1).reshape(MLA_Q_RANK, QK_W).astype(BF16)
    wqb = jnp.concatenate([zq(MLA_NOPE), _rot(wq_rope), zq(tail)], axis=-1).reshape(MLA_Q_RANK, QK_W).astype(BF16)
    wkv = w_ukv[0].reshape(MLA_KV_RANK, MLA_HEADS, MLA_NOPE + MLA_V)
    wk = jnp.concatenate([wkv[..., :MLA_NOPE], jnp.zeros((MLA_KV_RANK, MLA_HEADS, HEAD_PAD - MLA_NOPE), F32)],
                         axis=-1).reshape(MLA_KV_RANK, QK_W).astype(BF16)
    wv = wkv[..., MLA_NOPE:].reshape(MLA_KV_RANK, V_W).astype(BF16)
    e_np = np.zeros((LANES, QK_W), np.float32)
    for h in range(MLA_HEADS):
        for j in range(MLA_ROPE):
            e_np[j, h * HEAD_PAD + MLA_NOPE + j] = 1.0
            e_np[MLA_ROPE + j, h * HEAD_PAD + MLA_NOPE + j] = 1.0
    e_mat = jnp.asarray(e_np, BF16)

    proj_weights = [
        ffn1_norm[0][None], jnp.concatenate([ffn1_w_gate[0], ffn1_w_up[0]], axis=1).astype(BF16),
        ffn1_w_down[0].astype(BF16), mix_norm[0][None], win_a, q_norm[0][None], wqa, wqb,
        kv_norm[0][None], wk, wv, e_mat]

    tabs_x = _rope_tables(N_META + jnp.arange(seq))
    h1, q, k, v, gqkv, gba, z = _token_proj(x.reshape(batch * seq, d), tabs_x, proj_weights, tm, seq // tm)
    tabs_m = _rope_tables(jnp.arange(N_META))
    _, _, k_m, v_m, gqkv_m, gba_m, _ = _token_proj(meta_tokens.astype(F32), tabs_m, proj_weights, N_META, 1)

    pad_rows = lambda a, n, front: jnp.pad(a, ((n - a.shape[0], 0) if front else (0, n - a.shape[0]), (0, 0)))
    o_mla = _mla_attn(q, k, v, pad_rows(k_m, LANES, False), pad_rows(v_m, LANES, False), batch, seq, tq)

    hpad = lambda a: jnp.zeros((1, LANES), F32).at[0, GDN_HEADS:2 * GDN_HEADS].set(a)
    arate = hpad(-jnp.exp(a_log[0].astype(F32)))
    dtb = hpad(dt_bias[0].astype(F32))
    gnorm = jnp.tile(gdn_norm[0].astype(F32), GDN_HEADS)[None]
    o_gdn = _gdn(gqkv, pad_rows(gqkv_m, CHUNK, True), gba, pad_rows(gba_m, CHUNK, True), z,
                 conv_w[0].astype(F32), arate, dtb, gnorm, batch, seq)

    merge_weights = [
        mix_norm[0][None], win_g, w_mla_o[0].astype(BF16), w_gdn_o[0].astype(BF16), w_out[0].astype(BF16),
        ffn2_norm[0][None], jnp.concatenate([ffn2_w_gate[0], ffn2_w_up[0]], axis=1).astype(BF16),
        ffn2_w_down[0].astype(BF16), final_norm[None]]
    out = _merge_ffn(h1, o_mla, o_gdn, merge_weights, tm)
    return out.reshape(batch, seq, d)
```

```python
import functools
import math

import jax
import jax.numpy as jnp
import numpy as np
from jax import lax
from jax.experimental import pallas as pl
from jax.experimental.pallas import tpu as pltpu

F32 = jnp.float32
BF16 = jnp.bfloat16

D_MODEL = 1024
N_META = 16
EPS = 1e-6
D_FF = 2816
MLA_HEADS = 8
MLA_Q_RANK = 256
MLA_KV_RANK = 128
MLA_NOPE = 64
MLA_ROPE = 32
MLA_V = 64
ROPE_THETA = 10000.0
GDN_HEADS = 8
GDN_DK = 64
GDN_DV = 64
CONV_K = 4
CHUNK = 64

LANES = 128
HEAD_PAD = 128
QK_W = MLA_HEADS * HEAD_PAD
V_W = MLA_HEADS * MLA_V
GDN_W = GDN_HEADS * GDN_DK
GROUP_HEADS = 4
GROUP_W = GROUP_HEADS * GDN_DK
N_GROUPS = GDN_HEADS // GROUP_HEADS
PROJ_W = MLA_Q_RANK + MLA_KV_RANK + LANES + 3 * GDN_W + LANES + GDN_W
VMEM_LIMIT = 56 * 1024 * 1024
NEG_BIG = -1e30


def _const_spec(shape):
    zeros = (0,) * len(shape)
    return pl.BlockSpec(shape, lambda *_: zeros, pipeline_mode=pl.Buffered(1))


def _rms(x, w):
    return x * lax.rsqrt(jnp.mean(x * x, axis=-1, keepdims=True) + EPS) * w


def _dot(a, b):
    return jnp.dot(a, b, preferred_element_type=F32)


def _dot_nt(a, b):
    return lax.dot_general(a, b, (((1,), (1,)), ((), ())), preferred_element_type=F32)


def _silu(x):
    return x * jax.nn.sigmoid(x)


def _split(x, n):
    pieces = []
    for _ in range(n - 1):
        hi = x.astype(BF16)
        pieces.append(hi)
        x = x - hi.astype(F32)
    pieces.append(x.astype(BF16))
    return pieces


def _dot_exact_rhs(x, rhs, n):
    return sum(_dot(p, rhs) for p in _split(x, n))


def _dot_exact_lhs(lhs, x, n):
    return sum(_dot(lhs, p) for p in _split(x, n))


def _token_proj_kernel(x_ref, tc_ref, ts_ref, tk_ref, n1_ref, wgu_ref, wd_ref, nm_ref, win_ref,
                       qn_ref, wqa_ref, wqb_ref, kvn_ref, wk_ref, wv_ref, e_ref,
                       h1_ref, q_ref, k_ref, v_ref, gqkv_ref, gba_ref, z_ref):
    x = x_ref[...]
    xn = _rms(x, n1_ref[...]).astype(BF16)
    gu = _dot(xn, wgu_ref[...])
    act = (_silu(gu[:, :D_FF]) * gu[:, D_FF:]).astype(BF16)
    h1 = x + 0.5 * _dot(act, wd_ref[...])
    h1_ref[...] = h1

    un = _rms(h1, nm_ref[...]).astype(BF16)
    p = _dot(un, win_ref[...])
    o = 0
    cq = p[:, o:o + MLA_Q_RANK]; o += MLA_Q_RANK
    ckv = p[:, o:o + MLA_KV_RANK]; o += MLA_KV_RANK
    kr = p[:, o:o + LANES]; o += LANES
    gqkv_ref[...] = p[:, o:o + 3 * GDN_W]; o += 3 * GDN_W
    gba_ref[...] = p[:, o:o + LANES]; o += LANES
    z_ref[...] = p[:, o:o + GDN_W]

    cqn = _rms(cq, qn_ref[...]).astype(BF16)
    tc = jnp.concatenate([tc_ref[...]] * MLA_HEADS, axis=1)
    ts = jnp.concatenate([ts_ref[...]] * MLA_HEADS, axis=1)
    q = _dot(cqn, wqa_ref[...]) * tc + _dot(cqn, wqb_ref[...]) * ts
    q_ref[...] = q.astype(BF16)

    ckvn = _rms(ckv, kvn_ref[...]).astype(BF16)
    k = _dot(ckvn, wk_ref[...]) + _dot((kr * tk_ref[...]).astype(BF16), e_ref[...])
    k_ref[...] = k.astype(BF16)
    v_ref[...] = _dot(ckvn, wv_ref[...]).astype(BF16)


def _token_proj(x2d, tabs, weights, tm, tab_blocks):
    n = x2d.shape[0]
    assert n % tm == 0
    row = lambda w: pl.BlockSpec((tm, w), lambda i: (i, 0))
    tab = pl.BlockSpec((tm, LANES), lambda i: (i % tab_blocks, 0))
    out_widths = (D_MODEL, QK_W, QK_W, V_W, 3 * GDN_W, LANES, GDN_W)
    out_dtypes = (F32, BF16, BF16, BF16, F32, F32, F32)
    return pl.pallas_call(
        _token_proj_kernel,
        grid=(n // tm,),
        in_specs=[row(D_MODEL), tab, tab, tab] + [_const_spec(w.shape) for w in weights],
        out_specs=[row(w) for w in out_widths],
        out_shape=[jax.ShapeDtypeStruct((n, w), d) for w, d in zip(out_widths, out_dtypes)],
        compiler_params=pltpu.CompilerParams(dimension_semantics=("arbitrary",),
                                             vmem_limit_bytes=VMEM_LIMIT),
        name="token_proj",
    )(x2d, *tabs, *weights)


def _mla_kernel(q_ref, k_ref, v_ref, km_ref, vm_ref, o_ref, *, tq, tk):
    qi = pl.program_id(2)
    meta_rows = km_ref.shape[0]
    meta_valid = lax.broadcasted_iota(jnp.int32, (tq, meta_rows), 1) < N_META
    causal = (lax.broadcasted_iota(jnp.int32, (tq, tk), 0)
              >= lax.broadcasted_iota(jnp.int32, (tq, tk), 1))
    outs = []
    for hh in range(2):
        hs = slice(hh * HEAD_PAD, (hh + 1) * HEAD_PAD)
        q = q_ref[:, hs]

        s = jnp.where(meta_valid, _dot_nt(q, km_ref[:, hs]), NEG_BIG)
        m = jnp.max(s, axis=-1, keepdims=True)
        p = jnp.exp(s - m)
        l = jnp.sum(p, axis=-1, keepdims=True)
        acc = _dot(p.astype(BF16), vm_ref[...])

        def block(ki, carry, masked):
            m, l, acc = carry
            rows = pl.ds(pl.multiple_of(ki * tk, tk), tk)
            s = _dot_nt(q, k_ref[rows, hs])
            if masked:
                s = jnp.where(causal, s, NEG_BIG)
            m_new = jnp.maximum(m, jnp.max(s, axis=-1, keepdims=True))
            alpha = jnp.exp(m - m_new)
            p = jnp.exp(s - m_new)
            l = alpha * l + jnp.sum(p, axis=-1, keepdims=True)
            acc = alpha * acc + _dot(p.astype(BF16), v_ref[rows, :])
            return m_new, l, acc

        carry = lax.fori_loop(0, qi, functools.partial(block, masked=False), (m, l, acc))
        m, l, acc = block(qi, carry, masked=True)
        outs.append(acc / l)
    lane = lax.broadcasted_iota(jnp.int32, outs[0].shape, 1)
    o_ref[...] = jnp.where(lane < MLA_V, outs[0], outs[1]).astype(BF16)


def _mla_attn(q, k, v, k_meta, v_meta, batch, seq, tq):
    nq = seq // tq
    kern = functools.partial(_mla_kernel, tq=tq, tk=tq)
    return pl.pallas_call(
        kern,
        grid=(batch, MLA_HEADS // 2, nq),
        in_specs=[
            pl.BlockSpec((tq, 2 * HEAD_PAD), lambda b, hp, i: (b * nq + i, hp)),
            pl.BlockSpec((seq, 2 * HEAD_PAD), lambda b, hp, i: (b, hp)),
            pl.BlockSpec((seq, 2 * MLA_V), lambda b, hp, i: (b, hp)),
            pl.BlockSpec((k_meta.shape[0], 2 * HEAD_PAD), lambda b, hp, i: (0, hp)),
            pl.BlockSpec((v_meta.shape[0], 2 * MLA_V), lambda b, hp, i: (0, hp)),
        ],
        out_specs=pl.BlockSpec((tq, 2 * MLA_V), lambda b, hp, i: (b * nq + i, hp)),
        out_shape=jax.ShapeDtypeStruct((batch * seq, V_W), BF16),
        compiler_params=pltpu.CompilerParams(
            dimension_semantics=("arbitrary", "arbitrary", "arbitrary"),
            vmem_limit_bytes=VMEM_LIMIT),
        name="mla_attn",
    )(q, k, v, k_meta, v_meta)


LEVELS = (1, 2, 4, 8, 16, 32)
GDN_CONST_NAMES = ("bd_ones", "expand_b", "expand_a", "ltri", "eye_t", "tril_t", "stril_t",
                   "level_masks", "bd_mask", "head_masks")


def _gdn_constants(cps):
    i = np.arange(CHUNK)[:, None]
    lane = np.arange(GDN_W)[None, :]
    j = lane % GDN_DK
    c = {}
    c["bd_ones"] = (np.arange(GDN_W)[:, None] // GDN_DK == lane // GDN_DK)
    r = np.arange(LANES)[:, None]
    c["expand_b"] = (r == lane // GDN_DK)
    c["expand_a"] = (r == GDN_HEADS + lane // GDN_DK)
    t = np.arange(cps * CHUNK)
    c["ltri"] = (t[:, None] >= t[None, :]) & (t[:, None] // CHUNK == t[None, :] // CHUNK)
    c["eye_t"] = (i == j)
    c["tril_t"] = (i >= j)
    c["stril_t"] = (i > j)
    jg = j[:, :GROUP_W]
    c["level_masks"] = np.stack([
        ((i // (2 * s) == jg // (2 * s)) & ((i // s) % 2 == 1) & ((jg // s) % 2 == 0))
        for s in LEVELS])
    g = np.arange(GROUP_W)
    c["bd_mask"] = (g[:, None] // GDN_DK == g[None, :] // GDN_DK)
    c["head_masks"] = np.stack([(g[None, :] // GDN_DK == h) for h in range(GROUP_HEADS)])
    bf = ("bd_ones", "expand_b", "expand_a", "ltri", "bd_mask")
    return [jnp.asarray(c[k].astype(np.float32), BF16 if k in bf else F32) for k in GDN_CONST_NAMES]


def _block_diag(y, bd_mask):
    return jnp.concatenate([y] * GROUP_HEADS, axis=0) * bd_mask


def _head_matmul(x, y, bd_mask):
    return _dot(x.astype(BF16), _block_diag(y.astype(BF16), bd_mask))


def _diag_blocks(m, head_masks):
    reps = m.shape[1] // GROUP_W
    return sum(m[h * GDN_DK:(h + 1) * GDN_DK] * jnp.concatenate([head_masks[h]] * reps, axis=1)
               for h in range(GROUP_HEADS))


def _gdn_kernel(gx_ref, ba_ref, z_ref, s0_ref, c0_ref, convw_ref, arate_ref, dtb_ref, gnorm_ref,
                bd_ones_ref, expand_b_ref, expand_a_ref, ltri_ref, eye_t_ref, tril_t_ref, stril_t_ref,
                level_masks_ref, bd_mask_ref, head_masks_ref,
                o_ref, sout_ref, state_ref, carry_ref, *, cps, front_pad):
    c = pl.program_id(1)
    rows = cps * CHUNK

    @pl.when(c == 0)
    def _():
        state_ref[...] = s0_ref[...]
        carry_ref[...] = c0_ref[...]

    raw = gx_ref[...]
    xe = jnp.concatenate([carry_ref[...], raw], axis=0)
    w = convw_ref[...]
    y = sum(w[t:t + 1] * xe[8 - (CONV_K - 1) + t:8 - (CONV_K - 1) + t + rows] for t in range(CONV_K))
    carry_ref[...] = raw[rows - 8:]
    qkv = _silu(y)
    q, k, v = qkv[:, :GDN_W], qkv[:, GDN_W:2 * GDN_W], qkv[:, 2 * GDN_W:]

    bd_ones = bd_ones_ref[...]
    ss = _dot_exact_rhs(jnp.concatenate([q * q, k * k], axis=0), bd_ones, 2)
    qn = q * lax.rsqrt(ss[:rows] + EPS) * (GDN_DK ** -0.5)
    kn = k * lax.rsqrt(ss[rows:] + EPS)

    ba = ba_ref[...]
    beta = jax.nn.sigmoid(ba)
    sp_in = ba + dtb_ref[...]
    g = arate_ref[...] * (jnp.maximum(sp_in, 0.0) + jnp.log1p(jnp.exp(-jnp.abs(sp_in))))
    if front_pad:
        valid = lax.broadcasted_iota(jnp.int32, (rows, 1), 0) >= front_pad
        beta = jnp.where(valid, beta, 0.0)
        g = jnp.where(valid, g, 0.0)
    beta_e = _dot_exact_rhs(beta, expand_b_ref[...], 2)
    gc = _dot_exact_lhs(ltri_ref[...], g, 3)
    gc_e = _dot_exact_rhs(gc, expand_a_ref[...], 3)
    egc = jnp.exp(gc_e)
    kb = kn * beta_e
    vb = v * beta_e
    kbg = kb * egc
    qg = qn * egc
    eye, tril, stril = eye_t_ref[...], tril_t_ref[...], stril_t_ref[...]
    bd_mask = bd_mask_ref[...]
    head_masks = [head_masks_ref[h] for h in range(GROUP_HEADS)]
    head_masks_bf = [m.astype(BF16) for m in head_masks]

    chains = [(j, gi) for j in range(cps) for gi in range(N_GROUPS)]
    rsl = lambda j: slice(j * CHUNK, (j + 1) * CHUNK)
    gsl = lambda gi: slice(gi * GROUP_W, (gi + 1) * GROUP_W)
    decay, kdec, s_decay = [], [], []
    for j in range(cps):
        gce = gc_e[rsl(j)]
        gc_t = jnp.sum(gce * eye, axis=0, keepdims=True)
        decay.append(jnp.exp(jnp.where(tril > 0.5, gce - gc_t, -jnp.inf)))
        g_last = gce[CHUNK - 1:CHUNK]
        kdec.append(kn[rsl(j)] * jnp.exp(g_last - gce))
        s_decay.append(jnp.exp(g_last))

    lmat, amat = [], []
    for j, gi in chains:
        rs, gs = rsl(j), gsl(gi)
        k4 = kn[rs, gs].astype(BF16)
        kstack = jnp.concatenate([k4 * m for m in head_masks_bf], axis=0)
        lhs = jnp.concatenate([kb[rs, gs], qn[rs, gs]], axis=0).astype(BF16)
        sc = _dot_nt(lhs, kstack)
        lmat.append(sc[:CHUNK] * decay[j][:, gs] * stril[:, gs])
        amat.append(sc[CHUNK:] * decay[j][:, gs])

    xinv = [eye[:, gsl(gi)] - lm * level_masks_ref[0] for (j, gi), lm in zip(chains, lmat)]
    for li in range(1, len(LEVELS)):
        xo = [_head_matmul(x, lm * level_masks_ref[li], bd_mask) for x, lm in zip(xinv, lmat)]
        xinv = [x - _head_matmul(y, x, bd_mask) for x, y in zip(xinv, xo)]

    uw = []
    for (j, gi), x in zip(chains, xinv):
        rs, gs = rsl(j), gsl(gi)
        rhs = jnp.concatenate([_block_diag(vb[rs, gs].astype(BF16), bd_mask),
                               _block_diag(kbg[rs, gs].astype(BF16), bd_mask)], axis=1)
        uw.append(_dot(x.astype(BF16), rhs))
    kwu = [_diag_blocks(_dot(kdec[j][:, gsl(gi)].T.astype(BF16), m.astype(BF16)), head_masks)
           for (j, gi), m in zip(chains, uw)]

    states = [state_ref[gi] for gi in range(N_GROUPS)]
    o_rows = [[None] * N_GROUPS for _ in range(cps)]
    for ci, (j, gi) in enumerate(chains):
        rs, gs = rsl(j), gsl(gi)
        u, wmat = uw[ci][:, :GROUP_W], uw[ci][:, GROUP_W:]
        ku, kw = kwu[ci][:, :GROUP_W], kwu[ci][:, GROUP_W:]
        state = states[gi]
        lhs = jnp.concatenate([kw, wmat, qg[rs, gs]], axis=0).astype(BF16)
        big = _dot(lhs, _block_diag(state.astype(BF16), bd_mask))
        states[gi] = s_decay[j][:, gs] * state - big[:CHUNK] + ku
        v_new = u - big[CHUNK:2 * CHUNK]
        o_rows[j][gi] = big[2 * CHUNK:] + _head_matmul(amat[ci], v_new, bd_mask)
    for gi in range(N_GROUPS):
        state_ref[gi] = states[gi]

    @pl.when(c == pl.num_programs(1) - 1)
    def _():
        sout_ref[...] = state_ref[...]

    o = jnp.concatenate([jnp.concatenate(r, axis=1) for r in o_rows], axis=0)
    ms = _dot_exact_rhs(o * o, bd_ones, 2) * (1.0 / GDN_DV)
    o = o * lax.rsqrt(ms + EPS) * gnorm_ref[...] * _silu(z_ref[...])
    o_ref[...] = o.astype(BF16)


def _gdn(gqkv, gba, z, state0, carry0, params, batch, seq, cps, front_pad):
    rows = cps * CHUNK
    steps = seq // rows
    consts = _gdn_constants(cps)
    xrow = lambda w: pl.BlockSpec((rows, w), lambda b, c: (b * steps + c, 0))
    state_shape = (N_GROUPS, GDN_DK, GROUP_W)
    return pl.pallas_call(
        functools.partial(_gdn_kernel, cps=cps, front_pad=front_pad),
        grid=(batch, steps),
        in_specs=[xrow(3 * GDN_W), xrow(LANES), xrow(GDN_W), _const_spec(state_shape),
                  _const_spec(carry0.shape)]
                 + [_const_spec(a.shape) for a in params]
                 + [_const_spec(a.shape) for a in consts],
        out_specs=[xrow(GDN_W), pl.BlockSpec(state_shape, lambda b, c: (0, 0, 0))],
        out_shape=[jax.ShapeDtypeStruct((batch * seq, GDN_W), BF16),
                   jax.ShapeDtypeStruct(state_shape, F32)],
        scratch_shapes=[pltpu.VMEM(state_shape, F32), pltpu.VMEM((8, 3 * GDN_W), F32)],
        compiler_params=pltpu.CompilerParams(dimension_semantics=("arbitrary", "arbitrary"),
                                             vmem_limit_bytes=VMEM_LIMIT),
        name="gdn_chunk",
    )(gqkv, gba, z, state0, carry0, *params, *consts)


def _merge_ffn_kernel(h1_ref, om_ref, og_ref, nm_ref, wing_ref, wmo_ref, wgo_ref, wout_ref,
                      n2_ref, wgu_ref, wd_ref, nf_ref, out_ref):
    h1 = h1_ref[...]
    un = _rms(h1, nm_ref[...]).astype(BF16)
    gates = jax.nn.sigmoid(_dot(un, wing_ref[...]))
    merged = (gates[:, :D_MODEL] * _dot(om_ref[...], wmo_ref[...])
              + gates[:, D_MODEL:] * _dot(og_ref[...], wgo_ref[...]))
    h2 = h1 + _dot(merged.astype(BF16), wout_ref[...])
    xn = _rms(h2, n2_ref[...]).astype(BF16)
    gu = _dot(xn, wgu_ref[...])
    act = (_silu(gu[:, :D_FF]) * gu[:, D_FF:]).astype(BF16)
    h3 = h2 + 0.5 * _dot(act, wd_ref[...])
    out_ref[...] = _rms(h3, nf_ref[...])


def _merge_ffn(h1, o_mla, o_gdn, weights, tm):
    n = h1.shape[0]
    row = lambda w: pl.BlockSpec((tm, w), lambda i: (i, 0))
    return pl.pallas_call(
        _merge_ffn_kernel,
        grid=(n // tm,),
        in_specs=[row(D_MODEL), row(V_W), row(GDN_W)] + [_const_spec(w.shape) for w in weights],
        out_specs=row(D_MODEL),
        out_shape=jax.ShapeDtypeStruct((n, D_MODEL), F32),
        compiler_params=pltpu.CompilerParams(dimension_semantics=("arbitrary",),
                                             vmem_limit_bytes=VMEM_LIMIT),
        name="merge_ffn",
    )(h1, o_mla, o_gdn, *weights)


def _rope_tables(pos):
    inv = ROPE_THETA ** (-jnp.arange(0, MLA_ROPE, 2, dtype=F32) / MLA_ROPE)
    ang = pos.astype(F32)[:, None] * inv[None, :]
    cos, sin = jnp.cos(ang), jnp.sin(ang)
    cos2 = jnp.concatenate([cos, cos], axis=1)
    sin2 = jnp.concatenate([sin, sin], axis=1)
    n = pos.shape[0]
    scale = (MLA_NOPE + MLA_ROPE) ** -0.5
    pad = jnp.zeros((n, HEAD_PAD - MLA_NOPE - MLA_ROPE), F32)
    tab_c = jnp.concatenate([jnp.ones((n, MLA_NOPE), F32), cos2, pad], axis=1) * scale
    tab_s = jnp.concatenate([jnp.zeros((n, MLA_NOPE), F32), sin2, pad], axis=1) * scale
    tab_k = jnp.concatenate([cos2, sin2, jnp.zeros((n, LANES - 2 * MLA_ROPE), F32)], axis=1)
    return tab_c, tab_s, tab_k


def _rot(w):
    half = MLA_ROPE // 2
    return jnp.concatenate([-w[..., half:], w[..., :half]], axis=-1)


def kernel(x, meta_tokens, ffn1_norm, ffn1_w_gate, ffn1_w_up, ffn1_w_down, mix_norm, w_in, q_norm, w_uq,
           kv_norm, w_ukv, w_mla_o, conv_w, a_log, dt_bias, gdn_norm, w_gdn_o, w_out, ffn2_norm,
           ffn2_w_gate, ffn2_w_up, ffn2_w_down, final_norm):
    assert ffn1_norm.shape[0] == 1, "single-layer block"
    batch, seq, d = x.shape
    assert d == D_MODEL and seq % CHUNK == 0
    tm = min(256, seq)
    tq = min(512, seq)
    assert seq % tm == 0 and seq % tq == 0

    wi = w_in[0]
    sizes = (MLA_Q_RANK, MLA_KV_RANK, MLA_ROPE, GDN_W, GDN_W, GDN_W, GDN_HEADS, GDN_HEADS, GDN_W,
             D_MODEL, D_MODEL)
    offs = np.concatenate([[0], np.cumsum(sizes)])
    col = lambda a, b: wi[:, int(offs[a]):int(offs[b])]
    zcols = lambda n: jnp.zeros((D_MODEL, n), F32)
    w_kr = col(2, 3)
    win_a = jnp.concatenate([col(0, 2), w_kr, _rot(w_kr), zcols(LANES - 2 * MLA_ROPE), col(3, 6),
                             col(6, 8), zcols(LANES - 2 * GDN_HEADS), col(8, 9)], axis=1).astype(BF16)
    assert win_a.shape[1] == PROJ_W
    win_g = col(9, 11).astype(BF16)

    wq = w_uq[0].reshape(MLA_Q_RANK, MLA_HEADS, MLA_NOPE + MLA_ROPE)
    wq_nope, wq_rope = wq[..., :MLA_NOPE], wq[..., MLA_NOPE:]
    zq = lambda n: jnp.zeros((MLA_Q_RANK, MLA_HEADS, n), F32)
    tail = HEAD_PAD - MLA_NOPE - MLA_ROPE
    wqa = jnp.concatenate([wq_nope, wq_rope, zq(tail)], axis=-1).reshape(MLA_Q_RANK, QK_W).astype(BF16)
    wqb = jnp.concatenate([zq(MLA_NOPE), _rot(wq_rope), zq(tail)], axis=-1).reshape(MLA_Q_RANK, QK_W).astype(BF16)
    wkv = w_ukv[0].reshape(MLA_KV_RANK, MLA_HEADS, MLA_NOPE + MLA_V)
    wk = jnp.concatenate([wkv[..., :MLA_NOPE], jnp.zeros((MLA_KV_RANK, MLA_HEADS, HEAD_PAD - MLA_NOPE), F32)],
                         axis=-1).reshape(MLA_KV_RANK, QK_W).astype(BF16)
    wv = wkv[..., MLA_NOPE:].reshape(MLA_KV_RANK, V_W).astype(BF16)
    e_np = np.zeros((LANES, QK_W), np.float32)
    for h in range(MLA_HEADS):
        for j in range(MLA_ROPE):
            e_np[j, h * HEAD_PAD + MLA_NOPE + j] = 1.0
            e_np[MLA_ROPE + j, h * HEAD_PAD + MLA_NOPE + j] = 1.0
    e_mat = jnp.asarray(e_np, BF16)

    proj_weights = [
        ffn1_norm[0][None], jnp.concatenate([ffn1_w_gate[0], ffn1_w_up[0]], axis=1).astype(BF16),
        ffn1_w_down[0].astype(BF16), mix_norm[0][None], win_a, q_norm[0][None], wqa, wqb,
        kv_norm[0][None], wk, wv, e_mat]

    tabs_x = _rope_tables(N_META + jnp.arange(seq))
    h1, q, k, v, gqkv, gba, z = _token_proj(x.reshape(batch * seq, d), tabs_x, proj_weights, tm, seq // tm)
    tabs_m = _rope_tables(jnp.arange(N_META))
    _, _, k_m, v_m, gqkv_m, gba_m, _ = _token_proj(meta_tokens.astype(F32), tabs_m, proj_weights, N_META, 1)

    pad_rows = lambda a, n, front: jnp.pad(a, ((n - a.shape[0], 0) if front else (0, n - a.shape[0]), (0, 0)))
    o_mla = _mla_attn(q, k, v, pad_rows(k_m, LANES, False), pad_rows(v_m, LANES, False), batch, seq, tq)

    hpad = lambda a: jnp.zeros((1, LANES), F32).at[0, GDN_HEADS:2 * GDN_HEADS].set(a)
    arate = hpad(-jnp.exp(a_log[0].astype(F32)))
    dtb = hpad(dt_bias[0].astype(F32))
    gnorm = jnp.tile(gdn_norm[0].astype(F32), GDN_HEADS)[None]
    gdn_params = (conv_w[0].astype(F32), arate, dtb, gnorm)
    cps = 4 if seq % (4 * CHUNK) == 0 else 1
    gqkv_mp = pad_rows(gqkv_m, CHUNK, True)
    _, state_meta = _gdn(gqkv_mp, pad_rows(gba_m, CHUNK, True), jnp.zeros((CHUNK, GDN_W), F32),
                         jnp.zeros((N_GROUPS, GDN_DK, GROUP_W), F32), jnp.zeros((8, 3 * GDN_W), F32),
                         gdn_params, 1, CHUNK, 1, CHUNK - N_META)
    o_gdn, _ = _gdn(gqkv, gba, z, state_meta, gqkv_mp[CHUNK - 8:], gdn_params, batch, seq, cps, 0)

    merge_weights = [
        mix_norm[0][None], win_g, w_mla_o[0].astype(BF16), w_gdn_o[0].astype(BF16), w_out[0].astype(BF16),
        ffn2_norm[0][None], jnp.concatenate([ffn2_w_gate[0], ffn2_w_up[0]], axis=1).astype(BF16),
        ffn2_w_down[0].astype(BF16), final_norm[None]]
    out = _merge_ffn(h1, o_mla, o_gdn, merge_weights, tm)
    return out.reshape(batch, seq, d)
```

```python
import functools
import math

import jax
import jax.numpy as jnp
import numpy as np
from jax import lax
from jax.experimental import pallas as pl
from jax.experimental.pallas import tpu as pltpu

F32 = jnp.float32
BF16 = jnp.bfloat16

D_MODEL = 1024
N_META = 16
EPS = 1e-6
D_FF = 2816
MLA_HEADS = 8
MLA_Q_RANK = 256
MLA_KV_RANK = 128
MLA_NOPE = 64
MLA_ROPE = 32
MLA_V = 64
ROPE_THETA = 10000.0
GDN_HEADS = 8
GDN_DK = 64
GDN_DV = 64
CONV_K = 4
CHUNK = 64

LANES = 128
HEAD_PAD = 128
QK_W = MLA_HEADS * HEAD_PAD
V_W = MLA_HEADS * MLA_V
GDN_W = GDN_HEADS * GDN_DK
GROUP_HEADS = 4
GROUP_W = GROUP_HEADS * GDN_DK
N_GROUPS = GDN_HEADS // GROUP_HEADS
PROJ_W = MLA_Q_RANK + MLA_KV_RANK + LANES + 3 * GDN_W + LANES + GDN_W
VMEM_LIMIT = 56 * 1024 * 1024
NEG_BIG = -1e30


def _const_spec(shape):
    zeros = (0,) * len(shape)
    return pl.BlockSpec(shape, lambda *_: zeros, pipeline_mode=pl.Buffered(1))


def _rms(x, w):
    return x * lax.rsqrt(jnp.mean(x * x, axis=-1, keepdims=True) + EPS) * w


def _dot(a, b):
    return jnp.dot(a, b, preferred_element_type=F32)


def _dot_nt(a, b):
    return lax.dot_general(a, b, (((1,), (1,)), ((), ())), preferred_element_type=F32)


def _silu(x):
    return x * jax.nn.sigmoid(x)


def _split(x, n):
    pieces = []
    for _ in range(n - 1):
        hi = x.astype(BF16)
        pieces.append(hi)
        x = x - hi.astype(F32)
    pieces.append(x.astype(BF16))
    return pieces


def _dot_exact_rhs(x, rhs, n):
    return sum(_dot(p, rhs) for p in _split(x, n))


def _dot_exact_lhs(lhs, x, n):
    return sum(_dot(lhs, p) for p in _split(x, n))


def _token_proj_kernel(x_ref, tc_ref, ts_ref, tk_ref, n1_ref, wgu_ref, wd_ref, nm_ref, win_ref,
                       qn_ref, wqa_ref, wqb_ref, kvn_ref, wk_ref, wvt_ref, e_ref,
                       h1_ref, q_ref, k_ref, vt_ref, gqkv_ref, gba_ref, z_ref):
    x = x_ref[...]
    xn = _rms(x, n1_ref[...]).astype(BF16)
    gu = _dot(xn, wgu_ref[...])
    act = (_silu(gu[:, :D_FF]) * gu[:, D_FF:]).astype(BF16)
    h1 = x + 0.5 * _dot(act, wd_ref[...])
    h1_ref[...] = h1

    un = _rms(h1, nm_ref[...]).astype(BF16)
    p = _dot(un, win_ref[...])
    o = 0
    cq = p[:, o:o + MLA_Q_RANK]; o += MLA_Q_RANK
    ckv = p[:, o:o + MLA_KV_RANK]; o += MLA_KV_RANK
    kr = p[:, o:o + LANES]; o += LANES
    gqkv_ref[...] = p[:, o:o + 3 * GDN_W]; o += 3 * GDN_W
    gba_ref[...] = p[:, o:o + LANES]; o += LANES
    z_ref[...] = p[:, o:o + GDN_W]

    cqn = _rms(cq, qn_ref[...]).astype(BF16)
    tc = jnp.concatenate([tc_ref[...]] * MLA_HEADS, axis=1)
    ts = jnp.concatenate([ts_ref[...]] * MLA_HEADS, axis=1)
    q = _dot(cqn, wqa_ref[...]) * tc + _dot(cqn, wqb_ref[...]) * ts
    q_ref[...] = q.astype(BF16)

    ckvn = _rms(ckv, kvn_ref[...]).astype(BF16)
    k = _dot(ckvn, wk_ref[...]) + _dot((kr * tk_ref[...]).astype(BF16), e_ref[...])
    k_ref[...] = k.astype(BF16)
    vt_ref[0] = _dot_nt(wvt_ref[...], ckvn).astype(BF16)


def _token_proj(x2d, tabs, weights, tm, tab_blocks):
    n = x2d.shape[0]
    assert n % tm == 0
    row = lambda w: pl.BlockSpec((tm, w), lambda i: (i, 0))
    tab = pl.BlockSpec((tm, LANES), lambda i: (i % tab_blocks, 0))
    out_widths = (D_MODEL, QK_W, QK_W, None, 3 * GDN_W, LANES, GDN_W)
    out_dtypes = (F32, BF16, BF16, BF16, F32, F32, F32)
    vt_spec = pl.BlockSpec((1, V_W, tm), lambda i: (i, 0, 0))
    return pl.pallas_call(
        _token_proj_kernel,
        grid=(n // tm,),
        in_specs=[row(D_MODEL), tab, tab, tab] + [_const_spec(w.shape) for w in weights],
        out_specs=[vt_spec if w is None else row(w) for w in out_widths],
        out_shape=[jax.ShapeDtypeStruct((n // tm, V_W, tm) if w is None else (n, w), d)
                   for w, d in zip(out_widths, out_dtypes)],
        compiler_params=pltpu.CompilerParams(dimension_semantics=("arbitrary",),
                                             vmem_limit_bytes=VMEM_LIMIT),
        name="token_proj",
    )(x2d, *tabs, *weights)


def _mla_kernel(q_ref, k_ref, vt_ref, km_ref, vmt_ref, o_ref, st_ref, *, tq, tk):
    qi = pl.program_id(2)
    heads = range(2)
    hs = [slice(h * HEAD_PAD, (h + 1) * HEAD_PAD) for h in heads]
    vs = [slice(h * MLA_V, (h + 1) * MLA_V) for h in heads]
    q = [q_ref[:, s] for s in hs]
    colmax = lambda s: jnp.max(s, axis=0, keepdims=True)

    def with_ones(vt):
        return jnp.concatenate([vt, jnp.ones((8, vt.shape[1]), BF16)], axis=0)

    meta_valid = lax.broadcasted_iota(jnp.int32, (km_ref.shape[0], tq), 0) < N_META
    st = [jnp.where(meta_valid, _dot_nt(km_ref[:, hs[h]], q[h]), NEG_BIG) for h in heads]
    m = [colmax(s) for s in st]
    acc = [_dot(with_ones(vmt_ref[vs[h], :]), jnp.exp2(st[h] - m[h]).astype(BF16)) for h in heads]

    def scores(ki, slot):
        rows = pl.ds(pl.multiple_of(ki * tk, tk), tk)
        for h in heads:
            st_ref[slot, h] = _dot_nt(k_ref[rows, hs[h]], q[h])

    def update(ki, slot, m, acc, mask):
        st = [st_ref[slot, h] for h in heads]
        if mask is not None:
            st = [jnp.where(mask, s, NEG_BIG) for s in st]
        m_new = [jnp.maximum(m[h], colmax(st[h])) for h in heads]
        p = [jnp.exp2(st[h] - m_new[h]).astype(BF16) for h in heads]
        vt = vt_ref[ki]
        acc = [jnp.exp2(m[h] - m_new[h]) * acc[h] + _dot(with_ones(vt[vs[h]]), p[h]) for h in heads]
        return m_new, acc

    def body(i, carry):
        ki = 2 * i
        scores(ki + 1, 1)
        carry = update(ki, 0, *carry, None)
        scores(ki + 2, 0)
        return update(ki + 1, 1, *carry, None)

    scores(0, 0)
    m, acc = lax.fori_loop(0, qi, body, (m, acc))
    n_full = 2 * qi
    qk_diff = (lax.broadcasted_iota(jnp.int32, (tk, tq), 1) - lax.broadcasted_iota(jnp.int32, (tk, tq), 0))
    scores(n_full + 1, 1)
    m, acc = update(n_full, 0, m, acc, qk_diff >= 0)
    m, acc = update(n_full + 1, 1, m, acc, qk_diff >= tk)
    o_ref[...] = jnp.concatenate([(a[:MLA_V] / a[MLA_V:MLA_V + 1]).T for a in acc], axis=1).astype(BF16)


def _mla_attn(q, k, vt, k_meta, vt_meta, batch, seq, tq, tk):
    nq = seq // tq
    assert vt.shape[2] == tk and tq == 2 * tk
    kern = functools.partial(_mla_kernel, tq=tq, tk=tk)
    return pl.pallas_call(
        kern,
        grid=(batch, MLA_HEADS // 2, nq),
        in_specs=[
            pl.BlockSpec((tq, 2 * HEAD_PAD), lambda b, hp, i: (b * nq + i, hp)),
            pl.BlockSpec((seq, 2 * HEAD_PAD), lambda b, hp, i: (b, hp)),
            pl.BlockSpec((seq // tk, 2 * MLA_V, tk), lambda b, hp, i: (b, hp, 0)),
            pl.BlockSpec((k_meta.shape[0], 2 * HEAD_PAD), lambda b, hp, i: (0, hp)),
            pl.BlockSpec((2 * MLA_V, vt_meta.shape[1]), lambda b, hp, i: (hp, 0)),
        ],
        out_specs=pl.BlockSpec((tq, 2 * MLA_V), lambda b, hp, i: (b * nq + i, hp)),
        out_shape=jax.ShapeDtypeStruct((batch * seq, V_W), BF16),
        scratch_shapes=[pltpu.VMEM((2, 2, tk, tq), F32)],
        compiler_params=pltpu.CompilerParams(
            dimension_semantics=("arbitrary", "arbitrary", "arbitrary"),
            vmem_limit_bytes=VMEM_LIMIT),
        name="mla_attn",
    )(q, k, vt, k_meta, vt_meta)


LEVELS = (1, 2, 4, 8, 16, 32)
GDN_CONST_NAMES = ("bd_ones", "expand_b", "expand_a", "ltri", "eye_t", "tril_t", "stril_t",
                   "level_masks", "bd_mask", "head_masks")


def _gdn_constants(cps):
    i = np.arange(CHUNK)[:, None]
    lane = np.arange(GDN_W)[None, :]
    j = lane % GDN_DK
    c = {}
    c["bd_ones"] = (np.arange(GDN_W)[:, None] // GDN_DK == lane // GDN_DK)
    r = np.arange(LANES)[:, None]
    c["expand_b"] = (r == lane // GDN_DK)
    c["expand_a"] = (r == GDN_HEADS + lane // GDN_DK)
    t = np.arange(cps * CHUNK)
    c["ltri"] = (t[:, None] >= t[None, :]) & (t[:, None] // CHUNK == t[None, :] // CHUNK)
    c["eye_t"] = (i == j)
    c["tril_t"] = (i >= j)
    c["stril_t"] = (i > j)
    jg = j[:, :GROUP_W]
    c["level_masks"] = np.stack([
        ((i // (2 * s) == jg // (2 * s)) & ((i // s) % 2 == 1) & ((jg // s) % 2 == 0))
        for s in LEVELS])
    g = np.arange(GROUP_W)
    c["bd_mask"] = (g[:, None] // GDN_DK == g[None, :] // GDN_DK)
    c["head_masks"] = np.stack([(g[None, :] // GDN_DK == h) for h in range(GROUP_HEADS)])
    bf = ("bd_ones", "expand_b", "expand_a", "ltri", "bd_mask")
    return [jnp.asarray(c[k].astype(np.float32), BF16 if k in bf else F32) for k in GDN_CONST_NAMES]


def _block_diag(y, bd_mask):
    return jnp.concatenate([y] * GROUP_HEADS, axis=0) * bd_mask


def _head_matmul(x, y, bd_mask):
    return _dot(x.astype(BF16), _block_diag(y.astype(BF16), bd_mask))


def _diag_blocks(m, head_masks):
    reps = m.shape[1] // GROUP_W
    return sum(m[h * GDN_DK:(h + 1) * GDN_DK] * jnp.concatenate([head_masks[h]] * reps, axis=1)
               for h in range(GROUP_HEADS))


def _gdn_kernel(gx_ref, ba_ref, z_ref, s0_ref, c0_ref, convw_ref, arate_ref, dtb_ref, gnorm_ref,
                bd_ones_ref, expand_b_ref, expand_a_ref, ltri_ref, eye_t_ref, tril_t_ref, stril_t_ref,
                level_masks_ref, bd_mask_ref, head_masks_ref,
                o_ref, sout_ref, state_ref, carry_ref, *, cps, front_pad):
    c = pl.program_id(1)
    rows = cps * CHUNK

    @pl.when(c == 0)
    def _():
        state_ref[...] = s0_ref[...]
        carry_ref[...] = c0_ref[...]

    raw = gx_ref[...]
    xe = jnp.concatenate([carry_ref[...], raw], axis=0)
    w = convw_ref[...]
    y = sum(w[t:t + 1] * xe[8 - (CONV_K - 1) + t:8 - (CONV_K - 1) + t + rows] for t in range(CONV_K))
    carry_ref[...] = raw[rows - 8:]
    qkv = _silu(y)
    q, k, v = qkv[:, :GDN_W], qkv[:, GDN_W:2 * GDN_W], qkv[:, 2 * GDN_W:]

    bd_ones = bd_ones_ref[...]
    ss = _dot_exact_rhs(jnp.concatenate([q * q, k * k], axis=0), bd_ones, 2)
    qn = q * lax.rsqrt(ss[:rows] + EPS) * (GDN_DK ** -0.5)
    kn = k * lax.rsqrt(ss[rows:] + EPS)

    ba = ba_ref[...]
    beta = jax.nn.sigmoid(ba)
    sp_in = ba + dtb_ref[...]
    g = arate_ref[...] * (jnp.maximum(sp_in, 0.0) + jnp.log1p(jnp.exp(-jnp.abs(sp_in))))
    if front_pad:
        valid = lax.broadcasted_iota(jnp.int32, (rows, 1), 0) >= front_pad
        beta = jnp.where(valid, beta, 0.0)
        g = jnp.where(valid, g, 0.0)
    beta_e = _dot_exact_rhs(beta, expand_b_ref[...], 2)
    gc = _dot_exact_lhs(ltri_ref[...], g, 3)
    gc_e = _dot_exact_rhs(gc, expand_a_ref[...], 3)
    egc = jnp.exp(gc_e)
    kb = kn * beta_e
    vb = v * beta_e
    kbg = kb * egc
    qg = qn * egc
    eye, tril, stril = eye_t_ref[...], tril_t_ref[...], stril_t_ref[...]
    bd_mask = bd_mask_ref[...]
    head_masks = [head_masks_ref[h] for h in range(GROUP_HEADS)]
    head_masks_bf = [m.astype(BF16) for m in head_masks]

    chains = [(j, gi) for j in range(cps) for gi in range(N_GROUPS)]
    rsl = lambda j: slice(j * CHUNK, (j + 1) * CHUNK)
    gsl = lambda gi: slice(gi * GROUP_W, (gi + 1) * GROUP_W)
    decay, kdec, s_decay = [], [], []
    for j in range(cps):
        gce = gc_e[rsl(j)]
        gc_t = jnp.sum(gce * eye, axis=0, keepdims=True)
        decay.append(jnp.exp(jnp.where(tril > 0.5, gce - gc_t, -jnp.inf)))
        g_last = gce[CHUNK - 1:CHUNK]
        kdec.append(kn[rsl(j)] * jnp.exp(g_last - gce))
        s_decay.append(jnp.exp(g_last))

    lmat, amat = [], []
    for j, gi in chains:
        rs, gs = rsl(j), gsl(gi)
        k4 = kn[rs, gs].astype(BF16)
        kstack = jnp.concatenate([k4 * m for m in head_masks_bf], axis=0)
        lhs = jnp.concatenate([kb[rs, gs], qn[rs, gs]], axis=0).astype(BF16)
        sc = _dot_nt(lhs, kstack)
        lmat.append(sc[:CHUNK] * decay[j][:, gs] * stril[:, gs])
        amat.append(sc[CHUNK:] * decay[j][:, gs])

    xinv = [eye[:, gsl(gi)] - lm * level_masks_ref[0] for (j, gi), lm in zip(chains, lmat)]
    for li in range(1, len(LEVELS)):
        xo = [_head_matmul(x, lm * level_masks_ref[li], bd_mask) for x, lm in zip(xinv, lmat)]
        xinv = [x - _head_matmul(y, x, bd_mask) for x, y in zip(xinv, xo)]

    uw = []
    for (j, gi), x in zip(chains, xinv):
        rs, gs = rsl(j), gsl(gi)
        rhs = jnp.concatenate([_block_diag(vb[rs, gs].astype(BF16), bd_mask),
                               _block_diag(kbg[rs, gs].astype(BF16), bd_mask)], axis=1)
        uw.append(_dot(x.astype(BF16), rhs))
    kwu = [_diag_blocks(_dot(kdec[j][:, gsl(gi)].T.astype(BF16), m.astype(BF16)), head_masks)
           for (j, gi), m in zip(chains, uw)]

    states = [state_ref[gi] for gi in range(N_GROUPS)]
    o_rows = [[None] * N_GROUPS for _ in range(cps)]
    for ci, (j, gi) in enumerate(chains):
        rs, gs = rsl(j), gsl(gi)
        u, wmat = uw[ci][:, :GROUP_W], uw[ci][:, GROUP_W:]
        ku, kw = kwu[ci][:, :GROUP_W], kwu[ci][:, GROUP_W:]
        state = states[gi]
        lhs = jnp.concatenate([kw, wmat, qg[rs, gs]], axis=0).astype(BF16)
        big = _dot(lhs, _block_diag(state.astype(BF16), bd_mask))
        states[gi] = s_decay[j][:, gs] * state - big[:CHUNK] + ku
        v_new = u - big[CHUNK:2 * CHUNK]
        o_rows[j][gi] = big[2 * CHUNK:] + _head_matmul(amat[ci], v_new, bd_mask)
    for gi in range(N_GROUPS):
        state_ref[gi] = states[gi]

    @pl.when(c == pl.num_programs(1) - 1)
    def _():
        sout_ref[...] = state_ref[...]

    o = jnp.concatenate([jnp.concatenate(r, axis=1) for r in o_rows], axis=0)
    ms = _dot_exact_rhs(o * o, bd_ones, 2) * (1.0 / GDN_DV)
    o = o * lax.rsqrt(ms + EPS) * gnorm_ref[...] * _silu(z_ref[...])
    o_ref[...] = o.astype(BF16)


def _gdn(gqkv, gba, z, state0, carry0, params, batch, seq, cps, front_pad):
    rows = cps * CHUNK
    steps = seq // rows
    consts = _gdn_constants(cps)
    xrow = lambda w: pl.BlockSpec((rows, w), lambda b, c: (b * steps + c, 0))
    state_shape = (N_GROUPS, GDN_DK, GROUP_W)
    return pl.pallas_call(
        functools.partial(_gdn_kernel, cps=cps, front_pad=front_pad),
        grid=(batch, steps),
        in_specs=[xrow(3 * GDN_W), xrow(LANES), xrow(GDN_W), _const_spec(state_shape),
                  _const_spec(carry0.shape)]
                 + [_const_spec(a.shape) for a in params]
                 + [_const_spec(a.shape) for a in consts],
        out_specs=[xrow(GDN_W), pl.BlockSpec(state_shape, lambda b, c: (0, 0, 0))],
        out_shape=[jax.ShapeDtypeStruct((batch * seq, GDN_W), BF16),
                   jax.ShapeDtypeStruct(state_shape, F32)],
        scratch_shapes=[pltpu.VMEM(state_shape, F32), pltpu.VMEM((8, 3 * GDN_W), F32)],
        compiler_params=pltpu.CompilerParams(dimension_semantics=("arbitrary", "arbitrary"),
                                             vmem_limit_bytes=VMEM_LIMIT),
        name="gdn_chunk",
    )(gqkv, gba, z, state0, carry0, *params, *consts)


def _merge_ffn_kernel(h1_ref, om_ref, og_ref, nm_ref, wing_ref, wmo_ref, wgo_ref, wout_ref,
                      n2_ref, wgu_ref, wd_ref, nf_ref, out_ref):
    h1 = h1_ref[...]
    un = _rms(h1, nm_ref[...]).astype(BF16)
    gates = jax.nn.sigmoid(_dot(un, wing_ref[...]))
    merged = (gates[:, :D_MODEL] * _dot(om_ref[...], wmo_ref[...])
              + gates[:, D_MODEL:] * _dot(og_ref[...], wgo_ref[...]))
    h2 = h1 + _dot(merged.astype(BF16), wout_ref[...])
    xn = _rms(h2, n2_ref[...]).astype(BF16)
    gu = _dot(xn, wgu_ref[...])
    act = (_silu(gu[:, :D_FF]) * gu[:, D_FF:]).astype(BF16)
    h3 = h2 + 0.5 * _dot(act, wd_ref[...])
    out_ref[...] = _rms(h3, nf_ref[...])


def _merge_ffn(h1, o_mla, o_gdn, weights, tm):
    n = h1.shape[0]
    row = lambda w: pl.BlockSpec((tm, w), lambda i: (i, 0))
    return pl.pallas_call(
        _merge_ffn_kernel,
        grid=(n // tm,),
        in_specs=[row(D_MODEL), row(V_W), row(GDN_W)] + [_const_spec(w.shape) for w in weights],
        out_specs=row(D_MODEL),
        out_shape=jax.ShapeDtypeStruct((n, D_MODEL), F32),
        compiler_params=pltpu.CompilerParams(dimension_semantics=("arbitrary",),
                                             vmem_limit_bytes=VMEM_LIMIT),
        name="merge_ffn",
    )(h1, o_mla, o_gdn, *weights)


def _rope_tables(pos):
    inv = ROPE_THETA ** (-jnp.arange(0, MLA_ROPE, 2, dtype=F32) / MLA_ROPE)
    ang = pos.astype(F32)[:, None] * inv[None, :]
    cos, sin = jnp.cos(ang), jnp.sin(ang)
    cos2 = jnp.concatenate([cos, cos], axis=1)
    sin2 = jnp.concatenate([sin, sin], axis=1)
    n = pos.shape[0]
    scale = (MLA_NOPE + MLA_ROPE) ** -0.5 * math.log2(math.e)
    pad = jnp.zeros((n, HEAD_PAD - MLA_NOPE - MLA_ROPE), F32)
    tab_c = jnp.concatenate([jnp.ones((n, MLA_NOPE), F32), cos2, pad], axis=1) * scale
    tab_s = jnp.concatenate([jnp.zeros((n, MLA_NOPE), F32), sin2, pad], axis=1) * scale
    tab_k = jnp.concatenate([cos2, sin2, jnp.zeros((n, LANES - 2 * MLA_ROPE), F32)], axis=1)
    return tab_c, tab_s, tab_k


def _rot(w):
    half = MLA_ROPE // 2
    return jnp.concatenate([-w[..., half:], w[..., :half]], axis=-1)


def kernel(x, meta_tokens, ffn1_norm, ffn1_w_gate, ffn1_w_up, ffn1_w_down, mix_norm, w_in, q_norm, w_uq,
           kv_norm, w_ukv, w_mla_o, conv_w, a_log, dt_bias, gdn_norm, w_gdn_o, w_out, ffn2_norm,
           ffn2_w_gate, ffn2_w_up, ffn2_w_down, final_norm):
    assert ffn1_norm.shape[0] == 1, "single-layer block"
    batch, seq, d = x.shape
    assert d == D_MODEL and seq % CHUNK == 0
    tm = min(256, seq)
    tq = min(512, seq)
    assert seq % tm == 0 and seq % tq == 0

    wi = w_in[0]
    sizes = (MLA_Q_RANK, MLA_KV_RANK, MLA_ROPE, GDN_W, GDN_W, GDN_W, GDN_HEADS, GDN_HEADS, GDN_W,
             D_MODEL, D_MODEL)
    offs = np.concatenate([[0], np.cumsum(sizes)])
    col = lambda a, b: wi[:, int(offs[a]):int(offs[b])]
    zcols = lambda n: jnp.zeros((D_MODEL, n), F32)
    w_kr = col(2, 3)
    win_a = jnp.concatenate([col(0, 2), w_kr, _rot(w_kr), zcols(LANES - 2 * MLA_ROPE), col(3, 6),
                             col(6, 8), zcols(LANES - 2 * GDN_HEADS), col(8, 9)], axis=1).astype(BF16)
    assert win_a.shape[1] == PROJ_W
    win_g = col(9, 11).astype(BF16)

    wq = w_uq[0].reshape(MLA_Q_RANK, MLA_HEADS, MLA_NOPE + MLA_ROPE)
    wq_nope, wq_rope = wq[..., :MLA_NOPE], wq[..., MLA_NOPE:]
    zq = lambda n: jnp.zeros((MLA_Q_RANK, MLA_HEADS, n), F32)
    tail = HEAD_PAD - MLA_NOPE - MLA_ROPE
    wqa = jnp.concatenate([wq_nope, wq_rope, zq(tail)], axis=-1).reshape(MLA_Q_RANK, QK_W).astype(BF16)
    wqb = jnp.concatenate([zq(MLA_NOPE), _rot(wq_rope), zq(tail)], axis=-1).reshape(MLA_Q_RANK, QK_W).astype(BF16)
    wkv = w_ukv[0].reshape(MLA_KV_RANK, MLA_HEADS, MLA_NOPE + MLA_V)
    wk = jnp.concatenate([wkv[..., :MLA_NOPE], jnp.zeros((MLA_KV_RANK, MLA_HEADS, HEAD_PAD - MLA_NOPE), F32)],
                         axis=-1).reshape(MLA_KV_RANK, QK_W).astype(BF16)
    wvt = wkv[..., MLA_NOPE:].reshape(MLA_KV_RANK, V_W).T.astype(BF16)
    e_np = np.zeros((LANES, QK_W), np.float32)
    for h in range(MLA_HEADS):
        for j in range(MLA_ROPE):
            e_np[j, h * HEAD_PAD + MLA_NOPE + j] = 1.0
            e_np[MLA_ROPE + j, h * HEAD_PAD + MLA_NOPE + j] = 1.0
    e_mat = jnp.asarray(e_np, BF16)

    proj_weights = [
        ffn1_norm[0][None], jnp.concatenate([ffn1_w_gate[0], ffn1_w_up[0]], axis=1).astype(BF16),
        ffn1_w_down[0].astype(BF16), mix_norm[0][None], win_a, q_norm[0][None], wqa, wqb,
        kv_norm[0][None], wk, wvt, e_mat]

    tabs_x = _rope_tables(N_META + jnp.arange(seq))
    h1, q, k, vt, gqkv, gba, z = _token_proj(x.reshape(batch * seq, d), tabs_x, proj_weights, tm, seq // tm)
    tabs_m = _rope_tables(jnp.arange(N_META))
    _, _, k_m, vt_m, gqkv_m, gba_m, _ = _token_proj(meta_tokens.astype(F32), tabs_m, proj_weights, N_META, 1)

    pad_rows = lambda a, n, front: jnp.pad(a, ((n - a.shape[0], 0) if front else (0, n - a.shape[0]), (0, 0)))
    vt_meta = jnp.pad(vt_m[0], ((0, 0), (0, LANES - N_META)))
    o_mla = _mla_attn(q, k, vt, pad_rows(k_m, LANES, False), vt_meta, batch, seq, tq, tm)

    hpad = lambda a: jnp.zeros((1, LANES), F32).at[0, GDN_HEADS:2 * GDN_HEADS].set(a)
    arate = hpad(-jnp.exp(a_log[0].astype(F32)))
    dtb = hpad(dt_bias[0].astype(F32))
    gnorm = jnp.tile(gdn_norm[0].astype(F32), GDN_HEADS)[None]
    gdn_params = (conv_w[0].astype(F32), arate, dtb, gnorm)
    cps = 4 if seq % (4 * CHUNK) == 0 else 1
    gqkv_mp = pad_rows(gqkv_m, CHUNK, True)
    _, state_meta = _gdn(gqkv_mp, pad_rows(gba_m, CHUNK, True), jnp.zeros((CHUNK, GDN_W), F32),
                         jnp.zeros((N_GROUPS, GDN_DK, GROUP_W), F32), jnp.zeros((8, 3 * GDN_W), F32),
                         gdn_params, 1, CHUNK, 1, CHUNK - N_META)
    o_gdn, _ = _gdn(gqkv, gba, z, state_meta, gqkv_mp[CHUNK - 8:], gdn_params, batch, seq, cps, 0)

    merge_weights = [
        mix_norm[0][None], win_g, w_mla_o[0].astype(BF16), w_gdn_o[0].astype(BF16), w_out[0].astype(BF16),
        ffn2_norm[0][None], jnp.concatenate([ffn2_w_gate[0], ffn2_w_up[0]], axis=1).astype(BF16),
        ffn2_w_down[0].astype(BF16), final_norm[None]]
    out = _merge_ffn(h1, o_mla, o_gdn, merge_weights, tm)
    return out.reshape(batch, seq, d)
```

```python
import functools
import math

import jax
import jax.numpy as jnp
import numpy as np
from jax import lax
from jax.experimental import pallas as pl
from jax.experimental.pallas import tpu as pltpu

F32 = jnp.float32
BF16 = jnp.bfloat16

D_MODEL = 1024
N_META = 16
EPS = 1e-6
D_FF = 2816
MLA_HEADS = 8
MLA_Q_RANK = 256
MLA_KV_RANK = 128
MLA_NOPE = 64
MLA_ROPE = 32
MLA_V = 64
ROPE_THETA = 10000.0
GDN_HEADS = 8
GDN_DK = 64
GDN_DV = 64
CONV_K = 4
CHUNK = 64

LANES = 128
HEAD_PAD = 128
QK_W = MLA_HEADS * HEAD_PAD
V_W = MLA_HEADS * MLA_V
GDN_W = GDN_HEADS * GDN_DK
GROUP_HEADS = 4
GROUP_W = GROUP_HEADS * GDN_DK
N_GROUPS = GDN_HEADS // GROUP_HEADS
PROJ_W = MLA_Q_RANK + MLA_KV_RANK + LANES + 3 * GDN_W + LANES + GDN_W
VMEM_LIMIT = 56 * 1024 * 1024
NEG_BIG = -1e30


def _const_spec(shape):
    zeros = (0,) * len(shape)
    return pl.BlockSpec(shape, lambda *_: zeros, pipeline_mode=pl.Buffered(1))


def _rms(x, w):
    return x * lax.rsqrt(jnp.mean(x * x, axis=-1, keepdims=True) + EPS) * w


def _dot(a, b):
    return jnp.dot(a, b, preferred_element_type=F32)


def _dot_nt(a, b):
    return lax.dot_general(a, b, (((1,), (1,)), ((), ())), preferred_element_type=F32)


def _silu(x):
    return x * jax.nn.sigmoid(x)


def _split(x, n):
    pieces = []
    for _ in range(n - 1):
        hi = x.astype(BF16)
        pieces.append(hi)
        x = x - hi.astype(F32)
    pieces.append(x.astype(BF16))
    return pieces


def _dot_exact_rhs(x, rhs, n):
    return sum(_dot(p, rhs) for p in _split(x, n))


def _dot_exact_lhs(lhs, x, n):
    return sum(_dot(lhs, p) for p in _split(x, n))


def _head_sums(x, bd_ones):
    return jnp.concatenate([_dot(x[:, g * GROUP_W:(g + 1) * GROUP_W].astype(BF16), bd_ones)
                            for g in range(N_GROUPS)], axis=1)


def _token_proj_kernel(x_ref, tc_ref, ts_ref, tk_ref, c0_ref, n1_ref, wg_ref, wu_ref, wd_ref, nm_ref, win_ref,
                       qn_ref, wqa_ref, wqb_ref, kvn_ref, wk_ref, wvt_ref, e_ref, convw_ref, bd_ones_ref,
                       h1_ref, q_ref, k_ref, vt_ref, gqkv_ref, gba_ref, z_ref, tail_ref, xe_ref,
                       *, tiles_per_seq):
    tm = x_ref.shape[0]
    step = pl.program_id(0)

    @pl.when(step == 0)
    def _():
        xe_ref[...] = jnp.zeros_like(xe_ref)

    x = x_ref[...]
    xn = _rms(x, n1_ref[...]).astype(BF16)
    gate = _dot(xn, wg_ref[...])

    w = convw_ref[...]
    y = sum(w[t:t + 1] * xe_ref[pl.ds(8 - (CONV_K - 1) + t, tm), :] for t in range(CONV_K))
    qkv = _silu(y)
    gq, gk, gv = qkv[:, :GDN_W], qkv[:, GDN_W:2 * GDN_W], qkv[:, 2 * GDN_W:]
    gqkv_ref[:, 2 * GDN_W:] = gv

    up = _dot(xn, wu_ref[...])
    bd_ones = bd_ones_ref[...]
    ssq = _head_sums(gq * gq, bd_ones)
    ssk = _head_sums(gk * gk, bd_ones)
    act = (_silu(gate) * up).astype(BF16)
    down = _dot(act, wd_ref[...])
    gqkv_ref[:, :GDN_W] = gq * lax.rsqrt(ssq + EPS) * (GDN_DK ** -0.5)
    gqkv_ref[:, GDN_W:2 * GDN_W] = gk * lax.rsqrt(ssk + EPS)
    h1 = x + 0.5 * down
    h1_ref[...] = h1

    un = _rms(h1, nm_ref[...]).astype(BF16)
    p = _dot(un, win_ref[...])
    o = 0
    cq = p[:, o:o + MLA_Q_RANK]; o += MLA_Q_RANK
    ckv = p[:, o:o + MLA_KV_RANK]; o += MLA_KV_RANK
    kr = p[:, o:o + LANES]; o += LANES
    gqkv_raw = p[:, o:o + 3 * GDN_W]; o += 3 * GDN_W
    gba_ref[...] = p[:, o:o + LANES]; o += LANES
    z_ref[...] = p[:, o:o + GDN_W]

    xe_ref[0:8] = jnp.where(step % tiles_per_seq == 0, c0_ref[...], xe_ref[tm:tm + 8])
    xe_ref[8:8 + tm] = gqkv_raw
    tail_ref[...] = gqkv_raw[tm - 8:]

    cqn = _rms(cq, qn_ref[...]).astype(BF16)
    tc = jnp.concatenate([tc_ref[...]] * MLA_HEADS, axis=1)
    ts = jnp.concatenate([ts_ref[...]] * MLA_HEADS, axis=1)
    q = _dot(cqn, wqa_ref[...]) * tc + _dot(cqn, wqb_ref[...]) * ts
    q_ref[...] = q.astype(BF16)

    ckvn = _rms(ckv, kvn_ref[...]).astype(BF16)
    k = _dot(ckvn, wk_ref[...]) + _dot((kr * tk_ref[...]).astype(BF16), e_ref[...])
    k_ref[...] = k.astype(BF16)
    vt_ref[0] = _dot_nt(wvt_ref[...], ckvn).astype(BF16)


def _token_proj(x2d, tabs, conv_carry, weights, tm, tiles_per_seq):
    n = x2d.shape[0]
    assert n % tm == 0
    last = n // tm - 1
    tile = lambda i: jnp.minimum(i, last)
    row = lambda w: pl.BlockSpec((tm, w), lambda i: (tile(i), 0))
    tab = pl.BlockSpec((tm, LANES), lambda i: (tile(i) % tiles_per_seq, 0))
    out_widths = (D_MODEL, QK_W, QK_W, None, 3 * GDN_W, LANES, GDN_W)
    out_dtypes = (F32, BF16, BF16, BF16, F32, F32, F32)
    vt_spec = pl.BlockSpec((1, V_W, tm), lambda i: (tile(i), 0, 0))
    conv_spec = pl.BlockSpec((tm, 3 * GDN_W), lambda i: (jnp.maximum(i - 1, 0), 0))
    tail_shape = (8, 3 * GDN_W)
    out_specs = [vt_spec if w is None else row(w) for w in out_widths]
    out_specs[4] = conv_spec
    return pl.pallas_call(
        functools.partial(_token_proj_kernel, tiles_per_seq=tiles_per_seq),
        grid=(n // tm + 1,),
        in_specs=[row(D_MODEL), tab, tab, tab, _const_spec(tail_shape)] + [_const_spec(w.shape) for w in weights],
        out_specs=out_specs + [pl.BlockSpec(tail_shape, lambda i: (0, 0))],
        out_shape=[jax.ShapeDtypeStruct((n // tm, V_W, tm) if w is None else (n, w), d)
                   for w, d in zip(out_widths, out_dtypes)] + [jax.ShapeDtypeStruct(tail_shape, F32)],
        scratch_shapes=[pltpu.VMEM((8 + tm, 3 * GDN_W), F32)],
        compiler_params=pltpu.CompilerParams(dimension_semantics=("arbitrary",),
                                             vmem_limit_bytes=VMEM_LIMIT),
        name="token_proj",
    )(x2d, *tabs, conv_carry, *weights)


def _mla_kernel(q_ref, k_ref, vt_ref, km_ref, vmt_ref, o_ref, st_ref, *, tq, tk):
    qi = pl.program_id(2)
    heads = range(2)
    hs = [slice(h * HEAD_PAD, (h + 1) * HEAD_PAD) for h in heads]
    vs = [slice(h * MLA_V, (h + 1) * MLA_V) for h in heads]
    q = [q_ref[:, s] for s in hs]
    colmax = lambda s: jnp.max(s, axis=0, keepdims=True)

    def with_ones(vt):
        return jnp.concatenate([vt, jnp.ones((8, vt.shape[1]), BF16)], axis=0)

    meta_valid = lax.broadcasted_iota(jnp.int32, (km_ref.shape[0], tq), 0) < N_META
    st = [jnp.where(meta_valid, _dot_nt(km_ref[:, hs[h]], q[h]), NEG_BIG) for h in heads]
    m = [colmax(s) for s in st]
    acc = [_dot(with_ones(vmt_ref[vs[h], :]), jnp.exp2(st[h] - m[h]).astype(BF16)) for h in heads]

    def scores(ki, slot):
        rows = pl.ds(pl.multiple_of(ki * tk, tk), tk)
        for h in heads:
            st_ref[slot, h] = _dot_nt(k_ref[rows, hs[h]], q[h])

    def update(ki, slot, m, acc, mask):
        st = [st_ref[slot, h] for h in heads]
        if mask is not None:
            st = [jnp.where(mask, s, NEG_BIG) for s in st]
        m_new = [jnp.maximum(m[h], colmax(st[h])) for h in heads]
        p = [jnp.exp2(st[h] - m_new[h]).astype(BF16) for h in heads]
        vt = vt_ref[ki]
        acc = [jnp.exp2(m[h] - m_new[h]) * acc[h] + _dot(with_ones(vt[vs[h]]), p[h]) for h in heads]
        return m_new, acc

    def body(i, carry):
        ki = 2 * i
        scores(ki + 1, 1)
        carry = update(ki, 0, *carry, None)
        scores(ki + 2, 0)
        return update(ki + 1, 1, *carry, None)

    scores(0, 0)
    m, acc = lax.fori_loop(0, qi, body, (m, acc))
    n_full = 2 * qi
    qk_diff = (lax.broadcasted_iota(jnp.int32, (tk, tq), 1) - lax.broadcasted_iota(jnp.int32, (tk, tq), 0))
    scores(n_full + 1, 1)
    m, acc = update(n_full, 0, m, acc, qk_diff >= 0)
    m, acc = update(n_full + 1, 1, m, acc, qk_diff >= tk)
    o_ref[...] = jnp.concatenate([(a[:MLA_V] / a[MLA_V:MLA_V + 1]).T for a in acc], axis=1).astype(BF16)


def _mla_attn(q, k, vt, k_meta, vt_meta, batch, seq, tq, tk):
    nq = seq // tq
    assert vt.shape[2] == tk and tq == 2 * tk
    kern = functools.partial(_mla_kernel, tq=tq, tk=tk)
    return pl.pallas_call(
        kern,
        grid=(batch, MLA_HEADS // 2, nq),
        in_specs=[
            pl.BlockSpec((tq, 2 * HEAD_PAD), lambda b, hp, i: (b * nq + i, hp)),
            pl.BlockSpec((seq, 2 * HEAD_PAD), lambda b, hp, i: (b, hp)),
            pl.BlockSpec((seq // tk, 2 * MLA_V, tk), lambda b, hp, i: (b, hp, 0)),
            pl.BlockSpec((k_meta.shape[0], 2 * HEAD_PAD), lambda b, hp, i: (0, hp)),
            pl.BlockSpec((2 * MLA_V, vt_meta.shape[1]), lambda b, hp, i: (hp, 0)),
        ],
        out_specs=pl.BlockSpec((tq, 2 * MLA_V), lambda b, hp, i: (b * nq + i, hp)),
        out_shape=jax.ShapeDtypeStruct((batch * seq, V_W), BF16),
        scratch_shapes=[pltpu.VMEM((2, 2, tk, tq), F32)],
        compiler_params=pltpu.CompilerParams(
            dimension_semantics=("arbitrary", "arbitrary", "arbitrary"),
            vmem_limit_bytes=VMEM_LIMIT),
        name="mla_attn",
    )(q, k, vt, k_meta, vt_meta)


LEVELS = (1, 2, 4, 8, 16, 32)
GDN_CONST_NAMES = ("expand_b", "expand_a", "ltri", "eye_t", "tril_t", "stril_t", "level_masks", "bd_ones")


def _gdn_constants(cps):
    i = np.arange(CHUNK)[:, None]
    lane = np.arange(GDN_W)[None, :]
    j = lane % GDN_DK
    c = {}
    r = np.arange(LANES)[:, None]
    c["expand_b"] = (r == lane // GDN_DK)
    c["expand_a"] = (r == GDN_HEADS + lane // GDN_DK)
    t = np.arange(cps * CHUNK)
    c["ltri"] = (t[:, None] >= t[None, :]) & (t[:, None] // CHUNK == t[None, :] // CHUNK)
    c["eye_t"] = (i == j)
    c["tril_t"] = (i >= j)
    c["stril_t"] = (i > j)
    jg = j[:, :GROUP_W]
    c["level_masks"] = np.stack([
        ((i // (2 * s) == jg // (2 * s)) & ((i // s) % 2 == 1) & ((jg // s) % 2 == 0))
        for s in LEVELS])
    g = np.arange(GROUP_W)
    c["bd_ones"] = (g[:, None] // GDN_DK == g[None, :] // GDN_DK)
    bf = ("bd_ones", "expand_b", "expand_a", "ltri")
    return [jnp.asarray(c[k].astype(np.float32), BF16 if k in bf else F32) for k in GDN_CONST_NAMES]


def _block_diag(y, lo_half):
    zeros = jnp.zeros((GDN_DK, LANES), y.dtype)
    blocks = []
    for h in range(GROUP_HEADS):
        t = h // 2
        tile = y[:, t * LANES:(t + 1) * LANES]
        piece = jnp.where(lo_half, tile, 0) if h % 2 == 0 else jnp.where(lo_half, 0, tile)
        blocks.append(jnp.concatenate([piece, zeros] if t == 0 else [zeros, piece], axis=1))
    return jnp.concatenate(blocks, axis=0)


def _head_matmul(x, y, lo_half):
    return _dot(x.astype(BF16), _block_diag(y.astype(BF16), lo_half))


def _diag_blocks(m, lo_half):
    tiles = []
    for t in range(m.shape[1] // LANES):
        h = 2 * (t % 2)
        cols = slice(t * LANES, (t + 1) * LANES)
        tiles.append(jnp.where(lo_half, m[h * GDN_DK:(h + 1) * GDN_DK, cols],
                               m[(h + 1) * GDN_DK:(h + 2) * GDN_DK, cols]))
    return jnp.concatenate(tiles, axis=1)


def _gdn_kernel(gx_ref, ba_ref, z_ref, s0_ref, arate_ref, dtb_ref, gnorm_ref,
                expand_b_ref, expand_a_ref, ltri_ref, eye_t_ref, tril_t_ref, stril_t_ref,
                level_masks_ref, bd_ones_ref,
                o_ref, sout_ref, state_ref, *, cps, front_pad):
    c = pl.program_id(1)
    rows = cps * CHUNK

    @pl.when(c == 0)
    def _():
        state_ref[...] = s0_ref[...]

    qn = gx_ref[:, :GDN_W]
    kn = gx_ref[:, GDN_W:2 * GDN_W]
    v = gx_ref[:, 2 * GDN_W:]

    ba = ba_ref[...]
    beta = jax.nn.sigmoid(ba)
    sp_in = ba + dtb_ref[...]
    g = arate_ref[...] * (jnp.maximum(sp_in, 0.0) + jnp.log1p(jnp.exp(-jnp.abs(sp_in))))
    if front_pad:
        valid = lax.broadcasted_iota(jnp.int32, (rows, 1), 0) >= front_pad
        beta = jnp.where(valid, beta, 0.0)
        g = jnp.where(valid, g, 0.0)
    beta_e = _dot_exact_rhs(beta, expand_b_ref[...], 2)
    gc = _dot_exact_lhs(ltri_ref[...], g, 3)
    gc_e = _dot_exact_rhs(gc, expand_a_ref[...], 3)
    egc = jnp.exp(gc_e)
    kb = kn * beta_e
    vb = v * beta_e
    kbg = kb * egc
    qg = qn * egc
    eye, tril, stril = eye_t_ref[...], tril_t_ref[...], stril_t_ref[...]
    lo_half = lax.broadcasted_iota(jnp.int32, (1, LANES), 1) < GDN_DK

    chains = [(j, gi) for j in range(cps) for gi in range(N_GROUPS)]
    rsl = lambda j: slice(j * CHUNK, (j + 1) * CHUNK)
    gsl = lambda gi: slice(gi * GROUP_W, (gi + 1) * GROUP_W)
    decay, kdec, s_decay = [], [], []
    for j in range(cps):
        gce = gc_e[rsl(j)]
        gc_t = jnp.sum(gce * eye, axis=0, keepdims=True)
        decay.append(jnp.exp(jnp.where(tril > 0.5, gce - gc_t, -jnp.inf)))
        g_last = gce[CHUNK - 1:CHUNK]
        kdec.append(kn[rsl(j)] * jnp.exp(g_last - gce))
        s_decay.append(jnp.exp(g_last))

    lmat, amat = [], []
    for j, gi in chains:
        rs, gs = rsl(j), gsl(gi)
        k4 = kn[rs, gs].astype(BF16)
        kstack = _block_diag(k4, lo_half)
        lhs = jnp.concatenate([kb[rs, gs], qn[rs, gs]], axis=0).astype(BF16)
        sc = _dot_nt(lhs, kstack)
        lmat.append(sc[:CHUNK] * decay[j][:, gs] * stril[:, gs])
        amat.append(sc[CHUNK:] * decay[j][:, gs])

    xinv = [eye[:, gsl(gi)] - lm * level_masks_ref[0] for (j, gi), lm in zip(chains, lmat)]
    for li in range(1, len(LEVELS)):
        xo = [_head_matmul(x, lm * level_masks_ref[li], lo_half) for x, lm in zip(xinv, lmat)]
        xinv = [x - _head_matmul(y, x, lo_half) for x, y in zip(xinv, xo)]

    uw = []
    for (j, gi), x in zip(chains, xinv):
        rs, gs = rsl(j), gsl(gi)
        rhs = jnp.concatenate([_block_diag(vb[rs, gs].astype(BF16), lo_half),
                               _block_diag(kbg[rs, gs].astype(BF16), lo_half)], axis=1)
        uw.append(_dot(x.astype(BF16), rhs))
    kwu = [_diag_blocks(_dot(kdec[j][:, gsl(gi)].T.astype(BF16), m.astype(BF16)), lo_half)
           for (j, gi), m in zip(chains, uw)]

    states = [state_ref[gi] for gi in range(N_GROUPS)]
    o_rows = [[None] * N_GROUPS for _ in range(cps)]
    for ci, (j, gi) in enumerate(chains):
        rs, gs = rsl(j), gsl(gi)
        u, wmat = uw[ci][:, :GROUP_W], uw[ci][:, GROUP_W:]
        ku, kw = kwu[ci][:, :GROUP_W], kwu[ci][:, GROUP_W:]
        state = states[gi]
        lhs = jnp.concatenate([kw, wmat, qg[rs, gs]], axis=0).astype(BF16)
        big = _dot(lhs, _block_diag(state.astype(BF16), lo_half))
        states[gi] = s_decay[j][:, gs] * state - big[:CHUNK] + ku
        v_new = u - big[CHUNK:2 * CHUNK]
        o_rows[j][gi] = big[2 * CHUNK:] + _head_matmul(amat[ci], v_new, lo_half)
    for gi in range(N_GROUPS):
        state_ref[gi] = states[gi]

    @pl.when(c == pl.num_programs(1) - 1)
    def _():
        sout_ref[...] = state_ref[...]

    o = jnp.concatenate([jnp.concatenate(r, axis=1) for r in o_rows], axis=0)
    ms = _head_sums(o * o, bd_ones_ref[...]) * (1.0 / GDN_DV)
    o = o * lax.rsqrt(ms + EPS) * gnorm_ref[...] * _silu(z_ref[...])
    o_ref[...] = o.astype(BF16)


def _gdn(gqkv, gba, z, state0, params, batch, seq, cps, front_pad):
    rows = cps * CHUNK
    steps = seq // rows
    consts = _gdn_constants(cps)
    xrow = lambda w: pl.BlockSpec((rows, w), lambda b, c: (b * steps + c, 0))
    state_shape = (N_GROUPS, GDN_DK, GROUP_W)
    return pl.pallas_call(
        functools.partial(_gdn_kernel, cps=cps, front_pad=front_pad),
        grid=(batch, steps),
        in_specs=[xrow(3 * GDN_W), xrow(LANES), xrow(GDN_W), _const_spec(state_shape)]
                 + [_const_spec(a.shape) for a in params]
                 + [_const_spec(a.shape) for a in consts],
        out_specs=[xrow(GDN_W), pl.BlockSpec(state_shape, lambda b, c: (0, 0, 0))],
        out_shape=[jax.ShapeDtypeStruct((batch * seq, GDN_W), BF16),
                   jax.ShapeDtypeStruct(state_shape, F32)],
        scratch_shapes=[pltpu.VMEM(state_shape, F32)],
        compiler_params=pltpu.CompilerParams(dimension_semantics=("arbitrary", "arbitrary"),
                                             vmem_limit_bytes=VMEM_LIMIT),
        name="gdn_chunk",
    )(gqkv, gba, z, state0, *params, *consts)


def _merge_ffn_kernel(h1_ref, om_ref, og_ref, nm_ref, wing_ref, wmo_ref, wgo_ref, wout_ref,
                      n2_ref, wg_ref, wu_ref, wd_ref, nf_ref, out_ref):
    h1 = h1_ref[...]
    un = _rms(h1, nm_ref[...]).astype(BF16)
    gates = jax.nn.sigmoid(_dot(un, wing_ref[...]))
    merged = (gates[:, :D_MODEL] * _dot(om_ref[...], wmo_ref[...])
              + gates[:, D_MODEL:] * _dot(og_ref[...], wgo_ref[...]))
    h2 = h1 + _dot(merged.astype(BF16), wout_ref[...])
    xn = _rms(h2, n2_ref[...]).astype(BF16)
    act = (_silu(_dot(xn, wg_ref[...])) * _dot(xn, wu_ref[...])).astype(BF16)
    h3 = h2 + 0.5 * _dot(act, wd_ref[...])
    out_ref[...] = _rms(h3, nf_ref[...])


def _merge_ffn(h1, o_mla, o_gdn, weights, tm):
    n = h1.shape[0]
    row = lambda w: pl.BlockSpec((tm, w), lambda i: (i, 0))
    return pl.pallas_call(
        _merge_ffn_kernel,
        grid=(n // tm,),
        in_specs=[row(D_MODEL), row(V_W), row(GDN_W)] + [_const_spec(w.shape) for w in weights],
        out_specs=row(D_MODEL),
        out_shape=jax.ShapeDtypeStruct((n, D_MODEL), F32),
        compiler_params=pltpu.CompilerParams(dimension_semantics=("arbitrary",),
                                             vmem_limit_bytes=VMEM_LIMIT),
        name="merge_ffn",
    )(h1, o_mla, o_gdn, *weights)


def _rope_tables(first_pos, n):
    pos = np.arange(first_pos, first_pos + n, dtype=np.float64)
    inv = ROPE_THETA ** (-np.arange(0, MLA_ROPE, 2, dtype=np.float64) / MLA_ROPE)
    ang = pos[:, None] * inv[None, :]
    cos2 = np.tile(np.cos(ang), (1, 2))
    sin2 = np.tile(np.sin(ang), (1, 2))
    scale = (MLA_NOPE + MLA_ROPE) ** -0.5 * math.log2(math.e)
    pad = np.zeros((n, HEAD_PAD - MLA_NOPE - MLA_ROPE))
    tab_c = np.concatenate([np.ones((n, MLA_NOPE)), cos2, pad], axis=1) * scale
    tab_s = np.concatenate([np.zeros((n, MLA_NOPE)), sin2, pad], axis=1) * scale
    tab_k = np.concatenate([cos2, sin2, np.zeros((n, LANES - 2 * MLA_ROPE))], axis=1)
    return tuple(jnp.asarray(t, F32) for t in (tab_c, tab_s, tab_k))


def _rot(w):
    half = MLA_ROPE // 2
    return jnp.concatenate([-w[..., half:], w[..., :half]], axis=-1)


def kernel(x, meta_tokens, ffn1_norm, ffn1_w_gate, ffn1_w_up, ffn1_w_down, mix_norm, w_in, q_norm, w_uq,
           kv_norm, w_ukv, w_mla_o, conv_w, a_log, dt_bias, gdn_norm, w_gdn_o, w_out, ffn2_norm,
           ffn2_w_gate, ffn2_w_up, ffn2_w_down, final_norm):
    assert ffn1_norm.shape[0] == 1, "single-layer block"
    batch, seq, d = x.shape
    assert d == D_MODEL and seq % CHUNK == 0
    tm = min(256, seq)
    tq = min(512, seq)
    assert seq % tm == 0 and seq % tq == 0

    wi = w_in[0]
    sizes = (MLA_Q_RANK, MLA_KV_RANK, MLA_ROPE, GDN_W, GDN_W, GDN_W, GDN_HEADS, GDN_HEADS, GDN_W,
             D_MODEL, D_MODEL)
    offs = np.concatenate([[0], np.cumsum(sizes)])
    col = lambda a, b: wi[:, int(offs[a]):int(offs[b])]
    zcols = lambda n: jnp.zeros((D_MODEL, n), F32)
    w_kr = col(2, 3)
    win_a = jnp.concatenate([col(0, 2), w_kr, _rot(w_kr), zcols(LANES - 2 * MLA_ROPE), col(3, 6),
                             col(6, 8), zcols(LANES - 2 * GDN_HEADS), col(8, 9)], axis=1).astype(BF16)
    assert win_a.shape[1] == PROJ_W
    win_g = col(9, 11).astype(BF16)

    wq = w_uq[0].reshape(MLA_Q_RANK, MLA_HEADS, MLA_NOPE + MLA_ROPE)
    wq_nope, wq_rope = wq[..., :MLA_NOPE], wq[..., MLA_NOPE:]
    zq = lambda n: jnp.zeros((MLA_Q_RANK, MLA_HEADS, n), F32)
    tail = HEAD_PAD - MLA_NOPE - MLA_ROPE
    wqa = jnp.concatenate([wq_nope, wq_rope, zq(tail)], axis=-1).reshape(MLA_Q_RANK, QK_W).astype(BF16)
    wqb = jnp.concatenate([zq(MLA_NOPE), _rot(wq_rope), zq(tail)], axis=-1).reshape(MLA_Q_RANK, QK_W).astype(BF16)
    wkv = w_ukv[0].reshape(MLA_KV_RANK, MLA_HEADS, MLA_NOPE + MLA_V)
    wk = jnp.concatenate([wkv[..., :MLA_NOPE], jnp.zeros((MLA_KV_RANK, MLA_HEADS, HEAD_PAD - MLA_NOPE), F32)],
                         axis=-1).reshape(MLA_KV_RANK, QK_W).astype(BF16)
    wvt = wkv[..., MLA_NOPE:].reshape(MLA_KV_RANK, V_W).T.astype(BF16)
    e_np = np.zeros((LANES, QK_W), np.float32)
    for h in range(MLA_HEADS):
        for j in range(MLA_ROPE):
            e_np[j, h * HEAD_PAD + MLA_NOPE + j] = 1.0
            e_np[MLA_ROPE + j, h * HEAD_PAD + MLA_NOPE + j] = 1.0
    e_mat = jnp.asarray(e_np, BF16)

    g = np.arange(GROUP_W)
    bd_ones = jnp.asarray((g[:, None] // GDN_DK == g[None, :] // GDN_DK).astype(np.float32), BF16)
    proj_weights = [
        ffn1_norm[0][None], ffn1_w_gate[0].astype(BF16), ffn1_w_up[0].astype(BF16),
        ffn1_w_down[0].astype(BF16), mix_norm[0][None], win_a, q_norm[0][None], wqa, wqb,
        kv_norm[0][None], wk, wvt, e_mat, conv_w[0].astype(F32), bd_ones]

    zero_carry = jnp.zeros((8, 3 * GDN_W), F32)
    _, _, k_m, vt_m, gqkv_m, gba_m, _, meta_tail = _token_proj(
        meta_tokens.astype(F32), _rope_tables(0, N_META), zero_carry, proj_weights, N_META, 1)
    h1, q, k, vt, gqkv, gba, z, _ = _token_proj(
        x.reshape(batch * seq, d), _rope_tables(N_META, seq), meta_tail, proj_weights, tm, seq // tm)

    pad_rows = lambda a, n, front: jnp.pad(a, ((n - a.shape[0], 0) if front else (0, n - a.shape[0]), (0, 0)))
    vt_meta = jnp.pad(vt_m[0], ((0, 0), (0, LANES - N_META)))
    o_mla = _mla_attn(q, k, vt, pad_rows(k_m, LANES, False), vt_meta, batch, seq, tq, tm)

    hpad = lambda a: jnp.zeros((1, LANES), F32).at[0, GDN_HEADS:2 * GDN_HEADS].set(a)
    arate = hpad(-jnp.exp(a_log[0].astype(F32)))
    dtb = hpad(dt_bias[0].astype(F32))
    gnorm = jnp.tile(gdn_norm[0].astype(F32), GDN_HEADS)[None]
    gdn_params = (arate, dtb, gnorm)
    cps = 4 if seq % (4 * CHUNK) == 0 else 1
    _, state_meta = _gdn(pad_rows(gqkv_m, CHUNK, True), pad_rows(gba_m, CHUNK, True),
                         jnp.zeros((CHUNK, GDN_W), F32), jnp.zeros((N_GROUPS, GDN_DK, GROUP_W), F32),
                         gdn_params, 1, CHUNK, 1, CHUNK - N_META)
    o_gdn, _ = _gdn(gqkv, gba, z, state_meta, gdn_params, batch, seq, cps, 0)

    merge_weights = [
        mix_norm[0][None], win_g, w_mla_o[0].astype(BF16), w_gdn_o[0].astype(BF16), w_out[0].astype(BF16),
        ffn2_norm[0][None], ffn2_w_gate[0].astype(BF16), ffn2_w_up[0].astype(BF16),
        ffn2_w_down[0].astype(BF16), final_norm[None]]
    out = _merge_ffn(h1, o_mla, o_gdn, merge_weights, tm)
    return out.reshape(batch, seq, d)
```

```python
import functools
import math

import jax
import jax.numpy as jnp
import numpy as np
from jax import lax
from jax.experimental import pallas as pl
from jax.experimental.pallas import tpu as pltpu

F32 = jnp.float32
BF16 = jnp.bfloat16

D_MODEL = 1024
N_META = 16
EPS = 1e-6
D_FF = 2816
MLA_HEADS = 8
MLA_Q_RANK = 256
MLA_KV_RANK = 128
MLA_NOPE = 64
MLA_ROPE = 32
MLA_V = 64
ROPE_THETA = 10000.0
GDN_HEADS = 8
GDN_DK = 64
GDN_DV = 64
CONV_K = 4
CHUNK = 64

LANES = 128
HEAD_PAD = 128
QK_W = MLA_HEADS * HEAD_PAD
V_W = MLA_HEADS * MLA_V
GDN_W = GDN_HEADS * GDN_DK
GROUP_HEADS = 4
GROUP_W = GROUP_HEADS * GDN_DK
N_GROUPS = GDN_HEADS // GROUP_HEADS
FFN_CHUNK = 256
PROJ_W = MLA_Q_RANK + MLA_KV_RANK + LANES + 3 * GDN_W + LANES + GDN_W
VMEM_LIMIT = 56 * 1024 * 1024
NEG_BIG = -1e30


def _const_spec(shape):
    zeros = (0,) * len(shape)
    return pl.BlockSpec(shape, lambda *_: zeros, pipeline_mode=pl.Buffered(1))


def _rms(x, w):
    return x * lax.rsqrt(jnp.mean(x * x, axis=-1, keepdims=True) + EPS) * w


def _dot(a, b):
    return jnp.dot(a, b, preferred_element_type=F32)


def _dot_nt(a, b):
    return lax.dot_general(a, b, (((1,), (1,)), ((), ())), preferred_element_type=F32)


def _silu(x):
    return x * jax.nn.sigmoid(x)


def _split(x, n):
    pieces = []
    for _ in range(n - 1):
        hi = x.astype(BF16)
        pieces.append(hi)
        x = x - hi.astype(F32)
    pieces.append(x.astype(BF16))
    return pieces


def _dot_exact_rhs(x, rhs, n):
    return sum(_dot(p, rhs) for p in _split(x, n))


def _dot_exact_lhs(lhs, x, n):
    return sum(_dot(lhs, p) for p in _split(x, n))


def _head_sums(x, bd_ones):
    return jnp.concatenate([_dot(x[:, g * GROUP_W:(g + 1) * GROUP_W].astype(BF16), bd_ones)
                            for g in range(N_GROUPS)], axis=1)


def _token_proj_kernel(x_ref, tc_ref, ts_ref, tk_ref, c0_ref, n1_ref, wg_ref, wu_ref, wd_ref, nm_ref, win_ref,
                       qn_ref, wqa_ref, wqb_ref, kvn_ref, wk_ref, wvt_ref, e_ref, convw_ref, bd_ones_ref,
                       h1_ref, q_ref, k_ref, vt_ref, gqkv_ref, gba_ref, z_ref, tail_ref, xe_ref,
                       *, tiles_per_seq):
    tm = x_ref.shape[0]
    step = pl.program_id(0)

    @pl.when(step == 0)
    def _():
        xe_ref[...] = jnp.zeros_like(xe_ref)

    x = x_ref[...]
    xn = _rms(x, n1_ref[...]).astype(BF16)

    n_ffn = D_FF // FFN_CHUNK
    n_conv = 3 * GDN_W // LANES
    acts = []
    for c in range(max(n_ffn, n_conv)):
        if c < n_ffn:
            cols = slice(c * FFN_CHUNK, (c + 1) * FFN_CHUNK)
            acts.append((_silu(_dot(xn, wg_ref[:, cols])) * _dot(xn, wu_ref[:, cols])).astype(BF16))
        if c < n_conv:
            lanes = slice(c * LANES, (c + 1) * LANES)
            y = sum(convw_ref[t:t + 1, lanes] * xe_ref[pl.ds(8 - (CONV_K - 1) + t, tm), lanes]
                    for t in range(CONV_K))
            gqkv_ref[:, lanes] = _silu(y)
    h1 = x + 0.5 * _dot(jnp.concatenate(acts, axis=1), wd_ref[...])
    h1_ref[...] = h1

    un = _rms(h1, nm_ref[...]).astype(BF16)
    p = _dot(un, win_ref[...])

    bd_ones = bd_ones_ref[...]
    gq = gqkv_ref[:, :GDN_W]
    gk = gqkv_ref[:, GDN_W:2 * GDN_W]
    gqkv_ref[:, :GDN_W] = gq * lax.rsqrt(_head_sums(gq * gq, bd_ones) + EPS) * (GDN_DK ** -0.5)
    gqkv_ref[:, GDN_W:2 * GDN_W] = gk * lax.rsqrt(_head_sums(gk * gk, bd_ones) + EPS)
    o = 0
    cq = p[:, o:o + MLA_Q_RANK]; o += MLA_Q_RANK
    ckv = p[:, o:o + MLA_KV_RANK]; o += MLA_KV_RANK
    kr = p[:, o:o + LANES]; o += LANES
    gqkv_raw = p[:, o:o + 3 * GDN_W]; o += 3 * GDN_W
    gba_ref[...] = p[:, o:o + LANES]; o += LANES
    z_ref[...] = p[:, o:o + GDN_W]

    xe_ref[0:8] = jnp.where(step % tiles_per_seq == 0, c0_ref[...], xe_ref[tm:tm + 8])
    xe_ref[8:8 + tm] = gqkv_raw
    tail_ref[...] = gqkv_raw[tm - 8:]

    cqn = _rms(cq, qn_ref[...]).astype(BF16)
    tc = jnp.concatenate([tc_ref[...]] * MLA_HEADS, axis=1)
    ts = jnp.concatenate([ts_ref[...]] * MLA_HEADS, axis=1)
    q = _dot(cqn, wqa_ref[...]) * tc + _dot(cqn, wqb_ref[...]) * ts
    q_ref[...] = q.astype(BF16)

    ckvn = _rms(ckv, kvn_ref[...]).astype(BF16)
    k = _dot(ckvn, wk_ref[...]) + _dot((kr * tk_ref[...]).astype(BF16), e_ref[...])
    k_ref[...] = k.astype(BF16)
    vt_ref[0] = _dot_nt(wvt_ref[...], ckvn).astype(BF16)


def _token_proj(x2d, tabs, conv_carry, weights, tm, tiles_per_seq):
    n = x2d.shape[0]
    assert n % tm == 0
    last = n // tm - 1
    tile = lambda i: jnp.minimum(i, last)
    row = lambda w: pl.BlockSpec((tm, w), lambda i: (tile(i), 0))
    tab = pl.BlockSpec((tm, LANES), lambda i: (tile(i) % tiles_per_seq, 0))
    out_widths = (D_MODEL, QK_W, QK_W, None, 3 * GDN_W, LANES, GDN_W)
    out_dtypes = (F32, BF16, BF16, BF16, F32, F32, F32)
    vt_spec = pl.BlockSpec((1, V_W, tm), lambda i: (tile(i), 0, 0))
    conv_spec = pl.BlockSpec((tm, 3 * GDN_W), lambda i: (jnp.maximum(i - 1, 0), 0))
    tail_shape = (8, 3 * GDN_W)
    out_specs = [vt_spec if w is None else row(w) for w in out_widths]
    out_specs[4] = conv_spec
    return pl.pallas_call(
        functools.partial(_token_proj_kernel, tiles_per_seq=tiles_per_seq),
        grid=(n // tm + 1,),
        in_specs=[row(D_MODEL), tab, tab, tab, _const_spec(tail_shape)] + [_const_spec(w.shape) for w in weights],
        out_specs=out_specs + [pl.BlockSpec(tail_shape, lambda i: (0, 0))],
        out_shape=[jax.ShapeDtypeStruct((n // tm, V_W, tm) if w is None else (n, w), d)
                   for w, d in zip(out_widths, out_dtypes)] + [jax.ShapeDtypeStruct(tail_shape, F32)],
        scratch_shapes=[pltpu.VMEM((8 + tm, 3 * GDN_W), F32)],
        compiler_params=pltpu.CompilerParams(dimension_semantics=("arbitrary",),
                                             vmem_limit_bytes=VMEM_LIMIT),
        name="token_proj",
    )(x2d, *tabs, conv_carry, *weights)


def _mla_kernel(q_ref, k_ref, vt_ref, km_ref, vmt_ref, o_ref, st_ref, *, tq, tk):
    qi = pl.program_id(2)
    heads = range(2)
    hs = [slice(h * HEAD_PAD, (h + 1) * HEAD_PAD) for h in heads]
    vs = [slice(h * MLA_V, (h + 1) * MLA_V) for h in heads]
    q = [q_ref[:, s] for s in hs]
    colmax = lambda s: jnp.max(s, axis=0, keepdims=True)

    def with_ones(vt):
        return jnp.concatenate([vt, jnp.ones((8, vt.shape[1]), BF16)], axis=0)

    def scores(ki, slot, q_from=0):
        rows = pl.ds(pl.multiple_of(ki * tk, tk), tk)
        for h in heads:
            st_ref[slot, h, :, q_from:] = _dot_nt(k_ref[rows, hs[h]], q[h][q_from:])

    def update(ki, slot, m, acc, mask, q_from=0):
        st = [st_ref[slot, h, :, q_from:] for h in heads]
        if mask is not None:
            st = [jnp.where(mask, s, NEG_BIG) for s in st]
        m_old = [x[:, q_from:] for x in m]
        m_new = [jnp.maximum(m_old[h], colmax(st[h])) for h in heads]
        p = [jnp.exp2(st[h] - m_new[h]).astype(BF16) for h in heads]
        vt = vt_ref[ki]
        acc_new = [jnp.exp2(m_old[h] - m_new[h]) * acc[h][:, q_from:] + _dot(with_ones(vt[vs[h]]), p[h])
                   for h in heads]
        if q_from:
            m_new = [jnp.concatenate([m[h][:, :q_from], m_new[h]], axis=1) for h in heads]
            acc_new = [jnp.concatenate([acc[h][:, :q_from], acc_new[h]], axis=1) for h in heads]
        return m_new, acc_new

    scores(0, 0)
    meta_valid = lax.broadcasted_iota(jnp.int32, (km_ref.shape[0], tq), 0) < N_META
    st = [jnp.where(meta_valid, _dot_nt(km_ref[:, hs[h]], q[h]), NEG_BIG) for h in heads]
    m = [colmax(s) for s in st]
    acc = [_dot(with_ones(vmt_ref[vs[h], :]), jnp.exp2(st[h] - m[h]).astype(BF16)) for h in heads]

    def body(i, carry):
        ki = 2 * i
        scores(ki + 1, 1)
        carry = update(ki, 0, *carry, None)
        scores(ki + 2, 0)
        return update(ki + 1, 1, *carry, None)

    m, acc = lax.fori_loop(0, qi, body, (m, acc))
    n_full = 2 * qi
    causal = (lax.broadcasted_iota(jnp.int32, (tk, tq), 1) >= lax.broadcasted_iota(jnp.int32, (tk, tq), 0))
    scores(n_full + 1, 1, q_from=tk)
    m, acc = update(n_full, 0, m, acc, causal)
    m, acc = update(n_full + 1, 1, m, acc, causal[:, :tk], q_from=tk)
    o_ref[...] = jnp.concatenate([(a[:MLA_V] * (1.0 / a[MLA_V:MLA_V + 1])).T for a in acc],
                                 axis=1).astype(BF16)


def _mla_attn(q, k, vt, k_meta, vt_meta, batch, seq, tq, tk):
    nq = seq // tq
    assert vt.shape[2] == tk and tq == 2 * tk
    kern = functools.partial(_mla_kernel, tq=tq, tk=tk)
    return pl.pallas_call(
        kern,
        grid=(batch, MLA_HEADS // 2, nq),
        in_specs=[
            pl.BlockSpec((tq, 2 * HEAD_PAD), lambda b, hp, i: (b * nq + i, hp)),
            pl.BlockSpec((seq, 2 * HEAD_PAD), lambda b, hp, i: (b, hp)),
            pl.BlockSpec((seq // tk, 2 * MLA_V, tk), lambda b, hp, i: (b, hp, 0)),
            pl.BlockSpec((k_meta.shape[0], 2 * HEAD_PAD), lambda b, hp, i: (0, hp)),
            pl.BlockSpec((2 * MLA_V, vt_meta.shape[1]), lambda b, hp, i: (hp, 0)),
        ],
        out_specs=pl.BlockSpec((tq, 2 * MLA_V), lambda b, hp, i: (b * nq + i, hp)),
        out_shape=jax.ShapeDtypeStruct((batch * seq, V_W), BF16),
        scratch_shapes=[pltpu.VMEM((2, 2, tk, tq), F32)],
        compiler_params=pltpu.CompilerParams(
            dimension_semantics=("arbitrary", "arbitrary", "arbitrary"),
            vmem_limit_bytes=VMEM_LIMIT),
        name="mla_attn",
    )(q, k, vt, k_meta, vt_meta)


LEVELS = (1, 2, 4, 8, 16, 32)
GDN_CONST_NAMES = ("expand_b", "expand_a", "ltri", "eye_t", "tril_t", "stril_t", "level_masks", "bd_ones")


def _gdn_constants(cps):
    i = np.arange(CHUNK)[:, None]
    lane = np.arange(GDN_W)[None, :]
    j = lane % GDN_DK
    c = {}
    r = np.arange(LANES)[:, None]
    c["expand_b"] = (r == lane // GDN_DK)
    c["expand_a"] = (r == GDN_HEADS + lane // GDN_DK)
    t = np.arange(cps * CHUNK)
    c["ltri"] = (t[:, None] >= t[None, :]) & (t[:, None] // CHUNK == t[None, :] // CHUNK)
    c["eye_t"] = (i == j)
    c["tril_t"] = (i >= j)
    c["stril_t"] = (i > j)
    jg = j[:, :GROUP_W]
    c["level_masks"] = np.stack([
        ((i // (2 * s) == jg // (2 * s)) & ((i // s) % 2 == 1) & ((jg // s) % 2 == 0))
        for s in LEVELS])
    g = np.arange(GROUP_W)
    c["bd_ones"] = (g[:, None] // GDN_DK == g[None, :] // GDN_DK)
    bf = ("bd_ones", "expand_b", "expand_a", "ltri")
    return [jnp.asarray(c[k].astype(np.float32), BF16 if k in bf else F32) for k in GDN_CONST_NAMES]


def _block_diag(y, lo_half):
    zeros = jnp.zeros((GDN_DK, LANES), y.dtype)
    blocks = []
    for h in range(GROUP_HEADS):
        t = h // 2
        tile = y[:, t * LANES:(t + 1) * LANES]
        piece = jnp.where(lo_half, tile, 0) if h % 2 == 0 else jnp.where(lo_half, 0, tile)
        blocks.append(jnp.concatenate([piece, zeros] if t == 0 else [zeros, piece], axis=1))
    return jnp.concatenate(blocks, axis=0)


def _head_matmul(x, y, lo_half):
    return _dot(x.astype(BF16), _block_diag(y.astype(BF16), lo_half))


def _diag_blocks(m, lo_half):
    tiles = []
    for t in range(m.shape[1] // LANES):
        h = 2 * (t % 2)
        cols = slice(t * LANES, (t + 1) * LANES)
        tiles.append(jnp.where(lo_half, m[h * GDN_DK:(h + 1) * GDN_DK, cols],
                               m[(h + 1) * GDN_DK:(h + 2) * GDN_DK, cols]))
    return jnp.concatenate(tiles, axis=1)


def _gdn_kernel(gx_ref, ba_ref, z_ref, s0_ref, arate_ref, dtb_ref, gnorm_ref,
                expand_b_ref, expand_a_ref, ltri_ref, eye_t_ref, tril_t_ref, stril_t_ref,
                level_masks_ref, bd_ones_ref,
                o_ref, sout_ref, state_ref, *, cps, front_pad):
    c = pl.program_id(1)
    rows = cps * CHUNK

    @pl.when(c == 0)
    def _():
        state_ref[...] = s0_ref[...]

    qn = gx_ref[:, :GDN_W]
    kn = gx_ref[:, GDN_W:2 * GDN_W]
    v = gx_ref[:, 2 * GDN_W:]

    ba = ba_ref[...]
    beta = jax.nn.sigmoid(ba)
    sp_in = ba + dtb_ref[...]
    g = arate_ref[...] * (jnp.maximum(sp_in, 0.0) + jnp.log1p(jnp.exp(-jnp.abs(sp_in))))
    if front_pad:
        valid = lax.broadcasted_iota(jnp.int32, (rows, 1), 0) >= front_pad
        beta = jnp.where(valid, beta, 0.0)
        g = jnp.where(valid, g, 0.0)
    beta_e = _dot_exact_rhs(beta, expand_b_ref[...], 2)
    gc = _dot_exact_lhs(ltri_ref[...], g, 3)
    gc_e = _dot_exact_rhs(gc, expand_a_ref[...], 3)
    egc = jnp.exp(gc_e)
    kb = kn * beta_e
    vb = v * beta_e
    kbg = kb * egc
    qg = qn * egc
    eye, tril, stril = eye_t_ref[...], tril_t_ref[...], stril_t_ref[...]
    lo_half = lax.broadcasted_iota(jnp.int32, (1, LANES), 1) < GDN_DK

    chains = [(j, gi) for j in range(cps) for gi in range(N_GROUPS)]
    rsl = lambda j: slice(j * CHUNK, (j + 1) * CHUNK)
    gsl = lambda gi: slice(gi * GROUP_W, (gi + 1) * GROUP_W)
    decay, kdec, s_decay = [], [], []
    for j in range(cps):
        gce = gc_e[rsl(j)]
        gc_t = jnp.sum(gce * eye, axis=0, keepdims=True)
        decay.append(jnp.exp(jnp.where(tril > 0.5, gce - gc_t, -jnp.inf)))
        g_last = gce[CHUNK - 1:CHUNK]
        kdec.append(kn[rsl(j)] * jnp.exp(g_last - gce))
        s_decay.append(jnp.exp(g_last))

    lmat, amat = [], []
    for j, gi in chains:
        rs, gs = rsl(j), gsl(gi)
        k4 = kn[rs, gs].astype(BF16)
        kstack = _block_diag(k4, lo_half)
        lhs = jnp.concatenate([kb[rs, gs], qn[rs, gs]], axis=0).astype(BF16)
        sc = _dot_nt(lhs, kstack)
        lmat.append(sc[:CHUNK] * decay[j][:, gs] * stril[:, gs])
        amat.append(sc[CHUNK:] * decay[j][:, gs])

    xinv = [eye[:, gsl(gi)] - lm * level_masks_ref[0] for (j, gi), lm in zip(chains, lmat)]
    for li in range(1, len(LEVELS)):
        xo = [_head_matmul(x, lm * level_masks_ref[li], lo_half) for x, lm in zip(xinv, lmat)]
        xinv = [x - _head_matmul(y, x, lo_half) for x, y in zip(xinv, xo)]

    uw = []
    for (j, gi), x in zip(chains, xinv):
        rs, gs = rsl(j), gsl(gi)
        rhs = jnp.concatenate([_block_diag(vb[rs, gs].astype(BF16), lo_half),
                               _block_diag(kbg[rs, gs].astype(BF16), lo_half)], axis=1)
        uw.append(_dot(x.astype(BF16), rhs))
    kwu = [_diag_blocks(_dot(kdec[j][:, gsl(gi)].T.astype(BF16), m.astype(BF16)), lo_half)
           for (j, gi), m in zip(chains, uw)]

    states = [state_ref[gi] for gi in range(N_GROUPS)]
    o_rows = [[None] * N_GROUPS for _ in range(cps)]
    for ci, (j, gi) in enumerate(chains):
        rs, gs = rsl(j), gsl(gi)
        u, wmat = uw[ci][:, :GROUP_W], uw[ci][:, GROUP_W:]
        ku, kw = kwu[ci][:, :GROUP_W], kwu[ci][:, GROUP_W:]
        state = states[gi]
        lhs = jnp.concatenate([kw, wmat, qg[rs, gs]], axis=0).astype(BF16)
        big = _dot(lhs, _block_diag(state.astype(BF16), lo_half))
        states[gi] = s_decay[j][:, gs] * state - big[:CHUNK] + ku
        v_new = u - big[CHUNK:2 * CHUNK]
        o_rows[j][gi] = big[2 * CHUNK:] + _head_matmul(amat[ci], v_new, lo_half)
    for gi in range(N_GROUPS):
        state_ref[gi] = states[gi]

    @pl.when(c == pl.num_programs(1) - 1)
    def _():
        sout_ref[...] = state_ref[...]

    o = jnp.concatenate([jnp.concatenate(r, axis=1) for r in o_rows], axis=0)
    ms = _head_sums(o * o, bd_ones_ref[...]) * (1.0 / GDN_DV)
    o = o * lax.rsqrt(ms + EPS) * gnorm_ref[...] * _silu(z_ref[...])
    o_ref[...] = o.astype(BF16)


def _gdn(gqkv, gba, z, state0, params, batch, seq, cps, front_pad):
    rows = cps * CHUNK
    steps = seq // rows
    consts = _gdn_constants(cps)
    xrow = lambda w: pl.BlockSpec((rows, w), lambda b, c: (b * steps + c, 0))
    state_shape = (N_GROUPS, GDN_DK, GROUP_W)
    return pl.pallas_call(
        functools.partial(_gdn_kernel, cps=cps, front_pad=front_pad),
        grid=(batch, steps),
        in_specs=[xrow(3 * GDN_W), xrow(LANES), xrow(GDN_W), _const_spec(state_shape)]
                 + [_const_spec(a.shape) for a in params]
                 + [_const_spec(a.shape) for a in consts],
        out_specs=[xrow(GDN_W), pl.BlockSpec(state_shape, lambda b, c: (0, 0, 0))],
        out_shape=[jax.ShapeDtypeStruct((batch * seq, GDN_W), BF16),
                   jax.ShapeDtypeStruct(state_shape, F32)],
        scratch_shapes=[pltpu.VMEM(state_shape, F32)],
        compiler_params=pltpu.CompilerParams(dimension_semantics=("arbitrary", "arbitrary"),
                                             vmem_limit_bytes=VMEM_LIMIT),
        name="gdn_chunk",
    )(gqkv, gba, z, state0, *params, *consts)


def _merge_ffn_kernel(h1_ref, om_ref, og_ref, nm_ref, wing_ref, wmo_ref, wgo_ref, wout_ref,
                      n2_ref, wg_ref, wu_ref, wd_ref, nf_ref, out_ref):
    h1 = h1_ref[...]
    un = _rms(h1, nm_ref[...]).astype(BF16)
    gates = jax.nn.sigmoid(_dot(un, wing_ref[...]))
    merged = (gates[:, :D_MODEL] * _dot(om_ref[...], wmo_ref[...])
              + gates[:, D_MODEL:] * _dot(og_ref[...], wgo_ref[...]))
    h2 = h1 + _dot(merged.astype(BF16), wout_ref[...])
    xn = _rms(h2, n2_ref[...]).astype(BF16)
    act = (_silu(_dot(xn, wg_ref[...])) * _dot(xn, wu_ref[...])).astype(BF16)
    h3 = h2 + 0.5 * _dot(act, wd_ref[...])
    out_ref[...] = _rms(h3, nf_ref[...])


def _merge_ffn(h1, o_mla, o_gdn, weights, tm):
    n = h1.shape[0]
    row = lambda w: pl.BlockSpec((tm, w), lambda i: (i, 0))
    return pl.pallas_call(
        _merge_ffn_kernel,
        grid=(n // tm,),
        in_specs=[row(D_MODEL), row(V_W), row(GDN_W)] + [_const_spec(w.shape) for w in weights],
        out_specs=row(D_MODEL),
        out_shape=jax.ShapeDtypeStruct((n, D_MODEL), F32),
        compiler_params=pltpu.CompilerParams(dimension_semantics=("arbitrary",),
                                             vmem_limit_bytes=VMEM_LIMIT),
        name="merge_ffn",
    )(h1, o_mla, o_gdn, *weights)


def _rope_tables(first_pos, n):
    pos = np.arange(first_pos, first_pos + n, dtype=np.float64)
    inv = ROPE_THETA ** (-np.arange(0, MLA_ROPE, 2, dtype=np.float64) / MLA_ROPE)
    ang = pos[:, None] * inv[None, :]
    cos2 = np.tile(np.cos(ang), (1, 2))
    sin2 = np.tile(np.sin(ang), (1, 2))
    scale = (MLA_NOPE + MLA_ROPE) ** -0.5 * math.log2(math.e)
    pad = np.zeros((n, HEAD_PAD - MLA_NOPE - MLA_ROPE))
    tab_c = np.concatenate([np.ones((n, MLA_NOPE)), cos2, pad], axis=1) * scale
    tab_s = np.concatenate([np.zeros((n, MLA_NOPE)), sin2, pad], axis=1) * scale
    tab_k = np.concatenate([cos2, sin2, np.zeros((n, LANES - 2 * MLA_ROPE))], axis=1)
    return tuple(jnp.asarray(t, F32) for t in (tab_c, tab_s, tab_k))


def _rot(w):
    half = MLA_ROPE // 2
    return jnp.concatenate([-w[..., half:], w[..., :half]], axis=-1)


def kernel(x, meta_tokens, ffn1_norm, ffn1_w_gate, ffn1_w_up, ffn1_w_down, mix_norm, w_in, q_norm, w_uq,
           kv_norm, w_ukv, w_mla_o, conv_w, a_log, dt_bias, gdn_norm, w_gdn_o, w_out, ffn2_norm,
           ffn2_w_gate, ffn2_w_up, ffn2_w_down, final_norm):
    assert ffn1_norm.shape[0] == 1, "single-layer block"
    batch, seq, d = x.shape
    assert d == D_MODEL and seq % CHUNK == 0
    tm = min(256, seq)
    tq = min(512, seq)
    assert seq % tm == 0 and seq % tq == 0

    wi = w_in[0]
    sizes = (MLA_Q_RANK, MLA_KV_RANK, MLA_ROPE, GDN_W, GDN_W, GDN_W, GDN_HEADS, GDN_HEADS, GDN_W,
             D_MODEL, D_MODEL)
    offs = np.concatenate([[0], np.cumsum(sizes)])
    col = lambda a, b: wi[:, int(offs[a]):int(offs[b])]
    zcols = lambda n: jnp.zeros((D_MODEL, n), F32)
    w_kr = col(2, 3)
    win_a = jnp.concatenate([col(0, 2), w_kr, _rot(w_kr), zcols(LANES - 2 * MLA_ROPE), col(3, 6),
                             col(6, 8), zcols(LANES - 2 * GDN_HEADS), col(8, 9)], axis=1).astype(BF16)
    assert win_a.shape[1] == PROJ_W
    win_g = col(9, 11).astype(BF16)

    wq = w_uq[0].reshape(MLA_Q_RANK, MLA_HEADS, MLA_NOPE + MLA_ROPE)
    wq_nope, wq_rope = wq[..., :MLA_NOPE], wq[..., MLA_NOPE:]
    zq = lambda n: jnp.zeros((MLA_Q_RANK, MLA_HEADS, n), F32)
    tail = HEAD_PAD - MLA_NOPE - MLA_ROPE
    wqa = jnp.concatenate([wq_nope, wq_rope, zq(tail)], axis=-1).reshape(MLA_Q_RANK, QK_W).astype(BF16)
    wqb = jnp.concatenate([zq(MLA_NOPE), _rot(wq_rope), zq(tail)], axis=-1).reshape(MLA_Q_RANK, QK_W).astype(BF16)
    wkv = w_ukv[0].reshape(MLA_KV_RANK, MLA_HEADS, MLA_NOPE + MLA_V)
    wk = jnp.concatenate([wkv[..., :MLA_NOPE], jnp.zeros((MLA_KV_RANK, MLA_HEADS, HEAD_PAD - MLA_NOPE), F32)],
                         axis=-1).reshape(MLA_KV_RANK, QK_W).astype(BF16)
    wvt = wkv[..., MLA_NOPE:].reshape(MLA_KV_RANK, V_W).T.astype(BF16)
    e_np = np.zeros((LANES, QK_W), np.float32)
    for h in range(MLA_HEADS):
        for j in range(MLA_ROPE):
            e_np[j, h * HEAD_PAD + MLA_NOPE + j] = 1.0
            e_np[MLA_ROPE + j, h * HEAD_PAD + MLA_NOPE + j] = 1.0
    e_mat = jnp.asarray(e_np, BF16)

    g = np.arange(GROUP_W)
    bd_ones = jnp.asarray((g[:, None] // GDN_DK == g[None, :] // GDN_DK).astype(np.float32), BF16)
    proj_weights = [
        ffn1_norm[0][None], ffn1_w_gate[0].astype(BF16), ffn1_w_up[0].astype(BF16),
        ffn1_w_down[0].astype(BF16), mix_norm[0][None], win_a, q_norm[0][None], wqa, wqb,
        kv_norm[0][None], wk, wvt, e_mat, conv_w[0].astype(F32), bd_ones]

    zero_carry = jnp.zeros((8, 3 * GDN_W), F32)
    _, _, k_m, vt_m, gqkv_m, gba_m, _, meta_tail = _token_proj(
        meta_tokens.astype(F32), _rope_tables(0, N_META), zero_carry, proj_weights, N_META, 1)
    h1, q, k, vt, gqkv, gba, z, _ = _token_proj(
        x.reshape(batch * seq, d), _rope_tables(N_META, seq), meta_tail, proj_weights, tm, seq // tm)

    pad_rows = lambda a, n, front: jnp.pad(a, ((n - a.shape[0], 0) if front else (0, n - a.shape[0]), (0, 0)))
    vt_meta = jnp.pad(vt_m[0], ((0, 0), (0, LANES - N_META)))
    o_mla = _mla_attn(q, k, vt, pad_rows(k_m, LANES, False), vt_meta, batch, seq, tq, tm)

    hpad = lambda a: jnp.zeros((1, LANES), F32).at[0, GDN_HEADS:2 * GDN_HEADS].set(a)
    arate = hpad(-jnp.exp(a_log[0].astype(F32)))
    dtb = hpad(dt_bias[0].astype(F32))
    gnorm = jnp.tile(gdn_norm[0].astype(F32), GDN_HEADS)[None]
    gdn_params = (arate, dtb, gnorm)
    cps = 4 if seq % (4 * CHUNK) == 0 else 1
    _, state_meta = _gdn(pad_rows(gqkv_m, CHUNK, True), pad_rows(gba_m, CHUNK, True),
                         jnp.zeros((CHUNK, GDN_W), F32), jnp.zeros((N_GROUPS, GDN_DK, GROUP_W), F32),
                         gdn_params, 1, CHUNK, 1, CHUNK - N_META)
    o_gdn, _ = _gdn(gqkv, gba, z, state_meta, gdn_params, batch, seq, cps, 0)

    merge_weights = [
        mix_norm[0][None], win_g, w_mla_o[0].astype(BF16), w_gdn_o[0].astype(BF16), w_out[0].astype(BF16),
        ffn2_norm[0][None], ffn2_w_gate[0].astype(BF16), ffn2_w_up[0].astype(BF16),
        ffn2_w_down[0].astype(BF16), final_norm[None]]
    out = _merge_ffn(h1, o_mla, o_gdn, merge_weights, tm)
    return out.reshape(batch, seq, d)
```

```python
import functools
import math

import jax
import jax.numpy as jnp
import numpy as np
from jax import lax
from jax.experimental import pallas as pl
from jax.experimental.pallas import tpu as pltpu

F32 = jnp.float32
BF16 = jnp.bfloat16

D_MODEL = 1024
N_META = 16
EPS = 1e-6
D_FF = 2816
MLA_HEADS = 8
MLA_Q_RANK = 256
MLA_KV_RANK = 128
MLA_NOPE = 64
MLA_ROPE = 32
MLA_V = 64
ROPE_THETA = 10000.0
GDN_HEADS = 8
GDN_DK = 64
GDN_DV = 64
CONV_K = 4
CHUNK = 64

LANES = 128
HEAD_PAD = 128
QK_W = MLA_HEADS * HEAD_PAD
V_W = MLA_HEADS * MLA_V
GDN_W = GDN_HEADS * GDN_DK
GROUP_HEADS = 4
GROUP_W = GROUP_HEADS * GDN_DK
N_GROUPS = GDN_HEADS // GROUP_HEADS
FFN_CHUNK = 256
PROJ_W = MLA_Q_RANK + MLA_KV_RANK + LANES + 3 * GDN_W + LANES + GDN_W
VMEM_LIMIT = 56 * 1024 * 1024
NEG_BIG = -1e30


def _const_spec(shape):
    zeros = (0,) * len(shape)
    return pl.BlockSpec(shape, lambda *_: zeros, pipeline_mode=pl.Buffered(1))


def _rms(x, w):
    return x * lax.rsqrt(jnp.mean(x * x, axis=-1, keepdims=True) + EPS) * w


def _dot(a, b):
    return jnp.dot(a, b, preferred_element_type=F32)


def _dot_nt(a, b):
    return lax.dot_general(a, b, (((1,), (1,)), ((), ())), preferred_element_type=F32)


def _silu(x):
    return x * jax.nn.sigmoid(x)


def _split(x, n):
    pieces = []
    for _ in range(n - 1):
        hi = x.astype(BF16)
        pieces.append(hi)
        x = x - hi.astype(F32)
    pieces.append(x.astype(BF16))
    return pieces


def _dot_exact_rhs(x, rhs, n):
    return sum(_dot(p, rhs) for p in _split(x, n))


def _dot_exact_lhs(lhs, x, n):
    return sum(_dot(lhs, p) for p in _split(x, n))


def _head_sums(x, bd_ones):
    return jnp.concatenate([_dot(x[:, g * GROUP_W:(g + 1) * GROUP_W].astype(BF16), bd_ones)
                            for g in range(N_GROUPS)], axis=1)


def _token_proj_kernel(x_ref, tc_ref, ts_ref, tk_ref, c0_ref, n1_ref, wg_ref, wu_ref, wd_ref, nm_ref, win_ref,
                       qn_ref, wqa_ref, wqb_ref, kvn_ref, wk_ref, wvt_ref, e_ref, convw_ref, bd_ones_ref,
                       h1_ref, q_ref, k_ref, vt_ref, gqkv_ref, gba_ref, z_ref, tail_ref, xe_ref,
                       *, tiles_per_seq):
    tm = x_ref.shape[0]
    step = pl.program_id(0)

    @pl.when(step == 0)
    def _():
        xe_ref[...] = jnp.zeros_like(xe_ref)

    x = x_ref[...]
    xn = _rms(x, n1_ref[...]).astype(BF16)

    n_ffn = D_FF // FFN_CHUNK
    n_conv = 3 * GDN_W // LANES
    acts = []
    for c in range(max(n_ffn, n_conv)):
        if c < n_ffn:
            cols = slice(c * FFN_CHUNK, (c + 1) * FFN_CHUNK)
            acts.append((_silu(_dot(xn, wg_ref[:, cols])) * _dot(xn, wu_ref[:, cols])).astype(BF16))
        if c < n_conv:
            lanes = slice(c * LANES, (c + 1) * LANES)
            y = sum(convw_ref[t:t + 1, lanes] * xe_ref[pl.ds(8 - (CONV_K - 1) + t, tm), lanes]
                    for t in range(CONV_K))
            gqkv_ref[:, lanes] = _silu(y)
    h1 = x + 0.5 * _dot(jnp.concatenate(acts, axis=1), wd_ref[...])
    h1_ref[...] = h1

    un = _rms(h1, nm_ref[...]).astype(BF16)
    p = _dot(un, win_ref[...])

    bd_ones = bd_ones_ref[...]
    gq = gqkv_ref[:, :GDN_W]
    gk = gqkv_ref[:, GDN_W:2 * GDN_W]
    gqkv_ref[:, :GDN_W] = gq * lax.rsqrt(_head_sums(gq * gq, bd_ones) + EPS) * (GDN_DK ** -0.5)
    gqkv_ref[:, GDN_W:2 * GDN_W] = gk * lax.rsqrt(_head_sums(gk * gk, bd_ones) + EPS)
    o = 0
    cq = p[:, o:o + MLA_Q_RANK]; o += MLA_Q_RANK
    ckv = p[:, o:o + MLA_KV_RANK]; o += MLA_KV_RANK
    kr = p[:, o:o + LANES]; o += LANES
    gqkv_raw = p[:, o:o + 3 * GDN_W]; o += 3 * GDN_W
    gba_ref[...] = p[:, o:o + LANES]; o += LANES
    z_ref[...] = p[:, o:o + GDN_W]

    xe_ref[0:8] = jnp.where(step % tiles_per_seq == 0, c0_ref[...], xe_ref[tm:tm + 8])
    xe_ref[8:8 + tm] = gqkv_raw
    tail_ref[...] = gqkv_raw[tm - 8:]

    cqn = _rms(cq, qn_ref[...]).astype(BF16)
    tc = jnp.concatenate([tc_ref[...]] * MLA_HEADS, axis=1)
    ts = jnp.concatenate([ts_ref[...]] * MLA_HEADS, axis=1)
    q = _dot(cqn, wqa_ref[...]) * tc + _dot(cqn, wqb_ref[...]) * ts
    q_ref[...] = q.astype(BF16)

    ckvn = _rms(ckv, kvn_ref[...]).astype(BF16)
    k = _dot(ckvn, wk_ref[...]) + _dot((kr * tk_ref[...]).astype(BF16), e_ref[...])
    k_ref[...] = k.astype(BF16)
    vt_ref[0] = _dot_nt(wvt_ref[...], ckvn).astype(BF16)


def _token_proj(x2d, tabs, conv_carry, weights, tm, tiles_per_seq):
    n = x2d.shape[0]
    assert n % tm == 0
    last = n // tm - 1
    tile = lambda i: jnp.minimum(i, last)
    row = lambda w: pl.BlockSpec((tm, w), lambda i: (tile(i), 0))
    tab = pl.BlockSpec((tm, LANES), lambda i: (tile(i) % tiles_per_seq, 0))
    out_widths = (D_MODEL, QK_W, QK_W, None, 3 * GDN_W, LANES, GDN_W)
    out_dtypes = (F32, BF16, BF16, BF16, F32, F32, F32)
    vt_spec = pl.BlockSpec((1, V_W, tm), lambda i: (tile(i), 0, 0))
    conv_spec = pl.BlockSpec((tm, 3 * GDN_W), lambda i: (jnp.maximum(i - 1, 0), 0))
    tail_shape = (8, 3 * GDN_W)
    out_specs = [vt_spec if w is None else row(w) for w in out_widths]
    out_specs[4] = conv_spec
    return pl.pallas_call(
        functools.partial(_token_proj_kernel, tiles_per_seq=tiles_per_seq),
        grid=(n // tm + 1,),
        in_specs=[row(D_MODEL), tab, tab, tab, _const_spec(tail_shape)] + [_const_spec(w.shape) for w in weights],
        out_specs=out_specs + [pl.BlockSpec(tail_shape, lambda i: (0, 0))],
        out_shape=[jax.ShapeDtypeStruct((n // tm, V_W, tm) if w is None else (n, w), d)
                   for w, d in zip(out_widths, out_dtypes)] + [jax.ShapeDtypeStruct(tail_shape, F32)],
        scratch_shapes=[pltpu.VMEM((8 + tm, 3 * GDN_W), F32)],
        compiler_params=pltpu.CompilerParams(dimension_semantics=("arbitrary",),
                                             vmem_limit_bytes=VMEM_LIMIT),
        name="token_proj",
    )(x2d, *tabs, conv_carry, *weights)


def _mla_kernel(q_ref, k_ref, vt_ref, km_ref, vmt_ref, o_ref, st_ref, *, tq, tk):
    qi = pl.program_id(2)
    heads = range(2)
    hs = [slice(h * HEAD_PAD, (h + 1) * HEAD_PAD) for h in heads]
    vs = [slice(h * MLA_V, (h + 1) * MLA_V) for h in heads]
    q = [q_ref[:, s] for s in hs]
    colmax = lambda s: jnp.max(s, axis=0, keepdims=True)

    def with_ones(vt):
        return jnp.concatenate([vt, jnp.ones((8, vt.shape[1]), BF16)], axis=0)

    def scores(ki, slot, q_from=0):
        rows = pl.ds(pl.multiple_of(ki * tk, tk), tk)
        block_max = []
        for h in heads:
            s = _dot_nt(k_ref[rows, hs[h]], q[h][q_from:])
            st_ref[slot, h, :, q_from:] = s
            block_max.append(colmax(s))
        return block_max

    def update(ki, slot, m, acc, block_max, mask, q_from=0):
        st = [st_ref[slot, h, :, q_from:] for h in heads]
        if mask is not None:
            st = [jnp.where(mask, s, NEG_BIG) for s in st]
            block_max = [colmax(s) for s in st]
        m_old = [x[:, q_from:] for x in m]
        m_new = [jnp.maximum(m_old[h], block_max[h]) for h in heads]
        p = [jnp.exp2(st[h] - m_new[h]).astype(BF16) for h in heads]
        vt = vt_ref[ki]
        acc_new = [jnp.exp2(m_old[h] - m_new[h]) * acc[h][:, q_from:] + _dot(with_ones(vt[vs[h]]), p[h])
                   for h in heads]
        if q_from:
            m_new = [jnp.concatenate([m[h][:, :q_from], m_new[h]], axis=1) for h in heads]
            acc_new = [jnp.concatenate([acc[h][:, :q_from], acc_new[h]], axis=1) for h in heads]
        return m_new, acc_new

    bm_a = scores(0, 0)
    bm_b = scores(1, 1)
    meta_valid = lax.broadcasted_iota(jnp.int32, (km_ref.shape[0], tq), 0) < N_META
    st = [jnp.where(meta_valid, _dot_nt(km_ref[:, hs[h]], q[h]), NEG_BIG) for h in heads]
    m = [colmax(s) for s in st]
    acc = [_dot(with_ones(vmt_ref[vs[h], :]), jnp.exp2(st[h] - m[h]).astype(BF16)) for h in heads]

    def stage(ka, carry, cur, nxt):
        m, acc, bm_a, bm_b = carry
        bm_a2 = scores(ka + 2, nxt[0])
        m, acc = update(ka, cur[0], m, acc, bm_a, None)
        bm_b2 = scores(ka + 3, nxt[1])
        m, acc = update(ka + 1, cur[1], m, acc, bm_b, None)
        return m, acc, bm_a2, bm_b2

    def body(i, carry):
        return stage(4 * i + 2, stage(4 * i, carry, (0, 1), (2, 3)), (2, 3), (0, 1))

    causal = (lax.broadcasted_iota(jnp.int32, (tk, tq), 1) >= lax.broadcasted_iota(jnp.int32, (tk, tq), 0))
    n_full = 2 * qi

    def finish(carry, cur):
        m, acc, _, _ = carry
        m, acc = update(n_full, cur[0], m, acc, None, causal)
        m, acc = update(n_full + 1, cur[1], m, acc, None, causal[:, :tk], q_from=tk)
        return acc

    carry = lax.fori_loop(0, qi // 2, body, (m, acc, bm_a, bm_b))
    acc = lax.cond(qi % 2 == 1,
                   lambda c: finish(stage(n_full - 2, c, (0, 1), (2, 3)), (2, 3)),
                   lambda c: finish(c, (0, 1)), carry)
    o_ref[...] = jnp.concatenate([(a[:MLA_V] * (1.0 / a[MLA_V:MLA_V + 1])).T for a in acc],
                                 axis=1).astype(BF16)


def _mla_attn(q, k, vt, k_meta, vt_meta, batch, seq, tq, tk):
    nq = seq // tq
    assert vt.shape[2] == tk and tq == 2 * tk
    kern = functools.partial(_mla_kernel, tq=tq, tk=tk)
    return pl.pallas_call(
        kern,
        grid=(batch, MLA_HEADS // 2, nq),
        in_specs=[
            pl.BlockSpec((tq, 2 * HEAD_PAD), lambda b, hp, i: (b * nq + i, hp)),
            pl.BlockSpec((seq, 2 * HEAD_PAD), lambda b, hp, i: (b, hp)),
            pl.BlockSpec((seq // tk, 2 * MLA_V, tk), lambda b, hp, i: (b, hp, 0)),
            pl.BlockSpec((k_meta.shape[0], 2 * HEAD_PAD), lambda b, hp, i: (0, hp)),
            pl.BlockSpec((2 * MLA_V, vt_meta.shape[1]), lambda b, hp, i: (hp, 0)),
        ],
        out_specs=pl.BlockSpec((tq, 2 * MLA_V), lambda b, hp, i: (b * nq + i, hp)),
        out_shape=jax.ShapeDtypeStruct((batch * seq, V_W), BF16),
        scratch_shapes=[pltpu.VMEM((4, 2, tk, tq), F32)],
        compiler_params=pltpu.CompilerParams(
            dimension_semantics=("arbitrary", "arbitrary", "arbitrary"),
            vmem_limit_bytes=VMEM_LIMIT),
        name="mla_attn",
    )(q, k, vt, k_meta, vt_meta)


LEVELS = (1, 2, 4, 8, 16, 32)
GDN_CONST_NAMES = ("expand_b", "expand_a", "ltri", "eye_t", "tril_t", "stril_t", "level_masks", "bd_ones")


def _gdn_constants(cps):
    i = np.arange(CHUNK)[:, None]
    lane = np.arange(GDN_W)[None, :]
    j = lane % GDN_DK
    c = {}
    r = np.arange(LANES)[:, None]
    c["expand_b"] = (r == lane // GDN_DK)
    c["expand_a"] = (r == GDN_HEADS + lane // GDN_DK)
    t = np.arange(cps * CHUNK)
    c["ltri"] = (t[:, None] >= t[None, :]) & (t[:, None] // CHUNK == t[None, :] // CHUNK)
    c["eye_t"] = (i == j)
    c["tril_t"] = (i >= j)
    c["stril_t"] = (i > j)
    jg = j[:, :GROUP_W]
    c["level_masks"] = np.stack([
        ((i // (2 * s) == jg // (2 * s)) & ((i // s) % 2 == 1) & ((jg // s) % 2 == 0))
        for s in LEVELS])
    g = np.arange(GROUP_W)
    c["bd_ones"] = (g[:, None] // GDN_DK == g[None, :] // GDN_DK)
    bf = ("bd_ones", "expand_b", "expand_a", "ltri")
    return [jnp.asarray(c[k].astype(np.float32), BF16 if k in bf else F32) for k in GDN_CONST_NAMES]


def _block_diag(y, lo_half):
    zeros = jnp.zeros((GDN_DK, LANES), y.dtype)
    blocks = []
    for h in range(GROUP_HEADS):
        t = h // 2
        tile = y[:, t * LANES:(t + 1) * LANES]
        piece = jnp.where(lo_half, tile, 0) if h % 2 == 0 else jnp.where(lo_half, 0, tile)
        blocks.append(jnp.concatenate([piece, zeros] if t == 0 else [zeros, piece], axis=1))
    return jnp.concatenate(blocks, axis=0)


def _head_matmul(x, y, lo_half):
    return _dot(x.astype(BF16), _block_diag(y.astype(BF16), lo_half))


def _diag_blocks(m, lo_half):
    tiles = []
    for t in range(m.shape[1] // LANES):
        h = 2 * (t % 2)
        cols = slice(t * LANES, (t + 1) * LANES)
        tiles.append(jnp.where(lo_half, m[h * GDN_DK:(h + 1) * GDN_DK, cols],
                               m[(h + 1) * GDN_DK:(h + 2) * GDN_DK, cols]))
    return jnp.concatenate(tiles, axis=1)


def _gdn_kernel(gx_ref, ba_ref, z_ref, s0_ref, arate_ref, dtb_ref, gnorm_ref,
                expand_b_ref, expand_a_ref, ltri_ref, eye_t_ref, tril_t_ref, stril_t_ref,
                level_masks_ref, bd_ones_ref,
                o_ref, sout_ref, state_ref, *, cps, front_pad):
    c = pl.program_id(1)
    rows = cps * CHUNK

    @pl.when(c == 0)
    def _():
        state_ref[...] = s0_ref[...]

    qn = gx_ref[:, :GDN_W]
    kn = gx_ref[:, GDN_W:2 * GDN_W]
    v = gx_ref[:, 2 * GDN_W:]

    ba = ba_ref[...]
    beta = jax.nn.sigmoid(ba)
    sp_in = ba + dtb_ref[...]
    g = arate_ref[...] * (jnp.maximum(sp_in, 0.0) + jnp.log1p(jnp.exp(-jnp.abs(sp_in))))
    if front_pad:
        valid = lax.broadcasted_iota(jnp.int32, (rows, 1), 0) >= front_pad
        beta = jnp.where(valid, beta, 0.0)
        g = jnp.where(valid, g, 0.0)
    beta_e = _dot_exact_rhs(beta, expand_b_ref[...], 2)
    gc = _dot_exact_lhs(ltri_ref[...], g, 3)
    gc_e = _dot_exact_rhs(gc, expand_a_ref[...], 3)
    egc = jnp.exp(gc_e)
    kb = kn * beta_e
    vb = v * beta_e
    kbg = kb * egc
    qg = qn * egc
    eye, tril, stril = eye_t_ref[...], tril_t_ref[...], stril_t_ref[...]
    lo_half = lax.broadcasted_iota(jnp.int32, (1, LANES), 1) < GDN_DK

    chains = [(j, gi) for j in range(cps) for gi in range(N_GROUPS)]
    rsl = lambda j: slice(j * CHUNK, (j + 1) * CHUNK)
    gsl = lambda gi: slice(gi * GROUP_W, (gi + 1) * GROUP_W)
    decay, kdec, s_decay = [], [], []
    for j in range(cps):
        gce = gc_e[rsl(j)]
        gc_t = jnp.sum(gce * eye, axis=0, keepdims=True)
        decay.append(jnp.exp(jnp.where(tril > 0.5, gce - gc_t, -jnp.inf)))
        g_last = gce[CHUNK - 1:CHUNK]
        kdec.append(kn[rsl(j)] * jnp.exp(g_last - gce))
        s_decay.append(jnp.exp(g_last))

    lmat, amat = [], []
    for j, gi in chains:
        rs, gs = rsl(j), gsl(gi)
        k4 = kn[rs, gs].astype(BF16)
        kstack = _block_diag(k4, lo_half)
        lhs = jnp.concatenate([kb[rs, gs], qn[rs, gs]], axis=0).astype(BF16)
        sc = _dot_nt(lhs, kstack)
        lmat.append(sc[:CHUNK] * decay[j][:, gs] * stril[:, gs])
        amat.append(sc[CHUNK:] * decay[j][:, gs])

    xinv = [eye[:, gsl(gi)] - lm * level_masks_ref[0] for (j, gi), lm in zip(chains, lmat)]
    for li in range(1, len(LEVELS)):
        xo = [_head_matmul(x, lm * level_masks_ref[li], lo_half) for x, lm in zip(xinv, lmat)]
        xinv = [x - _head_matmul(y, x, lo_half) for x, y in zip(xinv, xo)]

    uw = []
    for (j, gi), x in zip(chains, xinv):
        rs, gs = rsl(j), gsl(gi)
        rhs = jnp.concatenate([_block_diag(vb[rs, gs].astype(BF16), lo_half),
                               _block_diag(kbg[rs, gs].astype(BF16), lo_half)], axis=1)
        uw.append(_dot(x.astype(BF16), rhs))
    kwu = [_diag_blocks(_dot(kdec[j][:, gsl(gi)].T.astype(BF16), m.astype(BF16)), lo_half)
           for (j, gi), m in zip(chains, uw)]

    states = [state_ref[gi] for gi in range(N_GROUPS)]
    o_rows = [[None] * N_GROUPS for _ in range(cps)]
    for ci, (j, gi) in enumerate(chains):
        rs, gs = rsl(j), gsl(gi)
        u, wmat = uw[ci][:, :GROUP_W], uw[ci][:, GROUP_W:]
        ku, kw = kwu[ci][:, :GROUP_W], kwu[ci][:, GROUP_W:]
        state = states[gi]
        lhs = jnp.concatenate([kw, wmat, qg[rs, gs]], axis=0).astype(BF16)
        big = _dot(lhs, _block_diag(state.astype(BF16), lo_half))
        states[gi] = s_decay[j][:, gs] * state - big[:CHUNK] + ku
        v_new = u - big[CHUNK:2 * CHUNK]
        o_rows[j][gi] = big[2 * CHUNK:] + _head_matmul(amat[ci], v_new, lo_half)
    for gi in range(N_GROUPS):
        state_ref[gi] = states[gi]

    @pl.when(c == pl.num_programs(1) - 1)
    def _():
        sout_ref[...] = state_ref[...]

    o = jnp.concatenate([jnp.concatenate(r, axis=1) for r in o_rows], axis=0)
    ms = _head_sums(o * o, bd_ones_ref[...]) * (1.0 / GDN_DV)
    o = o * lax.rsqrt(ms + EPS) * gnorm_ref[...] * _silu(z_ref[...])
    o_ref[...] = o.astype(BF16)


def _gdn(gqkv, gba, z, state0, params, batch, seq, cps, front_pad):
    rows = cps * CHUNK
    steps = seq // rows
    consts = _gdn_constants(cps)
    xrow = lambda w: pl.BlockSpec((rows, w), lambda b, c: (b * steps + c, 0))
    state_shape = (N_GROUPS, GDN_DK, GROUP_W)
    return pl.pallas_call(
        functools.partial(_gdn_kernel, cps=cps, front_pad=front_pad),
        grid=(batch, steps),
        in_specs=[xrow(3 * GDN_W), xrow(LANES), xrow(GDN_W), _const_spec(state_shape)]
                 + [_const_spec(a.shape) for a in params]
                 + [_const_spec(a.shape) for a in consts],
        out_specs=[xrow(GDN_W), pl.BlockSpec(state_shape, lambda b, c: (0, 0, 0))],
        out_shape=[jax.ShapeDtypeStruct((batch * seq, GDN_W), BF16),
                   jax.ShapeDtypeStruct(state_shape, F32)],
        scratch_shapes=[pltpu.VMEM(state_shape, F32)],
        compiler_params=pltpu.CompilerParams(dimension_semantics=("arbitrary", "arbitrary"),
                                             vmem_limit_bytes=VMEM_LIMIT),
        name="gdn_chunk",
    )(gqkv, gba, z, state0, *params, *consts)


def _merge_ffn_kernel(h1_ref, om_ref, og_ref, nm_ref, wing_ref, wmo_ref, wgo_ref, wout_ref,
                      n2_ref, wg_ref, wu_ref, wd_ref, nf_ref, out_ref):
    h1 = h1_ref[...]
    un = _rms(h1, nm_ref[...]).astype(BF16)
    gates = jax.nn.sigmoid(_dot(un, wing_ref[...]))
    merged = (gates[:, :D_MODEL] * _dot(om_ref[...], wmo_ref[...])
              + gates[:, D_MODEL:] * _dot(og_ref[...], wgo_ref[...]))
    h2 = h1 + _dot(merged.astype(BF16), wout_ref[...])
    xn = _rms(h2, n2_ref[...]).astype(BF16)
    act = (_silu(_dot(xn, wg_ref[...])) * _dot(xn, wu_ref[...])).astype(BF16)
    h3 = h2 + 0.5 * _dot(act, wd_ref[...])
    out_ref[...] = _rms(h3, nf_ref[...])


def _merge_ffn(h1, o_mla, o_gdn, weights, tm):
    n = h1.shape[0]
    row = lambda w: pl.BlockSpec((tm, w), lambda i: (i, 0))
    return pl.pallas_call(
        _merge_ffn_kernel,
        grid=(n // tm,),
        in_specs=[row(D_MODEL), row(V_W), row(GDN_W)] + [_const_spec(w.shape) for w in weights],
        out_specs=row(D_MODEL),
        out_shape=jax.ShapeDtypeStruct((n, D_MODEL), F32),
        compiler_params=pltpu.CompilerParams(dimension_semantics=("arbitrary",),
                                             vmem_limit_bytes=VMEM_LIMIT),
        name="merge_ffn",
    )(h1, o_mla, o_gdn, *weights)


def _rope_tables(first_pos, n):
    pos = np.arange(first_pos, first_pos + n, dtype=np.float64)
    inv = ROPE_THETA ** (-np.arange(0, MLA_ROPE, 2, dtype=np.float64) / MLA_ROPE)
    ang = pos[:, None] * inv[None, :]
    cos2 = np.tile(np.cos(ang), (1, 2))
    sin2 = np.tile(np.sin(ang), (1, 2))
    scale = (MLA_NOPE + MLA_ROPE) ** -0.5 * math.log2(math.e)
    pad = np.zeros((n, HEAD_PAD - MLA_NOPE - MLA_ROPE))
    tab_c = np.concatenate([np.ones((n, MLA_NOPE)), cos2, pad], axis=1) * scale
    tab_s = np.concatenate([np.zeros((n, MLA_NOPE)), sin2, pad], axis=1) * scale
    tab_k = np.concatenate([cos2, sin2, np.zeros((n, LANES - 2 * MLA_ROPE))], axis=1)
    return tuple(jnp.asarray(t, F32) for t in (tab_c, tab_s, tab_k))


def _rot(w):
    half = MLA_ROPE // 2
    return jnp.concatenate([-w[..., half:], w[..., :half]], axis=-1)


def kernel(x, meta_tokens, ffn1_norm, ffn1_w_gate, ffn1_w_up, ffn1_w_down, mix_norm, w_in, q_norm, w_uq,
           kv_norm, w_ukv, w_mla_o, conv_w, a_log, dt_bias, gdn_norm, w_gdn_o, w_out, ffn2_norm,
           ffn2_w_gate, ffn2_w_up, ffn2_w_down, final_norm):
    assert ffn1_norm.shape[0] == 1, "single-layer block"
    batch, seq, d = x.shape
    assert d == D_MODEL and seq % CHUNK == 0
    tm = min(256, seq)
    tq = min(512, seq)
    assert seq % tm == 0 and seq % tq == 0

    wi = w_in[0]
    sizes = (MLA_Q_RANK, MLA_KV_RANK, MLA_ROPE, GDN_W, GDN_W, GDN_W, GDN_HEADS, GDN_HEADS, GDN_W,
             D_MODEL, D_MODEL)
    offs = np.concatenate([[0], np.cumsum(sizes)])
    col = lambda a, b: wi[:, int(offs[a]):int(offs[b])]
    zcols = lambda n: jnp.zeros((D_MODEL, n), F32)
    w_kr = col(2, 3)
    win_a = jnp.concatenate([col(0, 2), w_kr, _rot(w_kr), zcols(LANES - 2 * MLA_ROPE), col(3, 6),
                             col(6, 8), zcols(LANES - 2 * GDN_HEADS), col(8, 9)], axis=1).astype(BF16)
    assert win_a.shape[1] == PROJ_W
    win_g = col(9, 11).astype(BF16)

    wq = w_uq[0].reshape(MLA_Q_RANK, MLA_HEADS, MLA_NOPE + MLA_ROPE)
    wq_nope, wq_rope = wq[..., :MLA_NOPE], wq[..., MLA_NOPE:]
    zq = lambda n: jnp.zeros((MLA_Q_RANK, MLA_HEADS, n), F32)
    tail = HEAD_PAD - MLA_NOPE - MLA_ROPE
    wqa = jnp.concatenate([wq_nope, wq_rope, zq(tail)], axis=-1).reshape(MLA_Q_RANK, QK_W).astype(BF16)
    wqb = jnp.concatenate([zq(MLA_NOPE), _rot(wq_rope), zq(tail)], axis=-1).reshape(MLA_Q_RANK, QK_W).astype(BF16)
    wkv = w_ukv[0].reshape(MLA_KV_RANK, MLA_HEADS, MLA_NOPE + MLA_V)
    wk = jnp.concatenate([wkv[..., :MLA_NOPE], jnp.zeros((MLA_KV_RANK, MLA_HEADS, HEAD_PAD - MLA_NOPE), F32)],
                         axis=-1).reshape(MLA_KV_RANK, QK_W).astype(BF16)
    wvt = wkv[..., MLA_NOPE:].reshape(MLA_KV_RANK, V_W).T.astype(BF16)
    e_np = np.zeros((LANES, QK_W), np.float32)
    for h in range(MLA_HEADS):
        for j in range(MLA_ROPE):
            e_np[j, h * HEAD_PAD + MLA_NOPE + j] = 1.0
            e_np[MLA_ROPE + j, h * HEAD_PAD + MLA_NOPE + j] = 1.0
    e_mat = jnp.asarray(e_np, BF16)

    g = np.arange(GROUP_W)
    bd_ones = jnp.asarray((g[:, None] // GDN_DK == g[None, :] // GDN_DK).astype(np.float32), BF16)
    proj_weights = [
        ffn1_norm[0][None], ffn1_w_gate[0].astype(BF16), ffn1_w_up[0].astype(BF16),
        ffn1_w_down[0].astype(BF16), mix_norm[0][None], win_a, q_norm[0][None], wqa, wqb,
        kv_norm[0][None], wk, wvt, e_mat, conv_w[0].astype(F32), bd_ones]

    zero_carry = jnp.zeros((8, 3 * GDN_W), F32)
    _, _, k_m, vt_m, gqkv_m, gba_m, _, meta_tail = _token_proj(
        meta_tokens.astype(F32), _rope_tables(0, N_META), zero_carry, proj_weights, N_META, 1)
    h1, q, k, vt, gqkv, gba, z, _ = _token_proj(
        x.reshape(batch * seq, d), _rope_tables(N_META, seq), meta_tail, proj_weights, tm, seq // tm)

    pad_rows = lambda a, n, front: jnp.pad(a, ((n - a.shape[0], 0) if front else (0, n - a.shape[0]), (0, 0)))
    vt_meta = jnp.pad(vt_m[0], ((0, 0), (0, LANES - N_META)))
    o_mla = _mla_attn(q, k, vt, pad_rows(k_m, LANES, False), vt_meta, batch, seq, tq, tm)

    hpad = lambda a: jnp.zeros((1, LANES), F32).at[0, GDN_HEADS:2 * GDN_HEADS].set(a)
    arate = hpad(-jnp.exp(a_log[0].astype(F32)))
    dtb = hpad(dt_bias[0].astype(F32))
    gnorm = jnp.tile(gdn_norm[0].astype(F32), GDN_HEADS)[None]
    gdn_params = (arate, dtb, gnorm)
    cps = 4 if seq % (4 * CHUNK) == 0 else 1
    _, state_meta = _gdn(pad_rows(gqkv_m, CHUNK, True), pad_rows(gba_m, CHUNK, True),
                         jnp.zeros((CHUNK, GDN_W), F32), jnp.zeros((N_GROUPS, GDN_DK, GROUP_W), F32),
                         gdn_params, 1, CHUNK, 1, CHUNK - N_META)
    o_gdn, _ = _gdn(gqkv, gba, z, state_meta, gdn_params, batch, seq, cps, 0)

    merge_weights = [
        mix_norm[0][None], win_g, w_mla_o[0].astype(BF16), w_gdn_o[0].astype(BF16), w_out[0].astype(BF16),
        ffn2_norm[0][None], ffn2_w_gate[0].astype(BF16), ffn2_w_up[0].astype(BF16),
        ffn2_w_down[0].astype(BF16), final_norm[None]]
    out = _merge_ffn(h1, o_mla, o_gdn, merge_weights, tm)
    return out.reshape(batch, seq, d)
```

```python
import functools
import math

import jax
import jax.numpy as jnp
import numpy as np
from jax import lax
from jax.experimental import pallas as pl
from jax.experimental.pallas import tpu as pltpu

F32 = jnp.float32
BF16 = jnp.bfloat16

D_MODEL = 1024
N_META = 16
EPS = 1e-6
D_FF = 2816
MLA_HEADS = 8
MLA_Q_RANK = 256
MLA_KV_RANK = 128
MLA_NOPE = 64
MLA_ROPE = 32
MLA_V = 64
ROPE_THETA = 10000.0
GDN_HEADS = 8
GDN_DK = 64
GDN_DV = 64
CONV_K = 4
CHUNK = 64

LANES = 128
HEAD_PAD = 128
QK_W = MLA_HEADS * HEAD_PAD
V_W = MLA_HEADS * MLA_V
GDN_W = GDN_HEADS * GDN_DK
GROUP_HEADS = 4
GROUP_W = GROUP_HEADS * GDN_DK
N_GROUPS = GDN_HEADS // GROUP_HEADS
FFN_CHUNK = 256
PROJ_W = MLA_Q_RANK + MLA_KV_RANK + LANES + 3 * GDN_W + LANES + GDN_W
W_IN_SPLITS = (MLA_Q_RANK, MLA_KV_RANK, MLA_ROPE, GDN_W, GDN_W, GDN_W, GDN_HEADS, GDN_HEADS, GDN_W,
               D_MODEL, D_MODEL)
(OFF_CQ, OFF_CKV, OFF_KR, OFF_GQ, OFF_GK, OFF_GV, OFF_GB, OFF_GA, OFF_GZ, OFF_GATES, _OFF_GATE_GDN,
 D_IN) = (int(v) for v in np.concatenate([[0], np.cumsum(W_IN_SPLITS)]))
D_IN_PAD = -(-D_IN // LANES) * LANES
W_TOKEN_COLS = -(-OFF_GATES // LANES) * LANES
W_GATE_START = OFF_GATES // LANES * LANES
VMEM_LIMIT = 56 * 1024 * 1024
NEG_BIG = -1e30


def _const_spec(shape):
    zeros = (0,) * len(shape)
    return pl.BlockSpec(shape, lambda *_: zeros, pipeline_mode=pl.Buffered(1))


def _rms(x, w):
    return x * lax.rsqrt(jnp.mean(x * x, axis=-1, keepdims=True) + EPS) * w


def _dot(a, b):
    return jnp.dot(a, b, preferred_element_type=F32)


def _dot_nt(a, b):
    return lax.dot_general(a, b, (((1,), (1,)), ((), ())), preferred_element_type=F32)


def _silu(x):
    return x * jax.nn.sigmoid(x)


def _split(x, n):
    pieces = []
    for _ in range(n - 1):
        hi = x.astype(BF16)
        pieces.append(hi)
        x = x - hi.astype(F32)
    pieces.append(x.astype(BF16))
    return pieces


def _dot_exact_rhs(x, rhs, n):
    return sum(_dot(p, rhs) for p in _split(x, n))


def _dot_exact_lhs(lhs, x, n):
    return sum(_dot(lhs, p) for p in _split(x, n))


def _head_sums(x, bd_ones):
    return jnp.concatenate([_dot(x[:, g * GROUP_W:(g + 1) * GROUP_W].astype(BF16), bd_ones)
                            for g in range(N_GROUPS)], axis=1)


def _token_proj_kernel(x_ref, tc_ref, ts_ref, tkc_ref, tks_ref, c0_ref, n1_ref, wg_ref, wu_ref, wd_ref, nm_ref,
                       wraw_ref, qn_ref, wqa_ref, wqb_ref, kvn_ref, wk_ref, wvt_ref, e_ref, convw_ref,
                       bd_ones_ref,
                       h1_ref, q_ref, k_ref, vt_ref, gqkv_ref, gba_ref, z_ref, tail_ref, xe_ref, win_ref,
                       *, tiles_per_seq):
    tm = x_ref.shape[0]
    step = pl.program_id(0)

    @pl.when(step == 0)
    def _():
        xe_ref[...] = jnp.zeros_like(xe_ref)
        o = OFF_GQ + LANES - MLA_ROPE
        win_ref[:, :o] = wraw_ref[:, :o]
        win_ref[:, o:o + 3 * GDN_W] = wraw_ref[:, OFF_GQ:OFF_GB]
        o += 3 * GDN_W
        lane = lax.broadcasted_iota(jnp.int32, (1, LANES), 1)
        win_ref[:, o:o + LANES] = jnp.where(lane < 2 * GDN_HEADS, wraw_ref[:, OFF_GB:OFF_GB + LANES], 0)
        win_ref[:, o + LANES:] = wraw_ref[:, OFF_GZ:OFF_GATES]

    x = x_ref[...]
    xn = _rms(x, n1_ref[...]).astype(BF16)

    n_ffn = D_FF // FFN_CHUNK
    n_conv = 3 * GDN_W // LANES
    acts = []
    for c in range(max(n_ffn, n_conv)):
        if c < n_ffn:
            cols = slice(c * FFN_CHUNK, (c + 1) * FFN_CHUNK)
            acts.append((_silu(_dot(xn, wg_ref[:, cols])) * _dot(xn, wu_ref[:, cols])).astype(BF16))
        if c < n_conv:
            lanes = slice(c * LANES, (c + 1) * LANES)
            y = sum(convw_ref[t:t + 1, lanes] * xe_ref[pl.ds(8 - (CONV_K - 1) + t, tm), lanes]
                    for t in range(CONV_K))
            gqkv_ref[:, lanes] = _silu(y)
    h1 = x + 0.5 * _dot(jnp.concatenate(acts, axis=1), wd_ref[...])
    h1_ref[...] = h1

    un = _rms(h1, nm_ref[...]).astype(BF16)
    p = _dot(un, win_ref[...])

    bd_ones = bd_ones_ref[...]
    gq = gqkv_ref[:, :GDN_W]
    gk = gqkv_ref[:, GDN_W:2 * GDN_W]
    gqkv_ref[:, :GDN_W] = gq * lax.rsqrt(_head_sums(gq * gq, bd_ones) + EPS) * (GDN_DK ** -0.5)
    gqkv_ref[:, GDN_W:2 * GDN_W] = gk * lax.rsqrt(_head_sums(gk * gk, bd_ones) + EPS)
    o = 0
    cq = p[:, o:o + MLA_Q_RANK]; o += MLA_Q_RANK
    ckv = p[:, o:o + MLA_KV_RANK]; o += MLA_KV_RANK
    kr = p[:, o:o + LANES]; o += LANES
    gqkv_raw = p[:, o:o + 3 * GDN_W]; o += 3 * GDN_W
    gba_ref[...] = p[:, o:o + LANES]; o += LANES
    z_ref[...] = p[:, o:o + GDN_W]

    xe_ref[0:8] = jnp.where(step % tiles_per_seq == 0, c0_ref[...], xe_ref[tm:tm + 8])
    xe_ref[8:8 + tm] = gqkv_raw
    tail_ref[...] = gqkv_raw[tm - 8:]

    cqn = _rms(cq, qn_ref[...]).astype(BF16)
    tc = jnp.concatenate([tc_ref[...]] * MLA_HEADS, axis=1)
    ts = jnp.concatenate([ts_ref[...]] * MLA_HEADS, axis=1)
    q = _dot(cqn, wqa_ref[...]) * tc + _dot(cqn, wqb_ref[...]) * ts
    q_ref[...] = q.astype(BF16)

    ckvn = _rms(ckv, kvn_ref[...]).astype(BF16)
    kr_terms = jnp.concatenate([kr * tkc_ref[...], kr * tks_ref[...]], axis=1).astype(BF16)
    k = _dot(ckvn, wk_ref[...]) + _dot(kr_terms, e_ref[...])
    k_ref[...] = k.astype(BF16)
    vt_ref[0] = _dot_nt(wvt_ref[...], ckvn).astype(BF16)


def _token_proj(x2d, tabs, conv_carry, weights, tm, tiles_per_seq):
    n = x2d.shape[0]
    assert n % tm == 0
    last = n // tm - 1
    tile = lambda i: jnp.minimum(i, last)
    row = lambda w: pl.BlockSpec((tm, w), lambda i: (tile(i), 0))
    tab = pl.BlockSpec((tm, LANES), lambda i: (tile(i) % tiles_per_seq, 0))
    out_widths = (D_MODEL, QK_W, QK_W, None, 3 * GDN_W, LANES, GDN_W)
    out_dtypes = (F32, BF16, BF16, BF16, F32, F32, F32)
    vt_spec = pl.BlockSpec((1, V_W, tm), lambda i: (tile(i), 0, 0))
    conv_spec = pl.BlockSpec((tm, 3 * GDN_W), lambda i: (jnp.maximum(i - 1, 0), 0))
    tail_shape = (8, 3 * GDN_W)
    out_specs = [vt_spec if w is None else row(w) for w in out_widths]
    out_specs[4] = conv_spec
    return pl.pallas_call(
        functools.partial(_token_proj_kernel, tiles_per_seq=tiles_per_seq),
        grid=(n // tm + 1,),
        in_specs=[row(D_MODEL)] + [tab] * len(tabs) + [_const_spec(tail_shape)]
                 + [_const_spec((D_MODEL, W_TOKEN_COLS) if w.shape == (D_MODEL, D_IN_PAD) else w.shape)
                    for w in weights],
        out_specs=out_specs + [pl.BlockSpec(tail_shape, lambda i: (0, 0))],
        out_shape=[jax.ShapeDtypeStruct((n // tm, V_W, tm) if w is None else (n, w), d)
                   for w, d in zip(out_widths, out_dtypes)] + [jax.ShapeDtypeStruct(tail_shape, F32)],
        scratch_shapes=[pltpu.VMEM((8 + tm, 3 * GDN_W), F32), pltpu.VMEM((D_MODEL, PROJ_W), BF16)],
        compiler_params=pltpu.CompilerParams(dimension_semantics=("arbitrary",),
                                             vmem_limit_bytes=VMEM_LIMIT),
        name="token_proj",
    )(x2d, *tabs, conv_carry, *weights)


def _mla_kernel(q_ref, k_ref, vt_ref, km_ref, vmt_ref, o_ref, st_ref, *, tq, tk):
    qi = pl.program_id(2)
    heads = range(2)
    hs = [slice(h * HEAD_PAD, (h + 1) * HEAD_PAD) for h in heads]
    vs = [slice(h * MLA_V, (h + 1) * MLA_V) for h in heads]
    q = [q_ref[:, s] for s in hs]
    colmax = lambda s: jnp.max(s, axis=0, keepdims=True)

    def with_ones(vt):
        return jnp.concatenate([vt, jnp.ones((8, vt.shape[1]), BF16)], axis=0)

    def scores(ki, slot, q_from=0):
        rows = pl.ds(pl.multiple_of(ki * tk, tk), tk)
        block_max = []
        for h in heads:
            s = _dot_nt(k_ref[rows, hs[h]], q[h][q_from:])
            st_ref[slot, h, :, q_from:] = s
            block_max.append(colmax(s))
        return block_max

    def update(ki, slot, m, acc, block_max, mask, q_from=0):
        st = [st_ref[slot, h, :, q_from:] for h in heads]
        if mask is not None:
            st = [jnp.where(mask, s, NEG_BIG) for s in st]
            block_max = [colmax(s) for s in st]
        m_old = [x[:, q_from:] for x in m]
        m_new = [jnp.maximum(m_old[h], block_max[h]) for h in heads]
        p = [jnp.exp2(st[h] - m_new[h]).astype(BF16) for h in heads]
        vt = vt_ref[ki]
        acc_new = [jnp.exp2(m_old[h] - m_new[h]) * acc[h][:, q_from:] + _dot(with_ones(vt[vs[h]]), p[h])
                   for h in heads]
        if q_from:
            m_new = [jnp.concatenate([m[h][:, :q_from], m_new[h]], axis=1) for h in heads]
            acc_new = [jnp.concatenate([acc[h][:, :q_from], acc_new[h]], axis=1) for h in heads]
        return m_new, acc_new

    bm_a = scores(0, 0)
    bm_b = scores(1, 1)
    meta_valid = lax.broadcasted_iota(jnp.int32, (km_ref.shape[0], tq), 0) < N_META
    st = [jnp.where(meta_valid, _dot_nt(km_ref[:, hs[h]], q[h]), NEG_BIG) for h in heads]
    m = [colmax(s) for s in st]
    acc = [_dot(with_ones(vmt_ref[vs[h], :]), jnp.exp2(st[h] - m[h]).astype(BF16)) for h in heads]

    def stage(ka, carry, cur, nxt):
        m, acc, bm_a, bm_b = carry
        bm_a2 = scores(ka + 2, nxt[0])
        m, acc = update(ka, cur[0], m, acc, bm_a, None)
        bm_b2 = scores(ka + 3, nxt[1])
        m, acc = update(ka + 1, cur[1], m, acc, bm_b, None)
        return m, acc, bm_a2, bm_b2

    def body(i, carry):
        return stage(4 * i + 2, stage(4 * i, carry, (0, 1), (2, 3)), (2, 3), (0, 1))

    causal = (lax.broadcasted_iota(jnp.int32, (tk, tq), 1) >= lax.broadcasted_iota(jnp.int32, (tk, tq), 0))
    n_full = 2 * qi

    def finish(carry, cur):
        m, acc, _, _ = carry
        m, acc = update(n_full, cur[0], m, acc, None, causal)
        m, acc = update(n_full + 1, cur[1], m, acc, None, causal[:, :tk], q_from=tk)
        return acc

    carry = lax.fori_loop(0, qi // 2, body, (m, acc, bm_a, bm_b))
    acc = lax.cond(qi % 2 == 1,
                   lambda c: finish(stage(n_full - 2, c, (0, 1), (2, 3)), (2, 3)),
                   lambda c: finish(c, (0, 1)), carry)
    o_ref[...] = jnp.concatenate([(a[:MLA_V] * (1.0 / a[MLA_V:MLA_V + 1])).T for a in acc],
                                 axis=1).astype(BF16)


def _mla_attn(q, k, vt, k_meta, vt_meta, batch, seq, tq, tk):
    nq = seq // tq
    assert vt.shape[2] == tk and tq == 2 * tk
    kern = functools.partial(_mla_kernel, tq=tq, tk=tk)
    return pl.pallas_call(
        kern,
        grid=(batch, MLA_HEADS // 2, nq),
        in_specs=[
            pl.BlockSpec((tq, 2 * HEAD_PAD), lambda b, hp, i: (b * nq + i, hp)),
            pl.BlockSpec((seq, 2 * HEAD_PAD), lambda b, hp, i: (b, hp)),
            pl.BlockSpec((seq // tk, 2 * MLA_V, tk), lambda b, hp, i: (b, hp, 0)),
            pl.BlockSpec((k_meta.shape[0], 2 * HEAD_PAD), lambda b, hp, i: (0, hp)),
            pl.BlockSpec((2 * MLA_V, vt_meta.shape[1]), lambda b, hp, i: (hp, 0)),
        ],
        out_specs=pl.BlockSpec((tq, 2 * MLA_V), lambda b, hp, i: (b * nq + i, hp)),
        out_shape=jax.ShapeDtypeStruct((batch * seq, V_W), BF16),
        scratch_shapes=[pltpu.VMEM((4, 2, tk, tq), F32)],
        compiler_params=pltpu.CompilerParams(
            dimension_semantics=("arbitrary", "arbitrary", "arbitrary"),
            vmem_limit_bytes=VMEM_LIMIT),
        name="mla_attn",
    )(q, k, vt, k_meta, vt_meta)


LEVELS = (1, 2, 4, 8, 16, 32)
GDN_CONST_NAMES = ("expand_b", "expand_a", "ltri", "eye_t", "tril_t", "stril_t", "level_masks", "bd_ones")


def _gdn_constants(cps):
    i = np.arange(CHUNK)[:, None]
    lane = np.arange(GDN_W)[None, :]
    j = lane % GDN_DK
    c = {}
    r = np.arange(LANES)[:, None]
    c["expand_b"] = (r == lane // GDN_DK)
    c["expand_a"] = (r == GDN_HEADS + lane // GDN_DK)
    t = np.arange(cps * CHUNK)
    c["ltri"] = (t[:, None] >= t[None, :]) & (t[:, None] // CHUNK == t[None, :] // CHUNK)
    c["eye_t"] = (i == j)
    c["tril_t"] = (i >= j)
    c["stril_t"] = (i > j)
    jg = j[:, :GROUP_W]
    c["level_masks"] = np.stack([
        ((i // (2 * s) == jg // (2 * s)) & ((i // s) % 2 == 1) & ((jg // s) % 2 == 0))
        for s in LEVELS])
    g = np.arange(GROUP_W)
    c["bd_ones"] = (g[:, None] // GDN_DK == g[None, :] // GDN_DK)
    bf = ("bd_ones", "expand_b", "expand_a", "ltri")
    return [jnp.asarray(c[k].astype(np.float32), BF16 if k in bf else F32) for k in GDN_CONST_NAMES]


def _block_diag(y, lo_half):
    zeros = jnp.zeros((GDN_DK, LANES), y.dtype)
    blocks = []
    for h in range(GROUP_HEADS):
        t = h // 2
        tile = y[:, t * LANES:(t + 1) * LANES]
        piece = jnp.where(lo_half, tile, 0) if h % 2 == 0 else jnp.where(lo_half, 0, tile)
        blocks.append(jnp.concatenate([piece, zeros] if t == 0 else [zeros, piece], axis=1))
    return jnp.concatenate(blocks, axis=0)


def _head_matmul(x, y, lo_half):
    return _dot(x.astype(BF16), _block_diag(y.astype(BF16), lo_half))


def _diag_blocks(m, lo_half):
    tiles = []
    for t in range(m.shape[1] // LANES):
        h = 2 * (t % 2)
        cols = slice(t * LANES, (t + 1) * LANES)
        tiles.append(jnp.where(lo_half, m[h * GDN_DK:(h + 1) * GDN_DK, cols],
                               m[(h + 1) * GDN_DK:(h + 2) * GDN_DK, cols]))
    return jnp.concatenate(tiles, axis=1)


def _gdn_kernel(gx_ref, ba_ref, z_ref, s0_ref, arate_ref, dtb_ref, gnorm_ref,
                expand_b_ref, expand_a_ref, ltri_ref, eye_t_ref, tril_t_ref, stril_t_ref,
                level_masks_ref, bd_ones_ref,
                o_ref, sout_ref, state_ref, *, cps, front_pad):
    c = pl.program_id(1)
    rows = cps * CHUNK

    @pl.when(c == 0)
    def _():
        state_ref[...] = s0_ref[...]

    qn = gx_ref[:, :GDN_W]
    kn = gx_ref[:, GDN_W:2 * GDN_W]
    v = gx_ref[:, 2 * GDN_W:]

    ba = ba_ref[...]
    beta = jax.nn.sigmoid(ba)
    sp_in = ba + dtb_ref[...]
    g = arate_ref[...] * (jnp.maximum(sp_in, 0.0) + jnp.log1p(jnp.exp(-jnp.abs(sp_in))))
    if front_pad:
        valid = lax.broadcasted_iota(jnp.int32, (rows, 1), 0) >= front_pad
        beta = jnp.where(valid, beta, 0.0)
        g = jnp.where(valid, g, 0.0)
    beta_e = _dot_exact_rhs(beta, expand_b_ref[...], 2)
    gc = _dot_exact_lhs(ltri_ref[...], g, 3)
    gc_e = _dot_exact_rhs(gc, expand_a_ref[...], 3)
    egc = jnp.exp(gc_e)
    kb = kn * beta_e
    vb = v * beta_e
    kbg = kb * egc
    qg = qn * egc
    eye, tril, stril = eye_t_ref[...], tril_t_ref[...], stril_t_ref[...]
    lo_half = lax.broadcasted_iota(jnp.int32, (1, LANES), 1) < GDN_DK

    chains = [(j, gi) for j in range(cps) for gi in range(N_GROUPS)]
    rsl = lambda j: slice(j * CHUNK, (j + 1) * CHUNK)
    gsl = lambda gi: slice(gi * GROUP_W, (gi + 1) * GROUP_W)
    decay, kdec, s_decay = [], [], []
    for j in range(cps):
        gce = gc_e[rsl(j)]
        gc_t = jnp.sum(gce * eye, axis=0, keepdims=True)
        decay.append(jnp.exp(jnp.where(tril > 0.5, gce - gc_t, -jnp.inf)))
        g_last = gce[CHUNK - 1:CHUNK]
        kdec.append(kn[rsl(j)] * jnp.exp(g_last - gce))
        s_decay.append(jnp.exp(g_last))

    lmat, amat = [], []
    for j, gi in chains:
        rs, gs = rsl(j), gsl(gi)
        k4 = kn[rs, gs].astype(BF16)
        kstack = _block_diag(k4, lo_half)
        lhs = jnp.concatenate([kb[rs, gs], qn[rs, gs]], axis=0).astype(BF16)
        sc = _dot_nt(lhs, kstack)
        lmat.append(sc[:CHUNK] * decay[j][:, gs] * stril[:, gs])
        amat.append(sc[CHUNK:] * decay[j][:, gs])

    xinv = [eye[:, gsl(gi)] - lm * level_masks_ref[0] for (j, gi), lm in zip(chains, lmat)]
    for li in range(1, len(LEVELS)):
        xo = [_head_matmul(x, lm * level_masks_ref[li], lo_half) for x, lm in zip(xinv, lmat)]
        xinv = [x - _head_matmul(y, x, lo_half) for x, y in zip(xinv, xo)]

    uw = []
    for (j, gi), x in zip(chains, xinv):
        rs, gs = rsl(j), gsl(gi)
        rhs = jnp.concatenate([_block_diag(vb[rs, gs].astype(BF16), lo_half),
                               _block_diag(kbg[rs, gs].astype(BF16), lo_half)], axis=1)
        uw.append(_dot(x.astype(BF16), rhs))
    kwu = [_diag_blocks(_dot(kdec[j][:, gsl(gi)].T.astype(BF16), m.astype(BF16)), lo_half)
           for (j, gi), m in zip(chains, uw)]

    states = [state_ref[gi] for gi in range(N_GROUPS)]
    o_rows = [[None] * N_GROUPS for _ in range(cps)]
    for ci, (j, gi) in enumerate(chains):
        rs, gs = rsl(j), gsl(gi)
        u, wmat = uw[ci][:, :GROUP_W], uw[ci][:, GROUP_W:]
        ku, kw = kwu[ci][:, :GROUP_W], kwu[ci][:, GROUP_W:]
        state = states[gi]
        lhs = jnp.concatenate([kw, wmat, qg[rs, gs]], axis=0).astype(BF16)
        big = _dot(lhs, _block_diag(state.astype(BF16), lo_half))
        states[gi] = s_decay[j][:, gs] * state - big[:CHUNK] + ku
        v_new = u - big[CHUNK:2 * CHUNK]
        o_rows[j][gi] = big[2 * CHUNK:] + _head_matmul(amat[ci], v_new, lo_half)
    for gi in range(N_GROUPS):
        state_ref[gi] = states[gi]

    @pl.when(c == pl.num_programs(1) - 1)
    def _():
        sout_ref[...] = state_ref[...]

    o = jnp.concatenate([jnp.concatenate(r, axis=1) for r in o_rows], axis=0)
    ms = _head_sums(o * o, bd_ones_ref[...]) * (1.0 / GDN_DV)
    o = o * lax.rsqrt(ms + EPS) * gnorm_ref[...] * _silu(z_ref[...])
    o_ref[...] = o.astype(BF16)


def _gdn(gqkv, gba, z, state0, params, batch, seq, cps, front_pad):
    rows = cps * CHUNK
    steps = seq // rows
    consts = _gdn_constants(cps)
    xrow = lambda w: pl.BlockSpec((rows, w), lambda b, c: (b * steps + c, 0))
    state_shape = (N_GROUPS, GDN_DK, GROUP_W)
    return pl.pallas_call(
        functools.partial(_gdn_kernel, cps=cps, front_pad=front_pad),
        grid=(batch, steps),
        in_specs=[xrow(3 * GDN_W), xrow(LANES), xrow(GDN_W), _const_spec(state_shape)]
                 + [_const_spec(a.shape) for a in params]
                 + [_const_spec(a.shape) for a in consts],
        out_specs=[xrow(GDN_W), pl.BlockSpec(state_shape, lambda b, c: (0, 0, 0))],
        out_shape=[jax.ShapeDtypeStruct((batch * seq, GDN_W), BF16),
                   jax.ShapeDtypeStruct(state_shape, F32)],
        scratch_shapes=[pltpu.VMEM(state_shape, F32)],
        compiler_params=pltpu.CompilerParams(dimension_semantics=("arbitrary", "arbitrary"),
                                             vmem_limit_bytes=VMEM_LIMIT),
        name="gdn_chunk",
    )(gqkv, gba, z, state0, *params, *consts)


def _merge_ffn_kernel(h1_ref, om_ref, og_ref, nm_ref, wraw_ref, wmo_ref, wgo_ref, wout_ref,
                      n2_ref, wg_ref, wu_ref, wd_ref, nf_ref, out_ref, wgate_ref):
    @pl.when(pl.program_id(0) == 0)
    def _():
        wgate_ref[...] = wraw_ref[:, OFF_GATES - W_GATE_START:OFF_GATES - W_GATE_START + 2 * D_MODEL]

    h1 = h1_ref[...]
    un = _rms(h1, nm_ref[...]).astype(BF16)
    gates = jax.nn.sigmoid(_dot(un, wgate_ref[...]))
    merged = (gates[:, :D_MODEL] * _dot(om_ref[...], wmo_ref[...])
              + gates[:, D_MODEL:] * _dot(og_ref[...], wgo_ref[...]))
    h2 = h1 + _dot(merged.astype(BF16), wout_ref[...])
    xn = _rms(h2, n2_ref[...]).astype(BF16)
    act = (_silu(_dot(xn, wg_ref[...])) * _dot(xn, wu_ref[...])).astype(BF16)
    h3 = h2 + 0.5 * _dot(act, wd_ref[...])
    out_ref[...] = _rms(h3, nf_ref[...])


def _merge_ffn(h1, o_mla, o_gdn, weights, tm):
    n = h1.shape[0]
    row = lambda w: pl.BlockSpec((tm, w), lambda i: (i, 0))
    return pl.pallas_call(
        _merge_ffn_kernel,
        grid=(n // tm,),
        in_specs=[row(D_MODEL), row(V_W), row(GDN_W)] + [_const_spec(w.shape) for w in weights],
        out_specs=row(D_MODEL),
        out_shape=jax.ShapeDtypeStruct((n, D_MODEL), F32),
        scratch_shapes=[pltpu.VMEM((D_MODEL, 2 * D_MODEL), BF16)],
        compiler_params=pltpu.CompilerParams(dimension_semantics=("arbitrary",),
                                             vmem_limit_bytes=VMEM_LIMIT),
        name="merge_ffn",
    )(h1, o_mla, o_gdn, *weights)


def _rope_tables(first_pos, n):
    pos = np.arange(first_pos, first_pos + n, dtype=np.float64)
    inv = ROPE_THETA ** (-np.arange(0, MLA_ROPE, 2, dtype=np.float64) / MLA_ROPE)
    ang = pos[:, None] * inv[None, :]
    cos2 = np.tile(np.cos(ang), (1, 2))
    sin2 = np.tile(np.sin(ang), (1, 2))
    scale = (MLA_NOPE + MLA_ROPE) ** -0.5 * math.log2(math.e)
    pad = np.zeros((n, HEAD_PAD - MLA_NOPE - MLA_ROPE))
    tab_c = np.concatenate([np.ones((n, MLA_NOPE)), cos2, pad], axis=1) * scale
    tab_s = np.concatenate([np.zeros((n, MLA_NOPE)), sin2, pad], axis=1) * scale
    kpad = np.zeros((n, LANES - MLA_ROPE))
    tab_kc = np.concatenate([cos2, kpad], axis=1)
    tab_ks = np.concatenate([sin2, kpad], axis=1)
    return tuple(jnp.asarray(t, F32) for t in (tab_c, tab_s, tab_kc, tab_ks))


def _rot(w):
    half = MLA_ROPE // 2
    return jnp.concatenate([-w[..., half:], w[..., :half]], axis=-1)


def kernel(x, meta_tokens, ffn1_norm, ffn1_w_gate, ffn1_w_up, ffn1_w_down, mix_norm, w_in, q_norm, w_uq,
           kv_norm, w_ukv, w_mla_o, conv_w, a_log, dt_bias, gdn_norm, w_gdn_o, w_out, ffn2_norm,
           ffn2_w_gate, ffn2_w_up, ffn2_w_down, final_norm):
    assert ffn1_norm.shape[0] == 1, "single-layer block"
    batch, seq, d = x.shape
    assert d == D_MODEL and seq % CHUNK == 0
    tm = min(256, seq)
    tq = min(512, seq)
    assert seq % tm == 0 and seq % tq == 0

    assert w_in.shape[2] == D_IN
    w_in_bf = jnp.pad(w_in[0].astype(BF16), ((0, 0), (0, D_IN_PAD - D_IN)))
    win_g = w_in_bf[:, W_GATE_START:]

    wq = w_uq[0].reshape(MLA_Q_RANK, MLA_HEADS, MLA_NOPE + MLA_ROPE)
    wq_nope, wq_rope = wq[..., :MLA_NOPE], wq[..., MLA_NOPE:]
    zq = lambda n: jnp.zeros((MLA_Q_RANK, MLA_HEADS, n), F32)
    tail = HEAD_PAD - MLA_NOPE - MLA_ROPE
    wqa = jnp.concatenate([wq_nope, wq_rope, zq(tail)], axis=-1).reshape(MLA_Q_RANK, QK_W).astype(BF16)
    wqb = jnp.concatenate([zq(MLA_NOPE), _rot(wq_rope), zq(tail)], axis=-1).reshape(MLA_Q_RANK, QK_W).astype(BF16)
    wkv = w_ukv[0].reshape(MLA_KV_RANK, MLA_HEADS, MLA_NOPE + MLA_V)
    wk = jnp.concatenate([wkv[..., :MLA_NOPE], jnp.zeros((MLA_KV_RANK, MLA_HEADS, HEAD_PAD - MLA_NOPE), F32)],
                         axis=-1).reshape(MLA_KV_RANK, QK_W).astype(BF16)
    wvt = wkv[..., MLA_NOPE:].reshape(MLA_KV_RANK, V_W).T.astype(BF16)
    e_np = np.zeros((2 * LANES, QK_W), np.float32)
    half = MLA_ROPE // 2
    for h in range(MLA_HEADS):
        base = h * HEAD_PAD + MLA_NOPE
        for j in range(MLA_ROPE):
            e_np[j, base + j] = 1.0
        for j in range(half):
            e_np[LANES + half + j, base + j] = -1.0
            e_np[LANES + j, base + half + j] = 1.0
    e_mat = jnp.asarray(e_np, BF16)

    g = np.arange(GROUP_W)
    bd_ones = jnp.asarray((g[:, None] // GDN_DK == g[None, :] // GDN_DK).astype(np.float32), BF16)
    proj_weights = [
        ffn1_norm[0][None], ffn1_w_gate[0].astype(BF16), ffn1_w_up[0].astype(BF16),
        ffn1_w_down[0].astype(BF16), mix_norm[0][None], w_in_bf, q_norm[0][None], wqa, wqb,
        kv_norm[0][None], wk, wvt, e_mat, conv_w[0].astype(F32), bd_ones]

    zero_carry = jnp.zeros((8, 3 * GDN_W), F32)
    _, _, k_m, vt_m, gqkv_m, gba_m, _, meta_tail = _token_proj(
        meta_tokens.astype(F32), _rope_tables(0, N_META), zero_carry, proj_weights, N_META, 1)
    h1, q, k, vt, gqkv, gba, z, _ = _token_proj(
        x.reshape(batch * seq, d), _rope_tables(N_META, seq), meta_tail, proj_weights, tm, seq // tm)

    pad_rows = lambda a, n, front: jnp.pad(a, ((n - a.shape[0], 0) if front else (0, n - a.shape[0]), (0, 0)))
    vt_meta = jnp.pad(vt_m[0], ((0, 0), (0, LANES - N_META)))
    o_mla = _mla_attn(q, k, vt, pad_rows(k_m, LANES, False), vt_meta, batch, seq, tq, tm)

    hpad = lambda a: jnp.zeros((1, LANES), F32).at[0, GDN_HEADS:2 * GDN_HEADS].set(a)
    arate = hpad(-jnp.exp(a_log[0].astype(F32)))
    dtb = hpad(dt_bias[0].astype(F32))
    gnorm = jnp.tile(gdn_norm[0].astype(F32), GDN_HEADS)[None]
    gdn_params = (arate, dtb, gnorm)
    cps = 4 if seq % (4 * CHUNK) == 0 else 1
    _, state_meta = _gdn(pad_rows(gqkv_m, CHUNK, True), pad_rows(gba_m, CHUNK, True),
                         jnp.zeros((CHUNK, GDN_W), F32), jnp.zeros((N_GROUPS, GDN_DK, GROUP_W), F32),
                         gdn_params, 1, CHUNK, 1, CHUNK - N_META)
    o_gdn, _ = _gdn(gqkv, gba, z, state_meta, gdn_params, batch, seq, cps, 0)

    merge_weights = [
        mix_norm[0][None], win_g, w_mla_o[0].astype(BF16), w_gdn_o[0].astype(BF16), w_out[0].astype(BF16),
        ffn2_norm[0][None], ffn2_w_gate[0].astype(BF16), ffn2_w_up[0].astype(BF16),
        ffn2_w_down[0].astype(BF16), final_norm[None]]
    out = _merge_ffn(h1, o_mla, o_gdn, merge_weights, tm)
    return out.reshape(batch, seq, d)
```

```python
import functools
import math

import jax
import jax.numpy as jnp
import numpy as np
from jax import lax
from jax.experimental import pallas as pl
from jax.experimental.pallas import tpu as pltpu

F32 = jnp.float32
BF16 = jnp.bfloat16

D_MODEL = 1024
N_META = 16
EPS = 1e-6
D_FF = 2816
MLA_HEADS = 8
MLA_Q_RANK = 256
MLA_KV_RANK = 128
MLA_NOPE = 64
MLA_ROPE = 32
MLA_V = 64
ROPE_THETA = 10000.0
GDN_HEADS = 8
GDN_DK = 64
GDN_DV = 64
CONV_K = 4
CHUNK = 64

LANES = 128
HEAD_PAD = 128
QK_W = MLA_HEADS * HEAD_PAD
V_W = MLA_HEADS * MLA_V
GDN_W = GDN_HEADS * GDN_DK
GROUP_HEADS = 4
GROUP_W = GROUP_HEADS * GDN_DK
N_GROUPS = GDN_HEADS // GROUP_HEADS
FFN_CHUNK = 256
PROJ_W = MLA_Q_RANK + MLA_KV_RANK + LANES + 3 * GDN_W + LANES + GDN_W
W_IN_SPLITS = (MLA_Q_RANK, MLA_KV_RANK, MLA_ROPE, GDN_W, GDN_W, GDN_W, GDN_HEADS, GDN_HEADS, GDN_W,
               D_MODEL, D_MODEL)
(OFF_CQ, OFF_CKV, OFF_KR, OFF_GQ, OFF_GK, OFF_GV, OFF_GB, OFF_GA, OFF_GZ, OFF_GATES, _OFF_GATE_GDN,
 D_IN) = (int(v) for v in np.concatenate([[0], np.cumsum(W_IN_SPLITS)]))
W_TOKEN_COLS = -(-OFF_GATES // LANES) * LANES
W_GATE_START = OFF_GATES // LANES * LANES
VMEM_LIMIT = 56 * 1024 * 1024
NEG_BIG = -1e30


def _const_spec(shape):
    zeros = (0,) * len(shape)
    return pl.BlockSpec(shape, lambda *_: zeros, pipeline_mode=pl.Buffered(1))


def _rms(x, w):
    return x * lax.rsqrt(jnp.mean(x * x, axis=-1, keepdims=True) + EPS) * w


def _dot(a, b):
    return jnp.dot(a, b, preferred_element_type=F32)


def _dot_nt(a, b):
    return lax.dot_general(a, b, (((1,), (1,)), ((), ())), preferred_element_type=F32)


def _silu(x):
    return x * jax.nn.sigmoid(x)


def _split(x, n):
    pieces = []
    for _ in range(n - 1):
        hi = x.astype(BF16)
        pieces.append(hi)
        x = x - hi.astype(F32)
    pieces.append(x.astype(BF16))
    return pieces


def _dot_exact_rhs(x, rhs, n):
    return sum(_dot(p, rhs) for p in _split(x, n))


def _dot_exact_lhs(lhs, x, n):
    return sum(_dot(lhs, p) for p in _split(x, n))


def _head_sums(x, bd_ones):
    return jnp.concatenate([_dot(x[:, g * GROUP_W:(g + 1) * GROUP_W].astype(BF16), bd_ones)
                            for g in range(N_GROUPS)], axis=1)


def _token_proj_kernel(x_ref, tc_ref, ts_ref, tkc_ref, tks_ref, c0_ref, n1_ref, wg_ref, wu_ref, wd_ref, nm_ref,
                       wraw_ref, qn_ref, wqa_ref, wqb_ref, kvn_ref, wk_ref, wvt_ref, e_ref, convw_ref,
                       bd_ones_ref,
                       h1_ref, q_ref, k_ref, vt_ref, gqkv_ref, gba_ref, z_ref, tail_ref, xe_ref, win_ref,
                       *, tiles_per_seq):
    tm = x_ref.shape[0]
    step = pl.program_id(0)

    @pl.when(step == 0)
    def _():
        xe_ref[...] = jnp.zeros_like(xe_ref)
        o = OFF_GQ + LANES - MLA_ROPE
        win_ref[:, :o] = wraw_ref[:, :o]
        win_ref[:, o:o + 3 * GDN_W] = wraw_ref[:, OFF_GQ:OFF_GB]
        o += 3 * GDN_W
        lane = lax.broadcasted_iota(jnp.int32, (1, LANES), 1)
        win_ref[:, o:o + LANES] = jnp.where(lane < 2 * GDN_HEADS, wraw_ref[:, OFF_GB:OFF_GB + LANES], 0)
        win_ref[:, o + LANES:] = wraw_ref[:, OFF_GZ:OFF_GATES]

    x = x_ref[...]
    xn = _rms(x, n1_ref[...]).astype(BF16)

    n_ffn = D_FF // FFN_CHUNK
    n_conv = 3 * GDN_W // LANES
    acts = []
    for c in range(max(n_ffn, n_conv)):
        if c < n_ffn:
            cols = slice(c * FFN_CHUNK, (c + 1) * FFN_CHUNK)
            acts.append((_silu(_dot(xn, wg_ref[:, cols])) * _dot(xn, wu_ref[:, cols])).astype(BF16))
        if c < n_conv:
            lanes = slice(c * LANES, (c + 1) * LANES)
            y = sum(convw_ref[t:t + 1, lanes] * xe_ref[pl.ds(8 - (CONV_K - 1) + t, tm), lanes]
                    for t in range(CONV_K))
            gqkv_ref[:, lanes] = _silu(y)
    h1 = x + 0.5 * _dot(jnp.concatenate(acts, axis=1), wd_ref[...])
    h1_ref[...] = h1

    un = _rms(h1, nm_ref[...]).astype(BF16)
    p = _dot(un, win_ref[...])

    bd_ones = bd_ones_ref[...]
    gq = gqkv_ref[:, :GDN_W]
    gk = gqkv_ref[:, GDN_W:2 * GDN_W]
    gqkv_ref[:, :GDN_W] = gq * lax.rsqrt(_head_sums(gq * gq, bd_ones) + EPS) * (GDN_DK ** -0.5)
    gqkv_ref[:, GDN_W:2 * GDN_W] = gk * lax.rsqrt(_head_sums(gk * gk, bd_ones) + EPS)
    o = 0
    cq = p[:, o:o + MLA_Q_RANK]; o += MLA_Q_RANK
    ckv = p[:, o:o + MLA_KV_RANK]; o += MLA_KV_RANK
    kr = p[:, o:o + LANES]; o += LANES
    gqkv_raw = p[:, o:o + 3 * GDN_W]; o += 3 * GDN_W
    gba_ref[...] = p[:, o:o + LANES]; o += LANES
    z_ref[...] = p[:, o:o + GDN_W]

    xe_ref[0:8] = jnp.where(step % tiles_per_seq == 0, c0_ref[...], xe_ref[tm:tm + 8])
    xe_ref[8:8 + tm] = gqkv_raw
    tail_ref[...] = gqkv_raw[tm - 8:]

    cqn = _rms(cq, qn_ref[...]).astype(BF16)
    tc = jnp.concatenate([tc_ref[...]] * MLA_HEADS, axis=1)
    ts = jnp.concatenate([ts_ref[...]] * MLA_HEADS, axis=1)
    q = _dot(cqn, wqa_ref[...]) * tc + _dot(cqn, wqb_ref[...]) * ts
    q_ref[...] = q.astype(BF16)

    ckvn = _rms(ckv, kvn_ref[...]).astype(BF16)
    kr_terms = jnp.concatenate([kr * tkc_ref[...], kr * tks_ref[...]], axis=1).astype(BF16)
    k = _dot(ckvn, wk_ref[...]) + _dot(kr_terms, e_ref[...])
    k_ref[...] = k.astype(BF16)
    vt_ref[0] = _dot_nt(wvt_ref[...], ckvn).astype(BF16)


def _token_proj(x2d, tabs, conv_carry, weights, tm, tiles_per_seq):
    n = x2d.shape[0]
    assert n % tm == 0
    last = n // tm - 1
    tile = lambda i: jnp.minimum(i, last)
    row = lambda w: pl.BlockSpec((tm, w), lambda i: (tile(i), 0))
    tab = pl.BlockSpec((tm, LANES), lambda i: (tile(i) % tiles_per_seq, 0))
    out_widths = (D_MODEL, QK_W, QK_W, None, 3 * GDN_W, LANES, GDN_W)
    out_dtypes = (F32, BF16, BF16, BF16, F32, F32, F32)
    vt_spec = pl.BlockSpec((1, V_W, tm), lambda i: (tile(i), 0, 0))
    conv_spec = pl.BlockSpec((tm, 3 * GDN_W), lambda i: (jnp.maximum(i - 1, 0), 0))
    tail_shape = (8, 3 * GDN_W)
    out_specs = [vt_spec if w is None else row(w) for w in out_widths]
    out_specs[4] = conv_spec
    return pl.pallas_call(
        functools.partial(_token_proj_kernel, tiles_per_seq=tiles_per_seq),
        grid=(n // tm + 1,),
        in_specs=[row(D_MODEL)] + [tab] * len(tabs) + [_const_spec(tail_shape)]
                 + [_const_spec((D_MODEL, W_TOKEN_COLS) if w.shape == (D_MODEL, D_IN) else w.shape)
                    for w in weights],
        out_specs=out_specs + [pl.BlockSpec(tail_shape, lambda i: (0, 0))],
        out_shape=[jax.ShapeDtypeStruct((n // tm, V_W, tm) if w is None else (n, w), d)
                   for w, d in zip(out_widths, out_dtypes)] + [jax.ShapeDtypeStruct(tail_shape, F32)],
        scratch_shapes=[pltpu.VMEM((8 + tm, 3 * GDN_W), F32), pltpu.VMEM((D_MODEL, PROJ_W), BF16)],
        compiler_params=pltpu.CompilerParams(dimension_semantics=("arbitrary",),
                                             vmem_limit_bytes=VMEM_LIMIT),
        name="token_proj",
    )(x2d, *tabs, conv_carry, *weights)


def _mla_kernel(q_ref, k_ref, vt_ref, km_ref, vmt_ref, o_ref, st_ref, *, tq, tk):
    def tile(qi, carry):
        rows = pl.ds(pl.multiple_of(qi * tq, tq), tq)
        _mla_tile(qi, q_ref.at[rows], k_ref, vt_ref, km_ref, vmt_ref, o_ref.at[rows], st_ref, tq=tq, tk=tk)
        return carry

    lax.fori_loop(0, q_ref.shape[0] // tq, tile, 0)


def _mla_tile(qi, q_ref, k_ref, vt_ref, km_ref, vmt_ref, o_ref, st_ref, *, tq, tk):
    heads = range(2)
    hs = [slice(h * HEAD_PAD, (h + 1) * HEAD_PAD) for h in heads]
    vs = [slice(h * MLA_V, (h + 1) * MLA_V) for h in heads]
    q = [q_ref[:, s] for s in hs]
    colmax = lambda s: jnp.max(s, axis=0, keepdims=True)

    def with_ones(vt):
        return jnp.concatenate([vt, jnp.ones((8, vt.shape[1]), BF16)], axis=0)

    def scores(ki, slot, q_from=0):
        rows = pl.ds(pl.multiple_of(ki * tk, tk), tk)
        block_max = []
        for h in heads:
            s = _dot_nt(k_ref[rows, hs[h]], q[h][q_from:])
            st_ref[slot, h, :, q_from:] = s
            block_max.append(colmax(s))
        return block_max

    def update(ki, slot, m, acc, block_max, mask, q_from=0):
        st = [st_ref[slot, h, :, q_from:] for h in heads]
        if mask is not None:
            st = [jnp.where(mask, s, NEG_BIG) for s in st]
            block_max = [colmax(s) for s in st]
        m_old = [x[:, q_from:] for x in m]
        m_new = [jnp.maximum(m_old[h], block_max[h]) for h in heads]
        p = [jnp.exp2(st[h] - m_new[h]).astype(BF16) for h in heads]
        vt = vt_ref[ki]
        acc_new = [jnp.exp2(m_old[h] - m_new[h]) * acc[h][:, q_from:] + _dot(with_ones(vt[vs[h]]), p[h])
                   for h in heads]
        if q_from:
            m_new = [jnp.concatenate([m[h][:, :q_from], m_new[h]], axis=1) for h in heads]
            acc_new = [jnp.concatenate([acc[h][:, :q_from], acc_new[h]], axis=1) for h in heads]
        return m_new, acc_new

    bm_a = scores(0, 0)
    bm_b = scores(1, 1)
    meta_valid = lax.broadcasted_iota(jnp.int32, (km_ref.shape[0], tq), 0) < N_META
    st = [jnp.where(meta_valid, _dot_nt(km_ref[:, hs[h]], q[h]), NEG_BIG) for h in heads]
    m = [colmax(s) for s in st]
    acc = [_dot(with_ones(vmt_ref[vs[h], :]), jnp.exp2(st[h] - m[h]).astype(BF16)) for h in heads]

    def stage(ka, carry, cur, nxt):
        m, acc, bm_a, bm_b = carry
        bm_a2 = scores(ka + 2, nxt[0])
        m, acc = update(ka, cur[0], m, acc, bm_a, None)
        bm_b2 = scores(ka + 3, nxt[1])
        m, acc = update(ka + 1, cur[1], m, acc, bm_b, None)
        return m, acc, bm_a2, bm_b2

    def body(i, carry):
        return stage(4 * i + 2, stage(4 * i, carry, (0, 1), (2, 3)), (2, 3), (0, 1))

    causal = (lax.broadcasted_iota(jnp.int32, (tk, tq), 1) >= lax.broadcasted_iota(jnp.int32, (tk, tq), 0))
    n_full = 2 * qi

    def finish(carry, cur):
        m, acc, _, _ = carry
        m, acc = update(n_full, cur[0], m, acc, None, causal)
        m, acc = update(n_full + 1, cur[1], m, acc, None, causal[:, :tk], q_from=tk)
        return acc

    carry = lax.fori_loop(0, qi // 2, body, (m, acc, bm_a, bm_b))
    acc = lax.cond(qi % 2 == 1,
                   lambda c: finish(stage(n_full - 2, c, (0, 1), (2, 3)), (2, 3)),
                   lambda c: finish(c, (0, 1)), carry)
    o_ref[...] = jnp.concatenate([(a[:MLA_V] * (1.0 / a[MLA_V:MLA_V + 1])).T for a in acc],
                                 axis=1).astype(BF16)


def _mla_attn(q, k, vt, k_meta, vt_meta, batch, seq, tq, tk):
    nq = seq // tq
    assert vt.shape[2] == tk and tq == 2 * tk
    kern = functools.partial(_mla_kernel, tq=tq, tk=tk)
    return pl.pallas_call(
        kern,
        grid=(batch, MLA_HEADS // 2),
        in_specs=[
            pl.BlockSpec((seq, 2 * HEAD_PAD), lambda b, hp: (b, hp)),
            pl.BlockSpec((seq, 2 * HEAD_PAD), lambda b, hp: (b, hp)),
            pl.BlockSpec((seq // tk, 2 * MLA_V, tk), lambda b, hp: (b, hp, 0)),
            pl.BlockSpec((k_meta.shape[0], 2 * HEAD_PAD), lambda b, hp: (0, hp)),
            pl.BlockSpec((2 * MLA_V, vt_meta.shape[1]), lambda b, hp: (hp, 0)),
        ],
        out_specs=pl.BlockSpec((seq, 2 * MLA_V), lambda b, hp: (b, hp)),
        out_shape=jax.ShapeDtypeStruct((batch * seq, V_W), BF16),
        scratch_shapes=[pltpu.VMEM((4, 2, tk, tq), F32)],
        compiler_params=pltpu.CompilerParams(
            dimension_semantics=("arbitrary", "arbitrary"),
            vmem_limit_bytes=VMEM_LIMIT),
        name="mla_attn",
    )(q, k, vt, k_meta, vt_meta)


LEVELS = (1, 2, 4, 8, 16, 32)
GDN_CONST_NAMES = ("expand_b", "expand_a", "ltri", "eye_t", "tril_t", "stril_t", "level_masks", "bd_ones")


def _gdn_constants(cps):
    i = np.arange(CHUNK)[:, None]
    lane = np.arange(GDN_W)[None, :]
    j = lane % GDN_DK
    c = {}
    r = np.arange(LANES)[:, None]
    c["expand_b"] = (r == lane // GDN_DK)
    c["expand_a"] = (r == GDN_HEADS + lane // GDN_DK)
    t = np.arange(cps * CHUNK)
    c["ltri"] = (t[:, None] >= t[None, :]) & (t[:, None] // CHUNK == t[None, :] // CHUNK)
    c["eye_t"] = (i == j)
    c["tril_t"] = (i >= j)
    c["stril_t"] = (i > j)
    jg = j[:, :GROUP_W]
    c["level_masks"] = np.stack([
        ((i // (2 * s) == jg // (2 * s)) & ((i // s) % 2 == 1) & ((jg // s) % 2 == 0))
        for s in LEVELS])
    g = np.arange(GROUP_W)
    c["bd_ones"] = (g[:, None] // GDN_DK == g[None, :] // GDN_DK)
    bf = ("bd_ones", "expand_b", "expand_a", "ltri")
    return [jnp.asarray(c[k].astype(np.float32), BF16 if k in bf else F32) for k in GDN_CONST_NAMES]


def _block_diag(y, lo_half):
    zeros = jnp.zeros((GDN_DK, LANES), y.dtype)
    blocks = []
    for h in range(GROUP_HEADS):
        t = h // 2
        tile = y[:, t * LANES:(t + 1) * LANES]
        piece = jnp.where(lo_half, tile, 0) if h % 2 == 0 else jnp.where(lo_half, 0, tile)
        blocks.append(jnp.concatenate([piece, zeros] if t == 0 else [zeros, piece], axis=1))
    return jnp.concatenate(blocks, axis=0)


def _head_matmul(x, y, lo_half):
    return _dot(x.astype(BF16), _block_diag(y.astype(BF16), lo_half))


def _diag_blocks(m, lo_half):
    tiles = []
    for t in range(m.shape[1] // LANES):
        h = 2 * (t % 2)
        cols = slice(t * LANES, (t + 1) * LANES)
        tiles.append(jnp.where(lo_half, m[h * GDN_DK:(h + 1) * GDN_DK, cols],
                               m[(h + 1) * GDN_DK:(h + 2) * GDN_DK, cols]))
    return jnp.concatenate(tiles, axis=1)


def _gdn_kernel(gx_ref, ba_ref, z_ref, s0_ref, arate_ref, dtb_ref, gnorm_ref,
                expand_b_ref, expand_a_ref, ltri_ref, eye_t_ref, tril_t_ref, stril_t_ref,
                level_masks_ref, bd_ones_ref,
                o_ref, sout_ref, state_ref, *, cps, front_pad):
    c = pl.program_id(1)
    rows = cps * CHUNK

    @pl.when(c == 0)
    def _():
        state_ref[...] = s0_ref[...]

    qn = gx_ref[:, :GDN_W]
    kn = gx_ref[:, GDN_W:2 * GDN_W]
    v = gx_ref[:, 2 * GDN_W:]

    ba = ba_ref[...]
    beta = jax.nn.sigmoid(ba)
    sp_in = ba + dtb_ref[...]
    g = arate_ref[...] * (jnp.maximum(sp_in, 0.0) + jnp.log1p(jnp.exp(-jnp.abs(sp_in))))
    if front_pad:
        valid = lax.broadcasted_iota(jnp.int32, (rows, 1), 0) >= front_pad
        beta = jnp.where(valid, beta, 0.0)
        g = jnp.where(valid, g, 0.0)
    beta_e = _dot_exact_rhs(beta, expand_b_ref[...], 2)
    gc = _dot_exact_lhs(ltri_ref[...], g, 3)
    gc_e = _dot_exact_rhs(gc, expand_a_ref[...], 3)
    egc = jnp.exp(gc_e)
    kb = kn * beta_e
    vb = v * beta_e
    kbg = kb * egc
    qg = qn * egc
    eye, tril, stril = eye_t_ref[...], tril_t_ref[...], stril_t_ref[...]
    lo_half = lax.broadcasted_iota(jnp.int32, (1, LANES), 1) < GDN_DK

    chains = [(j, gi) for j in range(cps) for gi in range(N_GROUPS)]
    rsl = lambda j: slice(j * CHUNK, (j + 1) * CHUNK)
    gsl = lambda gi: slice(gi * GROUP_W, (gi + 1) * GROUP_W)
    decay, kdec, s_decay = [], [], []
    for j in range(cps):
        gce = gc_e[rsl(j)]
        gc_t = jnp.sum(gce * eye, axis=0, keepdims=True)
        decay.append(jnp.exp(jnp.where(tril > 0.5, gce - gc_t, -jnp.inf)))
        g_last = gce[CHUNK - 1:CHUNK]
        kdec.append(kn[rsl(j)] * jnp.exp(g_last - gce))
        s_decay.append(jnp.exp(g_last))

    lmat, amat = [], []
    for j, gi in chains:
        rs, gs = rsl(j), gsl(gi)
        k4 = kn[rs, gs].astype(BF16)
        kstack = _block_diag(k4, lo_half)
        lhs = jnp.concatenate([kb[rs, gs], qn[rs, gs]], axis=0).astype(BF16)
        sc = _dot_nt(lhs, kstack)
        lmat.append(sc[:CHUNK] * decay[j][:, gs] * stril[:, gs])
        amat.append(sc[CHUNK:] * decay[j][:, gs])

    xinv = [eye[:, gsl(gi)] - lm * level_masks_ref[0] for (j, gi), lm in zip(chains, lmat)]
    for li in range(1, len(LEVELS)):
        xo = [_head_matmul(x, lm * level_masks_ref[li], lo_half) for x, lm in zip(xinv, lmat)]
        xinv = [x - _head_matmul(y, x, lo_half) for x, y in zip(xinv, xo)]

    uw = []
    for (j, gi), x in zip(chains, xinv):
        rs, gs = rsl(j), gsl(gi)
        rhs = jnp.concatenate([_block_diag(vb[rs, gs].astype(BF16), lo_half),
                               _block_diag(kbg[rs, gs].astype(BF16), lo_half)], axis=1)
        uw.append(_dot(x.astype(BF16), rhs))
    kwu = [_diag_blocks(_dot(kdec[j][:, gsl(gi)].T.astype(BF16), m.astype(BF16)), lo_half)
           for (j, gi), m in zip(chains, uw)]

    states = [state_ref[gi] for gi in range(N_GROUPS)]
    o_rows = [[None] * N_GROUPS for _ in range(cps)]
    for ci, (j, gi) in enumerate(chains):
        rs, gs = rsl(j), gsl(gi)
        u, wmat = uw[ci][:, :GROUP_W], uw[ci][:, GROUP_W:]
        ku, kw = kwu[ci][:, :GROUP_W], kwu[ci][:, GROUP_W:]
        state = states[gi]
        lhs = jnp.concatenate([kw, wmat, qg[rs, gs]], axis=0).astype(BF16)
        big = _dot(lhs, _block_diag(state.astype(BF16), lo_half))
        states[gi] = s_decay[j][:, gs] * state - big[:CHUNK] + ku
        v_new = u - big[CHUNK:2 * CHUNK]
        o_rows[j][gi] = big[2 * CHUNK:] + _head_matmul(amat[ci], v_new, lo_half)
    for gi in range(N_GROUPS):
        state_ref[gi] = states[gi]

    @pl.when(c == pl.num_programs(1) - 1)
    def _():
        sout_ref[...] = state_ref[...]

    o = jnp.concatenate([jnp.concatenate(r, axis=1) for r in o_rows], axis=0)
    ms = _head_sums(o * o, bd_ones_ref[...]) * (1.0 / GDN_DV)
    o = o * lax.rsqrt(ms + EPS) * gnorm_ref[...] * _silu(z_ref[...])
    o_ref[...] = o.astype(BF16)


def _gdn(gqkv, gba, z, state0, params, batch, seq, cps, front_pad):
    rows = cps * CHUNK
    steps = seq // rows
    consts = _gdn_constants(cps)
    xrow = lambda w: pl.BlockSpec((rows, w), lambda b, c: (b * steps + c, 0))
    state_shape = (N_GROUPS, GDN_DK, GROUP_W)
    return pl.pallas_call(
        functools.partial(_gdn_kernel, cps=cps, front_pad=front_pad),
        grid=(batch, steps),
        in_specs=[xrow(3 * GDN_W), xrow(LANES), xrow(GDN_W), _const_spec(state_shape)]
                 + [_const_spec(a.shape) for a in params]
                 + [_const_spec(a.shape) for a in consts],
        out_specs=[xrow(GDN_W), pl.BlockSpec(state_shape, lambda b, c: (0, 0, 0))],
        out_shape=[jax.ShapeDtypeStruct((batch * seq, GDN_W), BF16),
                   jax.ShapeDtypeStruct(state_shape, F32)],
        scratch_shapes=[pltpu.VMEM(state_shape, F32)],
        compiler_params=pltpu.CompilerParams(dimension_semantics=("arbitrary", "arbitrary"),
                                             vmem_limit_bytes=VMEM_LIMIT),
        name="gdn_chunk",
    )(gqkv, gba, z, state0, *params, *consts)


def _merge_ffn_kernel(h1_ref, om_ref, og_ref, nm_ref, wraw_ref, wmo_ref, wgo_ref, wout_ref,
                      n2_ref, wg_ref, wu_ref, wd_ref, nf_ref, out_ref, wgate_ref):
    @pl.when(pl.program_id(0) == 0)
    def _():
        wgate_ref[...] = wraw_ref[:, OFF_GATES - W_GATE_START:OFF_GATES - W_GATE_START + 2 * D_MODEL]

    h1 = h1_ref[...]
    un = _rms(h1, nm_ref[...]).astype(BF16)
    gates = jax.nn.sigmoid(_dot(un, wgate_ref[...]))
    merged = (gates[:, :D_MODEL] * _dot(om_ref[...], wmo_ref[...])
              + gates[:, D_MODEL:] * _dot(og_ref[...], wgo_ref[...]))
    h2 = h1 + _dot(merged.astype(BF16), wout_ref[...])
    xn = _rms(h2, n2_ref[...]).astype(BF16)
    act = (_silu(_dot(xn, wg_ref[...])) * _dot(xn, wu_ref[...])).astype(BF16)
    h3 = h2 + 0.5 * _dot(act, wd_ref[...])
    out_ref[...] = _rms(h3, nf_ref[...])


def _merge_ffn(h1, o_mla, o_gdn, weights, tm):
    n = h1.shape[0]
    row = lambda w: pl.BlockSpec((tm, w), lambda i: (i, 0))
    assert 2 * W_GATE_START >= D_IN
    gate_cols = pl.BlockSpec((D_MODEL, W_GATE_START), lambda i: (0, 1), pipeline_mode=pl.Buffered(1))
    return pl.pallas_call(
        _merge_ffn_kernel,
        grid=(n // tm,),
        in_specs=[row(D_MODEL), row(V_W), row(GDN_W)]
                 + [gate_cols if w.shape == (D_MODEL, D_IN) else _const_spec(w.shape) for w in weights],
        out_specs=row(D_MODEL),
        out_shape=jax.ShapeDtypeStruct((n, D_MODEL), F32),
        scratch_shapes=[pltpu.VMEM((D_MODEL, 2 * D_MODEL), BF16)],
        compiler_params=pltpu.CompilerParams(dimension_semantics=("arbitrary",),
                                             vmem_limit_bytes=VMEM_LIMIT),
        name="merge_ffn",
    )(h1, o_mla, o_gdn, *weights)


def _rope_tables(first_pos, n):
    pos = np.arange(first_pos, first_pos + n, dtype=np.float64)
    inv = ROPE_THETA ** (-np.arange(0, MLA_ROPE, 2, dtype=np.float64) / MLA_ROPE)
    ang = pos[:, None] * inv[None, :]
    cos2 = np.tile(np.cos(ang), (1, 2))
    sin2 = np.tile(np.sin(ang), (1, 2))
    scale = (MLA_NOPE + MLA_ROPE) ** -0.5 * math.log2(math.e)
    pad = np.zeros((n, HEAD_PAD - MLA_NOPE - MLA_ROPE))
    tab_c = np.concatenate([np.ones((n, MLA_NOPE)), cos2, pad], axis=1) * scale
    tab_s = np.concatenate([np.zeros((n, MLA_NOPE)), sin2, pad], axis=1) * scale
    kpad = np.zeros((n, LANES - MLA_ROPE))
    tab_kc = np.concatenate([cos2, kpad], axis=1)
    tab_ks = np.concatenate([sin2, kpad], axis=1)
    return tuple(jnp.asarray(t, F32) for t in (tab_c, tab_s, tab_kc, tab_ks))


def _rot(w):
    half = MLA_ROPE // 2
    return jnp.concatenate([-w[..., half:], w[..., :half]], axis=-1)


def kernel(x, meta_tokens, ffn1_norm, ffn1_w_gate, ffn1_w_up, ffn1_w_down, mix_norm, w_in, q_norm, w_uq,
           kv_norm, w_ukv, w_mla_o, conv_w, a_log, dt_bias, gdn_norm, w_gdn_o, w_out, ffn2_norm,
           ffn2_w_gate, ffn2_w_up, ffn2_w_down, final_norm):
    assert ffn1_norm.shape[0] == 1, "single-layer block"
    batch, seq, d = x.shape
    assert d == D_MODEL and seq % CHUNK == 0
    tm = min(256, seq)
    tq = min(512, seq)
    assert seq % tm == 0 and seq % tq == 0

    assert w_in.shape[2] == D_IN
    w_in_bf = w_in[0].astype(BF16)

    wq = w_uq[0].reshape(MLA_Q_RANK, MLA_HEADS, MLA_NOPE + MLA_ROPE)
    wq_nope, wq_rope = wq[..., :MLA_NOPE], wq[..., MLA_NOPE:]
    zq = lambda n: jnp.zeros((MLA_Q_RANK, MLA_HEADS, n), F32)
    tail = HEAD_PAD - MLA_NOPE - MLA_ROPE
    wqa = jnp.concatenate([wq_nope, wq_rope, zq(tail)], axis=-1).reshape(MLA_Q_RANK, QK_W).astype(BF16)
    wqb = jnp.concatenate([zq(MLA_NOPE), _rot(wq_rope), zq(tail)], axis=-1).reshape(MLA_Q_RANK, QK_W).astype(BF16)
    wkv = w_ukv[0].reshape(MLA_KV_RANK, MLA_HEADS, MLA_NOPE + MLA_V)
    wk = jnp.concatenate([wkv[..., :MLA_NOPE], jnp.zeros((MLA_KV_RANK, MLA_HEADS, HEAD_PAD - MLA_NOPE), F32)],
                         axis=-1).reshape(MLA_KV_RANK, QK_W).astype(BF16)
    wvt = wkv[..., MLA_NOPE:].reshape(MLA_KV_RANK, V_W).T.astype(BF16)
    e_np = np.zeros((2 * LANES, QK_W), np.float32)
    half = MLA_ROPE // 2
    for h in range(MLA_HEADS):
        base = h * HEAD_PAD + MLA_NOPE
        for j in range(MLA_ROPE):
            e_np[j, base + j] = 1.0
        for j in range(half):
            e_np[LANES + half + j, base + j] = -1.0
            e_np[LANES + j, base + half + j] = 1.0
    e_mat = jnp.asarray(e_np, BF16)

    g = np.arange(GROUP_W)
    bd_ones = jnp.asarray((g[:, None] // GDN_DK == g[None, :] // GDN_DK).astype(np.float32), BF16)
    proj_weights = [
        ffn1_norm[0][None], ffn1_w_gate[0].astype(BF16), ffn1_w_up[0].astype(BF16),
        ffn1_w_down[0].astype(BF16), mix_norm[0][None], w_in_bf, q_norm[0][None], wqa, wqb,
        kv_norm[0][None], wk, wvt, e_mat, conv_w[0].astype(F32), bd_ones]

    zero_carry = jnp.zeros((8, 3 * GDN_W), F32)
    _, _, k_m, vt_m, gqkv_m, gba_m, _, meta_tail = _token_proj(
        meta_tokens.astype(F32), _rope_tables(0, N_META), zero_carry, proj_weights, N_META, 1)
    h1, q, k, vt, gqkv, gba, z, _ = _token_proj(
        x.reshape(batch * seq, d), _rope_tables(N_META, seq), meta_tail, proj_weights, tm, seq // tm)

    pad_rows = lambda a, n, front: jnp.pad(a, ((n - a.shape[0], 0) if front else (0, n - a.shape[0]), (0, 0)))
    vt_meta = jnp.pad(vt_m[0], ((0, 0), (0, LANES - N_META)))
    o_mla = _mla_attn(q, k, vt, pad_rows(k_m, LANES, False), vt_meta, batch, seq, tq, tm)

    hpad = lambda a: jnp.zeros((1, LANES), F32).at[0, GDN_HEADS:2 * GDN_HEADS].set(a)
    arate = hpad(-jnp.exp(a_log[0].astype(F32)))
    dtb = hpad(dt_bias[0].astype(F32))
    gnorm = jnp.tile(gdn_norm[0].astype(F32), GDN_HEADS)[None]
    gdn_params = (arate, dtb, gnorm)
    cps = 4 if seq % (4 * CHUNK) == 0 else 1
    _, state_meta = _gdn(pad_rows(gqkv_m, CHUNK, True), pad_rows(gba_m, CHUNK, True),
                         jnp.zeros((CHUNK, GDN_W), F32), jnp.zeros((N_GROUPS, GDN_DK, GROUP_W), F32),
                         gdn_params, 1, CHUNK, 1, CHUNK - N_META)
    o_gdn, _ = _gdn(gqkv, gba, z, state_meta, gdn_params, batch, seq, cps, 0)

    merge_weights = [
        mix_norm[0][None], w_in_bf, w_mla_o[0].astype(BF16), w_gdn_o[0].astype(BF16), w_out[0].astype(BF16),
        ffn2_norm[0][None], ffn2_w_gate[0].astype(BF16), ffn2_w_up[0].astype(BF16),
        ffn2_w_down[0].astype(BF16), final_norm[None]]
    out = _merge_ffn(h1, o_mla, o_gdn, merge_weights, tm)
    return out.reshape(batch, seq, d)
```

```python
import functools
import math

import jax
import jax.numpy as jnp
import numpy as np
from jax import lax
from jax.experimental import pallas as pl
from jax.experimental.pallas import tpu as pltpu

F32 = jnp.float32
BF16 = jnp.bfloat16

D_MODEL = 1024
N_META = 16
EPS = 1e-6
D_FF = 2816
MLA_HEADS = 8
MLA_Q_RANK = 256
MLA_KV_RANK = 128
MLA_NOPE = 64
MLA_ROPE = 32
MLA_V = 64
ROPE_THETA = 10000.0
GDN_HEADS = 8
GDN_DK = 64
GDN_DV = 64
CONV_K = 4
CHUNK = 64

LANES = 128
HEAD_PAD = 128
QK_W = MLA_HEADS * HEAD_PAD
V_W = MLA_HEADS * MLA_V
GDN_W = GDN_HEADS * GDN_DK
GROUP_HEADS = 4
GROUP_W = GROUP_HEADS * GDN_DK
N_GROUPS = GDN_HEADS // GROUP_HEADS
FFN_CHUNK = 256
MERGE_CHAIN_ROWS = 256
PROJ_W = MLA_Q_RANK + MLA_KV_RANK + LANES + 3 * GDN_W + LANES + GDN_W
W_IN_SPLITS = (MLA_Q_RANK, MLA_KV_RANK, MLA_ROPE, GDN_W, GDN_W, GDN_W, GDN_HEADS, GDN_HEADS, GDN_W,
               D_MODEL, D_MODEL)
(OFF_CQ, OFF_CKV, OFF_KR, OFF_GQ, OFF_GK, OFF_GV, OFF_GB, OFF_GA, OFF_GZ, OFF_GATES, _OFF_GATE_GDN,
 D_IN) = (int(v) for v in np.concatenate([[0], np.cumsum(W_IN_SPLITS)]))
W_TOKEN_COLS = -(-OFF_GATES // LANES) * LANES
W_GATE_START = OFF_GATES // LANES * LANES
VMEM_LIMIT = 56 * 1024 * 1024
NEG_BIG = -1e30


def _const_spec(shape):
    zeros = (0,) * len(shape)
    return pl.BlockSpec(shape, lambda *_: zeros, pipeline_mode=pl.Buffered(1))


def _rms(x, w):
    return x * lax.rsqrt(jnp.mean(x * x, axis=-1, keepdims=True) + EPS) * w


def _dot(a, b):
    return jnp.dot(a, b, preferred_element_type=F32)


def _dot_nt(a, b):
    return lax.dot_general(a, b, (((1,), (1,)), ((), ())), preferred_element_type=F32)


def _silu(x):
    return x * jax.nn.sigmoid(x)


def _split(x, n):
    pieces = []
    for _ in range(n - 1):
        hi = x.astype(BF16)
        pieces.append(hi)
        x = x - hi.astype(F32)
    pieces.append(x.astype(BF16))
    return pieces


def _dot_exact_rhs(x, rhs, n):
    return sum(_dot(p, rhs) for p in _split(x, n))


def _dot_exact_lhs(lhs, x, n):
    return sum(_dot(lhs, p) for p in _split(x, n))


def _head_sums(x, bd_ones):
    return jnp.concatenate([_dot(x[:, g * GROUP_W:(g + 1) * GROUP_W].astype(BF16), bd_ones)
                            for g in range(N_GROUPS)], axis=1)


def _token_proj_kernel(x_ref, tc_ref, ts_ref, tkc_ref, tks_ref, c0_ref, n1_ref, wg_ref, wu_ref, wd_ref, nm_ref,
                       wraw_ref, qn_ref, wqa_ref, wqb_ref, kvn_ref, wk_ref, wvt_ref, e_ref, convw_ref,
                       bd_ones_ref,
                       h1_ref, q_ref, k_ref, vt_ref, gqkv_ref, gba_ref, z_ref, tail_ref, xe_ref, win_ref,
                       *, tiles_per_seq):
    tm = x_ref.shape[0]
    step = pl.program_id(0)

    @pl.when(step == 0)
    def _():
        xe_ref[...] = jnp.zeros_like(xe_ref)
        o = OFF_GQ + LANES - MLA_ROPE
        win_ref[:, :o] = wraw_ref[:, :o]
        win_ref[:, o:o + 3 * GDN_W] = wraw_ref[:, OFF_GQ:OFF_GB]
        o += 3 * GDN_W
        lane = lax.broadcasted_iota(jnp.int32, (1, LANES), 1)
        win_ref[:, o:o + LANES] = jnp.where(lane < 2 * GDN_HEADS, wraw_ref[:, OFF_GB:OFF_GB + LANES], 0)
        win_ref[:, o + LANES:] = wraw_ref[:, OFF_GZ:OFF_GATES]

    x = x_ref[...]
    xn = _rms(x, n1_ref[...]).astype(BF16)

    n_ffn = D_FF // FFN_CHUNK
    n_conv = 3 * GDN_W // LANES
    acts = []
    for c in range(max(n_ffn, n_conv)):
        if c < n_ffn:
            cols = slice(c * FFN_CHUNK, (c + 1) * FFN_CHUNK)
            acts.append((_silu(_dot(xn, wg_ref[:, cols])) * _dot(xn, wu_ref[:, cols])).astype(BF16))
        if c < n_conv:
            lanes = slice(c * LANES, (c + 1) * LANES)
            y = sum(convw_ref[t:t + 1, lanes] * xe_ref[pl.ds(8 - (CONV_K - 1) + t, tm), lanes]
                    for t in range(CONV_K))
            gqkv_ref[:, lanes] = _silu(y)
    h1 = x + 0.5 * _dot(jnp.concatenate(acts, axis=1), wd_ref[...])
    h1_ref[...] = h1

    un = _rms(h1, nm_ref[...]).astype(BF16)
    p = _dot(un, win_ref[...])

    bd_ones = bd_ones_ref[...]
    gq = gqkv_ref[:, :GDN_W]
    gk = gqkv_ref[:, GDN_W:2 * GDN_W]
    gqkv_ref[:, :GDN_W] = gq * lax.rsqrt(_head_sums(gq * gq, bd_ones) + EPS) * (GDN_DK ** -0.5)
    gqkv_ref[:, GDN_W:2 * GDN_W] = gk * lax.rsqrt(_head_sums(gk * gk, bd_ones) + EPS)
    o = 0
    cq = p[:, o:o + MLA_Q_RANK]; o += MLA_Q_RANK
    ckv = p[:, o:o + MLA_KV_RANK]; o += MLA_KV_RANK
    kr = p[:, o:o + LANES]; o += LANES
    gqkv_raw = p[:, o:o + 3 * GDN_W]; o += 3 * GDN_W
    gba_ref[...] = p[:, o:o + LANES]; o += LANES
    z_ref[...] = p[:, o:o + GDN_W]

    xe_ref[0:8] = jnp.where(step % tiles_per_seq == 0, c0_ref[...], xe_ref[tm:tm + 8])
    xe_ref[8:8 + tm] = gqkv_raw
    tail_ref[...] = gqkv_raw[tm - 8:]

    cqn = _rms(cq, qn_ref[...]).astype(BF16)
    tc = jnp.concatenate([tc_ref[...]] * MLA_HEADS, axis=1)
    ts = jnp.concatenate([ts_ref[...]] * MLA_HEADS, axis=1)
    q = _dot(cqn, wqa_ref[...]) * tc + _dot(cqn, wqb_ref[...]) * ts
    q_ref[...] = q.astype(BF16)

    ckvn = _rms(ckv, kvn_ref[...]).astype(BF16)
    kr_terms = jnp.concatenate([kr * tkc_ref[...], kr * tks_ref[...]], axis=1).astype(BF16)
    k = _dot(ckvn, wk_ref[...]) + _dot(kr_terms, e_ref[...])
    k_ref[...] = k.astype(BF16)
    vt_ref[0] = _dot_nt(wvt_ref[...], ckvn).astype(BF16)


def _token_proj(x2d, tabs, conv_carry, weights, tm, tiles_per_seq):
    n = x2d.shape[0]
    assert n % tm == 0
    last = n // tm - 1
    tile = lambda i: jnp.minimum(i, last)
    row = lambda w: pl.BlockSpec((tm, w), lambda i: (tile(i), 0))
    tab = pl.BlockSpec((tm, LANES), lambda i: (tile(i) % tiles_per_seq, 0))
    out_widths = (D_MODEL, QK_W, QK_W, None, 3 * GDN_W, LANES, GDN_W)
    out_dtypes = (F32, BF16, BF16, BF16, F32, F32, F32)
    vt_spec = pl.BlockSpec((1, V_W, tm), lambda i: (tile(i), 0, 0))
    conv_spec = pl.BlockSpec((tm, 3 * GDN_W), lambda i: (jnp.maximum(i - 1, 0), 0))
    tail_shape = (8, 3 * GDN_W)
    out_specs = [vt_spec if w is None else row(w) for w in out_widths]
    out_specs[4] = conv_spec
    return pl.pallas_call(
        functools.partial(_token_proj_kernel, tiles_per_seq=tiles_per_seq),
        grid=(n // tm + 1,),
        in_specs=[row(D_MODEL)] + [tab] * len(tabs) + [_const_spec(tail_shape)]
                 + [_const_spec((D_MODEL, W_TOKEN_COLS) if w.shape == (D_MODEL, D_IN) else w.shape)
                    for w in weights],
        out_specs=out_specs + [pl.BlockSpec(tail_shape, lambda i: (0, 0))],
        out_shape=[jax.ShapeDtypeStruct((n // tm, V_W, tm) if w is None else (n, w), d)
                   for w, d in zip(out_widths, out_dtypes)] + [jax.ShapeDtypeStruct(tail_shape, F32)],
        scratch_shapes=[pltpu.VMEM((8 + tm, 3 * GDN_W), F32), pltpu.VMEM((D_MODEL, PROJ_W), BF16)],
        compiler_params=pltpu.CompilerParams(dimension_semantics=("arbitrary",),
                                             vmem_limit_bytes=VMEM_LIMIT),
        name="token_proj",
    )(x2d, *tabs, conv_carry, *weights)


def _mla_kernel(q_ref, k_ref, vt_ref, km_ref, vmt_ref, o_ref, st_ref, *, tq, tk):
    def tile(qi, carry):
        rows = pl.ds(pl.multiple_of(qi * tq, tq), tq)
        _mla_tile(qi, q_ref.at[rows], k_ref, vt_ref, km_ref, vmt_ref, o_ref.at[rows], st_ref, tq=tq, tk=tk)
        return carry

    lax.fori_loop(0, q_ref.shape[0] // tq, tile, 0)


def _mla_tile(qi, q_ref, k_ref, vt_ref, km_ref, vmt_ref, o_ref, st_ref, *, tq, tk):
    heads = range(2)
    hs = [slice(h * HEAD_PAD, (h + 1) * HEAD_PAD) for h in heads]
    vs = [slice(h * MLA_V, (h + 1) * MLA_V) for h in heads]
    q = [q_ref[:, s] for s in hs]
    colmax = lambda s: jnp.max(s, axis=0, keepdims=True)

    def with_ones(vt):
        return jnp.concatenate([vt, jnp.ones((8, vt.shape[1]), BF16)], axis=0)

    def scores(ki, slot, q_from=0):
        rows = pl.ds(pl.multiple_of(ki * tk, tk), tk)
        block_max = []
        for h in heads:
            s = _dot_nt(k_ref[rows, hs[h]], q[h][q_from:])
            st_ref[slot, h, :, q_from:] = s
            block_max.append(colmax(s))
        return block_max

    def update(ki, slot, m, acc, block_max, mask, q_from=0):
        st = [st_ref[slot, h, :, q_from:] for h in heads]
        if mask is not None:
            st = [jnp.where(mask, s, NEG_BIG) for s in st]
            block_max = [colmax(s) for s in st]
        m_old = [x[:, q_from:] for x in m]
        m_new = [jnp.maximum(m_old[h], block_max[h]) for h in heads]
        p = [jnp.exp2(st[h] - m_new[h]).astype(BF16) for h in heads]
        vt = vt_ref[ki]
        acc_new = [jnp.exp2(m_old[h] - m_new[h]) * acc[h][:, q_from:] + _dot(with_ones(vt[vs[h]]), p[h])
                   for h in heads]
        if q_from:
            m_new = [jnp.concatenate([m[h][:, :q_from], m_new[h]], axis=1) for h in heads]
            acc_new = [jnp.concatenate([acc[h][:, :q_from], acc_new[h]], axis=1) for h in heads]
        return m_new, acc_new

    bm_a = scores(0, 0)
    bm_b = scores(1, 1)
    meta_valid = lax.broadcasted_iota(jnp.int32, (km_ref.shape[0], tq), 0) < N_META
    st = [jnp.where(meta_valid, _dot_nt(km_ref[:, hs[h]], q[h]), NEG_BIG) for h in heads]
    m = [colmax(s) for s in st]
    acc = [_dot(with_ones(vmt_ref[vs[h], :]), jnp.exp2(st[h] - m[h]).astype(BF16)) for h in heads]

    def stage(ka, carry, cur, nxt):
        m, acc, bm_a, bm_b = carry
        bm_a2 = scores(ka + 2, nxt[0])
        m, acc = update(ka, cur[0], m, acc, bm_a, None)
        bm_b2 = scores(ka + 3, nxt[1])
        m, acc = update(ka + 1, cur[1], m, acc, bm_b, None)
        return m, acc, bm_a2, bm_b2

    def body(i, carry):
        return stage(4 * i + 2, stage(4 * i, carry, (0, 1), (2, 3)), (2, 3), (0, 1))

    causal = (lax.broadcasted_iota(jnp.int32, (tk, tq), 1) >= lax.broadcasted_iota(jnp.int32, (tk, tq), 0))
    n_full = 2 * qi

    def finish(carry, cur):
        m, acc, _, _ = carry
        m, acc = update(n_full, cur[0], m, acc, None, causal)
        m, acc = update(n_full + 1, cur[1], m, acc, None, causal[:, :tk], q_from=tk)
        return acc

    carry = lax.fori_loop(0, qi // 2, body, (m, acc, bm_a, bm_b))
    acc = lax.cond(qi % 2 == 1,
                   lambda c: finish(stage(n_full - 2, c, (0, 1), (2, 3)), (2, 3)),
                   lambda c: finish(c, (0, 1)), carry)
    o_ref[...] = jnp.concatenate([(a[:MLA_V] * (1.0 / a[MLA_V:MLA_V + 1])).T for a in acc],
                                 axis=1).astype(BF16)


def _mla_attn(q, k, vt, k_meta, vt_meta, batch, seq, tq, tk):
    nq = seq // tq
    assert vt.shape[2] == tk and tq == 2 * tk
    kern = functools.partial(_mla_kernel, tq=tq, tk=tk)
    return pl.pallas_call(
        kern,
        grid=(batch, MLA_HEADS // 2),
        in_specs=[
            pl.BlockSpec((seq, 2 * HEAD_PAD), lambda b, hp: (b, hp)),
            pl.BlockSpec((seq, 2 * HEAD_PAD), lambda b, hp: (b, hp)),
            pl.BlockSpec((seq // tk, 2 * MLA_V, tk), lambda b, hp: (b, hp, 0)),
            pl.BlockSpec((k_meta.shape[0], 2 * HEAD_PAD), lambda b, hp: (0, hp)),
            pl.BlockSpec((2 * MLA_V, vt_meta.shape[1]), lambda b, hp: (hp, 0)),
        ],
        out_specs=pl.BlockSpec((seq, 2 * MLA_V), lambda b, hp: (b, hp)),
        out_shape=jax.ShapeDtypeStruct((batch * seq, V_W), BF16),
        scratch_shapes=[pltpu.VMEM((4, 2, tk, tq), F32)],
        compiler_params=pltpu.CompilerParams(
            dimension_semantics=("arbitrary", "arbitrary"),
            vmem_limit_bytes=VMEM_LIMIT),
        name="mla_attn",
    )(q, k, vt, k_meta, vt_meta)


LEVELS = (1, 2, 4, 8, 16, 32)
GDN_CONST_NAMES = ("expand_b", "expand_a", "ltri", "eye_t", "tril_t", "stril_t", "level_masks", "bd_ones")


def _gdn_constants(cps):
    i = np.arange(CHUNK)[:, None]
    lane = np.arange(GDN_W)[None, :]
    j = lane % GDN_DK
    c = {}
    r = np.arange(LANES)[:, None]
    c["expand_b"] = (r == lane // GDN_DK)
    c["expand_a"] = (r == GDN_HEADS + lane // GDN_DK)
    t = np.arange(cps * CHUNK)
    c["ltri"] = (t[:, None] >= t[None, :]) & (t[:, None] // CHUNK == t[None, :] // CHUNK)
    c["eye_t"] = (i == j)
    c["tril_t"] = (i >= j)
    c["stril_t"] = (i > j)
    jg = j[:, :GROUP_W]
    c["level_masks"] = np.stack([
        ((i // (2 * s) == jg // (2 * s)) & ((i // s) % 2 == 1) & ((jg // s) % 2 == 0))
        for s in LEVELS])
    g = np.arange(GROUP_W)
    c["bd_ones"] = (g[:, None] // GDN_DK == g[None, :] // GDN_DK)
    bf = ("bd_ones", "expand_b", "expand_a", "ltri")
    return [jnp.asarray(c[k].astype(np.float32), BF16 if k in bf else F32) for k in GDN_CONST_NAMES]


def _block_diag(y, lo_half):
    zeros = jnp.zeros((GDN_DK, LANES), y.dtype)
    blocks = []
    for h in range(GROUP_HEADS):
        t = h // 2
        tile = y[:, t * LANES:(t + 1) * LANES]
        piece = jnp.where(lo_half, tile, 0) if h % 2 == 0 else jnp.where(lo_half, 0, tile)
        blocks.append(jnp.concatenate([piece, zeros] if t == 0 else [zeros, piece], axis=1))
    return jnp.concatenate(blocks, axis=0)


def _head_matmul(x, y, lo_half):
    return _dot(x.astype(BF16), _block_diag(y.astype(BF16), lo_half))


def _diag_blocks(m, lo_half):
    tiles = []
    for t in range(m.shape[1] // LANES):
        h = 2 * (t % 2)
        cols = slice(t * LANES, (t + 1) * LANES)
        tiles.append(jnp.where(lo_half, m[h * GDN_DK:(h + 1) * GDN_DK, cols],
                               m[(h + 1) * GDN_DK:(h + 2) * GDN_DK, cols]))
    return jnp.concatenate(tiles, axis=1)


def _gdn_kernel(gx_ref, ba_ref, z_ref, s0_ref, arate_ref, dtb_ref, gnorm_ref,
                expand_b_ref, expand_a_ref, ltri_ref, eye_t_ref, tril_t_ref, stril_t_ref,
                level_masks_ref, bd_ones_ref,
                o_ref, sout_ref, state_ref, *, cps, front_pad):
    c = pl.program_id(1)
    rows = cps * CHUNK

    @pl.when(c == 0)
    def _():
        state_ref[...] = s0_ref[...]

    qn = gx_ref[:, :GDN_W]
    kn = gx_ref[:, GDN_W:2 * GDN_W]
    v = gx_ref[:, 2 * GDN_W:]

    ba = ba_ref[...]
    beta = jax.nn.sigmoid(ba)
    sp_in = ba + dtb_ref[...]
    g = arate_ref[...] * (jnp.maximum(sp_in, 0.0) + jnp.log1p(jnp.exp(-jnp.abs(sp_in))))
    if front_pad:
        valid = lax.broadcasted_iota(jnp.int32, (rows, 1), 0) >= front_pad
        beta = jnp.where(valid, beta, 0.0)
        g = jnp.where(valid, g, 0.0)
    beta_e = _dot_exact_rhs(beta, expand_b_ref[...], 2)
    gc = _dot_exact_lhs(ltri_ref[...], g, 3)
    gc_e = _dot_exact_rhs(gc, expand_a_ref[...], 3)
    egc = jnp.exp(gc_e)
    kb = kn * beta_e
    vb = v * beta_e
    kbg = kb * egc
    qg = qn * egc
    eye, tril, stril = eye_t_ref[...], tril_t_ref[...], stril_t_ref[...]
    lo_half = lax.broadcasted_iota(jnp.int32, (1, LANES), 1) < GDN_DK

    chains = [(j, gi) for j in range(cps) for gi in range(N_GROUPS)]
    rsl = lambda j: slice(j * CHUNK, (j + 1) * CHUNK)
    gsl = lambda gi: slice(gi * GROUP_W, (gi + 1) * GROUP_W)
    decay, kdec, s_decay = [], [], []
    for j in range(cps):
        gce = gc_e[rsl(j)]
        gc_t = jnp.sum(gce * eye, axis=0, keepdims=True)
        decay.append(jnp.exp(jnp.where(tril > 0.5, gce - gc_t, -jnp.inf)))
        g_last = gce[CHUNK - 1:CHUNK]
        kdec.append(kn[rsl(j)] * jnp.exp(g_last - gce))
        s_decay.append(jnp.exp(g_last))

    lmat, amat = [], []
    for j, gi in chains:
        rs, gs = rsl(j), gsl(gi)
        k4 = kn[rs, gs].astype(BF16)
        kstack = _block_diag(k4, lo_half)
        lhs = jnp.concatenate([kb[rs, gs], qn[rs, gs]], axis=0).astype(BF16)
        sc = _dot_nt(lhs, kstack)
        lmat.append(sc[:CHUNK] * decay[j][:, gs] * stril[:, gs])
        amat.append(sc[CHUNK:] * decay[j][:, gs])

    xinv = [eye[:, gsl(gi)] - lm * level_masks_ref[0] for (j, gi), lm in zip(chains, lmat)]
    for li in range(1, len(LEVELS)):
        xo = [_head_matmul(x, lm * level_masks_ref[li], lo_half) for x, lm in zip(xinv, lmat)]
        xinv = [x - _head_matmul(y, x, lo_half) for x, y in zip(xinv, xo)]

    uw = []
    for (j, gi), x in zip(chains, xinv):
        rs, gs = rsl(j), gsl(gi)
        rhs = jnp.concatenate([_block_diag(vb[rs, gs].astype(BF16), lo_half),
                               _block_diag(kbg[rs, gs].astype(BF16), lo_half)], axis=1)
        uw.append(_dot(x.astype(BF16), rhs))
    kwu = [_diag_blocks(_dot(kdec[j][:, gsl(gi)].T.astype(BF16), m.astype(BF16)), lo_half)
           for (j, gi), m in zip(chains, uw)]

    states = [state_ref[gi] for gi in range(N_GROUPS)]
    o_rows = [[None] * N_GROUPS for _ in range(cps)]
    for ci, (j, gi) in enumerate(chains):
        rs, gs = rsl(j), gsl(gi)
        u, wmat = uw[ci][:, :GROUP_W], uw[ci][:, GROUP_W:]
        ku, kw = kwu[ci][:, :GROUP_W], kwu[ci][:, GROUP_W:]
        state = states[gi]
        lhs = jnp.concatenate([kw, wmat, qg[rs, gs]], axis=0).astype(BF16)
        big = _dot(lhs, _block_diag(state.astype(BF16), lo_half))
        states[gi] = s_decay[j][:, gs] * state - big[:CHUNK] + ku
        v_new = u - big[CHUNK:2 * CHUNK]
        o_rows[j][gi] = big[2 * CHUNK:] + _head_matmul(amat[ci], v_new, lo_half)
    for gi in range(N_GROUPS):
        state_ref[gi] = states[gi]

    @pl.when(c == pl.num_programs(1) - 1)
    def _():
        sout_ref[...] = state_ref[...]

    o = jnp.concatenate([jnp.concatenate(r, axis=1) for r in o_rows], axis=0)
    ms = _head_sums(o * o, bd_ones_ref[...]) * (1.0 / GDN_DV)
    o = o * lax.rsqrt(ms + EPS) * gnorm_ref[...] * _silu(z_ref[...])
    o_ref[...] = o.astype(BF16)


def _gdn(gqkv, gba, z, state0, params, batch, seq, cps, front_pad):
    rows = cps * CHUNK
    steps = seq // rows
    consts = _gdn_constants(cps)
    xrow = lambda w: pl.BlockSpec((rows, w), lambda b, c: (b * steps + c, 0))
    state_shape = (N_GROUPS, GDN_DK, GROUP_W)
    return pl.pallas_call(
        functools.partial(_gdn_kernel, cps=cps, front_pad=front_pad),
        grid=(batch, steps),
        in_specs=[xrow(3 * GDN_W), xrow(LANES), xrow(GDN_W), _const_spec(state_shape)]
                 + [_const_spec(a.shape) for a in params]
                 + [_const_spec(a.shape) for a in consts],
        out_specs=[xrow(GDN_W), pl.BlockSpec(state_shape, lambda b, c: (0, 0, 0))],
        out_shape=[jax.ShapeDtypeStruct((batch * seq, GDN_W), BF16),
                   jax.ShapeDtypeStruct(state_shape, F32)],
        scratch_shapes=[pltpu.VMEM(state_shape, F32)],
        compiler_params=pltpu.CompilerParams(dimension_semantics=("arbitrary", "arbitrary"),
                                             vmem_limit_bytes=VMEM_LIMIT),
        name="gdn_chunk",
    )(gqkv, gba, z, state0, *params, *consts)


def _merge_ffn_kernel(h1_ref, om_ref, og_ref, nm_ref, wraw_ref, wmo_ref, wgo_ref, wout_ref,
                      n2_ref, wg_ref, wu_ref, wd_ref, nf_ref, out_ref, wgate_ref):
    @pl.when(pl.program_id(0) == 0)
    def _():
        wgate_ref[...] = wraw_ref[:, OFF_GATES - W_GATE_START:OFF_GATES - W_GATE_START + 2 * D_MODEL]

    n_chains = max(1, h1_ref.shape[0] // MERGE_CHAIN_ROWS)
    rows = [slice(c * MERGE_CHAIN_ROWS, (c + 1) * MERGE_CHAIN_ROWS) if n_chains > 1 else slice(None)
            for c in range(n_chains)]
    h1 = [h1_ref[r] for r in rows]
    un = [_rms(h, nm_ref[...]).astype(BF16) for h in h1]
    gates = [jax.nn.sigmoid(_dot(u, wgate_ref[...])) for u in un]
    merged = [(g[:, :D_MODEL] * _dot(om_ref[r], wmo_ref[...])
               + g[:, D_MODEL:] * _dot(og_ref[r], wgo_ref[...])).astype(BF16) for g, r in zip(gates, rows)]
    h2 = [h + _dot(m, wout_ref[...]) for h, m in zip(h1, merged)]
    xn = [_rms(h, n2_ref[...]).astype(BF16) for h in h2]
    acts = [[] for _ in rows]
    for c in range(D_FF // FFN_CHUNK):
        cols = slice(c * FFN_CHUNK, (c + 1) * FFN_CHUNK)
        for a, x in zip(acts, xn):
            a.append((_silu(_dot(x, wg_ref[:, cols])) * _dot(x, wu_ref[:, cols])).astype(BF16))
    h3 = [h + 0.5 * _dot(jnp.concatenate(a, axis=1), wd_ref[...]) for h, a in zip(h2, acts)]
    for r, h in zip(rows, h3):
        out_ref[r] = _rms(h, nf_ref[...])


def _merge_ffn(h1, o_mla, o_gdn, weights, tm):
    n = h1.shape[0]
    row = lambda w: pl.BlockSpec((tm, w), lambda i: (i, 0))
    assert 2 * W_GATE_START >= D_IN
    gate_cols = pl.BlockSpec((D_MODEL, W_GATE_START), lambda i: (0, 1), pipeline_mode=pl.Buffered(1))
    return pl.pallas_call(
        _merge_ffn_kernel,
        grid=(n // tm,),
        in_specs=[row(D_MODEL), row(V_W), row(GDN_W)]
                 + [gate_cols if w.shape == (D_MODEL, D_IN) else _const_spec(w.shape) for w in weights],
        out_specs=row(D_MODEL),
        out_shape=jax.ShapeDtypeStruct((n, D_MODEL), F32),
        scratch_shapes=[pltpu.VMEM((D_MODEL, 2 * D_MODEL), BF16)],
        compiler_params=pltpu.CompilerParams(dimension_semantics=("arbitrary",),
                                             vmem_limit_bytes=VMEM_LIMIT),
        name="merge_ffn",
    )(h1, o_mla, o_gdn, *weights)


def _rope_tables(first_pos, n):
    pos = np.arange(first_pos, first_pos + n, dtype=np.float64)
    inv = ROPE_THETA ** (-np.arange(0, MLA_ROPE, 2, dtype=np.float64) / MLA_ROPE)
    ang = pos[:, None] * inv[None, :]
    cos2 = np.tile(np.cos(ang), (1, 2))
    sin2 = np.tile(np.sin(ang), (1, 2))
    scale = (MLA_NOPE + MLA_ROPE) ** -0.5 * math.log2(math.e)
    pad = np.zeros((n, HEAD_PAD - MLA_NOPE - MLA_ROPE))
    tab_c = np.concatenate([np.ones((n, MLA_NOPE)), cos2, pad], axis=1) * scale
    tab_s = np.concatenate([np.zeros((n, MLA_NOPE)), sin2, pad], axis=1) * scale
    kpad = np.zeros((n, LANES - MLA_ROPE))
    tab_kc = np.concatenate([cos2, kpad], axis=1)
    tab_ks = np.concatenate([sin2, kpad], axis=1)
    return tuple(jnp.asarray(t, F32) for t in (tab_c, tab_s, tab_kc, tab_ks))


def _rot(w):
    half = MLA_ROPE // 2
    return jnp.concatenate([-w[..., half:], w[..., :half]], axis=-1)


def kernel(x, meta_tokens, ffn1_norm, ffn1_w_gate, ffn1_w_up, ffn1_w_down, mix_norm, w_in, q_norm, w_uq,
           kv_norm, w_ukv, w_mla_o, conv_w, a_log, dt_bias, gdn_norm, w_gdn_o, w_out, ffn2_norm,
           ffn2_w_gate, ffn2_w_up, ffn2_w_down, final_norm):
    assert ffn1_norm.shape[0] == 1, "single-layer block"
    batch, seq, d = x.shape
    assert d == D_MODEL and seq % CHUNK == 0
    tm = min(256, seq)
    tq = min(512, seq)
    assert seq % tm == 0 and seq % tq == 0

    assert w_in.shape[2] == D_IN
    w_in_bf = w_in[0].astype(BF16)

    wq = w_uq[0].reshape(MLA_Q_RANK, MLA_HEADS, MLA_NOPE + MLA_ROPE)
    wq_nope, wq_rope = wq[..., :MLA_NOPE], wq[..., MLA_NOPE:]
    zq = lambda n: jnp.zeros((MLA_Q_RANK, MLA_HEADS, n), F32)
    tail = HEAD_PAD - MLA_NOPE - MLA_ROPE
    wqa = jnp.concatenate([wq_nope, wq_rope, zq(tail)], axis=-1).reshape(MLA_Q_RANK, QK_W).astype(BF16)
    wqb = jnp.concatenate([zq(MLA_NOPE), _rot(wq_rope), zq(tail)], axis=-1).reshape(MLA_Q_RANK, QK_W).astype(BF16)
    wkv = w_ukv[0].reshape(MLA_KV_RANK, MLA_HEADS, MLA_NOPE + MLA_V)
    wk = jnp.concatenate([wkv[..., :MLA_NOPE], jnp.zeros((MLA_KV_RANK, MLA_HEADS, HEAD_PAD - MLA_NOPE), F32)],
                         axis=-1).reshape(MLA_KV_RANK, QK_W).astype(BF16)
    wvt = wkv[..., MLA_NOPE:].reshape(MLA_KV_RANK, V_W).T.astype(BF16)
    e_np = np.zeros((2 * LANES, QK_W), np.float32)
    half = MLA_ROPE // 2
    for h in range(MLA_HEADS):
        base = h * HEAD_PAD + MLA_NOPE
        for j in range(MLA_ROPE):
            e_np[j, base + j] = 1.0
        for j in range(half):
            e_np[LANES + half + j, base + j] = -1.0
            e_np[LANES + j, base + half + j] = 1.0
    e_mat = jnp.asarray(e_np, BF16)

    g = np.arange(GROUP_W)
    bd_ones = jnp.asarray((g[:, None] // GDN_DK == g[None, :] // GDN_DK).astype(np.float32), BF16)
    proj_weights = [
        ffn1_norm[0][None], ffn1_w_gate[0].astype(BF16), ffn1_w_up[0].astype(BF16),
        ffn1_w_down[0].astype(BF16), mix_norm[0][None], w_in_bf, q_norm[0][None], wqa, wqb,
        kv_norm[0][None], wk, wvt, e_mat, conv_w[0].astype(F32), bd_ones]

    zero_carry = jnp.zeros((8, 3 * GDN_W), F32)
    _, _, k_m, vt_m, gqkv_m, gba_m, _, meta_tail = _token_proj(
        meta_tokens.astype(F32), _rope_tables(0, N_META), zero_carry, proj_weights, N_META, 1)
    h1, q, k, vt, gqkv, gba, z, _ = _token_proj(
        x.reshape(batch * seq, d), _rope_tables(N_META, seq), meta_tail, proj_weights, tm, seq // tm)

    pad_rows = lambda a, n, front: jnp.pad(a, ((n - a.shape[0], 0) if front else (0, n - a.shape[0]), (0, 0)))
    vt_meta = jnp.pad(vt_m[0], ((0, 0), (0, LANES - N_META)))
    o_mla = _mla_attn(q, k, vt, pad_rows(k_m, LANES, False), vt_meta, batch, seq, tq, tm)

    hpad = lambda a: jnp.zeros((1, LANES), F32).at[0, GDN_HEADS:2 * GDN_HEADS].set(a)
    arate = hpad(-jnp.exp(a_log[0].astype(F32)))
    dtb = hpad(dt_bias[0].astype(F32))
    gnorm = jnp.tile(gdn_norm[0].astype(F32), GDN_HEADS)[None]
    gdn_params = (arate, dtb, gnorm)
    cps = 4 if seq % (4 * CHUNK) == 0 else 1
    _, state_meta = _gdn(pad_rows(gqkv_m, CHUNK, True), pad_rows(gba_m, CHUNK, True),
                         jnp.zeros((CHUNK, GDN_W), F32), jnp.zeros((N_GROUPS, GDN_DK, GROUP_W), F32),
                         gdn_params, 1, CHUNK, 1, CHUNK - N_META)
    o_gdn, _ = _gdn(gqkv, gba, z, state_meta, gdn_params, batch, seq, cps, 0)

    merge_weights = [
        mix_norm[0][None], w_in_bf, w_mla_o[0].astype(BF16), w_gdn_o[0].astype(BF16), w_out[0].astype(BF16),
        ffn2_norm[0][None], ffn2_w_gate[0].astype(BF16), ffn2_w_up[0].astype(BF16),
        ffn2_w_down[0].astype(BF16), final_norm[None]]
    merge_rows = 2 * MERGE_CHAIN_ROWS if (batch * seq) % (2 * MERGE_CHAIN_ROWS) == 0 else tm
    out = _merge_ffn(h1, o_mla, o_gdn, merge_weights, merge_rows)
    return out.reshape(batch, seq, d)
```

```python
import functools
import math

import jax
import jax.numpy as jnp
import numpy as np
from jax import lax
from jax.experimental import pallas as pl
from jax.experimental.pallas import tpu as pltpu

F32 = jnp.float32
BF16 = jnp.bfloat16

D_MODEL = 1024
N_META = 16
EPS = 1e-6
D_FF = 2816
MLA_HEADS = 8
MLA_Q_RANK = 256
MLA_KV_RANK = 128
MLA_NOPE = 64
MLA_ROPE = 32
MLA_V = 64
ROPE_THETA = 10000.0
GDN_HEADS = 8
GDN_DK = 64
GDN_DV = 64
CONV_K = 4
CHUNK = 64

LANES = 128
HEAD_PAD = 128
QK_W = MLA_HEADS * HEAD_PAD
V_W = MLA_HEADS * MLA_V
GDN_W = GDN_HEADS * GDN_DK
GROUP_HEADS = 4
GROUP_W = GROUP_HEADS * GDN_DK
N_GROUPS = GDN_HEADS // GROUP_HEADS
FFN_CHUNK = 256
MERGE_CHAIN_ROWS = 256
PROJ_W = MLA_Q_RANK + MLA_KV_RANK + LANES + 3 * GDN_W + LANES + GDN_W
W_IN_SPLITS = (MLA_Q_RANK, MLA_KV_RANK, MLA_ROPE, GDN_W, GDN_W, GDN_W, GDN_HEADS, GDN_HEADS, GDN_W,
               D_MODEL, D_MODEL)
(OFF_CQ, OFF_CKV, OFF_KR, OFF_GQ, OFF_GK, OFF_GV, OFF_GB, OFF_GA, OFF_GZ, OFF_GATES, _OFF_GATE_GDN,
 D_IN) = (int(v) for v in np.concatenate([[0], np.cumsum(W_IN_SPLITS)]))
W_TOKEN_COLS = -(-OFF_GATES // LANES) * LANES
W_GATE_START = OFF_GATES // LANES * LANES
VMEM_LIMIT = 60 * 1024 * 1024
NEG_BIG = -1e30


def _const_spec(shape):
    zeros = (0,) * len(shape)
    return pl.BlockSpec(shape, lambda *_: zeros, pipeline_mode=pl.Buffered(1))


def _rms(x, w):
    return x * lax.rsqrt(jnp.mean(x * x, axis=-1, keepdims=True) + EPS) * w


def _dot(a, b):
    return jnp.dot(a, b, preferred_element_type=F32)


def _dot_nt(a, b):
    return lax.dot_general(a, b, (((1,), (1,)), ((), ())), preferred_element_type=F32)


def _silu(x):
    return x * jax.nn.sigmoid(x)


def _split(x, n):
    pieces = []
    for _ in range(n - 1):
        hi = x.astype(BF16)
        pieces.append(hi)
        x = x - hi.astype(F32)
    pieces.append(x.astype(BF16))
    return pieces


def _dot_exact_rhs(x, rhs, n):
    return sum(_dot(p, rhs) for p in _split(x, n))


def _dot_exact_lhs(lhs, x, n):
    return sum(_dot(lhs, p) for p in _split(x, n))


def _head_sums(x, bd_ones):
    return jnp.concatenate([_dot(x[:, g * GROUP_W:(g + 1) * GROUP_W].astype(BF16), bd_ones)
                            for g in range(N_GROUPS)], axis=1)


def _token_proj_kernel(*refs, tiles_per_seq, regroup):
    it = iter(refs)
    take = lambda n: [next(it) for _ in range(n)]
    x_ref, tc_ref, ts_ref, tkc_ref, tks_ref, c0_ref, n1_ref, wg_ref, wu_ref, wd_ref, nm_ref = take(11)
    if regroup:
        wraw_ref, wrawg_ref = take(2)
    else:
        win_ref, = take(1)
    qn_ref, wqa_ref, wqb_ref, kvn_ref, wk_ref, wvt_ref, e_ref, convw_ref, bd_ones_ref = take(9)
    h1_ref, q_ref, k_ref, vt_ref, gqkv_ref, gba_ref, z_ref, tail_ref = take(8)
    if regroup:
        win_ref, wgate_ref = take(2)
    xe_ref, = take(1)
    tm = x_ref.shape[0]
    step = pl.program_id(0)

    @pl.when(step == 0)
    def _():
        xe_ref[...] = jnp.zeros_like(xe_ref)
        if regroup:
            o = OFF_GQ + LANES - MLA_ROPE
            win_ref[:, :o] = wraw_ref[:, :o]
            win_ref[:, o:o + 3 * GDN_W] = wraw_ref[:, OFF_GQ:OFF_GB]
            o += 3 * GDN_W
            lane = lax.broadcasted_iota(jnp.int32, (1, LANES), 1)
            win_ref[:, o:o + LANES] = jnp.where(lane < 2 * GDN_HEADS, wraw_ref[:, OFF_GB:OFF_GB + LANES], 0)
            win_ref[:, o + LANES:] = wraw_ref[:, OFF_GZ:OFF_GATES]
            g0 = OFF_GATES - W_GATE_START
            wgate_ref[...] = wrawg_ref[:, g0:g0 + 2 * D_MODEL]

    n_chains = vt_ref.shape[0]
    cr = tm // n_chains
    rows = [slice(c * cr, (c + 1) * cr) for c in range(n_chains)]
    x = [x_ref[r] for r in rows]
    xn = [_rms(v, n1_ref[...]).astype(BF16) for v in x]

    n_ffn = D_FF // FFN_CHUNK
    n_conv = 3 * GDN_W // LANES
    acts = [[] for _ in rows]
    for c in range(max(n_ffn, n_conv)):
        if c < n_ffn:
            cols = slice(c * FFN_CHUNK, (c + 1) * FFN_CHUNK)
            for a, v in zip(acts, xn):
                a.append((_silu(_dot(v, wg_ref[:, cols])) * _dot(v, wu_ref[:, cols])).astype(BF16))
        if c < n_conv:
            lanes = slice(c * LANES, (c + 1) * LANES)
            y = sum(convw_ref[t:t + 1, lanes] * xe_ref[pl.ds(8 - (CONV_K - 1) + t, tm), lanes]
                    for t in range(CONV_K))
            gqkv_ref[:, lanes] = _silu(y)
    h1 = [v + 0.5 * _dot(jnp.concatenate(a, axis=1), wd_ref[...]) for v, a in zip(x, acts)]
    for r, v in zip(rows, h1):
        h1_ref[r] = v

    un = [_rms(v, nm_ref[...]).astype(BF16) for v in h1]
    proj = [_dot(v, win_ref[...]) for v in un]

    bd_ones = bd_ones_ref[...]
    gq = gqkv_ref[:, :GDN_W]
    gk = gqkv_ref[:, GDN_W:2 * GDN_W]
    gqkv_ref[:, :GDN_W] = gq * lax.rsqrt(_head_sums(gq * gq, bd_ones) + EPS) * (GDN_DK ** -0.5)
    gqkv_ref[:, GDN_W:2 * GDN_W] = gk * lax.rsqrt(_head_sums(gk * gk, bd_ones) + EPS)
    xe_ref[0:8] = jnp.where(step % tiles_per_seq == 0, c0_ref[...], xe_ref[tm:tm + 8])
    o_raw = MLA_Q_RANK + MLA_KV_RANK + LANES
    for r, p in zip(rows, proj):
        xe_ref[8 + r.start:8 + r.stop] = p[:, o_raw:o_raw + 3 * GDN_W]
        gba_ref[r] = p[:, o_raw + 3 * GDN_W:o_raw + 3 * GDN_W + LANES]
        z_ref[r] = p[:, o_raw + 3 * GDN_W + LANES:]
    tail_ref[...] = proj[-1][cr - 8:, o_raw:o_raw + 3 * GDN_W]

    cqn = [_rms(p[:, :MLA_Q_RANK], qn_ref[...]).astype(BF16) for p in proj]
    qa = [_dot(v, wqa_ref[...]) for v in cqn]
    qb = [_dot(v, wqb_ref[...]) for v in cqn]
    for r, a, b in zip(rows, qa, qb):
        tc = jnp.concatenate([tc_ref[r]] * MLA_HEADS, axis=1)
        ts = jnp.concatenate([ts_ref[r]] * MLA_HEADS, axis=1)
        q_ref[r] = (a * tc + b * ts).astype(BF16)

    ckvn = [_rms(p[:, MLA_Q_RANK:MLA_Q_RANK + MLA_KV_RANK], kvn_ref[...]).astype(BF16) for p in proj]
    for c, (r, p, v) in enumerate(zip(rows, proj, ckvn)):
        kr = p[:, MLA_Q_RANK + MLA_KV_RANK:o_raw]
        kr_terms = jnp.concatenate([kr * tkc_ref[r], kr * tks_ref[r]], axis=1).astype(BF16)
        k_ref[r] = (_dot(v, wk_ref[...]) + _dot(kr_terms, e_ref[...])).astype(BF16)
        vt_ref[c] = _dot_nt(wvt_ref[...], v).astype(BF16)


def _token_proj(x2d, tabs, conv_carry, weights, tm, tiles_per_seq, chain_rows):
    n = x2d.shape[0]
    assert n % tm == 0
    last = n // tm - 1
    tile = lambda i: jnp.minimum(i, last)
    row = lambda w: pl.BlockSpec((tm, w), lambda i: (tile(i), 0))
    tab = pl.BlockSpec((tm, LANES), lambda i: (tile(i) % tiles_per_seq, 0))
    out_widths = (D_MODEL, QK_W, QK_W, None, 3 * GDN_W, LANES, GDN_W)
    out_dtypes = (F32, BF16, BF16, BF16, F32, F32, F32)
    assert tm % chain_rows == 0
    vt_spec = pl.BlockSpec((tm // chain_rows, V_W, chain_rows), lambda i: (tile(i), 0, 0))
    conv_spec = pl.BlockSpec((tm, 3 * GDN_W), lambda i: (jnp.maximum(i - 1, 0), 0))
    tail_shape = (8, 3 * GDN_W)
    out_specs = [vt_spec if w is None else row(w) for w in out_widths]
    out_specs[4] = conv_spec
    out_specs.append(pl.BlockSpec(tail_shape, lambda i: (0, 0)))
    out_shape = [jax.ShapeDtypeStruct((n // chain_rows, V_W, chain_rows) if w is None else (n, w), d)
                 for w, d in zip(out_widths, out_dtypes)] + [jax.ShapeDtypeStruct(tail_shape, F32)]

    regroup = any(w.shape == (D_MODEL, D_IN) for w in weights)
    operands, w_specs = [], []
    for w in weights:
        if w.shape == (D_MODEL, D_IN):
            assert 2 * W_GATE_START >= D_IN
            operands += [w, w]
            w_specs += [_const_spec((D_MODEL, W_TOKEN_COLS)),
                        pl.BlockSpec((D_MODEL, W_GATE_START), lambda i: (0, 1), pipeline_mode=pl.Buffered(1))]
        else:
            operands.append(w)
            w_specs.append(_const_spec(w.shape))
    if regroup:
        for shape in ((D_MODEL, PROJ_W), (D_MODEL, 2 * D_MODEL)):
            out_specs.append(pl.BlockSpec(shape, lambda i: (0, 0)))
            out_shape.append(jax.ShapeDtypeStruct(shape, BF16))
    return pl.pallas_call(
        functools.partial(_token_proj_kernel, tiles_per_seq=tiles_per_seq, regroup=regroup),
        grid=(n // tm + 1,),
        in_specs=[row(D_MODEL)] + [tab] * len(tabs) + [_const_spec(tail_shape)] + w_specs,
        out_specs=out_specs,
        out_shape=out_shape,
        scratch_shapes=[pltpu.VMEM((8 + tm, 3 * GDN_W), F32)],
        compiler_params=pltpu.CompilerParams(dimension_semantics=("arbitrary",),
                                             vmem_limit_bytes=VMEM_LIMIT),
        name="token_proj",
    )(x2d, *tabs, conv_carry, *operands)


def _mla_kernel(q_ref, k_ref, vt_ref, km_ref, vmt_ref, o_ref, st_ref, *, tq, tk):
    def tile(qi, carry):
        rows = pl.ds(pl.multiple_of(qi * tq, tq), tq)
        _mla_tile(qi, q_ref.at[rows], k_ref, vt_ref, km_ref, vmt_ref, o_ref.at[rows], st_ref, tq=tq, tk=tk)
        return carry

    lax.fori_loop(0, q_ref.shape[0] // tq, tile, 0)


def _mla_tile(qi, q_ref, k_ref, vt_ref, km_ref, vmt_ref, o_ref, st_ref, *, tq, tk):
    heads = range(2)
    hs = [slice(h * HEAD_PAD, (h + 1) * HEAD_PAD) for h in heads]
    vs = [slice(h * MLA_V, (h + 1) * MLA_V) for h in heads]
    q = [q_ref[:, s] for s in hs]
    colmax = lambda s: jnp.max(s, axis=0, keepdims=True)

    def with_ones(vt):
        return jnp.concatenate([vt, jnp.ones((8, vt.shape[1]), BF16)], axis=0)

    def scores(ki, slot, q_from=0):
        rows = pl.ds(pl.multiple_of(ki * tk, tk), tk)
        block_max = []
        for h in heads:
            s = _dot_nt(k_ref[rows, hs[h]], q[h][q_from:])
            st_ref[slot, h, :, q_from:] = s
            block_max.append(colmax(s))
        return block_max

    def update(ki, slot, m, acc, block_max, mask, q_from=0):
        st = [st_ref[slot, h, :, q_from:] for h in heads]
        if mask is not None:
            st = [jnp.where(mask, s, NEG_BIG) for s in st]
            block_max = [colmax(s) for s in st]
        m_old = [x[:, q_from:] for x in m]
        m_new = [jnp.maximum(m_old[h], block_max[h]) for h in heads]
        p = [jnp.exp2(st[h] - m_new[h]).astype(BF16) for h in heads]
        vt = vt_ref[ki]
        acc_new = [jnp.exp2(m_old[h] - m_new[h]) * acc[h][:, q_from:] + _dot(with_ones(vt[vs[h]]), p[h])
                   for h in heads]
        if q_from:
            m_new = [jnp.concatenate([m[h][:, :q_from], m_new[h]], axis=1) for h in heads]
            acc_new = [jnp.concatenate([acc[h][:, :q_from], acc_new[h]], axis=1) for h in heads]
        return m_new, acc_new

    bm_a = scores(0, 0)
    bm_b = scores(1, 1)
    meta_valid = lax.broadcasted_iota(jnp.int32, (km_ref.shape[0], tq), 0) < N_META
    st = [jnp.where(meta_valid, _dot_nt(km_ref[:, hs[h]], q[h]), NEG_BIG) for h in heads]
    m = [colmax(s) for s in st]
    acc = [_dot(with_ones(vmt_ref[vs[h], :]), jnp.exp2(st[h] - m[h]).astype(BF16)) for h in heads]

    def stage(ka, carry, cur, nxt):
        m, acc, bm_a, bm_b = carry
        bm_a2 = scores(ka + 2, nxt[0])
        m, acc = update(ka, cur[0], m, acc, bm_a, None)
        bm_b2 = scores(ka + 3, nxt[1])
        m, acc = update(ka + 1, cur[1], m, acc, bm_b, None)
        return m, acc, bm_a2, bm_b2

    def body(i, carry):
        return stage(4 * i + 2, stage(4 * i, carry, (0, 1), (2, 3)), (2, 3), (0, 1))

    causal = (lax.broadcasted_iota(jnp.int32, (tk, tq), 1) >= lax.broadcasted_iota(jnp.int32, (tk, tq), 0))
    n_full = 2 * qi

    def finish(carry, cur):
        m, acc, _, _ = carry
        m, acc = update(n_full, cur[0], m, acc, None, causal)
        m, acc = update(n_full + 1, cur[1], m, acc, None, causal[:, :tk], q_from=tk)
        return acc

    carry = lax.fori_loop(0, qi // 2, body, (m, acc, bm_a, bm_b))
    acc = lax.cond(qi % 2 == 1,
                   lambda c: finish(stage(n_full - 2, c, (0, 1), (2, 3)), (2, 3)),
                   lambda c: finish(c, (0, 1)), carry)
    o_ref[...] = jnp.concatenate([(a[:MLA_V] * (1.0 / a[MLA_V:MLA_V + 1])).T for a in acc],
                                 axis=1).astype(BF16)


def _mla_attn(q, k, vt, k_meta, vt_meta, batch, seq, tq, tk):
    nq = seq // tq
    assert vt.shape[2] == tk and tq == 2 * tk
    kern = functools.partial(_mla_kernel, tq=tq, tk=tk)
    return pl.pallas_call(
        kern,
        grid=(batch, MLA_HEADS // 2),
        in_specs=[
            pl.BlockSpec((seq, 2 * HEAD_PAD), lambda b, hp: (b, hp)),
            pl.BlockSpec((seq, 2 * HEAD_PAD), lambda b, hp: (b, hp)),
            pl.BlockSpec((seq // tk, 2 * MLA_V, tk), lambda b, hp: (b, hp, 0)),
            pl.BlockSpec((k_meta.shape[0], 2 * HEAD_PAD), lambda b, hp: (0, hp)),
            pl.BlockSpec((2 * MLA_V, vt_meta.shape[1]), lambda b, hp: (hp, 0)),
        ],
        out_specs=pl.BlockSpec((seq, 2 * MLA_V), lambda b, hp: (b, hp)),
        out_shape=jax.ShapeDtypeStruct((batch * seq, V_W), BF16),
        scratch_shapes=[pltpu.VMEM((4, 2, tk, tq), F32)],
        compiler_params=pltpu.CompilerParams(
            dimension_semantics=("arbitrary", "arbitrary"),
            vmem_limit_bytes=VMEM_LIMIT),
        name="mla_attn",
    )(q, k, vt, k_meta, vt_meta)


LEVELS = (1, 2, 4, 8, 16, 32)
GDN_CONST_NAMES = ("expand_b", "expand_a", "ltri", "eye_t", "tril_t", "stril_t", "level_masks", "bd_ones")


def _gdn_constants(cps):
    i = np.arange(CHUNK)[:, None]
    lane = np.arange(GDN_W)[None, :]
    j = lane % GDN_DK
    c = {}
    r = np.arange(LANES)[:, None]
    c["expand_b"] = (r == lane // GDN_DK)
    c["expand_a"] = (r == GDN_HEADS + lane // GDN_DK)
    t = np.arange(cps * CHUNK)
    c["ltri"] = (t[:, None] >= t[None, :]) & (t[:, None] // CHUNK == t[None, :] // CHUNK)
    c["eye_t"] = (i == j)
    c["tril_t"] = (i >= j)
    c["stril_t"] = (i > j)
    jg = j[:, :GROUP_W]
    c["level_masks"] = np.stack([
        ((i // (2 * s) == jg // (2 * s)) & ((i // s) % 2 == 1) & ((jg // s) % 2 == 0))
        for s in LEVELS])
    g = np.arange(GROUP_W)
    c["bd_ones"] = (g[:, None] // GDN_DK == g[None, :] // GDN_DK)
    bf = ("bd_ones", "expand_b", "expand_a", "ltri")
    return [jnp.asarray(c[k].astype(np.float32), BF16 if k in bf else F32) for k in GDN_CONST_NAMES]


def _block_diag(y, lo_half):
    zeros = jnp.zeros((GDN_DK, LANES), y.dtype)
    blocks = []
    for h in range(GROUP_HEADS):
        t = h // 2
        tile = y[:, t * LANES:(t + 1) * LANES]
        piece = jnp.where(lo_half, tile, 0) if h % 2 == 0 else jnp.where(lo_half, 0, tile)
        blocks.append(jnp.concatenate([piece, zeros] if t == 0 else [zeros, piece], axis=1))
    return jnp.concatenate(blocks, axis=0)


def _head_matmul(x, y, lo_half):
    return _dot(x.astype(BF16), _block_diag(y.astype(BF16), lo_half))


def _diag_blocks(m, lo_half):
    tiles = []
    for t in range(m.shape[1] // LANES):
        h = 2 * (t % 2)
        cols = slice(t * LANES, (t + 1) * LANES)
        tiles.append(jnp.where(lo_half, m[h * GDN_DK:(h + 1) * GDN_DK, cols],
                               m[(h + 1) * GDN_DK:(h + 2) * GDN_DK, cols]))
    return jnp.concatenate(tiles, axis=1)


def _gdn_kernel(gx_ref, ba_ref, z_ref, s0_ref, arate_ref, dtb_ref, gnorm_ref,
                expand_b_ref, expand_a_ref, ltri_ref, eye_t_ref, tril_t_ref, stril_t_ref,
                level_masks_ref, bd_ones_ref,
                o_ref, sout_ref, state_ref, *, cps, front_pad):
    c = pl.program_id(1)
    rows = cps * CHUNK

    @pl.when(c == 0)
    def _():
        state_ref[...] = s0_ref[...]

    qn = gx_ref[:, :GDN_W]
    kn = gx_ref[:, GDN_W:2 * GDN_W]
    v = gx_ref[:, 2 * GDN_W:]

    ba = ba_ref[...]
    beta = jax.nn.sigmoid(ba)
    sp_in = ba + dtb_ref[...]
    g = arate_ref[...] * (jnp.maximum(sp_in, 0.0) + jnp.log1p(jnp.exp(-jnp.abs(sp_in))))
    if front_pad:
        valid = lax.broadcasted_iota(jnp.int32, (rows, 1), 0) >= front_pad
        beta = jnp.where(valid, beta, 0.0)
        g = jnp.where(valid, g, 0.0)
    beta_e = _dot_exact_rhs(beta, expand_b_ref[...], 2)
    gc = _dot_exact_lhs(ltri_ref[...], g, 3)
    gc_e = _dot_exact_rhs(gc, expand_a_ref[...], 3)
    egc = jnp.exp(gc_e)
    kb = kn * beta_e
    vb = v * beta_e
    kbg = kb * egc
    qg = qn * egc
    eye, tril, stril = eye_t_ref[...], tril_t_ref[...], stril_t_ref[...]
    lo_half = lax.broadcasted_iota(jnp.int32, (1, LANES), 1) < GDN_DK

    chains = [(j, gi) for j in range(cps) for gi in range(N_GROUPS)]
    rsl = lambda j: slice(j * CHUNK, (j + 1) * CHUNK)
    gsl = lambda gi: slice(gi * GROUP_W, (gi + 1) * GROUP_W)
    decay, kdec, s_decay = [], [], []
    for j in range(cps):
        gce = gc_e[rsl(j)]
        gc_t = jnp.sum(gce * eye, axis=0, keepdims=True)
        decay.append(jnp.exp(jnp.where(tril > 0.5, gce - gc_t, -jnp.inf)))
        g_last = gce[CHUNK - 1:CHUNK]
        kdec.append(kn[rsl(j)] * jnp.exp(g_last - gce))
        s_decay.append(jnp.exp(g_last))

    lmat, amat = [], []
    for j, gi in chains:
        rs, gs = rsl(j), gsl(gi)
        k4 = kn[rs, gs].astype(BF16)
        kstack = _block_diag(k4, lo_half)
        lhs = jnp.concatenate([kb[rs, gs], qn[rs, gs]], axis=0).astype(BF16)
        sc = _dot_nt(lhs, kstack)
        lmat.append(sc[:CHUNK] * decay[j][:, gs] * stril[:, gs])
        amat.append(sc[CHUNK:] * decay[j][:, gs])

    xinv = [eye[:, gsl(gi)] - lm * level_masks_ref[0] for (j, gi), lm in zip(chains, lmat)]
    for li in range(1, len(LEVELS)):
        xo = [_head_matmul(x, lm * level_masks_ref[li], lo_half) for x, lm in zip(xinv, lmat)]
        xinv = [x - _head_matmul(y, x, lo_half) for x, y in zip(xinv, xo)]

    uw = []
    for (j, gi), x in zip(chains, xinv):
        rs, gs = rsl(j), gsl(gi)
        rhs = jnp.concatenate([_block_diag(vb[rs, gs].astype(BF16), lo_half),
                               _block_diag(kbg[rs, gs].astype(BF16), lo_half)], axis=1)
        uw.append(_dot(x.astype(BF16), rhs))
    kwu = [_diag_blocks(_dot(kdec[j][:, gsl(gi)].T.astype(BF16), m.astype(BF16)), lo_half)
           for (j, gi), m in zip(chains, uw)]

    states = [state_ref[gi] for gi in range(N_GROUPS)]
    o_rows = [[None] * N_GROUPS for _ in range(cps)]
    for ci, (j, gi) in enumerate(chains):
        rs, gs = rsl(j), gsl(gi)
        u, wmat = uw[ci][:, :GROUP_W], uw[ci][:, GROUP_W:]
        ku, kw = kwu[ci][:, :GROUP_W], kwu[ci][:, GROUP_W:]
        state = states[gi]
        lhs = jnp.concatenate([kw, wmat, qg[rs, gs]], axis=0).astype(BF16)
        big = _dot(lhs, _block_diag(state.astype(BF16), lo_half))
        states[gi] = s_decay[j][:, gs] * state - big[:CHUNK] + ku
        v_new = u - big[CHUNK:2 * CHUNK]
        o_rows[j][gi] = big[2 * CHUNK:] + _head_matmul(amat[ci], v_new, lo_half)
    for gi in range(N_GROUPS):
        state_ref[gi] = states[gi]

    @pl.when(c == pl.num_programs(1) - 1)
    def _():
        sout_ref[...] = state_ref[...]

    o = jnp.concatenate([jnp.concatenate(r, axis=1) for r in o_rows], axis=0)
    ms = _head_sums(o * o, bd_ones_ref[...]) * (1.0 / GDN_DV)
    o = o * lax.rsqrt(ms + EPS) * gnorm_ref[...] * _silu(z_ref[...])
    o_ref[...] = o.astype(BF16)


def _gdn(gqkv, gba, z, state0, params, batch, seq, cps, front_pad):
    rows = cps * CHUNK
    steps = seq // rows
    consts = _gdn_constants(cps)
    xrow = lambda w: pl.BlockSpec((rows, w), lambda b, c: (b * steps + c, 0))
    state_shape = (N_GROUPS, GDN_DK, GROUP_W)
    return pl.pallas_call(
        functools.partial(_gdn_kernel, cps=cps, front_pad=front_pad),
        grid=(batch, steps),
        in_specs=[xrow(3 * GDN_W), xrow(LANES), xrow(GDN_W), _const_spec(state_shape)]
                 + [_const_spec(a.shape) for a in params]
                 + [_const_spec(a.shape) for a in consts],
        out_specs=[xrow(GDN_W), pl.BlockSpec(state_shape, lambda b, c: (0, 0, 0))],
        out_shape=[jax.ShapeDtypeStruct((batch * seq, GDN_W), BF16),
                   jax.ShapeDtypeStruct(state_shape, F32)],
        scratch_shapes=[pltpu.VMEM(state_shape, F32)],
        compiler_params=pltpu.CompilerParams(dimension_semantics=("arbitrary", "arbitrary"),
                                             vmem_limit_bytes=VMEM_LIMIT),
        name="gdn_chunk",
    )(gqkv, gba, z, state0, *params, *consts)


def _merge_ffn_kernel(h1_ref, om_ref, og_ref, nm_ref, wgate_ref, wmo_ref, wgo_ref, wout_ref,
                      n2_ref, wg_ref, wu_ref, wd_ref, nf_ref, out_ref):
    n_chains = max(1, h1_ref.shape[0] // MERGE_CHAIN_ROWS)
    rows = [slice(c * MERGE_CHAIN_ROWS, (c + 1) * MERGE_CHAIN_ROWS) if n_chains > 1 else slice(None)
            for c in range(n_chains)]
    h1 = [h1_ref[r] for r in rows]
    un = [_rms(h, nm_ref[...]).astype(BF16) for h in h1]
    gates = [jax.nn.sigmoid(_dot(u, wgate_ref[...])) for u in un]
    merged = [(g[:, :D_MODEL] * _dot(om_ref[r], wmo_ref[...])
               + g[:, D_MODEL:] * _dot(og_ref[r], wgo_ref[...])).astype(BF16) for g, r in zip(gates, rows)]
    h2 = [h + _dot(m, wout_ref[...]) for h, m in zip(h1, merged)]
    xn = [_rms(h, n2_ref[...]).astype(BF16) for h in h2]
    acts = [[] for _ in rows]
    for c in range(D_FF // FFN_CHUNK):
        cols = slice(c * FFN_CHUNK, (c + 1) * FFN_CHUNK)
        for a, x in zip(acts, xn):
            a.append((_silu(_dot(x, wg_ref[:, cols])) * _dot(x, wu_ref[:, cols])).astype(BF16))
    h3 = [h + 0.5 * _dot(jnp.concatenate(a, axis=1), wd_ref[...]) for h, a in zip(h2, acts)]
    for r, h in zip(rows, h3):
        out_ref[r] = _rms(h, nf_ref[...])


def _merge_ffn(h1, o_mla, o_gdn, weights, tm):
    n = h1.shape[0]
    row = lambda w: pl.BlockSpec((tm, w), lambda i: (i, 0))
    return pl.pallas_call(
        _merge_ffn_kernel,
        grid=(n // tm,),
        in_specs=[row(D_MODEL), row(V_W), row(GDN_W)] + [_const_spec(w.shape) for w in weights],
        out_specs=row(D_MODEL),
        out_shape=jax.ShapeDtypeStruct((n, D_MODEL), F32),
        compiler_params=pltpu.CompilerParams(dimension_semantics=("arbitrary",),
                                             vmem_limit_bytes=VMEM_LIMIT),
        name="merge_ffn",
    )(h1, o_mla, o_gdn, *weights)


def _rope_tables(first_pos, n):
    pos = np.arange(first_pos, first_pos + n, dtype=np.float64)
    inv = ROPE_THETA ** (-np.arange(0, MLA_ROPE, 2, dtype=np.float64) / MLA_ROPE)
    ang = pos[:, None] * inv[None, :]
    cos2 = np.tile(np.cos(ang), (1, 2))
    sin2 = np.tile(np.sin(ang), (1, 2))
    scale = (MLA_NOPE + MLA_ROPE) ** -0.5 * math.log2(math.e)
    pad = np.zeros((n, HEAD_PAD - MLA_NOPE - MLA_ROPE))
    tab_c = np.concatenate([np.ones((n, MLA_NOPE)), cos2, pad], axis=1) * scale
    tab_s = np.concatenate([np.zeros((n, MLA_NOPE)), sin2, pad], axis=1) * scale
    kpad = np.zeros((n, LANES - MLA_ROPE))
    tab_kc = np.concatenate([cos2, kpad], axis=1)
    tab_ks = np.concatenate([sin2, kpad], axis=1)
    return tuple(jnp.asarray(t, F32) for t in (tab_c, tab_s, tab_kc, tab_ks))


def _rot(w):
    half = MLA_ROPE // 2
    return jnp.concatenate([-w[..., half:], w[..., :half]], axis=-1)


def kernel(x, meta_tokens, ffn1_norm, ffn1_w_gate, ffn1_w_up, ffn1_w_down, mix_norm, w_in, q_norm, w_uq,
           kv_norm, w_ukv, w_mla_o, conv_w, a_log, dt_bias, gdn_norm, w_gdn_o, w_out, ffn2_norm,
           ffn2_w_gate, ffn2_w_up, ffn2_w_down, final_norm):
    assert ffn1_norm.shape[0] == 1, "single-layer block"
    batch, seq, d = x.shape
    assert d == D_MODEL and seq % CHUNK == 0
    tm = min(256, seq)
    tq = min(512, seq)
    assert seq % tm == 0 and seq % tq == 0

    assert w_in.shape[2] == D_IN
    w_in_bf = w_in[0].astype(BF16)

    wq = w_uq[0].reshape(MLA_Q_RANK, MLA_HEADS, MLA_NOPE + MLA_ROPE)
    wq_nope, wq_rope = wq[..., :MLA_NOPE], wq[..., MLA_NOPE:]
    zq = lambda n: jnp.zeros((MLA_Q_RANK, MLA_HEADS, n), F32)
    tail = HEAD_PAD - MLA_NOPE - MLA_ROPE
    wqa = jnp.concatenate([wq_nope, wq_rope, zq(tail)], axis=-1).reshape(MLA_Q_RANK, QK_W).astype(BF16)
    wqb = jnp.concatenate([zq(MLA_NOPE), _rot(wq_rope), zq(tail)], axis=-1).reshape(MLA_Q_RANK, QK_W).astype(BF16)
    wkv = w_ukv[0].reshape(MLA_KV_RANK, MLA_HEADS, MLA_NOPE + MLA_V)
    wk = jnp.concatenate([wkv[..., :MLA_NOPE], jnp.zeros((MLA_KV_RANK, MLA_HEADS, HEAD_PAD - MLA_NOPE), F32)],
                         axis=-1).reshape(MLA_KV_RANK, QK_W).astype(BF16)
    wvt = wkv[..., MLA_NOPE:].reshape(MLA_KV_RANK, V_W).T.astype(BF16)
    e_np = np.zeros((2 * LANES, QK_W), np.float32)
    half = MLA_ROPE // 2
    for h in range(MLA_HEADS):
        base = h * HEAD_PAD + MLA_NOPE
        for j in range(MLA_ROPE):
            e_np[j, base + j] = 1.0
        for j in range(half):
            e_np[LANES + half + j, base + j] = -1.0
            e_np[LANES + j, base + half + j] = 1.0
    e_mat = jnp.asarray(e_np, BF16)

    g = np.arange(GROUP_W)
    bd_ones = jnp.asarray((g[:, None] // GDN_DK == g[None, :] // GDN_DK).astype(np.float32), BF16)
    proj_weights = [
        ffn1_norm[0][None], ffn1_w_gate[0].astype(BF16), ffn1_w_up[0].astype(BF16),
        ffn1_w_down[0].astype(BF16), mix_norm[0][None], w_in_bf, q_norm[0][None], wqa, wqb,
        kv_norm[0][None], wk, wvt, e_mat, conv_w[0].astype(F32), bd_ones]

    zero_carry = jnp.zeros((8, 3 * GDN_W), F32)
    _, _, k_m, vt_m, gqkv_m, gba_m, _, meta_tail, win_a, wgate = _token_proj(
        meta_tokens.astype(F32), _rope_tables(0, N_META), zero_carry, proj_weights, N_META, 1, N_META)
    proj_weights = [win_a if w is w_in_bf else w for w in proj_weights]
    tp = 2 * tm if seq % (2 * tm) == 0 else tm
    h1, q, k, vt, gqkv, gba, z, _ = _token_proj(
        x.reshape(batch * seq, d), _rope_tables(N_META, seq), meta_tail, proj_weights, tp, seq // tp, tm)

    pad_rows = lambda a, n, front: jnp.pad(a, ((n - a.shape[0], 0) if front else (0, n - a.shape[0]), (0, 0)))
    vt_meta = jnp.pad(vt_m[0], ((0, 0), (0, LANES - N_META)))
    o_mla = _mla_attn(q, k, vt, pad_rows(k_m, LANES, False), vt_meta, batch, seq, tq, tm)

    hpad = lambda a: jnp.zeros((1, LANES), F32).at[0, GDN_HEADS:2 * GDN_HEADS].set(a)
    arate = hpad(-jnp.exp(a_log[0].astype(F32)))
    dtb = hpad(dt_bias[0].astype(F32))
    gnorm = jnp.tile(gdn_norm[0].astype(F32), GDN_HEADS)[None]
    gdn_params = (arate, dtb, gnorm)
    cps = 4 if seq % (4 * CHUNK) == 0 else 1
    _, state_meta = _gdn(pad_rows(gqkv_m, CHUNK, True), pad_rows(gba_m, CHUNK, True),
                         jnp.zeros((CHUNK, GDN_W), F32), jnp.zeros((N_GROUPS, GDN_DK, GROUP_W), F32),
                         gdn_params, 1, CHUNK, 1, CHUNK - N_META)
    o_gdn, _ = _gdn(gqkv, gba, z, state_meta, gdn_params, batch, seq, cps, 0)

    merge_weights = [
        mix_norm[0][None], wgate, w_mla_o[0].astype(BF16), w_gdn_o[0].astype(BF16), w_out[0].astype(BF16),
        ffn2_norm[0][None], ffn2_w_gate[0].astype(BF16), ffn2_w_up[0].astype(BF16),
        ffn2_w_down[0].astype(BF16), final_norm[None]]
    merge_rows = 2 * MERGE_CHAIN_ROWS if (batch * seq) % (2 * MERGE_CHAIN_ROWS) == 0 else tm
    out = _merge_ffn(h1, o_mla, o_gdn, merge_weights, merge_rows)
    return out.reshape(batch, seq, d)
```

```python
import functools
import math

import jax
import jax.numpy as jnp
import numpy as np
from jax import lax
from jax.experimental import pallas as pl
from jax.experimental.pallas import tpu as pltpu

F32 = jnp.float32
BF16 = jnp.bfloat16

D_MODEL = 1024
N_META = 16
EPS = 1e-6
D_FF = 2816
MLA_HEADS = 8
MLA_Q_RANK = 256
MLA_KV_RANK = 128
MLA_NOPE = 64
MLA_ROPE = 32
MLA_V = 64
ROPE_THETA = 10000.0
GDN_HEADS = 8
GDN_DK = 64
GDN_DV = 64
CONV_K = 4
CHUNK = 64

LANES = 128
HEAD_PAD = 128
QK_W = MLA_HEADS * HEAD_PAD
V_W = MLA_HEADS * MLA_V
GDN_W = GDN_HEADS * GDN_DK
GROUP_HEADS = 4
GROUP_W = GROUP_HEADS * GDN_DK
N_GROUPS = GDN_HEADS // GROUP_HEADS
FFN_CHUNK = 256
MERGE_CHAIN_ROWS = 256
PROJ_W = MLA_Q_RANK + MLA_KV_RANK + LANES + 3 * GDN_W + LANES + GDN_W
W_IN_SPLITS = (MLA_Q_RANK, MLA_KV_RANK, MLA_ROPE, GDN_W, GDN_W, GDN_W, GDN_HEADS, GDN_HEADS, GDN_W,
               D_MODEL, D_MODEL)
(OFF_CQ, OFF_CKV, OFF_KR, OFF_GQ, OFF_GK, OFF_GV, OFF_GB, OFF_GA, OFF_GZ, OFF_GATES, _OFF_GATE_GDN,
 D_IN) = (int(v) for v in np.concatenate([[0], np.cumsum(W_IN_SPLITS)]))
W_TOKEN_COLS = -(-OFF_GATES // LANES) * LANES
W_GATE_START = OFF_GATES // LANES * LANES
VMEM_LIMIT = 60 * 1024 * 1024
NEG_BIG = -1e30


def _const_spec(shape):
    zeros = (0,) * len(shape)
    return pl.BlockSpec(shape, lambda *_: zeros, pipeline_mode=pl.Buffered(1))


def _rms(x, w):
    return x * lax.rsqrt(jnp.mean(x * x, axis=-1, keepdims=True) + EPS) * w


def _dot(a, b):
    return jnp.dot(a, b, preferred_element_type=F32)


def _dot_nt(a, b):
    return lax.dot_general(a, b, (((1,), (1,)), ((), ())), preferred_element_type=F32)


def _silu(x):
    return x * jax.nn.sigmoid(x)


def _split(x, n):
    pieces = []
    for _ in range(n - 1):
        hi = x.astype(BF16)
        pieces.append(hi)
        x = x - hi.astype(F32)
    pieces.append(x.astype(BF16))
    return pieces


def _dot_exact_rhs(x, rhs, n):
    return sum(_dot(p, rhs) for p in _split(x, n))


def _dot_exact_lhs(lhs, x, n):
    return sum(_dot(lhs, p) for p in _split(x, n))


def _head_sums(x, bd_ones):
    return jnp.concatenate([_dot(x[:, g * GROUP_W:(g + 1) * GROUP_W].astype(BF16), bd_ones)
                            for g in range(N_GROUPS)], axis=1)


def _token_proj_kernel(*refs, tiles_per_seq, regroup):
    it = iter(refs)
    take = lambda n: [next(it) for _ in range(n)]
    x_ref, tc_ref, ts_ref, tkc_ref, tks_ref, c0_ref, n1_ref, wg_ref, wu_ref, wd_ref, nm_ref = take(11)
    if regroup:
        wraw_ref, wrawg_ref = take(2)
    else:
        win_ref, = take(1)
    qn_ref, wqa_ref, wqb_ref, kvn_ref, wk_ref, wvt_ref, e_ref, convw_ref, bd_ones_ref = take(9)
    h1_ref, q_ref, k_ref, vt_ref, gqkv_ref, gba_ref, z_ref, tail_ref = take(8)
    if regroup:
        win_ref, wgate_ref = take(2)
    xe_ref, = take(1)
    tm = x_ref.shape[0]
    step = pl.program_id(0)

    @pl.when(step == 0)
    def _():
        xe_ref[...] = jnp.zeros_like(xe_ref)
        if regroup:
            o = OFF_GQ + LANES - MLA_ROPE
            win_ref[:, :o] = wraw_ref[:, :o]
            win_ref[:, o:o + 3 * GDN_W] = wraw_ref[:, OFF_GQ:OFF_GB]
            o += 3 * GDN_W
            lane = lax.broadcasted_iota(jnp.int32, (1, LANES), 1)
            win_ref[:, o:o + LANES] = jnp.where(lane < 2 * GDN_HEADS, wraw_ref[:, OFF_GB:OFF_GB + LANES], 0)
            win_ref[:, o + LANES:] = wraw_ref[:, OFF_GZ:OFF_GATES]
            g0 = OFF_GATES - W_GATE_START
            wgate_ref[...] = wrawg_ref[:, g0:g0 + 2 * D_MODEL]

    n_chains = vt_ref.shape[0]
    cr = tm // n_chains
    rows = [slice(c * cr, (c + 1) * cr) for c in range(n_chains)]
    x = [x_ref[r] for r in rows]
    xn = [_rms(v, n1_ref[...]).astype(BF16) for v in x]

    n_ffn = D_FF // FFN_CHUNK
    n_conv = 3 * GDN_W // LANES
    acts = [[] for _ in rows]
    for c in range(max(n_ffn, n_conv)):
        if c < n_ffn:
            cols = slice(c * FFN_CHUNK, (c + 1) * FFN_CHUNK)
            for a, v in zip(acts, xn):
                a.append((_silu(_dot(v, wg_ref[:, cols])) * _dot(v, wu_ref[:, cols])).astype(BF16))
        if c < n_conv:
            lanes = slice(c * LANES, (c + 1) * LANES)
            y = sum(convw_ref[t:t + 1, lanes] * xe_ref[pl.ds(8 - (CONV_K - 1) + t, tm), lanes]
                    for t in range(CONV_K))
            gqkv_ref[:, lanes] = _silu(y)
    h1 = [v + 0.5 * _dot(jnp.concatenate(a, axis=1), wd_ref[...]) for v, a in zip(x, acts)]
    for r, v in zip(rows, h1):
        h1_ref[r] = v

    un = [_rms(v, nm_ref[...]).astype(BF16) for v in h1]
    proj = [_dot(v, win_ref[...]) for v in un]

    bd_ones = bd_ones_ref[...]
    gq = gqkv_ref[:, :GDN_W]
    gk = gqkv_ref[:, GDN_W:2 * GDN_W]
    gqkv_ref[:, :GDN_W] = gq * lax.rsqrt(_head_sums(gq * gq, bd_ones) + EPS) * (GDN_DK ** -0.5)
    gqkv_ref[:, GDN_W:2 * GDN_W] = gk * lax.rsqrt(_head_sums(gk * gk, bd_ones) + EPS)
    xe_ref[0:8] = jnp.where(step % tiles_per_seq == 0, c0_ref[...], xe_ref[tm:tm + 8])
    o_raw = MLA_Q_RANK + MLA_KV_RANK + LANES
    for r, p in zip(rows, proj):
        xe_ref[8 + r.start:8 + r.stop] = p[:, o_raw:o_raw + 3 * GDN_W]
        gba_ref[r] = p[:, o_raw + 3 * GDN_W:o_raw + 3 * GDN_W + LANES]
        z_ref[r] = p[:, o_raw + 3 * GDN_W + LANES:]
    tail_ref[...] = proj[-1][cr - 8:, o_raw:o_raw + 3 * GDN_W]

    cqn = [_rms(p[:, :MLA_Q_RANK], qn_ref[...]).astype(BF16) for p in proj]
    qa = [_dot(v, wqa_ref[...]) for v in cqn]
    qb = [_dot(v, wqb_ref[...]) for v in cqn]
    for r, a, b in zip(rows, qa, qb):
        tc = jnp.concatenate([tc_ref[r]] * MLA_HEADS, axis=1)
        ts = jnp.concatenate([ts_ref[r]] * MLA_HEADS, axis=1)
        q_ref[r] = (a * tc + b * ts).astype(BF16)

    ckvn = [_rms(p[:, MLA_Q_RANK:MLA_Q_RANK + MLA_KV_RANK], kvn_ref[...]).astype(BF16) for p in proj]
    for c, (r, p, v) in enumerate(zip(rows, proj, ckvn)):
        kr = p[:, MLA_Q_RANK + MLA_KV_RANK:o_raw]
        kr_terms = jnp.concatenate([kr * tkc_ref[r], kr * tks_ref[r]], axis=1).astype(BF16)
        k_ref[r] = (_dot(v, wk_ref[...]) + _dot(kr_terms, e_ref[...])).astype(BF16)
        vt_ref[c] = _dot_nt(wvt_ref[...], v).astype(BF16)


def _token_proj(x2d, tabs, conv_carry, weights, tm, tiles_per_seq, chain_rows):
    n = x2d.shape[0]
    assert n % tm == 0
    last = n // tm - 1
    tile = lambda i: jnp.minimum(i, last)
    row = lambda w: pl.BlockSpec((tm, w), lambda i: (tile(i), 0))
    tab = pl.BlockSpec((tm, LANES), lambda i: (tile(i) % tiles_per_seq, 0))
    out_widths = (D_MODEL, QK_W, QK_W, None, 3 * GDN_W, LANES, GDN_W)
    out_dtypes = (F32, BF16, BF16, BF16, F32, F32, F32)
    assert tm % chain_rows == 0
    vt_spec = pl.BlockSpec((tm // chain_rows, V_W, chain_rows), lambda i: (tile(i), 0, 0))
    conv_spec = pl.BlockSpec((tm, 3 * GDN_W), lambda i: (jnp.maximum(i - 1, 0), 0))
    tail_shape = (8, 3 * GDN_W)
    out_specs = [vt_spec if w is None else row(w) for w in out_widths]
    out_specs[4] = conv_spec
    out_specs.append(pl.BlockSpec(tail_shape, lambda i: (0, 0)))
    out_shape = [jax.ShapeDtypeStruct((n // chain_rows, V_W, chain_rows) if w is None else (n, w), d)
                 for w, d in zip(out_widths, out_dtypes)] + [jax.ShapeDtypeStruct(tail_shape, F32)]

    regroup = any(w.shape == (D_MODEL, D_IN) for w in weights)
    operands, w_specs = [], []
    for w in weights:
        if w.shape == (D_MODEL, D_IN):
            assert 2 * W_GATE_START >= D_IN
            operands += [w, w]
            w_specs += [_const_spec((D_MODEL, W_TOKEN_COLS)),
                        pl.BlockSpec((D_MODEL, W_GATE_START), lambda i: (0, 1), pipeline_mode=pl.Buffered(1))]
        else:
            operands.append(w)
            w_specs.append(_const_spec(w.shape))
    if regroup:
        for shape in ((D_MODEL, PROJ_W), (D_MODEL, 2 * D_MODEL)):
            out_specs.append(pl.BlockSpec(shape, lambda i: (0, 0)))
            out_shape.append(jax.ShapeDtypeStruct(shape, BF16))
    return pl.pallas_call(
        functools.partial(_token_proj_kernel, tiles_per_seq=tiles_per_seq, regroup=regroup),
        grid=(n // tm + 1,),
        in_specs=[row(D_MODEL)] + [tab] * len(tabs) + [_const_spec(tail_shape)] + w_specs,
        out_specs=out_specs,
        out_shape=out_shape,
        scratch_shapes=[pltpu.VMEM((8 + tm, 3 * GDN_W), F32)],
        compiler_params=pltpu.CompilerParams(dimension_semantics=("arbitrary",),
                                             vmem_limit_bytes=VMEM_LIMIT),
        name="token_proj",
    )(x2d, *tabs, conv_carry, *operands)


def _mla_kernel(q_ref, k_ref, vt_ref, km_ref, vmt_ref, o_ref, st_ref, *, tq, tk):
    def tile(qi, carry):
        rows = pl.ds(pl.multiple_of(qi * tq, tq), tq)
        _mla_tile(qi, q_ref.at[rows], k_ref, vt_ref, km_ref, vmt_ref, o_ref.at[rows], st_ref, tq=tq, tk=tk)
        return carry

    lax.fori_loop(0, q_ref.shape[0] // tq, tile, 0)


def _mla_tile(qi, q_ref, k_ref, vt_ref, km_ref, vmt_ref, o_ref, st_ref, *, tq, tk):
    heads = range(2)
    hs = [slice(h * HEAD_PAD, (h + 1) * HEAD_PAD) for h in heads]
    vs = [slice(h * MLA_V, (h + 1) * MLA_V) for h in heads]
    q = [q_ref[:, s] for s in hs]
    colmax = lambda s: jnp.max(s, axis=0, keepdims=True)

    def with_ones(vt):
        return jnp.concatenate([vt, jnp.ones((8, vt.shape[1]), BF16)], axis=0)

    def scores(ki, slot, q_from=0):
        rows = pl.ds(pl.multiple_of(ki * tk, tk), tk)
        block_max = []
        for h in heads:
            s = _dot_nt(k_ref[rows, hs[h]], q[h][q_from:])
            st_ref[slot, h, :, q_from:] = s
            block_max.append(colmax(s))
        return block_max

    def update(ki, slot, m, acc, block_max, mask, q_from=0):
        st = [st_ref[slot, h, :, q_from:] for h in heads]
        if mask is not None:
            st = [jnp.where(mask, s, NEG_BIG) for s in st]
            block_max = [colmax(s) for s in st]
        m_old = [x[:, q_from:] for x in m]
        m_new = [jnp.maximum(m_old[h], block_max[h]) for h in heads]
        p = [jnp.exp2(st[h] - m_new[h]).astype(BF16) for h in heads]
        vt = vt_ref[ki]
        acc_new = [jnp.exp2(m_old[h] - m_new[h]) * acc[h][:, q_from:] + _dot(with_ones(vt[vs[h]]), p[h])
                   for h in heads]
        if q_from:
            m_new = [jnp.concatenate([m[h][:, :q_from], m_new[h]], axis=1) for h in heads]
            acc_new = [jnp.concatenate([acc[h][:, :q_from], acc_new[h]], axis=1) for h in heads]
        return m_new, acc_new

    bm_a = scores(0, 0)
    bm_b = scores(1, 1)
    meta_valid = lax.broadcasted_iota(jnp.int32, (km_ref.shape[0], tq), 0) < N_META
    st = [jnp.where(meta_valid, _dot_nt(km_ref[:, hs[h]], q[h]), NEG_BIG) for h in heads]
    m = [colmax(s) for s in st]
    acc = [_dot(with_ones(vmt_ref[vs[h], :]), jnp.exp2(st[h] - m[h]).astype(BF16)) for h in heads]

    def stage(ka, carry, cur, nxt):
        m, acc, bm_a, bm_b = carry
        bm_a2 = scores(ka + 2, nxt[0])
        m, acc = update(ka, cur[0], m, acc, bm_a, None)
        bm_b2 = scores(ka + 3, nxt[1])
        m, acc = update(ka + 1, cur[1], m, acc, bm_b, None)
        return m, acc, bm_a2, bm_b2

    def body(i, carry):
        return stage(4 * i + 2, stage(4 * i, carry, (0, 1), (2, 3)), (2, 3), (0, 1))

    causal = (lax.broadcasted_iota(jnp.int32, (tk, tq), 1) >= lax.broadcasted_iota(jnp.int32, (tk, tq), 0))
    n_full = 2 * qi

    def finish(carry, cur):
        m, acc, _, _ = carry
        m, acc = update(n_full, cur[0], m, acc, None, causal)
        m, acc = update(n_full + 1, cur[1], m, acc, None, causal[:, :tk], q_from=tk)
        return acc

    carry = lax.fori_loop(0, qi // 2, body, (m, acc, bm_a, bm_b))
    acc = lax.cond(qi % 2 == 1,
                   lambda c: finish(stage(n_full - 2, c, (0, 1), (2, 3)), (2, 3)),
                   lambda c: finish(c, (0, 1)), carry)
    o_ref[...] = jnp.concatenate([(a[:MLA_V] * (1.0 / a[MLA_V:MLA_V + 1])).T for a in acc],
                                 axis=1).astype(BF16)


def _mla_attn(q, k, vt, k_meta, vt_meta, batch, seq, tq, tk):
    nq = seq // tq
    assert vt.shape[2] == tk and tq == 2 * tk
    kern = functools.partial(_mla_kernel, tq=tq, tk=tk)
    return pl.pallas_call(
        kern,
        grid=(batch, MLA_HEADS // 2),
        in_specs=[
            pl.BlockSpec((seq, 2 * HEAD_PAD), lambda b, hp: (b, hp)),
            pl.BlockSpec((seq, 2 * HEAD_PAD), lambda b, hp: (b, hp)),
            pl.BlockSpec((seq // tk, 2 * MLA_V, tk), lambda b, hp: (b, hp, 0)),
            pl.BlockSpec((k_meta.shape[0], 2 * HEAD_PAD), lambda b, hp: (0, hp)),
            pl.BlockSpec((2 * MLA_V, vt_meta.shape[1]), lambda b, hp: (hp, 0)),
        ],
        out_specs=pl.BlockSpec((seq, 2 * MLA_V), lambda b, hp: (b, hp)),
        out_shape=jax.ShapeDtypeStruct((batch * seq, V_W), BF16),
        scratch_shapes=[pltpu.VMEM((4, 2, tk, tq), F32)],
        compiler_params=pltpu.CompilerParams(
            dimension_semantics=("arbitrary", "arbitrary"),
            vmem_limit_bytes=VMEM_LIMIT),
        name="mla_attn",
    )(q, k, vt, k_meta, vt_meta)


LEVELS = (1, 2, 4, 8, 16, 32)
GDN_CONST_NAMES = ("expand_b", "expand_a", "ltri", "eye_t", "tril_t", "stril_t", "level_masks", "bd_ones")


def _gdn_constants(cps):
    i = np.arange(CHUNK)[:, None]
    lane = np.arange(GDN_W)[None, :]
    j = lane % GDN_DK
    c = {}
    r = np.arange(LANES)[:, None]
    c["expand_b"] = (r == lane // GDN_DK)
    c["expand_a"] = (r == GDN_HEADS + lane // GDN_DK)
    t = np.arange(cps * CHUNK)
    c["ltri"] = (t[:, None] >= t[None, :]) & (t[:, None] // CHUNK == t[None, :] // CHUNK)
    c["eye_t"] = (i == j)
    c["tril_t"] = (i >= j)
    c["stril_t"] = (i > j)
    jg = j[:, :GROUP_W]
    c["level_masks"] = np.stack([
        ((i // (2 * s) == jg // (2 * s)) & ((i // s) % 2 == 1) & ((jg // s) % 2 == 0))
        for s in LEVELS])
    g = np.arange(GROUP_W)
    c["bd_ones"] = (g[:, None] // GDN_DK == g[None, :] // GDN_DK)
    bf = ("bd_ones", "expand_b", "expand_a", "ltri")
    return [jnp.asarray(c[k].astype(np.float32), BF16 if k in bf else F32) for k in GDN_CONST_NAMES]


def _block_diag(y, lo_half):
    zeros = jnp.zeros((GDN_DK, LANES), y.dtype)
    blocks = []
    for h in range(GROUP_HEADS):
        t = h // 2
        tile = y[:, t * LANES:(t + 1) * LANES]
        piece = jnp.where(lo_half, tile, 0) if h % 2 == 0 else jnp.where(lo_half, 0, tile)
        blocks.append(jnp.concatenate([piece, zeros] if t == 0 else [zeros, piece], axis=1))
    return jnp.concatenate(blocks, axis=0)


def _head_matmul(x, y, lo_half):
    return _dot(x.astype(BF16), _block_diag(y.astype(BF16), lo_half))


def _diag_blocks(m, lo_half):
    tiles = []
    for t in range(m.shape[1] // LANES):
        h = 2 * (t % 2)
        cols = slice(t * LANES, (t + 1) * LANES)
        tiles.append(jnp.where(lo_half, m[h * GDN_DK:(h + 1) * GDN_DK, cols],
                               m[(h + 1) * GDN_DK:(h + 2) * GDN_DK, cols]))
    return jnp.concatenate(tiles, axis=1)


def _gdn_kernel(gx_ref, ba_ref, z_ref, s0_ref, arate_ref, dtb_ref, gnorm_ref,
                expand_b_ref, expand_a_ref, ltri_ref, eye_t_ref, tril_t_ref, stril_t_ref,
                level_masks_ref, bd_ones_ref,
                o_ref, sout_ref,
                state_ref, kn_ref, kbq_ref, vbk_ref, kdec_ref, qg_ref, decay_ref, sdec1_ref,
                lhs_ref, amat_ref, u_ref, ku_ref, sdec2_ref, *, cps, front_pad, blocks_per_seq):
    s = pl.program_id(0)
    rows = cps * CHUNK
    stage1 = (kn_ref, kbq_ref, vbk_ref, kdec_ref, qg_ref, decay_ref, sdec1_ref)
    stage2 = (lhs_ref, amat_ref, u_ref, ku_ref, sdec2_ref)

    @pl.when(s == 0)
    def _():
        for r in stage1 + stage2 + (state_ref,):
            r[...] = jnp.zeros_like(r)

    kn_r, kbq_r, vbk_r, kdec_r, qg_r, decay_r, sdec1_r = (r.at[1 - s % 2] for r in stage1)
    kn_w, kbq_w, vbk_w, kdec_w, qg_w, decay_w, sdec1_w = (r.at[s % 2] for r in stage1)

    lo_half = lax.broadcasted_iota(jnp.int32, (1, LANES), 1) < GDN_DK
    eye, tril, stril = eye_t_ref[...], tril_t_ref[...], stril_t_ref[...]
    chains = [(j, gi) for j in range(cps) for gi in range(N_GROUPS)]
    rsl = lambda j: slice(j * CHUNK, (j + 1) * CHUNK)
    gsl = lambda gi: slice(gi * GROUP_W, (gi + 1) * GROUP_W)

    first_of_seq = (s - 2) % blocks_per_seq == 0
    states = [jnp.where(first_of_seq, s0_ref[gi], state_ref[gi]) for gi in range(N_GROUPS)]
    o_rows = [[None] * N_GROUPS for _ in range(cps)]
    in_flight = {}

    def ride(i):
        if i - 1 in in_flight:
            for gi, big in enumerate(in_flight.pop(i - 1)):
                v_new = u_ref[i - 1, gi] - big[CHUNK:2 * CHUNK]
                o_rows[i - 1][gi] = big[2 * CHUNK:] + _dot(amat_ref[i - 1, gi],
                                                           _block_diag(v_new.astype(BF16), lo_half))
        if i < cps:
            bigs = []
            for gi in range(N_GROUPS):
                state = states[gi]
                big = _dot(lhs_ref[i, gi], _block_diag(state.astype(BF16), lo_half))
                states[gi] = sdec2_ref[i, gi][0:1] * state - big[:CHUNK] + ku_ref[i, gi]
                bigs.append(big)
            in_flight[i] = bigs

    ba = ba_ref[...]
    beta = jax.nn.sigmoid(ba)
    sp_in = ba + dtb_ref[...]
    g = arate_ref[...] * (jnp.maximum(sp_in, 0.0) + jnp.log1p(jnp.exp(-jnp.abs(sp_in))))
    if front_pad:
        valid = lax.broadcasted_iota(jnp.int32, (rows, 1), 0) >= front_pad
        beta = jnp.where(valid, beta, 0.0)
        g = jnp.where(valid, g, 0.0)
    beta_e = _dot_exact_rhs(beta, expand_b_ref[...], 2)
    gc = _dot_exact_lhs(ltri_ref[...], g, 3)
    ride(0)

    lmat, amat = [], []
    for j, gi in chains:
        rs, gs = rsl(j), gsl(gi)
        kstack = _block_diag(kn_r[rs, gs], lo_half)
        sc = _dot_nt(kbq_r[j, gi], kstack)
        lmat.append(sc[:CHUNK] * decay_r[rs, gs] * stril[:, gs])
        amat.append(sc[CHUNK:] * decay_r[rs, gs])

    gc_e = _dot_exact_rhs(gc, expand_a_ref[...], 3)
    ride(1)

    def level(xinv, li):
        xo = [_head_matmul(x, lm * level_masks_ref[li], lo_half) for x, lm in zip(xinv, lmat)]
        return [x - _head_matmul(y, x, lo_half) for x, y in zip(xinv, xo)]

    xinv = [eye[:, gsl(gi)] - lm * level_masks_ref[0] for (j, gi), lm in zip(chains, lmat)]
    xinv = level(xinv, 1)

    qn = gx_ref[:, :GDN_W]
    kn = gx_ref[:, GDN_W:2 * GDN_W]
    v = gx_ref[:, 2 * GDN_W:]
    egc = jnp.exp(gc_e)
    kb = kn * beta_e
    vb = v * beta_e
    kbg = kb * egc
    qg = qn * egc
    kn_w[...] = kn.astype(BF16)
    qg_w[...] = qg
    for j, gi in chains:
        rs, gs = rsl(j), gsl(gi)
        kbq_w[j, gi] = jnp.concatenate([kb[rs, gs], qn[rs, gs]], axis=0).astype(BF16)
        vbk_w[j, gi] = jnp.concatenate([_block_diag(vb[rs, gs].astype(BF16), lo_half),
                                        _block_diag(kbg[rs, gs].astype(BF16), lo_half)], axis=1)
    ride(2)
    xinv = level(xinv, 2)

    decay, kdec, s_decay = [], [], []
    for j in range(cps):
        gce = gc_e[rsl(j)]
        gc_t = jnp.sum(gce * eye, axis=0, keepdims=True)
        decay.append(jnp.exp(jnp.where(tril > 0.5, gce - gc_t, -jnp.inf)))
        g_last = gce[CHUNK - 1:CHUNK]
        kdec.append(kn[rsl(j)] * jnp.exp(g_last - gce))
        s_decay.append(jnp.exp(g_last))
        decay_w[rsl(j)] = decay[j]
        kdec_w[rsl(j)] = kdec[j]
        for gi in range(N_GROUPS):
            sdec1_w[j, gi] = jnp.broadcast_to(s_decay[j][:, gsl(gi)], (8, GROUP_W))
    ride(3)
    xinv = level(xinv, 3)
    for i in range(4, cps + 1):
        ride(i)
    ride(cps)

    for gi in range(N_GROUPS):
        state_ref[gi] = states[gi]

    @pl.when(s == pl.num_programs(0) - 1)
    def _():
        sout_ref[...] = state_ref[...]

    o = jnp.concatenate([jnp.concatenate(r, axis=1) for r in o_rows], axis=0)
    ms = _head_sums(o * o, bd_ones_ref[...]) * (1.0 / GDN_DV)
    o = o * lax.rsqrt(ms + EPS) * gnorm_ref[...] * _silu(z_ref[...])
    o_ref[...] = o.astype(BF16)

    for li in range(4, len(LEVELS)):
        xinv = level(xinv, li)

    uw = [_dot(x.astype(BF16), vbk_r[j, gi]) for (j, gi), x in zip(chains, xinv)]
    kwu = [_diag_blocks(_dot(kdec_r[rsl(j), gsl(gi)].T.astype(BF16), m.astype(BF16)), lo_half)
           for (j, gi), m in zip(chains, uw)]

    for ci, (j, gi) in enumerate(chains):
        rs, gs = rsl(j), gsl(gi)
        u, wmat = uw[ci][:, :GROUP_W], uw[ci][:, GROUP_W:]
        ku, kw = kwu[ci][:, :GROUP_W], kwu[ci][:, GROUP_W:]
        lhs_ref[j, gi] = jnp.concatenate([kw, wmat, qg_r[rs, gs]], axis=0).astype(BF16)
        amat_ref[j, gi] = amat[ci].astype(BF16)
        u_ref[j, gi] = u
        ku_ref[j, gi] = ku
        sdec2_ref[j, gi] = sdec1_r[j, gi]


def _gdn(gqkv, gba, z, state0, params, batch, seq, cps, front_pad):
    rows = cps * CHUNK
    blocks_per_seq = seq // rows
    n_blocks = batch * blocks_per_seq
    consts = _gdn_constants(cps)
    cur = lambda w: pl.BlockSpec((rows, w), lambda s: (jnp.minimum(s, n_blocks - 1), 0))
    done = lambda w: pl.BlockSpec((rows, w), lambda s: (jnp.maximum(s - 2, 0), 0))
    state_shape = (N_GROUPS, GDN_DK, GROUP_W)
    per_chain = lambda r, w, dt: pltpu.VMEM((cps, N_GROUPS, r, w), dt)
    per_row = lambda dt: pltpu.VMEM((rows, GDN_W), dt)
    two = lambda v: pltpu.VMEM((2,) + tuple(v.shape), v.dtype)
    return pl.pallas_call(
        functools.partial(_gdn_kernel, cps=cps, front_pad=front_pad, blocks_per_seq=blocks_per_seq),
        grid=(n_blocks + 2,),
        in_specs=[cur(3 * GDN_W), cur(LANES), done(GDN_W), _const_spec(state_shape)]
                 + [_const_spec(a.shape) for a in params]
                 + [_const_spec(a.shape) for a in consts],
        out_specs=[done(GDN_W), pl.BlockSpec(state_shape, lambda s: (0, 0, 0))],
        out_shape=[jax.ShapeDtypeStruct((batch * seq, GDN_W), BF16),
                   jax.ShapeDtypeStruct(state_shape, F32)],
        scratch_shapes=[pltpu.VMEM(state_shape, F32),
                        two(per_row(BF16)),
                        two(per_chain(2 * CHUNK, GROUP_W, BF16)),
                        two(per_chain(GROUP_W, 2 * GROUP_W, BF16)),
                        two(per_row(F32)),
                        two(per_row(F32)),
                        two(per_row(F32)),
                        two(per_chain(8, GROUP_W, F32)),
                        per_chain(3 * CHUNK, GROUP_W, BF16),
                        per_chain(CHUNK, GROUP_W, BF16),
                        per_chain(CHUNK, GROUP_W, F32),
                        per_chain(CHUNK, GROUP_W, F32),
                        per_chain(8, GROUP_W, F32)],
        compiler_params=pltpu.CompilerParams(dimension_semantics=("arbitrary",),
                                             vmem_limit_bytes=VMEM_LIMIT),
        name="gdn_chunk",
    )(gqkv, gba, z, state0, *params, *consts)


def _merge_ffn_kernel(h1_ref, om_ref, og_ref, nm_ref, wgate_ref, wmo_ref, wgo_ref, wout_ref,
                      n2_ref, wg_ref, wu_ref, wd_ref, nf_ref, out_ref):
    n_chains = max(1, h1_ref.shape[0] // MERGE_CHAIN_ROWS)
    rows = [slice(c * MERGE_CHAIN_ROWS, (c + 1) * MERGE_CHAIN_ROWS) if n_chains > 1 else slice(None)
            for c in range(n_chains)]
    h1 = [h1_ref[r] for r in rows]
    un = [_rms(h, nm_ref[...]).astype(BF16) for h in h1]
    gates = [jax.nn.sigmoid(_dot(u, wgate_ref[...])) for u in un]
    merged = [(g[:, :D_MODEL] * _dot(om_ref[r], wmo_ref[...])
               + g[:, D_MODEL:] * _dot(og_ref[r], wgo_ref[...])).astype(BF16) for g, r in zip(gates, rows)]
    h2 = [h + _dot(m, wout_ref[...]) for h, m in zip(h1, merged)]
    xn = [_rms(h, n2_ref[...]).astype(BF16) for h in h2]
    acts = [[] for _ in rows]
    for c in range(D_FF // FFN_CHUNK):
        cols = slice(c * FFN_CHUNK, (c + 1) * FFN_CHUNK)
        for a, x in zip(acts, xn):
            a.append((_silu(_dot(x, wg_ref[:, cols])) * _dot(x, wu_ref[:, cols])).astype(BF16))
    h3 = [h + 0.5 * _dot(jnp.concatenate(a, axis=1), wd_ref[...]) for h, a in zip(h2, acts)]
    for r, h in zip(rows, h3):
        out_ref[r] = _rms(h, nf_ref[...])


def _merge_ffn(h1, o_mla, o_gdn, weights, tm):
    n = h1.shape[0]
    row = lambda w: pl.BlockSpec((tm, w), lambda i: (i, 0))
    return pl.pallas_call(
        _merge_ffn_kernel,
        grid=(n // tm,),
        in_specs=[row(D_MODEL), row(V_W), row(GDN_W)] + [_const_spec(w.shape) for w in weights],
        out_specs=row(D_MODEL),
        out_shape=jax.ShapeDtypeStruct((n, D_MODEL), F32),
        compiler_params=pltpu.CompilerParams(dimension_semantics=("arbitrary",),
                                             vmem_limit_bytes=VMEM_LIMIT),
        name="merge_ffn",
    )(h1, o_mla, o_gdn, *weights)


def _rope_tables(first_pos, n):
    pos = np.arange(first_pos, first_pos + n, dtype=np.float64)
    inv = ROPE_THETA ** (-np.arange(0, MLA_ROPE, 2, dtype=np.float64) / MLA_ROPE)
    ang = pos[:, None] * inv[None, :]
    cos2 = np.tile(np.cos(ang), (1, 2))
    sin2 = np.tile(np.sin(ang), (1, 2))
    scale = (MLA_NOPE + MLA_ROPE) ** -0.5 * math.log2(math.e)
    pad = np.zeros((n, HEAD_PAD - MLA_NOPE - MLA_ROPE))
    tab_c = np.concatenate([np.ones((n, MLA_NOPE)), cos2, pad], axis=1) * scale
    tab_s = np.concatenate([np.zeros((n, MLA_NOPE)), sin2, pad], axis=1) * scale
    kpad = np.zeros((n, LANES - MLA_ROPE))
    tab_kc = np.concatenate([cos2, kpad], axis=1)
    tab_ks = np.concatenate([sin2, kpad], axis=1)
    return tuple(jnp.asarray(t, F32) for t in (tab_c, tab_s, tab_kc, tab_ks))


def _rot(w):
    half = MLA_ROPE // 2
    return jnp.concatenate([-w[..., half:], w[..., :half]], axis=-1)


def kernel(x, meta_tokens, ffn1_norm, ffn1_w_gate, ffn1_w_up, ffn1_w_down, mix_norm, w_in, q_norm, w_uq,
           kv_norm, w_ukv, w_mla_o, conv_w, a_log, dt_bias, gdn_norm, w_gdn_o, w_out, ffn2_norm,
           ffn2_w_gate, ffn2_w_up, ffn2_w_down, final_norm):
    assert ffn1_norm.shape[0] == 1, "single-layer block"
    batch, seq, d = x.shape
    assert d == D_MODEL and seq % CHUNK == 0
    tm = min(256, seq)
    tq = min(512, seq)
    assert seq % tm == 0 and seq % tq == 0

    assert w_in.shape[2] == D_IN
    w_in_bf = w_in[0].astype(BF16)

    wq = w_uq[0].reshape(MLA_Q_RANK, MLA_HEADS, MLA_NOPE + MLA_ROPE)
    wq_nope, wq_rope = wq[..., :MLA_NOPE], wq[..., MLA_NOPE:]
    zq = lambda n: jnp.zeros((MLA_Q_RANK, MLA_HEADS, n), F32)
    tail = HEAD_PAD - MLA_NOPE - MLA_ROPE
    wqa = jnp.concatenate([wq_nope, wq_rope, zq(tail)], axis=-1).reshape(MLA_Q_RANK, QK_W).astype(BF16)
    wqb = jnp.concatenate([zq(MLA_NOPE), _rot(wq_rope), zq(tail)], axis=-1).reshape(MLA_Q_RANK, QK_W).astype(BF16)
    wkv = w_ukv[0].reshape(MLA_KV_RANK, MLA_HEADS, MLA_NOPE + MLA_V)
    wk = jnp.concatenate([wkv[..., :MLA_NOPE], jnp.zeros((MLA_KV_RANK, MLA_HEADS, HEAD_PAD - MLA_NOPE), F32)],
                         axis=-1).reshape(MLA_KV_RANK, QK_W).astype(BF16)
    wvt = wkv[..., MLA_NOPE:].reshape(MLA_KV_RANK, V_W).T.astype(BF16)
    e_np = np.zeros((2 * LANES, QK_W), np.float32)
    half = MLA_ROPE // 2
    for h in range(MLA_HEADS):
        base = h * HEAD_PAD + MLA_NOPE
        for j in range(MLA_ROPE):
            e_np[j, base + j] = 1.0
        for j in range(half):
            e_np[LANES + half + j, base + j] = -1.0
            e_np[LANES + j, base + half + j] = 1.0
    e_mat = jnp.asarray(e_np, BF16)

    g = np.arange(GROUP_W)
    bd_ones = jnp.asarray((g[:, None] // GDN_DK == g[None, :] // GDN_DK).astype(np.float32), BF16)
    proj_weights = [
        ffn1_norm[0][None], ffn1_w_gate[0].astype(BF16), ffn1_w_up[0].astype(BF16),
        ffn1_w_down[0].astype(BF16), mix_norm[0][None], w_in_bf, q_norm[0][None], wqa, wqb,
        kv_norm[0][None], wk, wvt, e_mat, conv_w[0].astype(F32), bd_ones]

    zero_carry = jnp.zeros((8, 3 * GDN_W), F32)
    _, _, k_m, vt_m, gqkv_m, gba_m, _, meta_tail, win_a, wgate = _token_proj(
        meta_tokens.astype(F32), _rope_tables(0, N_META), zero_carry, proj_weights, N_META, 1, N_META)
    proj_weights = [win_a if w is w_in_bf else w for w in proj_weights]
    tp = 2 * tm if seq % (2 * tm) == 0 else tm
    h1, q, k, vt, gqkv, gba, z, _ = _token_proj(
        x.reshape(batch * seq, d), _rope_tables(N_META, seq), meta_tail, proj_weights, tp, seq // tp, tm)

    pad_rows = lambda a, n, front: jnp.pad(a, ((n - a.shape[0], 0) if front else (0, n - a.shape[0]), (0, 0)))
    vt_meta = jnp.pad(vt_m[0], ((0, 0), (0, LANES - N_META)))
    o_mla = _mla_attn(q, k, vt, pad_rows(k_m, LANES, False), vt_meta, batch, seq, tq, tm)

    hpad = lambda a: jnp.zeros((1, LANES), F32).at[0, GDN_HEADS:2 * GDN_HEADS].set(a)
    arate = hpad(-jnp.exp(a_log[0].astype(F32)))
    dtb = hpad(dt_bias[0].astype(F32))
    gnorm = jnp.tile(gdn_norm[0].astype(F32), GDN_HEADS)[None]
    gdn_params = (arate, dtb, gnorm)
    cps = 4 if seq % (4 * CHUNK) == 0 else 1
    _, state_meta = _gdn(pad_rows(gqkv_m, CHUNK, True), pad_rows(gba_m, CHUNK, True),
                         jnp.zeros((CHUNK, GDN_W), F32), jnp.zeros((N_GROUPS, GDN_DK, GROUP_W), F32),
                         gdn_params, 1, CHUNK, 1, CHUNK - N_META)
    o_gdn, _ = _gdn(gqkv, gba, z, state_meta, gdn_params, batch, seq, cps, 0)

    merge_weights = [
        mix_norm[0][None], wgate, w_mla_o[0].astype(BF16), w_gdn_o[0].astype(BF16), w_out[0].astype(BF16),
        ffn2_norm[0][None], ffn2_w_gate[0].astype(BF16), ffn2_w_up[0].astype(BF16),
        ffn2_w_down[0].astype(BF16), final_norm[None]]
    merge_rows = 2 * MERGE_CHAIN_ROWS if (batch * seq) % (2 * MERGE_CHAIN_ROWS) == 0 else tm
    out = _merge_ffn(h1, o_mla, o_gdn, merge_weights, merge_rows)
    return out.reshape(batch, seq, d)
```

```python
import functools
import math

import jax
import jax.numpy as jnp
import numpy as np
from jax import lax
from jax.experimental import pallas as pl
from jax.experimental.pallas import tpu as pltpu

F32 = jnp.float32
BF16 = jnp.bfloat16

D_MODEL = 1024
N_META = 16
EPS = 1e-6
D_FF = 2816
MLA_HEADS = 8
MLA_Q_RANK = 256
MLA_KV_RANK = 128
MLA_NOPE = 64
MLA_ROPE = 32
MLA_V = 64
ROPE_THETA = 10000.0
GDN_HEADS = 8
GDN_DK = 64
GDN_DV = 64
CONV_K = 4
CHUNK = 64

LANES = 128
HEAD_PAD = 128
QK_W = MLA_HEADS * HEAD_PAD
V_W = MLA_HEADS * MLA_V
GDN_W = GDN_HEADS * GDN_DK
GROUP_HEADS = 4
GROUP_W = GROUP_HEADS * GDN_DK
N_GROUPS = GDN_HEADS // GROUP_HEADS
FFN_CHUNK = 256
MERGE_CHAIN_ROWS = 256
PROJ_W = MLA_Q_RANK + MLA_KV_RANK + LANES + 3 * GDN_W + LANES + GDN_W
W_IN_SPLITS = (MLA_Q_RANK, MLA_KV_RANK, MLA_ROPE, GDN_W, GDN_W, GDN_W, GDN_HEADS, GDN_HEADS, GDN_W,
               D_MODEL, D_MODEL)
(OFF_CQ, OFF_CKV, OFF_KR, OFF_GQ, OFF_GK, OFF_GV, OFF_GB, OFF_GA, OFF_GZ, OFF_GATES, _OFF_GATE_GDN,
 D_IN) = (int(v) for v in np.concatenate([[0], np.cumsum(W_IN_SPLITS)]))
W_TOKEN_COLS = -(-OFF_GATES // LANES) * LANES
W_GATE_START = OFF_GATES // LANES * LANES
VMEM_LIMIT = 60 * 1024 * 1024
NEG_BIG = -1e30


def _const_spec(shape):
    zeros = (0,) * len(shape)
    return pl.BlockSpec(shape, lambda *_: zeros, pipeline_mode=pl.Buffered(1))


def _rms(x, w):
    return x * lax.rsqrt(jnp.mean(x * x, axis=-1, keepdims=True) + EPS) * w


def _dot(a, b):
    return jnp.dot(a, b, preferred_element_type=F32)


def _dot_nt(a, b):
    return lax.dot_general(a, b, (((1,), (1,)), ((), ())), preferred_element_type=F32)


def _silu(x):
    return x * jax.nn.sigmoid(x)


def _split(x, n):
    pieces = []
    for _ in range(n - 1):
        hi = x.astype(BF16)
        pieces.append(hi)
        x = x - hi.astype(F32)
    pieces.append(x.astype(BF16))
    return pieces


def _dot_exact_rhs(x, rhs, n):
    return sum(_dot(p, rhs) for p in _split(x, n))


def _dot_exact_lhs(lhs, x, n):
    return sum(_dot(lhs, p) for p in _split(x, n))


def _head_sums(x, bd_ones):
    return jnp.concatenate([_dot(x[:, g * GROUP_W:(g + 1) * GROUP_W].astype(BF16), bd_ones)
                            for g in range(N_GROUPS)], axis=1)


def _token_proj_kernel(*refs, tiles_per_seq, regroup):
    last = pl.num_programs(0) - 1

    @pl.when(pl.program_id(0) < last)
    def _():
        _token_proj_step(*refs, tiles_per_seq=tiles_per_seq, regroup=regroup, with_matmuls=True)

    @pl.when(pl.program_id(0) == last)
    def _():
        _token_proj_step(*refs, tiles_per_seq=tiles_per_seq, regroup=regroup, with_matmuls=False)


def _token_proj_step(*refs, tiles_per_seq, regroup, with_matmuls):
    it = iter(refs)
    take = lambda n: [next(it) for _ in range(n)]
    x_ref, tc_ref, ts_ref, tkc_ref, tks_ref, c0_ref, n1_ref, wg_ref, wu_ref, wd_ref, nm_ref = take(11)
    if regroup:
        wraw_ref, wrawg_ref = take(2)
    else:
        win_ref, = take(1)
    qn_ref, wqa_ref, wqb_ref, kvn_ref, wk_ref, wvt_ref, e_ref, convw_ref, bd_ones_ref = take(9)
    h1_ref, q_ref, k_ref, vt_ref, gqkv_ref, gba_ref, z_ref, tail_ref = take(8)
    if regroup:
        win_ref, wgate_ref = take(2)
    xe_ref, = take(1)
    tm = x_ref.shape[0]
    step = pl.program_id(0)

    @pl.when(step == 0)
    def _():
        xe_ref[...] = jnp.zeros_like(xe_ref)
        if regroup:
            o = OFF_GQ + LANES - MLA_ROPE
            win_ref[:, :o] = wraw_ref[:, :o]
            win_ref[:, o:o + 3 * GDN_W] = wraw_ref[:, OFF_GQ:OFF_GB]
            o += 3 * GDN_W
            lane = lax.broadcasted_iota(jnp.int32, (1, LANES), 1)
            win_ref[:, o:o + LANES] = jnp.where(lane < 2 * GDN_HEADS, wraw_ref[:, OFF_GB:OFF_GB + LANES], 0)
            win_ref[:, o + LANES:] = wraw_ref[:, OFF_GZ:OFF_GATES]
            g0 = OFF_GATES - W_GATE_START
            wgate_ref[...] = wrawg_ref[:, g0:g0 + 2 * D_MODEL]

    n_chains = vt_ref.shape[0]
    cr = tm // n_chains
    rows = [slice(c * cr, (c + 1) * cr) for c in range(n_chains)]
    x = [x_ref[r] for r in rows] if with_matmuls else []
    xn = [_rms(v, n1_ref[...]).astype(BF16) for v in x]

    n_ffn = D_FF // FFN_CHUNK
    n_conv = 3 * GDN_W // LANES
    acts = [[] for _ in rows]
    for c in range(max(n_ffn, n_conv)):
        if c < n_ffn and with_matmuls:
            cols = slice(c * FFN_CHUNK, (c + 1) * FFN_CHUNK)
            for a, v in zip(acts, xn):
                a.append((_silu(_dot(v, wg_ref[:, cols])) * _dot(v, wu_ref[:, cols])).astype(BF16))
        if c < n_conv:
            lanes = slice(c * LANES, (c + 1) * LANES)
            y = sum(convw_ref[t:t + 1, lanes] * xe_ref[pl.ds(8 - (CONV_K - 1) + t, tm), lanes]
                    for t in range(CONV_K))
            gqkv_ref[:, lanes] = _silu(y)

    def l2norm_qk():
        bd_ones = bd_ones_ref[...]
        gq = gqkv_ref[:, :GDN_W]
        gk = gqkv_ref[:, GDN_W:2 * GDN_W]
        gqkv_ref[:, :GDN_W] = gq * lax.rsqrt(_head_sums(gq * gq, bd_ones) + EPS) * (GDN_DK ** -0.5)
        gqkv_ref[:, GDN_W:2 * GDN_W] = gk * lax.rsqrt(_head_sums(gk * gk, bd_ones) + EPS)

    if not with_matmuls:
        l2norm_qk()
        return
    h1 = [v + 0.5 * _dot(jnp.concatenate(a, axis=1), wd_ref[...]) for v, a in zip(x, acts)]
    for r, v in zip(rows, h1):
        h1_ref[r] = v

    un = [_rms(v, nm_ref[...]).astype(BF16) for v in h1]
    proj = [_dot(v, win_ref[...]) for v in un]

    l2norm_qk()
    xe_ref[0:8] = jnp.where(step % tiles_per_seq == 0, c0_ref[...], xe_ref[tm:tm + 8])
    o_raw = MLA_Q_RANK + MLA_KV_RANK + LANES
    for r, p in zip(rows, proj):
        xe_ref[8 + r.start:8 + r.stop] = p[:, o_raw:o_raw + 3 * GDN_W]
        gba_ref[r] = p[:, o_raw + 3 * GDN_W:o_raw + 3 * GDN_W + LANES]
        z_ref[r] = p[:, o_raw + 3 * GDN_W + LANES:]
    tail_ref[...] = proj[-1][cr - 8:, o_raw:o_raw + 3 * GDN_W]

    cqn = [_rms(p[:, :MLA_Q_RANK], qn_ref[...]).astype(BF16) for p in proj]
    qa = [_dot(v, wqa_ref[...]) for v in cqn]
    qb = [_dot(v, wqb_ref[...]) for v in cqn]
    for r, a, b in zip(rows, qa, qb):
        tc = jnp.concatenate([tc_ref[r]] * MLA_HEADS, axis=1)
        ts = jnp.concatenate([ts_ref[r]] * MLA_HEADS, axis=1)
        q_ref[r] = (a * tc + b * ts).astype(BF16)

    ckvn = [_rms(p[:, MLA_Q_RANK:MLA_Q_RANK + MLA_KV_RANK], kvn_ref[...]).astype(BF16) for p in proj]
    for c, (r, p, v) in enumerate(zip(rows, proj, ckvn)):
        kr = p[:, MLA_Q_RANK + MLA_KV_RANK:o_raw]
        kr_terms = jnp.concatenate([kr * tkc_ref[r], kr * tks_ref[r]], axis=1).astype(BF16)
        k_ref[r] = (_dot(v, wk_ref[...]) + _dot(kr_terms, e_ref[...])).astype(BF16)
        vt_ref[c] = _dot_nt(wvt_ref[...], v).astype(BF16)


def _token_proj(x2d, tabs, conv_carry, weights, tm, tiles_per_seq, chain_rows):
    n = x2d.shape[0]
    assert n % tm == 0
    last = n // tm - 1
    tile = lambda i: jnp.minimum(i, last)
    row = lambda w: pl.BlockSpec((tm, w), lambda i: (tile(i), 0))
    tab = pl.BlockSpec((tm, LANES), lambda i: (tile(i) % tiles_per_seq, 0))
    out_widths = (D_MODEL, QK_W, QK_W, None, 3 * GDN_W, LANES, GDN_W)
    out_dtypes = (F32, BF16, BF16, BF16, F32, F32, F32)
    assert tm % chain_rows == 0
    vt_spec = pl.BlockSpec((tm // chain_rows, V_W, chain_rows), lambda i: (tile(i), 0, 0))
    conv_spec = pl.BlockSpec((tm, 3 * GDN_W), lambda i: (jnp.maximum(i - 1, 0), 0))
    tail_shape = (8, 3 * GDN_W)
    out_specs = [vt_spec if w is None else row(w) for w in out_widths]
    out_specs[4] = conv_spec
    out_specs.append(pl.BlockSpec(tail_shape, lambda i: (0, 0)))
    out_shape = [jax.ShapeDtypeStruct((n // chain_rows, V_W, chain_rows) if w is None else (n, w), d)
                 for w, d in zip(out_widths, out_dtypes)] + [jax.ShapeDtypeStruct(tail_shape, F32)]

    regroup = any(w.shape == (D_MODEL, D_IN) for w in weights)
    operands, w_specs = [], []
    for w in weights:
        if w.shape == (D_MODEL, D_IN):
            assert 2 * W_GATE_START >= D_IN
            operands += [w, w]
            w_specs += [_const_spec((D_MODEL, W_TOKEN_COLS)),
                        pl.BlockSpec((D_MODEL, W_GATE_START), lambda i: (0, 1), pipeline_mode=pl.Buffered(1))]
        else:
            operands.append(w)
            w_specs.append(_const_spec(w.shape))
    if regroup:
        for shape in ((D_MODEL, PROJ_W), (D_MODEL, 2 * D_MODEL)):
            out_specs.append(pl.BlockSpec(shape, lambda i: (0, 0)))
            out_shape.append(jax.ShapeDtypeStruct(shape, BF16))
    return pl.pallas_call(
        functools.partial(_token_proj_kernel, tiles_per_seq=tiles_per_seq, regroup=regroup),
        grid=(n // tm + 1,),
        in_specs=[row(D_MODEL)] + [tab] * len(tabs) + [_const_spec(tail_shape)] + w_specs,
        out_specs=out_specs,
        out_shape=out_shape,
        scratch_shapes=[pltpu.VMEM((8 + tm, 3 * GDN_W), F32)],
        compiler_params=pltpu.CompilerParams(dimension_semantics=("arbitrary",),
                                             vmem_limit_bytes=VMEM_LIMIT),
        name="token_proj",
    )(x2d, *tabs, conv_carry, *operands)


def _mla_kernel(q_ref, k_ref, vt_ref, km_ref, vmt_ref, o_ref, st_ref, *, tq, tk):
    def tile(qi, carry):
        rows = pl.ds(pl.multiple_of(qi * tq, tq), tq)
        _mla_tile(qi, q_ref.at[rows], k_ref, vt_ref, km_ref, vmt_ref, o_ref.at[rows], st_ref, tq=tq, tk=tk)
        return carry

    lax.fori_loop(0, q_ref.shape[0] // tq, tile, 0)


def _mla_tile(qi, q_ref, k_ref, vt_ref, km_ref, vmt_ref, o_ref, st_ref, *, tq, tk):
    heads = range(2)
    hs = [slice(h * HEAD_PAD, (h + 1) * HEAD_PAD) for h in heads]
    vs = [slice(h * MLA_V, (h + 1) * MLA_V) for h in heads]
    q = [q_ref[:, s] for s in hs]
    colmax = lambda s: jnp.max(s, axis=0, keepdims=True)

    def with_ones(vt):
        return jnp.concatenate([vt, jnp.ones((8, vt.shape[1]), BF16)], axis=0)

    def scores(ki, slot, q_from=0):
        rows = pl.ds(pl.multiple_of(ki * tk, tk), tk)
        block_max = []
        for h in heads:
            s = _dot_nt(k_ref[rows, hs[h]], q[h][q_from:])
            st_ref[slot, h, :, q_from:] = s
            block_max.append(colmax(s))
        return block_max

    def update(ki, slot, m, acc, block_max, mask, q_from=0):
        st = [st_ref[slot, h, :, q_from:] for h in heads]
        if mask is not None:
            st = [jnp.where(mask, s, NEG_BIG) for s in st]
            block_max = [colmax(s) for s in st]
        m_old = [x[:, q_from:] for x in m]
        m_new = [jnp.maximum(m_old[h], block_max[h]) for h in heads]
        p = [jnp.exp2(st[h] - m_new[h]).astype(BF16) for h in heads]
        vt = vt_ref[ki]
        acc_new = [jnp.exp2(m_old[h] - m_new[h]) * acc[h][:, q_from:] + _dot(with_ones(vt[vs[h]]), p[h])
                   for h in heads]
        if q_from:
            m_new = [jnp.concatenate([m[h][:, :q_from], m_new[h]], axis=1) for h in heads]
            acc_new = [jnp.concatenate([acc[h][:, :q_from], acc_new[h]], axis=1) for h in heads]
        return m_new, acc_new

    bm_a = scores(0, 0)
    bm_b = scores(1, 1)
    meta_valid = lax.broadcasted_iota(jnp.int32, (km_ref.shape[0], tq), 0) < N_META
    st = [jnp.where(meta_valid, _dot_nt(km_ref[:, hs[h]], q[h]), NEG_BIG) for h in heads]
    m = [colmax(s) for s in st]
    acc = [_dot(with_ones(vmt_ref[vs[h], :]), jnp.exp2(st[h] - m[h]).astype(BF16)) for h in heads]

    def stage(ka, carry, cur, nxt):
        m, acc, bm_a, bm_b = carry
        bm_a2 = scores(ka + 2, nxt[0])
        m, acc = update(ka, cur[0], m, acc, bm_a, None)
        bm_b2 = scores(ka + 3, nxt[1])
        m, acc = update(ka + 1, cur[1], m, acc, bm_b, None)
        return m, acc, bm_a2, bm_b2

    def body(i, carry):
        return stage(4 * i + 2, stage(4 * i, carry, (0, 1), (2, 3)), (2, 3), (0, 1))

    causal = (lax.broadcasted_iota(jnp.int32, (tk, tq), 1) >= lax.broadcasted_iota(jnp.int32, (tk, tq), 0))
    n_full = 2 * qi

    def finish(carry, cur):
        m, acc, _, _ = carry
        m, acc = update(n_full, cur[0], m, acc, None, causal)
        m, acc = update(n_full + 1, cur[1], m, acc, None, causal[:, :tk], q_from=tk)
        return acc

    carry = lax.fori_loop(0, qi // 2, body, (m, acc, bm_a, bm_b))
    acc = lax.cond(qi % 2 == 1,
                   lambda c: finish(stage(n_full - 2, c, (0, 1), (2, 3)), (2, 3)),
                   lambda c: finish(c, (0, 1)), carry)
    o_ref[...] = jnp.concatenate([(a[:MLA_V] * (1.0 / a[MLA_V:MLA_V + 1])).T for a in acc],
                                 axis=1).astype(BF16)


def _mla_attn(q, k, vt, k_meta, vt_meta, batch, seq, tq, tk):
    nq = seq // tq
    assert vt.shape[2] == tk and tq == 2 * tk
    kern = functools.partial(_mla_kernel, tq=tq, tk=tk)
    return pl.pallas_call(
        kern,
        grid=(batch, MLA_HEADS // 2),
        in_specs=[
            pl.BlockSpec((seq, 2 * HEAD_PAD), lambda b, hp: (b, hp)),
            pl.BlockSpec((seq, 2 * HEAD_PAD), lambda b, hp: (b, hp)),
            pl.BlockSpec((seq // tk, 2 * MLA_V, tk), lambda b, hp: (b, hp, 0)),
            pl.BlockSpec((k_meta.shape[0], 2 * HEAD_PAD), lambda b, hp: (0, hp)),
            pl.BlockSpec((2 * MLA_V, vt_meta.shape[1]), lambda b, hp: (hp, 0)),
        ],
        out_specs=pl.BlockSpec((seq, 2 * MLA_V), lambda b, hp: (b, hp)),
        out_shape=jax.ShapeDtypeStruct((batch * seq, V_W), BF16),
        scratch_shapes=[pltpu.VMEM((4, 2, tk, tq), F32)],
        compiler_params=pltpu.CompilerParams(
            dimension_semantics=("arbitrary", "arbitrary"),
            vmem_limit_bytes=VMEM_LIMIT),
        name="mla_attn",
    )(q, k, vt, k_meta, vt_meta)


LEVELS = (1, 2, 4, 8, 16, 32)
GDN_CONST_NAMES = ("expand_b", "expand_a", "ltri", "eye_t", "tril_t", "stril_t", "level_masks", "bd_ones")


def _gdn_constants(cps):
    i = np.arange(CHUNK)[:, None]
    lane = np.arange(GDN_W)[None, :]
    j = lane % GDN_DK
    c = {}
    r = np.arange(LANES)[:, None]
    c["expand_b"] = (r == lane // GDN_DK)
    c["expand_a"] = (r == GDN_HEADS + lane // GDN_DK)
    t = np.arange(cps * CHUNK)
    c["ltri"] = (t[:, None] >= t[None, :]) & (t[:, None] // CHUNK == t[None, :] // CHUNK)
    c["eye_t"] = (i == j)
    c["tril_t"] = (i >= j)
    c["stril_t"] = (i > j)
    jg = j[:, :GROUP_W]
    c["level_masks"] = np.stack([
        ((i // (2 * s) == jg // (2 * s)) & ((i // s) % 2 == 1) & ((jg // s) % 2 == 0))
        for s in LEVELS])
    g = np.arange(GROUP_W)
    c["bd_ones"] = (g[:, None] // GDN_DK == g[None, :] // GDN_DK)
    bf = ("bd_ones", "expand_b", "expand_a", "ltri")
    return [jnp.asarray(c[k].astype(np.float32), BF16 if k in bf else F32) for k in GDN_CONST_NAMES]


def _block_diag(y, lo_half):
    zeros = jnp.zeros((GDN_DK, LANES), y.dtype)
    blocks = []
    for h in range(GROUP_HEADS):
        t = h // 2
        tile = y[:, t * LANES:(t + 1) * LANES]
        piece = jnp.where(lo_half, tile, 0) if h % 2 == 0 else jnp.where(lo_half, 0, tile)
        blocks.append(jnp.concatenate([piece, zeros] if t == 0 else [zeros, piece], axis=1))
    return jnp.concatenate(blocks, axis=0)


def _head_matmul(x, y, lo_half):
    return _dot(x.astype(BF16), _block_diag(y.astype(BF16), lo_half))


def _diag_blocks(m, lo_half):
    tiles = []
    for t in range(m.shape[1] // LANES):
        h = 2 * (t % 2)
        cols = slice(t * LANES, (t + 1) * LANES)
        tiles.append(jnp.where(lo_half, m[h * GDN_DK:(h + 1) * GDN_DK, cols],
                               m[(h + 1) * GDN_DK:(h + 2) * GDN_DK, cols]))
    return jnp.concatenate(tiles, axis=1)


def _gdn_kernel(gx_ref, ba_ref, z_ref, s0_ref, arate_ref, dtb_ref, gnorm_ref,
                expand_b_ref, expand_a_ref, ltri_ref, eye_t_ref, tril_t_ref, stril_t_ref,
                level_masks_ref, bd_ones_ref,
                o_ref, sout_ref,
                state_ref, kn_ref, kbq_ref, vbk_ref, kdec_ref, qg_ref, decay_ref, sdec1_ref,
                lhs_ref, amat_ref, u_ref, ku_ref, sdec2_ref, *, cps, front_pad, blocks_per_seq):
    s = pl.program_id(0)
    rows = cps * CHUNK
    stage1 = (kn_ref, kbq_ref, vbk_ref, kdec_ref, qg_ref, decay_ref, sdec1_ref)
    stage2 = (lhs_ref, amat_ref, u_ref, ku_ref, sdec2_ref)

    @pl.when(s == 0)
    def _():
        for r in stage1 + stage2 + (state_ref,):
            r[...] = jnp.zeros_like(r)

    kn_r, kbq_r, vbk_r, kdec_r, qg_r, decay_r, sdec1_r = (r.at[1 - s % 2] for r in stage1)
    kn_w, kbq_w, vbk_w, kdec_w, qg_w, decay_w, sdec1_w = (r.at[s % 2] for r in stage1)

    lo_half = lax.broadcasted_iota(jnp.int32, (1, LANES), 1) < GDN_DK
    eye, tril, stril = eye_t_ref[...], tril_t_ref[...], stril_t_ref[...]
    chains = [(j, gi) for j in range(cps) for gi in range(N_GROUPS)]
    rsl = lambda j: slice(j * CHUNK, (j + 1) * CHUNK)
    gsl = lambda gi: slice(gi * GROUP_W, (gi + 1) * GROUP_W)

    first_of_seq = (s - 2) % blocks_per_seq == 0
    states = [jnp.where(first_of_seq, s0_ref[gi], state_ref[gi]) for gi in range(N_GROUPS)]
    o_rows = [[None] * N_GROUPS for _ in range(cps)]
    in_flight = {}

    def ride(i):
        if i - 1 in in_flight:
            for gi, big in enumerate(in_flight.pop(i - 1)):
                v_new = u_ref[i - 1, gi] - big[CHUNK:2 * CHUNK]
                o_rows[i - 1][gi] = big[2 * CHUNK:] + _dot(amat_ref[i - 1, gi],
                                                           _block_diag(v_new.astype(BF16), lo_half))
        if i < cps:
            bigs = []
            for gi in range(N_GROUPS):
                state = states[gi]
                big = _dot(lhs_ref[i, gi], _block_diag(state.astype(BF16), lo_half))
                states[gi] = sdec2_ref[i, gi][0:1] * state - big[:CHUNK] + ku_ref[i, gi]
                bigs.append(big)
            in_flight[i] = bigs

    ba = ba_ref[...]
    beta = jax.nn.sigmoid(ba)
    sp_in = ba + dtb_ref[...]
    g = arate_ref[...] * (jnp.maximum(sp_in, 0.0) + jnp.log1p(jnp.exp(-jnp.abs(sp_in))))
    if front_pad:
        valid = lax.broadcasted_iota(jnp.int32, (rows, 1), 0) >= front_pad
        beta = jnp.where(valid, beta, 0.0)
        g = jnp.where(valid, g, 0.0)
    beta_e = _dot_exact_rhs(beta, expand_b_ref[...], 2)
    gc = _dot_exact_lhs(ltri_ref[...], g, 3)
    ride(0)

    lmat, amat = [], []
    for j, gi in chains:
        rs, gs = rsl(j), gsl(gi)
        kstack = _block_diag(kn_r[rs, gs], lo_half)
        sc = _dot_nt(kbq_r[j, gi], kstack)
        lmat.append(sc[:CHUNK] * decay_r[rs, gs] * stril[:, gs])
        amat.append(sc[CHUNK:] * decay_r[rs, gs])

    gc_e = _dot_exact_rhs(gc, expand_a_ref[...], 3)
    ride(1)

    def level(xinv, li):
        xo = [_head_matmul(x, lm * level_masks_ref[li], lo_half) for x, lm in zip(xinv, lmat)]
        return [x - _head_matmul(y, x, lo_half) for x, y in zip(xinv, xo)]

    xinv = [eye[:, gsl(gi)] - lm * level_masks_ref[0] for (j, gi), lm in zip(chains, lmat)]
    xinv = level(xinv, 1)

    qn = gx_ref[:, :GDN_W]
    kn = gx_ref[:, GDN_W:2 * GDN_W]
    v = gx_ref[:, 2 * GDN_W:]
    egc = jnp.exp(gc_e)
    kb = kn * beta_e
    vb = v * beta_e
    kbg = kb * egc
    qg = qn * egc
    kn_w[...] = kn.astype(BF16)
    qg_w[...] = qg
    for j, gi in chains:
        rs, gs = rsl(j), gsl(gi)
        kbq_w[j, gi] = jnp.concatenate([kb[rs, gs], qn[rs, gs]], axis=0).astype(BF16)
        vbk_w[j, gi] = jnp.concatenate([_block_diag(vb[rs, gs].astype(BF16), lo_half),
                                        _block_diag(kbg[rs, gs].astype(BF16), lo_half)], axis=1)
    ride(2)
    xinv = level(xinv, 2)

    decay, kdec, s_decay = [], [], []
    for j in range(cps):
        gce = gc_e[rsl(j)]
        gc_t = jnp.sum(gce * eye, axis=0, keepdims=True)
        decay.append(jnp.exp(jnp.where(tril > 0.5, gce - gc_t, -jnp.inf)))
        g_last = gce[CHUNK - 1:CHUNK]
        kdec.append(kn[rsl(j)] * jnp.exp(g_last - gce))
        s_decay.append(jnp.exp(g_last))
        decay_w[rsl(j)] = decay[j]
        kdec_w[rsl(j)] = kdec[j]
        for gi in range(N_GROUPS):
            sdec1_w[j, gi] = jnp.broadcast_to(s_decay[j][:, gsl(gi)], (8, GROUP_W))
    ride(3)
    xinv = level(xinv, 3)
    for i in range(4, cps + 1):
        ride(i)
    ride(cps)

    for gi in range(N_GROUPS):
        state_ref[gi] = states[gi]

    @pl.when(s == pl.num_programs(0) - 1)
    def _():
        sout_ref[...] = state_ref[...]

    o = jnp.concatenate([jnp.concatenate(r, axis=1) for r in o_rows], axis=0)
    ms = _head_sums(o * o, bd_ones_ref[...]) * (1.0 / GDN_DV)
    o = o * lax.rsqrt(ms + EPS) * gnorm_ref[...] * _silu(z_ref[...])
    o_ref[...] = o.astype(BF16)

    for li in range(4, len(LEVELS)):
        xinv = level(xinv, li)

    uw = [_dot(x.astype(BF16), vbk_r[j, gi]) for (j, gi), x in zip(chains, xinv)]
    kwu = [_diag_blocks(_dot(kdec_r[rsl(j), gsl(gi)].T.astype(BF16), m.astype(BF16)), lo_half)
           for (j, gi), m in zip(chains, uw)]

    for ci, (j, gi) in enumerate(chains):
        rs, gs = rsl(j), gsl(gi)
        u, wmat = uw[ci][:, :GROUP_W], uw[ci][:, GROUP_W:]
        ku, kw = kwu[ci][:, :GROUP_W], kwu[ci][:, GROUP_W:]
        lhs_ref[j, gi] = jnp.concatenate([kw, wmat, qg_r[rs, gs]], axis=0).astype(BF16)
        amat_ref[j, gi] = amat[ci].astype(BF16)
        u_ref[j, gi] = u
        ku_ref[j, gi] = ku
        sdec2_ref[j, gi] = sdec1_r[j, gi]


def _gdn(gqkv, gba, z, state0, params, batch, seq, cps, front_pad):
    rows = cps * CHUNK
    blocks_per_seq = seq // rows
    n_blocks = batch * blocks_per_seq
    consts = _gdn_constants(cps)
    cur = lambda w: pl.BlockSpec((rows, w), lambda s: (jnp.minimum(s, n_blocks - 1), 0))
    done = lambda w: pl.BlockSpec((rows, w), lambda s: (jnp.maximum(s - 2, 0), 0))
    state_shape = (N_GROUPS, GDN_DK, GROUP_W)
    per_chain = lambda r, w, dt: pltpu.VMEM((cps, N_GROUPS, r, w), dt)
    per_row = lambda dt: pltpu.VMEM((rows, GDN_W), dt)
    two = lambda v: pltpu.VMEM((2,) + tuple(v.shape), v.dtype)
    return pl.pallas_call(
        functools.partial(_gdn_kernel, cps=cps, front_pad=front_pad, blocks_per_seq=blocks_per_seq),
        grid=(n_blocks + 2,),
        in_specs=[cur(3 * GDN_W), cur(LANES), done(GDN_W), _const_spec(state_shape)]
                 + [_const_spec(a.shape) for a in params]
                 + [_const_spec(a.shape) for a in consts],
        out_specs=[done(GDN_W), pl.BlockSpec(state_shape, lambda s: (0, 0, 0))],
        out_shape=[jax.ShapeDtypeStruct((batch * seq, GDN_W), BF16),
                   jax.ShapeDtypeStruct(state_shape, F32)],
        scratch_shapes=[pltpu.VMEM(state_shape, F32),
                        two(per_row(BF16)),
                        two(per_chain(2 * CHUNK, GROUP_W, BF16)),
                        two(per_chain(GROUP_W, 2 * GROUP_W, BF16)),
                        two(per_row(F32)),
                        two(per_row(F32)),
                        two(per_row(F32)),
                        two(per_chain(8, GROUP_W, F32)),
                        per_chain(3 * CHUNK, GROUP_W, BF16),
                        per_chain(CHUNK, GROUP_W, BF16),
                        per_chain(CHUNK, GROUP_W, F32),
                        per_chain(CHUNK, GROUP_W, F32),
                        per_chain(8, GROUP_W, F32)],
        compiler_params=pltpu.CompilerParams(dimension_semantics=("arbitrary",),
                                             vmem_limit_bytes=VMEM_LIMIT),
        name="gdn_chunk",
    )(gqkv, gba, z, state0, *params, *consts)


def _merge_ffn_kernel(h1_ref, om_ref, og_ref, nm_ref, wgate_ref, wmo_ref, wgo_ref, wout_ref,
                      n2_ref, wg_ref, wu_ref, wd_ref, nf_ref, out_ref):
    n_chains = max(1, h1_ref.shape[0] // MERGE_CHAIN_ROWS)
    rows = [slice(c * MERGE_CHAIN_ROWS, (c + 1) * MERGE_CHAIN_ROWS) if n_chains > 1 else slice(None)
            for c in range(n_chains)]
    h1 = [h1_ref[r] for r in rows]
    un = [_rms(h, nm_ref[...]).astype(BF16) for h in h1]
    gates = [jax.nn.sigmoid(_dot(u, wgate_ref[...])) for u in un]
    merged = [(g[:, :D_MODEL] * _dot(om_ref[r], wmo_ref[...])
               + g[:, D_MODEL:] * _dot(og_ref[r], wgo_ref[...])).astype(BF16) for g, r in zip(gates, rows)]
    h2 = [h + _dot(m, wout_ref[...]) for h, m in zip(h1, merged)]
    xn = [_rms(h, n2_ref[...]).astype(BF16) for h in h2]
    acts = [[] for _ in rows]
    for c in range(D_FF // FFN_CHUNK):
        cols = slice(c * FFN_CHUNK, (c + 1) * FFN_CHUNK)
        for a, x in zip(acts, xn):
            a.append((_silu(_dot(x, wg_ref[:, cols])) * _dot(x, wu_ref[:, cols])).astype(BF16))
    h3 = [h + 0.5 * _dot(jnp.concatenate(a, axis=1), wd_ref[...]) for h, a in zip(h2, acts)]
    for r, h in zip(rows, h3):
        out_ref[r] = _rms(h, nf_ref[...])


def _merge_ffn(h1, o_mla, o_gdn, weights, tm):
    n = h1.shape[0]
    row = lambda w: pl.BlockSpec((tm, w), lambda i: (i, 0))
    return pl.pallas_call(
        _merge_ffn_kernel,
        grid=(n // tm,),
        in_specs=[row(D_MODEL), row(V_W), row(GDN_W)] + [_const_spec(w.shape) for w in weights],
        out_specs=row(D_MODEL),
        out_shape=jax.ShapeDtypeStruct((n, D_MODEL), F32),
        compiler_params=pltpu.CompilerParams(dimension_semantics=("arbitrary",),
                                             vmem_limit_bytes=VMEM_LIMIT),
        name="merge_ffn",
    )(h1, o_mla, o_gdn, *weights)


def _rope_tables(first_pos, n):
    pos = np.arange(first_pos, first_pos + n, dtype=np.float64)
    inv = ROPE_THETA ** (-np.arange(0, MLA_ROPE, 2, dtype=np.float64) / MLA_ROPE)
    ang = pos[:, None] * inv[None, :]
    cos2 = np.tile(np.cos(ang), (1, 2))
    sin2 = np.tile(np.sin(ang), (1, 2))
    scale = (MLA_NOPE + MLA_ROPE) ** -0.5 * math.log2(math.e)
    pad = np.zeros((n, HEAD_PAD - MLA_NOPE - MLA_ROPE))
    tab_c = np.concatenate([np.ones((n, MLA_NOPE)), cos2, pad], axis=1) * scale
    tab_s = np.concatenate([np.zeros((n, MLA_NOPE)), sin2, pad], axis=1) * scale
    kpad = np.zeros((n, LANES - MLA_ROPE))
    tab_kc = np.concatenate([cos2, kpad], axis=1)
    tab_ks = np.concatenate([sin2, kpad], axis=1)
    return tuple(jnp.asarray(t, F32) for t in (tab_c, tab_s, tab_kc, tab_ks))


def _rot(w):
    half = MLA_ROPE // 2
    return jnp.concatenate([-w[..., half:], w[..., :half]], axis=-1)


def kernel(x, meta_tokens, ffn1_norm, ffn1_w_gate, ffn1_w_up, ffn1_w_down, mix_norm, w_in, q_norm, w_uq,
           kv_norm, w_ukv, w_mla_o, conv_w, a_log, dt_bias, gdn_norm, w_gdn_o, w_out, ffn2_norm,
           ffn2_w_gate, ffn2_w_up, ffn2_w_down, final_norm):
    assert ffn1_norm.shape[0] == 1, "single-layer block"
    batch, seq, d = x.shape
    assert d == D_MODEL and seq % CHUNK == 0
    tm = min(256, seq)
    tq = min(512, seq)
    assert seq % tm == 0 and seq % tq == 0

    assert w_in.shape[2] == D_IN
    w_in_bf = w_in[0].astype(BF16)

    wq = w_uq[0].reshape(MLA_Q_RANK, MLA_HEADS, MLA_NOPE + MLA_ROPE)
    wq_nope, wq_rope = wq[..., :MLA_NOPE], wq[..., MLA_NOPE:]
    zq = lambda n: jnp.zeros((MLA_Q_RANK, MLA_HEADS, n), F32)
    tail = HEAD_PAD - MLA_NOPE - MLA_ROPE
    wqa = jnp.concatenate([wq_nope, wq_rope, zq(tail)], axis=-1).reshape(MLA_Q_RANK, QK_W).astype(BF16)
    wqb = jnp.concatenate([zq(MLA_NOPE), _rot(wq_rope), zq(tail)], axis=-1).reshape(MLA_Q_RANK, QK_W).astype(BF16)
    wkv = w_ukv[0].reshape(MLA_KV_RANK, MLA_HEADS, MLA_NOPE + MLA_V)
    wk = jnp.concatenate([wkv[..., :MLA_NOPE], jnp.zeros((MLA_KV_RANK, MLA_HEADS, HEAD_PAD - MLA_NOPE), F32)],
                         axis=-1).reshape(MLA_KV_RANK, QK_W).astype(BF16)
    wvt = wkv[..., MLA_NOPE:].reshape(MLA_KV_RANK, V_W).T.astype(BF16)
    e_np = np.zeros((2 * LANES, QK_W), np.float32)
    half = MLA_ROPE // 2
    for h in range(MLA_HEADS):
        base = h * HEAD_PAD + MLA_NOPE
        for j in range(MLA_ROPE):
            e_np[j, base + j] = 1.0
        for j in range(half):
            e_np[LANES + half + j, base + j] = -1.0
            e_np[LANES + j, base + half + j] = 1.0
    e_mat = jnp.asarray(e_np, BF16)

    g = np.arange(GROUP_W)
    bd_ones = jnp.asarray((g[:, None] // GDN_DK == g[None, :] // GDN_DK).astype(np.float32), BF16)
    proj_weights = [
        ffn1_norm[0][None], ffn1_w_gate[0].astype(BF16), ffn1_w_up[0].astype(BF16),
        ffn1_w_down[0].astype(BF16), mix_norm[0][None], w_in_bf, q_norm[0][None], wqa, wqb,
        kv_norm[0][None], wk, wvt, e_mat, conv_w[0].astype(F32), bd_ones]

    zero_carry = jnp.zeros((8, 3 * GDN_W), F32)
    _, _, k_m, vt_m, gqkv_m, gba_m, _, meta_tail, win_a, wgate = _token_proj(
        meta_tokens.astype(F32), _rope_tables(0, N_META), zero_carry, proj_weights, N_META, 1, N_META)
    proj_weights = [win_a if w is w_in_bf else w for w in proj_weights]
    tp = 2 * tm if seq % (2 * tm) == 0 else tm
    h1, q, k, vt, gqkv, gba, z, _ = _token_proj(
        x.reshape(batch * seq, d), _rope_tables(N_META, seq), meta_tail, proj_weights, tp, seq // tp, tm)

    pad_rows = lambda a, n, front: jnp.pad(a, ((n - a.shape[0], 0) if front else (0, n - a.shape[0]), (0, 0)))
    vt_meta = jnp.pad(vt_m[0], ((0, 0), (0, LANES - N_META)))
    o_mla = _mla_attn(q, k, vt, pad_rows(k_m, LANES, False), vt_meta, batch, seq, tq, tm)

    hpad = lambda a: jnp.zeros((1, LANES), F32).at[0, GDN_HEADS:2 * GDN_HEADS].set(a)
    arate = hpad(-jnp.exp(a_log[0].astype(F32)))
    dtb = hpad(dt_bias[0].astype(F32))
    gnorm = jnp.tile(gdn_norm[0].astype(F32), GDN_HEADS)[None]
    gdn_params = (arate, dtb, gnorm)
    cps = 4 if seq % (4 * CHUNK) == 0 else 1
    _, state_meta = _gdn(pad_rows(gqkv_m, CHUNK, True), pad_rows(gba_m, CHUNK, True),
                         jnp.zeros((CHUNK, GDN_W), F32), jnp.zeros((N_GROUPS, GDN_DK, GROUP_W), F32),
                         gdn_params, 1, CHUNK, 1, CHUNK - N_META)
    o_gdn, _ = _gdn(gqkv, gba, z, state_meta, gdn_params, batch, seq, cps, 0)

    merge_weights = [
        mix_norm[0][None], wgate, w_mla_o[0].astype(BF16), w_gdn_o[0].astype(BF16), w_out[0].astype(BF16),
        ffn2_norm[0][None], ffn2_w_gate[0].astype(BF16), ffn2_w_up[0].astype(BF16),
        ffn2_w_down[0].astype(BF16), final_norm[None]]
    merge_rows = 2 * MERGE_CHAIN_ROWS if (batch * seq) % (2 * MERGE_CHAIN_ROWS) == 0 else tm
    out = _merge_ffn(h1, o_mla, o_gdn, merge_weights, merge_rows)
    return out.reshape(batch, seq, d)
```

```python
import functools
import math

import jax
import jax.numpy as jnp
import numpy as np
from jax import lax
from jax.experimental import pallas as pl
from jax.experimental.pallas import tpu as pltpu

F32 = jnp.float32
BF16 = jnp.bfloat16

D_MODEL = 1024
N_META = 16
EPS = 1e-6
D_FF = 2816
MLA_HEADS = 8
MLA_Q_RANK = 256
MLA_KV_RANK = 128
MLA_NOPE = 64
MLA_ROPE = 32
MLA_V = 64
ROPE_THETA = 10000.0
GDN_HEADS = 8
GDN_DK = 64
GDN_DV = 64
CONV_K = 4
CHUNK = 64

LANES = 128
HEAD_PAD = 128
QK_W = MLA_HEADS * HEAD_PAD
V_W = MLA_HEADS * MLA_V
GDN_W = GDN_HEADS * GDN_DK
GROUP_HEADS = 4
GROUP_W = GROUP_HEADS * GDN_DK
N_GROUPS = GDN_HEADS // GROUP_HEADS
FFN_CHUNK = 256
MERGE_CHAIN_ROWS = 256
PROJ_W = MLA_Q_RANK + MLA_KV_RANK + LANES + 3 * GDN_W + LANES + GDN_W
W_IN_SPLITS = (MLA_Q_RANK, MLA_KV_RANK, MLA_ROPE, GDN_W, GDN_W, GDN_W, GDN_HEADS, GDN_HEADS, GDN_W,
               D_MODEL, D_MODEL)
(OFF_CQ, OFF_CKV, OFF_KR, OFF_GQ, OFF_GK, OFF_GV, OFF_GB, OFF_GA, OFF_GZ, OFF_GATES, _OFF_GATE_GDN,
 D_IN) = (int(v) for v in np.concatenate([[0], np.cumsum(W_IN_SPLITS)]))
W_TOKEN_COLS = -(-OFF_GATES // LANES) * LANES
W_GATE_START = OFF_GATES // LANES * LANES
VMEM_LIMIT = 60 * 1024 * 1024
NEG_BIG = -1e30


def _const_spec(shape):
    zeros = (0,) * len(shape)
    return pl.BlockSpec(shape, lambda *_: zeros, pipeline_mode=pl.Buffered(1))


def _rms(x, w):
    return x * lax.rsqrt(jnp.mean(x * x, axis=-1, keepdims=True) + EPS) * w


def _dot(a, b):
    return jnp.dot(a, b, preferred_element_type=F32)


def _dot_nt(a, b):
    return lax.dot_general(a, b, (((1,), (1,)), ((), ())), preferred_element_type=F32)


def _silu(x):
    return x * jax.nn.sigmoid(x)


def _split(x, n):
    pieces = []
    for _ in range(n - 1):
        hi = x.astype(BF16)
        pieces.append(hi)
        x = x - hi.astype(F32)
    pieces.append(x.astype(BF16))
    return pieces


def _dot_exact_rhs(x, rhs, n):
    return sum(_dot(p, rhs) for p in _split(x, n))


def _dot_exact_lhs(lhs, x, n):
    return sum(_dot(lhs, p) for p in _split(x, n))


def _head_sums(x, bd_ones):
    return jnp.concatenate([_dot(x[:, g * GROUP_W:(g + 1) * GROUP_W].astype(BF16), bd_ones)
                            for g in range(N_GROUPS)], axis=1)


def _token_proj_kernel(*refs, tiles_per_seq, regroup):
    last = pl.num_programs(0) - 1

    @pl.when(pl.program_id(0) < last)
    def _():
        _token_proj_step(*refs, tiles_per_seq=tiles_per_seq, regroup=regroup, with_matmuls=True)

    @pl.when(pl.program_id(0) == last)
    def _():
        _token_proj_step(*refs, tiles_per_seq=tiles_per_seq, regroup=regroup, with_matmuls=False)


def _token_proj_step(*refs, tiles_per_seq, regroup, with_matmuls):
    it = iter(refs)
    take = lambda n: [next(it) for _ in range(n)]
    x_ref, tc_ref, ts_ref, tkc_ref, tks_ref, c0_ref, n1_ref, wg_ref, wu_ref, wd_ref, nm_ref = take(11)
    if regroup:
        wraw_ref, wrawg_ref = take(2)
    else:
        win_ref, = take(1)
    qn_ref, wqa_ref, wqb_ref, kvn_ref, wk_ref, wvt_ref, e_ref, convw_ref, bd_ones_ref = take(9)
    h1_ref, q_ref, k_ref, vt_ref, gqkv_ref, gba_ref, z_ref, tail_ref = take(8)
    if regroup:
        win_ref, wgate_ref = take(2)
    xe_ref, = take(1)
    tm = x_ref.shape[0]
    step = pl.program_id(0)

    @pl.when(step == 0)
    def _():
        xe_ref[...] = jnp.zeros_like(xe_ref)
        if regroup:
            o = OFF_GQ + LANES - MLA_ROPE
            win_ref[:, :o] = wraw_ref[:, :o].astype(BF16)
            win_ref[:, o:o + 3 * GDN_W] = wraw_ref[:, OFF_GQ:OFF_GB].astype(BF16)
            o += 3 * GDN_W
            lane = lax.broadcasted_iota(jnp.int32, (1, LANES), 1)
            win_ref[:, o:o + LANES] = jnp.where(lane < 2 * GDN_HEADS, wraw_ref[:, OFF_GB:OFF_GB + LANES],
                                                0).astype(BF16)
            win_ref[:, o + LANES:] = wraw_ref[:, OFF_GZ:OFF_GATES].astype(BF16)
            g0 = OFF_GATES - W_GATE_START
            wgate_ref[...] = wrawg_ref[:, g0:g0 + 2 * D_MODEL].astype(BF16)

    n_chains = vt_ref.shape[0]
    cr = tm // n_chains
    rows = [slice(c * cr, (c + 1) * cr) for c in range(n_chains)]
    x = [x_ref[r] for r in rows] if with_matmuls else []
    xn = [_rms(v, n1_ref[...]).astype(BF16) for v in x]

    n_ffn = D_FF // FFN_CHUNK
    n_conv = 3 * GDN_W // LANES
    acts = [[] for _ in rows]
    for c in range(max(n_ffn, n_conv)):
        if c < n_ffn and with_matmuls:
            cols = slice(c * FFN_CHUNK, (c + 1) * FFN_CHUNK)
            for a, v in zip(acts, xn):
                a.append((_silu(_dot(v, wg_ref[:, cols])) * _dot(v, wu_ref[:, cols])).astype(BF16))
        if c < n_conv:
            lanes = slice(c * LANES, (c + 1) * LANES)
            y = sum(convw_ref[t:t + 1, lanes] * xe_ref[pl.ds(8 - (CONV_K - 1) + t, tm), lanes]
                    for t in range(CONV_K))
            gqkv_ref[:, lanes] = _silu(y)

    def l2norm_qk():
        bd_ones = bd_ones_ref[...]
        gq = gqkv_ref[:, :GDN_W]
        gk = gqkv_ref[:, GDN_W:2 * GDN_W]
        gqkv_ref[:, :GDN_W] = gq * lax.rsqrt(_head_sums(gq * gq, bd_ones) + EPS) * (GDN_DK ** -0.5)
        gqkv_ref[:, GDN_W:2 * GDN_W] = gk * lax.rsqrt(_head_sums(gk * gk, bd_ones) + EPS)

    if not with_matmuls:
        l2norm_qk()
        return
    h1 = [v + 0.5 * _dot(jnp.concatenate(a, axis=1), wd_ref[...]) for v, a in zip(x, acts)]
    for r, v in zip(rows, h1):
        h1_ref[r] = v

    un = [_rms(v, nm_ref[...]).astype(BF16) for v in h1]
    proj = [_dot(v, win_ref[...]) for v in un]

    l2norm_qk()
    xe_ref[0:8] = jnp.where(step % tiles_per_seq == 0, c0_ref[...], xe_ref[tm:tm + 8])
    o_raw = MLA_Q_RANK + MLA_KV_RANK + LANES
    for r, p in zip(rows, proj):
        xe_ref[8 + r.start:8 + r.stop] = p[:, o_raw:o_raw + 3 * GDN_W]
        gba_ref[r] = p[:, o_raw + 3 * GDN_W:o_raw + 3 * GDN_W + LANES]
        z_ref[r] = p[:, o_raw + 3 * GDN_W + LANES:]
    tail_ref[...] = proj[-1][cr - 8:, o_raw:o_raw + 3 * GDN_W]

    cqn = [_rms(p[:, :MLA_Q_RANK], qn_ref[...]).astype(BF16) for p in proj]
    qa = [_dot(v, wqa_ref[...]) for v in cqn]
    qb = [_dot(v, wqb_ref[...]) for v in cqn]
    for r, a, b in zip(rows, qa, qb):
        tc = jnp.concatenate([tc_ref[r]] * MLA_HEADS, axis=1)
        ts = jnp.concatenate([ts_ref[r]] * MLA_HEADS, axis=1)
        q_ref[r] = (a * tc + b * ts).astype(BF16)

    ckvn = [_rms(p[:, MLA_Q_RANK:MLA_Q_RANK + MLA_KV_RANK], kvn_ref[...]).astype(BF16) for p in proj]
    for c, (r, p, v) in enumerate(zip(rows, proj, ckvn)):
        kr = p[:, MLA_Q_RANK + MLA_KV_RANK:o_raw]
        kr_terms = jnp.concatenate([kr * tkc_ref[r], kr * tks_ref[r]], axis=1).astype(BF16)
        k_ref[r] = (_dot(v, wk_ref[...]) + _dot(kr_terms, e_ref[...])).astype(BF16)
        vt_ref[c] = _dot_nt(wvt_ref[...], v).astype(BF16)


def _token_proj(x2d, tabs, conv_carry, weights, tm, tiles_per_seq, chain_rows):
    n = x2d.shape[0]
    assert n % tm == 0
    last = n // tm - 1
    tile = lambda i: jnp.minimum(i, last)
    row = lambda w: pl.BlockSpec((tm, w), lambda i: (tile(i), 0))
    tab = pl.BlockSpec((tm, LANES), lambda i: (tile(i) % tiles_per_seq, 0))
    out_widths = (D_MODEL, QK_W, QK_W, None, 3 * GDN_W, LANES, GDN_W)
    out_dtypes = (F32, BF16, BF16, BF16, F32, F32, F32)
    assert tm % chain_rows == 0
    vt_spec = pl.BlockSpec((tm // chain_rows, V_W, chain_rows), lambda i: (tile(i), 0, 0))
    conv_spec = pl.BlockSpec((tm, 3 * GDN_W), lambda i: (jnp.maximum(i - 1, 0), 0))
    tail_shape = (8, 3 * GDN_W)
    out_specs = [vt_spec if w is None else row(w) for w in out_widths]
    out_specs[4] = conv_spec
    out_specs.append(pl.BlockSpec(tail_shape, lambda i: (0, 0)))
    out_shape = [jax.ShapeDtypeStruct((n // chain_rows, V_W, chain_rows) if w is None else (n, w), d)
                 for w, d in zip(out_widths, out_dtypes)] + [jax.ShapeDtypeStruct(tail_shape, F32)]

    regroup = any(w.shape == (D_MODEL, D_IN) for w in weights)
    operands, w_specs = [], []
    for w in weights:
        if w.shape == (D_MODEL, D_IN):
            assert 2 * W_GATE_START >= D_IN
            operands += [w, w]
            w_specs += [_const_spec((D_MODEL, W_TOKEN_COLS)),
                        pl.BlockSpec((D_MODEL, W_GATE_START), lambda i: (0, 1), pipeline_mode=pl.Buffered(1))]
        else:
            operands.append(w)
            w_specs.append(_const_spec(w.shape))
    if regroup:
        for shape in ((D_MODEL, PROJ_W), (D_MODEL, 2 * D_MODEL)):
            out_specs.append(pl.BlockSpec(shape, lambda i: (0, 0)))
            out_shape.append(jax.ShapeDtypeStruct(shape, BF16))
    return pl.pallas_call(
        functools.partial(_token_proj_kernel, tiles_per_seq=tiles_per_seq, regroup=regroup),
        grid=(n // tm + 1,),
        in_specs=[row(D_MODEL)] + [tab] * len(tabs) + [_const_spec(tail_shape)] + w_specs,
        out_specs=out_specs,
        out_shape=out_shape,
        scratch_shapes=[pltpu.VMEM((8 + tm, 3 * GDN_W), F32)],
        compiler_params=pltpu.CompilerParams(dimension_semantics=("arbitrary",),
                                             vmem_limit_bytes=VMEM_LIMIT),
        name="token_proj",
    )(x2d, *tabs, conv_carry, *operands)


def _mla_kernel(q_ref, k_ref, vt_ref, km_ref, vmt_ref, o_ref, st_ref, *, tq, tk):
    def tile(qi, carry):
        rows = pl.ds(pl.multiple_of(qi * tq, tq), tq)
        _mla_tile(qi, q_ref.at[rows], k_ref, vt_ref, km_ref, vmt_ref, o_ref.at[rows], st_ref, tq=tq, tk=tk)
        return carry

    lax.fori_loop(0, q_ref.shape[0] // tq, tile, 0)


def _mla_tile(qi, q_ref, k_ref, vt_ref, km_ref, vmt_ref, o_ref, st_ref, *, tq, tk):
    heads = range(2)
    hs = [slice(h * HEAD_PAD, (h + 1) * HEAD_PAD) for h in heads]
    vs = [slice(h * MLA_V, (h + 1) * MLA_V) for h in heads]
    q = [q_ref[:, s] for s in hs]
    colmax = lambda s: jnp.max(s, axis=0, keepdims=True)

    def with_ones(vt):
        return jnp.concatenate([vt, jnp.ones((8, vt.shape[1]), BF16)], axis=0)

    def scores(ki, slot, q_from=0):
        rows = pl.ds(pl.multiple_of(ki * tk, tk), tk)
        block_max = []
        for h in heads:
            s = _dot_nt(k_ref[rows, hs[h]], q[h][q_from:])
            st_ref[slot, h, :, q_from:] = s
            block_max.append(colmax(s))
        return block_max

    def update(ki, slot, m, acc, block_max, mask, q_from=0):
        st = [st_ref[slot, h, :, q_from:] for h in heads]
        if mask is not None:
            st = [jnp.where(mask, s, NEG_BIG) for s in st]
            block_max = [colmax(s) for s in st]
        m_old = [x[:, q_from:] for x in m]
        m_new = [jnp.maximum(m_old[h], block_max[h]) for h in heads]
        p = [jnp.exp2(st[h] - m_new[h]).astype(BF16) for h in heads]
        vt = vt_ref[ki]
        acc_new = [jnp.exp2(m_old[h] - m_new[h]) * acc[h][:, q_from:] + _dot(with_ones(vt[vs[h]]), p[h])
                   for h in heads]
        if q_from:
            m_new = [jnp.concatenate([m[h][:, :q_from], m_new[h]], axis=1) for h in heads]
            acc_new = [jnp.concatenate([acc[h][:, :q_from], acc_new[h]], axis=1) for h in heads]
        return m_new, acc_new

    bm_a = scores(0, 0)
    bm_b = scores(1, 1)
    meta_valid = lax.broadcasted_iota(jnp.int32, (km_ref.shape[0], tq), 0) < N_META
    st = [jnp.where(meta_valid, _dot_nt(km_ref[:, hs[h]], q[h]), NEG_BIG) for h in heads]
    m = [colmax(s) for s in st]
    acc = [_dot(with_ones(vmt_ref[vs[h], :]), jnp.exp2(st[h] - m[h]).astype(BF16)) for h in heads]

    def stage(ka, carry, cur, nxt):
        m, acc, bm_a, bm_b = carry
        bm_a2 = scores(ka + 2, nxt[0])
        m, acc = update(ka, cur[0], m, acc, bm_a, None)
        bm_b2 = scores(ka + 3, nxt[1])
        m, acc = update(ka + 1, cur[1], m, acc, bm_b, None)
        return m, acc, bm_a2, bm_b2

    def body(i, carry):
        return stage(4 * i + 2, stage(4 * i, carry, (0, 1), (2, 3)), (2, 3), (0, 1))

    causal = (lax.broadcasted_iota(jnp.int32, (tk, tq), 1) >= lax.broadcasted_iota(jnp.int32, (tk, tq), 0))
    n_full = 2 * qi

    def finish(carry, cur):
        m, acc, _, _ = carry
        m, acc = update(n_full, cur[0], m, acc, None, causal)
        m, acc = update(n_full + 1, cur[1], m, acc, None, causal[:, :tk], q_from=tk)
        return acc

    carry = lax.fori_loop(0, qi // 2, body, (m, acc, bm_a, bm_b))
    acc = lax.cond(qi % 2 == 1,
                   lambda c: finish(stage(n_full - 2, c, (0, 1), (2, 3)), (2, 3)),
                   lambda c: finish(c, (0, 1)), carry)
    o_ref[...] = jnp.concatenate([(a[:MLA_V] * (1.0 / a[MLA_V:MLA_V + 1])).T for a in acc],
                                 axis=1).astype(BF16)


def _mla_attn(q, k, vt, k_meta, vt_meta, batch, seq, tq, tk):
    nq = seq // tq
    assert vt.shape[2] == tk and tq == 2 * tk
    kern = functools.partial(_mla_kernel, tq=tq, tk=tk)
    return pl.pallas_call(
        kern,
        grid=(batch, MLA_HEADS // 2),
        in_specs=[
            pl.BlockSpec((seq, 2 * HEAD_PAD), lambda b, hp: (b, hp)),
            pl.BlockSpec((seq, 2 * HEAD_PAD), lambda b, hp: (b, hp)),
            pl.BlockSpec((seq // tk, 2 * MLA_V, tk), lambda b, hp: (b, hp, 0)),
            pl.BlockSpec((k_meta.shape[0], 2 * HEAD_PAD), lambda b, hp: (0, hp)),
            pl.BlockSpec((2 * MLA_V, vt_meta.shape[1]), lambda b, hp: (hp, 0)),
        ],
        out_specs=pl.BlockSpec((seq, 2 * MLA_V), lambda b, hp: (b, hp)),
        out_shape=jax.ShapeDtypeStruct((batch * seq, V_W), BF16),
        scratch_shapes=[pltpu.VMEM((4, 2, tk, tq), F32)],
        compiler_params=pltpu.CompilerParams(
            dimension_semantics=("arbitrary", "arbitrary"),
            vmem_limit_bytes=VMEM_LIMIT),
        name="mla_attn",
    )(q, k, vt, k_meta, vt_meta)


LEVELS = (1, 2, 4, 8, 16, 32)
GDN_CONST_NAMES = ("expand_b", "expand_a", "ltri", "eye_t", "tril_t", "stril_t", "level_masks", "bd_ones")


def _gdn_constants(cps):
    i = np.arange(CHUNK)[:, None]
    lane = np.arange(GDN_W)[None, :]
    j = lane % GDN_DK
    c = {}
    r = np.arange(LANES)[:, None]
    c["expand_b"] = (r == lane // GDN_DK)
    c["expand_a"] = (r == GDN_HEADS + lane // GDN_DK)
    t = np.arange(cps * CHUNK)
    c["ltri"] = (t[:, None] >= t[None, :]) & (t[:, None] // CHUNK == t[None, :] // CHUNK)
    c["eye_t"] = (i == j)
    c["tril_t"] = (i >= j)
    c["stril_t"] = (i > j)
    jg = j[:, :GROUP_W]
    c["level_masks"] = np.stack([
        ((i // (2 * s) == jg // (2 * s)) & ((i // s) % 2 == 1) & ((jg // s) % 2 == 0))
        for s in LEVELS])
    g = np.arange(GROUP_W)
    c["bd_ones"] = (g[:, None] // GDN_DK == g[None, :] // GDN_DK)
    bf = ("bd_ones", "expand_b", "expand_a", "ltri")
    return [jnp.asarray(c[k].astype(np.float32), BF16 if k in bf else F32) for k in GDN_CONST_NAMES]


def _block_diag(y, lo_half):
    zeros = jnp.zeros((GDN_DK, LANES), y.dtype)
    blocks = []
    for h in range(GROUP_HEADS):
        t = h // 2
        tile = y[:, t * LANES:(t + 1) * LANES]
        piece = jnp.where(lo_half, tile, 0) if h % 2 == 0 else jnp.where(lo_half, 0, tile)
        blocks.append(jnp.concatenate([piece, zeros] if t == 0 else [zeros, piece], axis=1))
    return jnp.concatenate(blocks, axis=0)


def _head_matmul(x, y, lo_half):
    return _dot(x.astype(BF16), _block_diag(y.astype(BF16), lo_half))


def _diag_blocks(m, lo_half):
    tiles = []
    for t in range(m.shape[1] // LANES):
        h = 2 * (t % 2)
        cols = slice(t * LANES, (t + 1) * LANES)
        tiles.append(jnp.where(lo_half, m[h * GDN_DK:(h + 1) * GDN_DK, cols],
                               m[(h + 1) * GDN_DK:(h + 2) * GDN_DK, cols]))
    return jnp.concatenate(tiles, axis=1)


def _gdn_kernel(gx_ref, ba_ref, z_ref, s0_ref, arate_ref, dtb_ref, gnorm_ref,
                expand_b_ref, expand_a_ref, ltri_ref, eye_t_ref, tril_t_ref, stril_t_ref,
                level_masks_ref, bd_ones_ref,
                o_ref, sout_ref,
                state_ref, kn_ref, kbq_ref, vbk_ref, kdec_ref, qg_ref, decay_ref, sdec1_ref,
                lhs_ref, amat_ref, u_ref, ku_ref, sdec2_ref, *, cps, front_pad, blocks_per_seq):
    s = pl.program_id(0)
    rows = cps * CHUNK
    stage1 = (kn_ref, kbq_ref, vbk_ref, kdec_ref, qg_ref, decay_ref, sdec1_ref)
    stage2 = (lhs_ref, amat_ref, u_ref, ku_ref, sdec2_ref)

    @pl.when(s == 0)
    def _():
        for r in stage1 + stage2 + (state_ref,):
            r[...] = jnp.zeros_like(r)

    kn_r, kbq_r, vbk_r, kdec_r, qg_r, decay_r, sdec1_r = (r.at[1 - s % 2] for r in stage1)
    kn_w, kbq_w, vbk_w, kdec_w, qg_w, decay_w, sdec1_w = (r.at[s % 2] for r in stage1)

    lo_half = lax.broadcasted_iota(jnp.int32, (1, LANES), 1) < GDN_DK
    eye, tril, stril = eye_t_ref[...], tril_t_ref[...], stril_t_ref[...]
    chains = [(j, gi) for j in range(cps) for gi in range(N_GROUPS)]
    rsl = lambda j: slice(j * CHUNK, (j + 1) * CHUNK)
    gsl = lambda gi: slice(gi * GROUP_W, (gi + 1) * GROUP_W)

    first_of_seq = (s - 2) % blocks_per_seq == 0
    states = [jnp.where(first_of_seq, s0_ref[gi], state_ref[gi]) for gi in range(N_GROUPS)]
    o_rows = [[None] * N_GROUPS for _ in range(cps)]
    in_flight = {}

    def ride(i):
        if i - 1 in in_flight:
            for gi, big in enumerate(in_flight.pop(i - 1)):
                v_new = u_ref[i - 1, gi] - big[CHUNK:2 * CHUNK]
                o_rows[i - 1][gi] = big[2 * CHUNK:] + _dot(amat_ref[i - 1, gi],
                                                           _block_diag(v_new.astype(BF16), lo_half))
        if i < cps:
            bigs = []
            for gi in range(N_GROUPS):
                state = states[gi]
                big = _dot(lhs_ref[i, gi], _block_diag(state.astype(BF16), lo_half))
                states[gi] = sdec2_ref[i, gi][0:1] * state - big[:CHUNK] + ku_ref[i, gi]
                bigs.append(big)
            in_flight[i] = bigs

    ba = ba_ref[...]
    beta = jax.nn.sigmoid(ba)
    sp_in = ba + dtb_ref[...]
    g = arate_ref[...] * (jnp.maximum(sp_in, 0.0) + jnp.log1p(jnp.exp(-jnp.abs(sp_in))))
    if front_pad:
        valid = lax.broadcasted_iota(jnp.int32, (rows, 1), 0) >= front_pad
        beta = jnp.where(valid, beta, 0.0)
        g = jnp.where(valid, g, 0.0)
    beta_e = _dot_exact_rhs(beta, expand_b_ref[...], 2)
    gc = _dot_exact_lhs(ltri_ref[...], g, 3)
    ride(0)

    lmat, amat = [], []
    for j, gi in chains:
        rs, gs = rsl(j), gsl(gi)
        kstack = _block_diag(kn_r[rs, gs], lo_half)
        sc = _dot_nt(kbq_r[j, gi], kstack)
        lmat.append(sc[:CHUNK] * decay_r[rs, gs] * stril[:, gs])
        amat.append(sc[CHUNK:] * decay_r[rs, gs])

    gc_e = _dot_exact_rhs(gc, expand_a_ref[...], 3)
    ride(1)

    def level(xinv, li):
        xo = [_head_matmul(x, lm * level_masks_ref[li], lo_half) for x, lm in zip(xinv, lmat)]
        return [x - _head_matmul(y, x, lo_half) for x, y in zip(xinv, xo)]

    xinv = [eye[:, gsl(gi)] - lm * level_masks_ref[0] for (j, gi), lm in zip(chains, lmat)]
    xinv = level(xinv, 1)

    qn = gx_ref[:, :GDN_W]
    kn = gx_ref[:, GDN_W:2 * GDN_W]
    v = gx_ref[:, 2 * GDN_W:]
    egc = jnp.exp(gc_e)
    kb = kn * beta_e
    vb = v * beta_e
    kbg = kb * egc
    qg = qn * egc
    kn_w[...] = kn.astype(BF16)
    qg_w[...] = qg
    for j, gi in chains:
        rs, gs = rsl(j), gsl(gi)
        kbq_w[j, gi] = jnp.concatenate([kb[rs, gs], qn[rs, gs]], axis=0).astype(BF16)
        vbk_w[j, gi] = jnp.concatenate([_block_diag(vb[rs, gs].astype(BF16), lo_half),
                                        _block_diag(kbg[rs, gs].astype(BF16), lo_half)], axis=1)
    ride(2)
    xinv = level(xinv, 2)

    decay, kdec, s_decay = [], [], []
    for j in range(cps):
        gce = gc_e[rsl(j)]
        gc_t = jnp.sum(gce * eye, axis=0, keepdims=True)
        decay.append(jnp.exp(jnp.where(tril > 0.5, gce - gc_t, -jnp.inf)))
        g_last = gce[CHUNK - 1:CHUNK]
        kdec.append(kn[rsl(j)] * jnp.exp(g_last - gce))
        s_decay.append(jnp.exp(g_last))
        decay_w[rsl(j)] = decay[j]
        kdec_w[rsl(j)] = kdec[j]
        for gi in range(N_GROUPS):
            sdec1_w[j, gi] = jnp.broadcast_to(s_decay[j][:, gsl(gi)], (8, GROUP_W))
    ride(3)
    xinv = level(xinv, 3)
    for i in range(4, cps + 1):
        ride(i)
    ride(cps)

    for gi in range(N_GROUPS):
        state_ref[gi] = states[gi]

    @pl.when(s == pl.num_programs(0) - 1)
    def _():
        sout_ref[...] = state_ref[...]

    o = jnp.concatenate([jnp.concatenate(r, axis=1) for r in o_rows], axis=0)
    ms = _head_sums(o * o, bd_ones_ref[...]) * (1.0 / GDN_DV)
    o = o * lax.rsqrt(ms + EPS) * gnorm_ref[...] * _silu(z_ref[...])
    o_ref[...] = o.astype(BF16)

    for li in range(4, len(LEVELS)):
        xinv = level(xinv, li)

    uw = [_dot(x.astype(BF16), vbk_r[j, gi]) for (j, gi), x in zip(chains, xinv)]
    kwu = [_diag_blocks(_dot(kdec_r[rsl(j), gsl(gi)].T.astype(BF16), m.astype(BF16)), lo_half)
           for (j, gi), m in zip(chains, uw)]

    for ci, (j, gi) in enumerate(chains):
        rs, gs = rsl(j), gsl(gi)
        u, wmat = uw[ci][:, :GROUP_W], uw[ci][:, GROUP_W:]
        ku, kw = kwu[ci][:, :GROUP_W], kwu[ci][:, GROUP_W:]
        lhs_ref[j, gi] = jnp.concatenate([kw, wmat, qg_r[rs, gs]], axis=0).astype(BF16)
        amat_ref[j, gi] = amat[ci].astype(BF16)
        u_ref[j, gi] = u
        ku_ref[j, gi] = ku
        sdec2_ref[j, gi] = sdec1_r[j, gi]


def _gdn(gqkv, gba, z, state0, params, batch, seq, cps, front_pad):
    rows = cps * CHUNK
    blocks_per_seq = seq // rows
    n_blocks = batch * blocks_per_seq
    consts = _gdn_constants(cps)
    cur = lambda w: pl.BlockSpec((rows, w), lambda s: (jnp.minimum(s, n_blocks - 1), 0))
    done = lambda w: pl.BlockSpec((rows, w), lambda s: (jnp.maximum(s - 2, 0), 0))
    state_shape = (N_GROUPS, GDN_DK, GROUP_W)
    per_chain = lambda r, w, dt: pltpu.VMEM((cps, N_GROUPS, r, w), dt)
    per_row = lambda dt: pltpu.VMEM((rows, GDN_W), dt)
    two = lambda v: pltpu.VMEM((2,) + tuple(v.shape), v.dtype)
    return pl.pallas_call(
        functools.partial(_gdn_kernel, cps=cps, front_pad=front_pad, blocks_per_seq=blocks_per_seq),
        grid=(n_blocks + 2,),
        in_specs=[cur(3 * GDN_W), cur(LANES), done(GDN_W), _const_spec(state_shape)]
                 + [_const_spec(a.shape) for a in params]
                 + [_const_spec(a.shape) for a in consts],
        out_specs=[done(GDN_W), pl.BlockSpec(state_shape, lambda s: (0, 0, 0))],
        out_shape=[jax.ShapeDtypeStruct((batch * seq, GDN_W), BF16),
                   jax.ShapeDtypeStruct(state_shape, F32)],
        scratch_shapes=[pltpu.VMEM(state_shape, F32),
                        two(per_row(BF16)),
                        two(per_chain(2 * CHUNK, GROUP_W, BF16)),
                        two(per_chain(GROUP_W, 2 * GROUP_W, BF16)),
                        two(per_row(F32)),
                        two(per_row(F32)),
                        two(per_row(F32)),
                        two(per_chain(8, GROUP_W, F32)),
                        per_chain(3 * CHUNK, GROUP_W, BF16),
                        per_chain(CHUNK, GROUP_W, BF16),
                        per_chain(CHUNK, GROUP_W, F32),
                        per_chain(CHUNK, GROUP_W, F32),
                        per_chain(8, GROUP_W, F32)],
        compiler_params=pltpu.CompilerParams(dimension_semantics=("arbitrary",),
                                             vmem_limit_bytes=VMEM_LIMIT),
        name="gdn_chunk",
    )(gqkv, gba, z, state0, *params, *consts)


def _merge_ffn_kernel(h1_ref, om_ref, og_ref, nm_ref, wgate_ref, wmo_ref, wgo_ref, wout_ref,
                      n2_ref, wg_ref, wu_ref, wd_ref, nf_ref, out_ref):
    n_chains = max(1, h1_ref.shape[0] // MERGE_CHAIN_ROWS)
    rows = [slice(c * MERGE_CHAIN_ROWS, (c + 1) * MERGE_CHAIN_ROWS) if n_chains > 1 else slice(None)
            for c in range(n_chains)]
    h1 = [h1_ref[r] for r in rows]
    un = [_rms(h, nm_ref[...]).astype(BF16) for h in h1]
    gates = [jax.nn.sigmoid(_dot(u, wgate_ref[...])) for u in un]
    merged = [(g[:, :D_MODEL] * _dot(om_ref[r], wmo_ref[...])
               + g[:, D_MODEL:] * _dot(og_ref[r], wgo_ref[...])).astype(BF16) for g, r in zip(gates, rows)]
    h2 = [h + _dot(m, wout_ref[...]) for h, m in zip(h1, merged)]
    xn = [_rms(h, n2_ref[...]).astype(BF16) for h in h2]
    acts = [[] for _ in rows]
    for c in range(D_FF // FFN_CHUNK):
        cols = slice(c * FFN_CHUNK, (c + 1) * FFN_CHUNK)
        for a, x in zip(acts, xn):
            a.append((_silu(_dot(x, wg_ref[:, cols])) * _dot(x, wu_ref[:, cols])).astype(BF16))
    h3 = [h + 0.5 * _dot(jnp.concatenate(a, axis=1), wd_ref[...]) for h, a in zip(h2, acts)]
    for r, h in zip(rows, h3):
        out_ref[r] = _rms(h, nf_ref[...])


def _merge_ffn(h1, o_mla, o_gdn, weights, tm):
    n = h1.shape[0]
    row = lambda w: pl.BlockSpec((tm, w), lambda i: (i, 0))
    return pl.pallas_call(
        _merge_ffn_kernel,
        grid=(n // tm,),
        in_specs=[row(D_MODEL), row(V_W), row(GDN_W)] + [_const_spec(w.shape) for w in weights],
        out_specs=row(D_MODEL),
        out_shape=jax.ShapeDtypeStruct((n, D_MODEL), F32),
        compiler_params=pltpu.CompilerParams(dimension_semantics=("arbitrary",),
                                             vmem_limit_bytes=VMEM_LIMIT),
        name="merge_ffn",
    )(h1, o_mla, o_gdn, *weights)


def _rope_tables(first_pos, n):
    pos = np.arange(first_pos, first_pos + n, dtype=np.float64)
    inv = ROPE_THETA ** (-np.arange(0, MLA_ROPE, 2, dtype=np.float64) / MLA_ROPE)
    ang = pos[:, None] * inv[None, :]
    cos2 = np.tile(np.cos(ang), (1, 2))
    sin2 = np.tile(np.sin(ang), (1, 2))
    scale = (MLA_NOPE + MLA_ROPE) ** -0.5 * math.log2(math.e)
    pad = np.zeros((n, HEAD_PAD - MLA_NOPE - MLA_ROPE))
    tab_c = np.concatenate([np.ones((n, MLA_NOPE)), cos2, pad], axis=1) * scale
    tab_s = np.concatenate([np.zeros((n, MLA_NOPE)), sin2, pad], axis=1) * scale
    kpad = np.zeros((n, LANES - MLA_ROPE))
    tab_kc = np.concatenate([cos2, kpad], axis=1)
    tab_ks = np.concatenate([sin2, kpad], axis=1)
    return tuple(jnp.asarray(t, F32) for t in (tab_c, tab_s, tab_kc, tab_ks))


def _rot(w):
    half = MLA_ROPE // 2
    return jnp.concatenate([-w[..., half:], w[..., :half]], axis=-1)


def kernel(x, meta_tokens, ffn1_norm, ffn1_w_gate, ffn1_w_up, ffn1_w_down, mix_norm, w_in, q_norm, w_uq,
           kv_norm, w_ukv, w_mla_o, conv_w, a_log, dt_bias, gdn_norm, w_gdn_o, w_out, ffn2_norm,
           ffn2_w_gate, ffn2_w_up, ffn2_w_down, final_norm):
    assert ffn1_norm.shape[0] == 1, "single-layer block"
    batch, seq, d = x.shape
    assert d == D_MODEL and seq % CHUNK == 0
    tm = min(256, seq)
    tq = min(512, seq)
    assert seq % tm == 0 and seq % tq == 0

    assert w_in.shape[2] == D_IN
    w_in_raw = w_in[0]

    wq = w_uq[0].reshape(MLA_Q_RANK, MLA_HEADS, MLA_NOPE + MLA_ROPE)
    wq_nope, wq_rope = wq[..., :MLA_NOPE], wq[..., MLA_NOPE:]
    zq = lambda n: jnp.zeros((MLA_Q_RANK, MLA_HEADS, n), F32)
    tail = HEAD_PAD - MLA_NOPE - MLA_ROPE
    wqa = jnp.concatenate([wq_nope, wq_rope, zq(tail)], axis=-1).reshape(MLA_Q_RANK, QK_W).astype(BF16)
    wqb = jnp.concatenate([zq(MLA_NOPE), _rot(wq_rope), zq(tail)], axis=-1).reshape(MLA_Q_RANK, QK_W).astype(BF16)
    wkv = w_ukv[0].reshape(MLA_KV_RANK, MLA_HEADS, MLA_NOPE + MLA_V)
    wk = jnp.concatenate([wkv[..., :MLA_NOPE], jnp.zeros((MLA_KV_RANK, MLA_HEADS, HEAD_PAD - MLA_NOPE), F32)],
                         axis=-1).reshape(MLA_KV_RANK, QK_W).astype(BF16)
    wvt = wkv[..., MLA_NOPE:].reshape(MLA_KV_RANK, V_W).T.astype(BF16)
    e_np = np.zeros((2 * LANES, QK_W), np.float32)
    half = MLA_ROPE // 2
    for h in range(MLA_HEADS):
        base = h * HEAD_PAD + MLA_NOPE
        for j in range(MLA_ROPE):
            e_np[j, base + j] = 1.0
        for j in range(half):
            e_np[LANES + half + j, base + j] = -1.0
            e_np[LANES + j, base + half + j] = 1.0
    e_mat = jnp.asarray(e_np, BF16)

    g = np.arange(GROUP_W)
    bd_ones = jnp.asarray((g[:, None] // GDN_DK == g[None, :] // GDN_DK).astype(np.float32), BF16)
    proj_weights = [
        ffn1_norm[0][None], ffn1_w_gate[0].astype(BF16), ffn1_w_up[0].astype(BF16),
        ffn1_w_down[0].astype(BF16), mix_norm[0][None], w_in_raw, q_norm[0][None], wqa, wqb,
        kv_norm[0][None], wk, wvt, e_mat, conv_w[0].astype(F32), bd_ones]

    zero_carry = jnp.zeros((8, 3 * GDN_W), F32)
    _, _, k_m, vt_m, gqkv_m, gba_m, _, meta_tail, win_a, wgate = _token_proj(
        meta_tokens.astype(F32), _rope_tables(0, N_META), zero_carry, proj_weights, N_META, 1, N_META)
    proj_weights = [win_a if w is w_in_raw else w for w in proj_weights]
    tp = 2 * tm if seq % (2 * tm) == 0 else tm
    h1, q, k, vt, gqkv, gba, z, _ = _token_proj(
        x.reshape(batch * seq, d), _rope_tables(N_META, seq), meta_tail, proj_weights, tp, seq // tp, tm)

    pad_rows = lambda a, n, front: jnp.pad(a, ((n - a.shape[0], 0) if front else (0, n - a.shape[0]), (0, 0)))
    o_mla = _mla_attn(q, k, vt, k_m, vt_m[0], batch, seq, tq, tm)

    hpad = lambda a: jnp.zeros((1, LANES), F32).at[0, GDN_HEADS:2 * GDN_HEADS].set(a)
    arate = hpad(-jnp.exp(a_log[0].astype(F32)))
    dtb = hpad(dt_bias[0].astype(F32))
    gnorm = jnp.tile(gdn_norm[0].astype(F32), GDN_HEADS)[None]
    gdn_params = (arate, dtb, gnorm)
    cps = 4 if seq % (4 * CHUNK) == 0 else 1
    _, state_meta = _gdn(pad_rows(gqkv_m, CHUNK, True), pad_rows(gba_m, CHUNK, True),
                         jnp.zeros((CHUNK, GDN_W), F32), jnp.zeros((N_GROUPS, GDN_DK, GROUP_W), F32),
                         gdn_params, 1, CHUNK, 1, CHUNK - N_META)
    o_gdn, _ = _gdn(gqkv, gba, z, state_meta, gdn_params, batch, seq, cps, 0)

    merge_weights = [
        mix_norm[0][None], wgate, w_mla_o[0].astype(BF16), w_gdn_o[0].astype(BF16), w_out[0].astype(BF16),
        ffn2_norm[0][None], ffn2_w_gate[0].astype(BF16), ffn2_w_up[0].astype(BF16),
        ffn2_w_down[0].astype(BF16), final_norm[None]]
    merge_rows = 2 * MERGE_CHAIN_ROWS if (batch * seq) % (2 * MERGE_CHAIN_ROWS) == 0 else tm
    out = _merge_ffn(h1, o_mla, o_gdn, merge_weights, merge_rows)
    return out.reshape(batch, seq, d)
```

```python
import functools
import math

import jax
import jax.numpy as jnp
import numpy as np
from jax import lax
from jax.experimental import pallas as pl
from jax.experimental.pallas import tpu as pltpu

F32 = jnp.float32
BF16 = jnp.bfloat16

D_MODEL = 1024
N_META = 16
EPS = 1e-6
D_FF = 2816
MLA_HEADS = 8
MLA_Q_RANK = 256
MLA_KV_RANK = 128
MLA_NOPE = 64
MLA_ROPE = 32
MLA_V = 64
ROPE_THETA = 10000.0
GDN_HEADS = 8
GDN_DK = 64
GDN_DV = 64
CONV_K = 4
CHUNK = 64

LANES = 128
HEAD_PAD = 128
QK_W = MLA_HEADS * HEAD_PAD
V_W = MLA_HEADS * MLA_V
GDN_W = GDN_HEADS * GDN_DK
GROUP_HEADS = 4
GROUP_W = GROUP_HEADS * GDN_DK
N_GROUPS = GDN_HEADS // GROUP_HEADS
FFN_CHUNK = 256
MERGE_CHAIN_ROWS = 256
PROJ_W = MLA_Q_RANK + MLA_KV_RANK + LANES + 3 * GDN_W + LANES + GDN_W
W_IN_SPLITS = (MLA_Q_RANK, MLA_KV_RANK, MLA_ROPE, GDN_W, GDN_W, GDN_W, GDN_HEADS, GDN_HEADS, GDN_W,
               D_MODEL, D_MODEL)
(OFF_CQ, OFF_CKV, OFF_KR, OFF_GQ, OFF_GK, OFF_GV, OFF_GB, OFF_GA, OFF_GZ, OFF_GATES, _OFF_GATE_GDN,
 D_IN) = (int(v) for v in np.concatenate([[0], np.cumsum(W_IN_SPLITS)]))
W_TOKEN_COLS = -(-OFF_GATES // LANES) * LANES
W_GATE_START = OFF_GATES // LANES * LANES
VMEM_LIMIT = 60 * 1024 * 1024
NEG_BIG = -1e30


def _const_spec(shape):
    zeros = (0,) * len(shape)
    return pl.BlockSpec(shape, lambda *_: zeros, pipeline_mode=pl.Buffered(1))


def _rms(x, w):
    return x * lax.rsqrt(jnp.mean(x * x, axis=-1, keepdims=True) + EPS) * w


def _dot(a, b):
    return jnp.dot(a, b, preferred_element_type=F32)


def _dot_nt(a, b):
    return lax.dot_general(a, b, (((1,), (1,)), ((), ())), preferred_element_type=F32)


def _silu(x):
    return x * jax.nn.sigmoid(x)


def _split(x, n):
    pieces = []
    for _ in range(n - 1):
        hi = x.astype(BF16)
        pieces.append(hi)
        x = x - hi.astype(F32)
    pieces.append(x.astype(BF16))
    return pieces


def _dot_exact_rhs(x, rhs, n):
    return sum(_dot(p, rhs) for p in _split(x, n))


def _dot_exact_lhs(lhs, x, n):
    return sum(_dot(lhs, p) for p in _split(x, n))


def _head_sums(x, bd_ones):
    return jnp.concatenate([_dot(x[:, g * GROUP_W:(g + 1) * GROUP_W].astype(BF16), bd_ones)
                            for g in range(N_GROUPS)], axis=1)


def _token_proj_kernel(*refs, tiles_per_seq, regroup):
    last = pl.num_programs(0) - 1

    @pl.when(pl.program_id(0) < last)
    def _():
        _token_proj_step(*refs, tiles_per_seq=tiles_per_seq, regroup=regroup, with_matmuls=True)

    @pl.when(pl.program_id(0) == last)
    def _():
        _token_proj_step(*refs, tiles_per_seq=tiles_per_seq, regroup=regroup, with_matmuls=False)


def _token_proj_step(*refs, tiles_per_seq, regroup, with_matmuls):
    it = iter(refs)
    take = lambda n: [next(it) for _ in range(n)]
    x_ref, tc_ref, ts_ref, tkc_ref, tks_ref, c0_ref, n1_ref, wg_ref, wu_ref, wd_ref, nm_ref = take(11)
    if regroup:
        wraw_ref, wrawg_ref = take(2)
    else:
        win_ref, = take(1)
    qn_ref, wqa_ref, wqb_ref, kvn_ref, wk_ref, wvt_ref, e_ref, convw_ref, bd_ones_ref = take(9)
    h1_ref, q_ref, k_ref, vt_ref, gqkv_ref, gba_ref, z_ref, tail_ref = take(8)
    if regroup:
        win_ref, wgate_ref = take(2)
    xe_ref, = take(1)
    tm = x_ref.shape[0]
    step = pl.program_id(0)

    @pl.when(step == 0)
    def _():
        xe_ref[...] = jnp.zeros_like(xe_ref)
        if regroup:
            o = OFF_GQ + LANES - MLA_ROPE
            win_ref[:o] = wraw_ref[:o].astype(BF16)
            win_ref[o:o + 3 * GDN_W] = wraw_ref[OFF_GQ:OFF_GB].astype(BF16)
            o += 3 * GDN_W
            row = lax.broadcasted_iota(jnp.int32, (LANES, 1), 0)
            win_ref[o:o + LANES] = jnp.where(row < 2 * GDN_HEADS, wraw_ref[OFF_GB:OFF_GB + LANES], 0).astype(BF16)
            win_ref[o + LANES:] = wraw_ref[OFF_GZ:OFF_GATES].astype(BF16)
            g0 = OFF_GATES - W_GATE_START
            wgate_ref[...] = wrawg_ref[g0:g0 + 2 * D_MODEL].astype(BF16)

    n_chains = vt_ref.shape[0]
    cr = tm // n_chains
    rows = [slice(c * cr, (c + 1) * cr) for c in range(n_chains)]
    x = [x_ref[r] for r in rows] if with_matmuls else []
    xn = [_rms(v, n1_ref[...]).astype(BF16) for v in x]

    n_ffn = D_FF // FFN_CHUNK
    n_conv = 3 * GDN_W // LANES
    acts = [[] for _ in rows]
    for c in range(max(n_ffn, n_conv)):
        if c < n_ffn and with_matmuls:
            cols = slice(c * FFN_CHUNK, (c + 1) * FFN_CHUNK)
            for a, v in zip(acts, xn):
                a.append((_silu(_dot(v, wg_ref[:, cols])) * _dot(v, wu_ref[:, cols])).astype(BF16))
        if c < n_conv:
            lanes = slice(c * LANES, (c + 1) * LANES)
            y = sum(convw_ref[t:t + 1, lanes] * xe_ref[pl.ds(8 - (CONV_K - 1) + t, tm), lanes]
                    for t in range(CONV_K))
            gqkv_ref[:, lanes] = _silu(y)

    def l2norm_qk():
        bd_ones = bd_ones_ref[...]
        gq = gqkv_ref[:, :GDN_W]
        gk = gqkv_ref[:, GDN_W:2 * GDN_W]
        gqkv_ref[:, :GDN_W] = gq * lax.rsqrt(_head_sums(gq * gq, bd_ones) + EPS) * (GDN_DK ** -0.5)
        gqkv_ref[:, GDN_W:2 * GDN_W] = gk * lax.rsqrt(_head_sums(gk * gk, bd_ones) + EPS)

    if not with_matmuls:
        l2norm_qk()
        return
    h1 = [v + 0.5 * _dot(jnp.concatenate(a, axis=1), wd_ref[...]) for v, a in zip(x, acts)]
    for r, v in zip(rows, h1):
        h1_ref[r] = v

    un = [_rms(v, nm_ref[...]).astype(BF16) for v in h1]
    proj = [_dot_nt(v, win_ref[...]) for v in un]

    l2norm_qk()
    xe_ref[0:8] = jnp.where(step % tiles_per_seq == 0, c0_ref[...], xe_ref[tm:tm + 8])
    o_raw = MLA_Q_RANK + MLA_KV_RANK + LANES
    for r, p in zip(rows, proj):
        xe_ref[8 + r.start:8 + r.stop] = p[:, o_raw:o_raw + 3 * GDN_W]
        gba_ref[r] = p[:, o_raw + 3 * GDN_W:o_raw + 3 * GDN_W + LANES]
        z_ref[r] = p[:, o_raw + 3 * GDN_W + LANES:]
    tail_ref[...] = proj[-1][cr - 8:, o_raw:o_raw + 3 * GDN_W]

    cqn = [_rms(p[:, :MLA_Q_RANK], qn_ref[...]).astype(BF16) for p in proj]
    qa = [_dot(v, wqa_ref[...]) for v in cqn]
    qb = [_dot(v, wqb_ref[...]) for v in cqn]
    for r, a, b in zip(rows, qa, qb):
        tc = jnp.concatenate([tc_ref[r]] * MLA_HEADS, axis=1)
        ts = jnp.concatenate([ts_ref[r]] * MLA_HEADS, axis=1)
        q_ref[r] = (a * tc + b * ts).astype(BF16)

    ckvn = [_rms(p[:, MLA_Q_RANK:MLA_Q_RANK + MLA_KV_RANK], kvn_ref[...]).astype(BF16) for p in proj]
    for c, (r, p, v) in enumerate(zip(rows, proj, ckvn)):
        kr = p[:, MLA_Q_RANK + MLA_KV_RANK:o_raw]
        kr_terms = jnp.concatenate([kr * tkc_ref[r], kr * tks_ref[r]], axis=1).astype(BF16)
        k_ref[r] = (_dot(v, wk_ref[...]) + _dot(kr_terms, e_ref[...])).astype(BF16)
        vt_ref[c] = _dot_nt(wvt_ref[...], v).astype(BF16)


def _token_proj(x2d, tabs, conv_carry, weights, tm, tiles_per_seq, chain_rows):
    n = x2d.shape[0]
    assert n % tm == 0
    last = n // tm - 1
    tile = lambda i: jnp.minimum(i, last)
    row = lambda w: pl.BlockSpec((tm, w), lambda i: (tile(i), 0))
    tab = pl.BlockSpec((tm, LANES), lambda i: (tile(i) % tiles_per_seq, 0))
    out_widths = (D_MODEL, QK_W, QK_W, None, 3 * GDN_W, LANES, GDN_W)
    out_dtypes = (F32, BF16, BF16, BF16, F32, F32, F32)
    assert tm % chain_rows == 0
    vt_spec = pl.BlockSpec((tm // chain_rows, V_W, chain_rows), lambda i: (tile(i), 0, 0))
    conv_spec = pl.BlockSpec((tm, 3 * GDN_W), lambda i: (jnp.maximum(i - 1, 0), 0))
    tail_shape = (8, 3 * GDN_W)
    out_specs = [vt_spec if w is None else row(w) for w in out_widths]
    out_specs[4] = conv_spec
    out_specs.append(pl.BlockSpec(tail_shape, lambda i: (0, 0)))
    out_shape = [jax.ShapeDtypeStruct((n // chain_rows, V_W, chain_rows) if w is None else (n, w), d)
                 for w, d in zip(out_widths, out_dtypes)] + [jax.ShapeDtypeStruct(tail_shape, F32)]

    regroup = any(w.shape == (D_IN, D_MODEL) for w in weights)
    operands, w_specs = [], []
    for w in weights:
        if w.shape == (D_IN, D_MODEL):
            assert 2 * W_GATE_START >= D_IN
            operands += [w, w]
            w_specs += [_const_spec((W_TOKEN_COLS, D_MODEL)),
                        pl.BlockSpec((W_GATE_START, D_MODEL), lambda i: (1, 0), pipeline_mode=pl.Buffered(1))]
        else:
            operands.append(w)
            w_specs.append(_const_spec(w.shape))
    if regroup:
        for shape in ((PROJ_W, D_MODEL), (2 * D_MODEL, D_MODEL)):
            out_specs.append(pl.BlockSpec(shape, lambda i: (0, 0)))
            out_shape.append(jax.ShapeDtypeStruct(shape, BF16))
    return pl.pallas_call(
        functools.partial(_token_proj_kernel, tiles_per_seq=tiles_per_seq, regroup=regroup),
        grid=(n // tm + 1,),
        in_specs=[row(D_MODEL)] + [tab] * len(tabs) + [_const_spec(tail_shape)] + w_specs,
        out_specs=out_specs,
        out_shape=out_shape,
        scratch_shapes=[pltpu.VMEM((8 + tm, 3 * GDN_W), F32)],
        compiler_params=pltpu.CompilerParams(dimension_semantics=("arbitrary",),
                                             vmem_limit_bytes=VMEM_LIMIT),
        name="token_proj",
    )(x2d, *tabs, conv_carry, *operands)


def _mla_kernel(q_ref, k_ref, vt_ref, km_ref, vmt_ref, o_ref, st_ref, *, tq, tk):
    def tile(qi, carry):
        rows = pl.ds(pl.multiple_of(qi * tq, tq), tq)
        _mla_tile(qi, q_ref.at[rows], k_ref, vt_ref, km_ref, vmt_ref, o_ref.at[rows], st_ref, tq=tq, tk=tk)
        return carry

    lax.fori_loop(0, q_ref.shape[0] // tq, tile, 0)


def _mla_tile(qi, q_ref, k_ref, vt_ref, km_ref, vmt_ref, o_ref, st_ref, *, tq, tk):
    heads = range(2)
    hs = [slice(h * HEAD_PAD, (h + 1) * HEAD_PAD) for h in heads]
    vs = [slice(h * MLA_V, (h + 1) * MLA_V) for h in heads]
    q = [q_ref[:, s] for s in hs]
    colmax = lambda s: jnp.max(s, axis=0, keepdims=True)

    def with_ones(vt):
        return jnp.concatenate([vt, jnp.ones((8, vt.shape[1]), BF16)], axis=0)

    def scores(ki, slot, q_from=0):
        rows = pl.ds(pl.multiple_of(ki * tk, tk), tk)
        block_max = []
        for h in heads:
            s = _dot_nt(k_ref[rows, hs[h]], q[h][q_from:])
            st_ref[slot, h, :, q_from:] = s
            block_max.append(colmax(s))
        return block_max

    def update(ki, slot, m, acc, block_max, mask, q_from=0):
        st = [st_ref[slot, h, :, q_from:] for h in heads]
        if mask is not None:
            st = [jnp.where(mask, s, NEG_BIG) for s in st]
            block_max = [colmax(s) for s in st]
        m_old = [x[:, q_from:] for x in m]
        m_new = [jnp.maximum(m_old[h], block_max[h]) for h in heads]
        p = [jnp.exp2(st[h] - m_new[h]).astype(BF16) for h in heads]
        vt = vt_ref[ki]
        acc_new = [jnp.exp2(m_old[h] - m_new[h]) * acc[h][:, q_from:] + _dot(with_ones(vt[vs[h]]), p[h])
                   for h in heads]
        if q_from:
            m_new = [jnp.concatenate([m[h][:, :q_from], m_new[h]], axis=1) for h in heads]
            acc_new = [jnp.concatenate([acc[h][:, :q_from], acc_new[h]], axis=1) for h in heads]
        return m_new, acc_new

    bm_a = scores(0, 0)
    bm_b = scores(1, 1)
    meta_valid = lax.broadcasted_iota(jnp.int32, (km_ref.shape[0], tq), 0) < N_META
    st = [jnp.where(meta_valid, _dot_nt(km_ref[:, hs[h]], q[h]), NEG_BIG) for h in heads]
    m = [colmax(s) for s in st]
    acc = [_dot(with_ones(vmt_ref[vs[h], :]), jnp.exp2(st[h] - m[h]).astype(BF16)) for h in heads]

    def stage(ka, carry, cur, nxt):
        m, acc, bm_a, bm_b = carry
        bm_a2 = scores(ka + 2, nxt[0])
        m, acc = update(ka, cur[0], m, acc, bm_a, None)
        bm_b2 = scores(ka + 3, nxt[1])
        m, acc = update(ka + 1, cur[1], m, acc, bm_b, None)
        return m, acc, bm_a2, bm_b2

    def body(i, carry):
        return stage(4 * i + 2, stage(4 * i, carry, (0, 1), (2, 3)), (2, 3), (0, 1))

    causal = (lax.broadcasted_iota(jnp.int32, (tk, tq), 1) >= lax.broadcasted_iota(jnp.int32, (tk, tq), 0))
    n_full = 2 * qi

    def finish(carry, cur):
        m, acc, _, _ = carry
        m, acc = update(n_full, cur[0], m, acc, None, causal)
        m, acc = update(n_full + 1, cur[1], m, acc, None, causal[:, :tk], q_from=tk)
        return acc

    carry = lax.fori_loop(0, qi // 2, body, (m, acc, bm_a, bm_b))
    acc = lax.cond(qi % 2 == 1,
                   lambda c: finish(stage(n_full - 2, c, (0, 1), (2, 3)), (2, 3)),
                   lambda c: finish(c, (0, 1)), carry)
    o_ref[...] = jnp.concatenate([(a[:MLA_V] * (1.0 / a[MLA_V:MLA_V + 1])).T for a in acc],
                                 axis=1).astype(BF16)


def _mla_attn(q, k, vt, k_meta, vt_meta, batch, seq, tq, tk):
    nq = seq // tq
    assert vt.shape[2] == tk and tq == 2 * tk
    kern = functools.partial(_mla_kernel, tq=tq, tk=tk)
    return pl.pallas_call(
        kern,
        grid=(batch, MLA_HEADS // 2),
        in_specs=[
            pl.BlockSpec((seq, 2 * HEAD_PAD), lambda b, hp: (b, hp)),
            pl.BlockSpec((seq, 2 * HEAD_PAD), lambda b, hp: (b, hp)),
            pl.BlockSpec((seq // tk, 2 * MLA_V, tk), lambda b, hp: (b, hp, 0)),
            pl.BlockSpec((k_meta.shape[0], 2 * HEAD_PAD), lambda b, hp: (0, hp)),
            pl.BlockSpec((2 * MLA_V, vt_meta.shape[1]), lambda b, hp: (hp, 0)),
        ],
        out_specs=pl.BlockSpec((seq, 2 * MLA_V), lambda b, hp: (b, hp)),
        out_shape=jax.ShapeDtypeStruct((batch * seq, V_W), BF16),
        scratch_shapes=[pltpu.VMEM((4, 2, tk, tq), F32)],
        compiler_params=pltpu.CompilerParams(
            dimension_semantics=("arbitrary", "arbitrary"),
            vmem_limit_bytes=VMEM_LIMIT),
        name="mla_attn",
    )(q, k, vt, k_meta, vt_meta)


LEVELS = (1, 2, 4, 8, 16, 32)
GDN_CONST_NAMES = ("expand_b", "expand_a", "ltri", "eye_t", "tril_t", "stril_t", "level_masks", "bd_ones")


def _gdn_constants(cps):
    i = np.arange(CHUNK)[:, None]
    lane = np.arange(GDN_W)[None, :]
    j = lane % GDN_DK
    c = {}
    r = np.arange(LANES)[:, None]
    c["expand_b"] = (r == lane // GDN_DK)
    c["expand_a"] = (r == GDN_HEADS + lane // GDN_DK)
    t = np.arange(cps * CHUNK)
    c["ltri"] = (t[:, None] >= t[None, :]) & (t[:, None] // CHUNK == t[None, :] // CHUNK)
    c["eye_t"] = (i == j)
    c["tril_t"] = (i >= j)
    c["stril_t"] = (i > j)
    jg = j[:, :GROUP_W]
    c["level_masks"] = np.stack([
        ((i // (2 * s) == jg // (2 * s)) & ((i // s) % 2 == 1) & ((jg // s) % 2 == 0))
        for s in LEVELS])
    g = np.arange(GROUP_W)
    c["bd_ones"] = (g[:, None] // GDN_DK == g[None, :] // GDN_DK)
    bf = ("bd_ones", "expand_b", "expand_a", "ltri")
    return [jnp.asarray(c[k].astype(np.float32), BF16 if k in bf else F32) for k in GDN_CONST_NAMES]


def _block_diag(y, lo_half):
    zeros = jnp.zeros((GDN_DK, LANES), y.dtype)
    blocks = []
    for h in range(GROUP_HEADS):
        t = h // 2
        tile = y[:, t * LANES:(t + 1) * LANES]
        piece = jnp.where(lo_half, tile, 0) if h % 2 == 0 else jnp.where(lo_half, 0, tile)
        blocks.append(jnp.concatenate([piece, zeros] if t == 0 else [zeros, piece], axis=1))
    return jnp.concatenate(blocks, axis=0)


def _head_matmul(x, y, lo_half):
    return _dot(x.astype(BF16), _block_diag(y.astype(BF16), lo_half))


def _diag_blocks(m, lo_half):
    tiles = []
    for t in range(m.shape[1] // LANES):
        h = 2 * (t % 2)
        cols = slice(t * LANES, (t + 1) * LANES)
        tiles.append(jnp.where(lo_half, m[h * GDN_DK:(h + 1) * GDN_DK, cols],
                               m[(h + 1) * GDN_DK:(h + 2) * GDN_DK, cols]))
    return jnp.concatenate(tiles, axis=1)


def _gdn_kernel(gx_ref, ba_ref, z_ref, s0_ref, arate_ref, dtb_ref, gnorm_ref,
                expand_b_ref, expand_a_ref, ltri_ref, eye_t_ref, tril_t_ref, stril_t_ref,
                level_masks_ref, bd_ones_ref,
                o_ref, sout_ref,
                state_ref, kn_ref, kbq_ref, vbk_ref, kdec_ref, qg_ref, decay_ref, sdec1_ref,
                lhs_ref, amat_ref, u_ref, ku_ref, sdec2_ref, *, cps, front_pad, blocks_per_seq):
    s = pl.program_id(0)
    rows = cps * CHUNK
    stage1 = (kn_ref, kbq_ref, vbk_ref, kdec_ref, qg_ref, decay_ref, sdec1_ref)
    stage2 = (lhs_ref, amat_ref, u_ref, ku_ref, sdec2_ref)

    @pl.when(s == 0)
    def _():
        for r in stage1 + stage2 + (state_ref,):
            r[...] = jnp.zeros_like(r)

    kn_r, kbq_r, vbk_r, kdec_r, qg_r, decay_r, sdec1_r = (r.at[1 - s % 2] for r in stage1)
    kn_w, kbq_w, vbk_w, kdec_w, qg_w, decay_w, sdec1_w = (r.at[s % 2] for r in stage1)

    lo_half = lax.broadcasted_iota(jnp.int32, (1, LANES), 1) < GDN_DK
    eye, tril, stril = eye_t_ref[...], tril_t_ref[...], stril_t_ref[...]
    chains = [(j, gi) for j in range(cps) for gi in range(N_GROUPS)]
    rsl = lambda j: slice(j * CHUNK, (j + 1) * CHUNK)
    gsl = lambda gi: slice(gi * GROUP_W, (gi + 1) * GROUP_W)

    first_of_seq = (s - 2) % blocks_per_seq == 0
    states = [jnp.where(first_of_seq, s0_ref[gi], state_ref[gi]) for gi in range(N_GROUPS)]
    o_rows = [[None] * N_GROUPS for _ in range(cps)]
    in_flight = {}

    def ride(i):
        if i - 1 in in_flight:
            for gi, big in enumerate(in_flight.pop(i - 1)):
                v_new = u_ref[i - 1, gi] - big[CHUNK:2 * CHUNK]
                o_rows[i - 1][gi] = big[2 * CHUNK:] + _dot(amat_ref[i - 1, gi],
                                                           _block_diag(v_new.astype(BF16), lo_half))
        if i < cps:
            bigs = []
            for gi in range(N_GROUPS):
                state = states[gi]
                big = _dot(lhs_ref[i, gi], _block_diag(state.astype(BF16), lo_half))
                states[gi] = sdec2_ref[i, gi][0:1] * state - big[:CHUNK] + ku_ref[i, gi]
                bigs.append(big)
            in_flight[i] = bigs

    ba = ba_ref[...]
    beta = jax.nn.sigmoid(ba)
    sp_in = ba + dtb_ref[...]
    g = arate_ref[...] * (jnp.maximum(sp_in, 0.0) + jnp.log1p(jnp.exp(-jnp.abs(sp_in))))
    if front_pad:
        valid = lax.broadcasted_iota(jnp.int32, (rows, 1), 0) >= front_pad
        beta = jnp.where(valid, beta, 0.0)
        g = jnp.where(valid, g, 0.0)
    beta_e = _dot_exact_rhs(beta, expand_b_ref[...], 2)
    gc = _dot_exact_lhs(ltri_ref[...], g, 3)
    ride(0)

    lmat, amat = [], []
    for j, gi in chains:
        rs, gs = rsl(j), gsl(gi)
        kstack = _block_diag(kn_r[rs, gs], lo_half)
        sc = _dot_nt(kbq_r[j, gi], kstack)
        lmat.append(sc[:CHUNK] * decay_r[rs, gs] * stril[:, gs])
        amat.append(sc[CHUNK:] * decay_r[rs, gs])

    gc_e = _dot_exact_rhs(gc, expand_a_ref[...], 3)
    ride(1)

    def level(xinv, li):
        xo = [_head_matmul(x, lm * level_masks_ref[li], lo_half) for x, lm in zip(xinv, lmat)]
        return [x - _head_matmul(y, x, lo_half) for x, y in zip(xinv, xo)]

    xinv = [eye[:, gsl(gi)] - lm * level_masks_ref[0] for (j, gi), lm in zip(chains, lmat)]
    xinv = level(xinv, 1)

    qn = gx_ref[:, :GDN_W]
    kn = gx_ref[:, GDN_W:2 * GDN_W]
    v = gx_ref[:, 2 * GDN_W:]
    egc = jnp.exp(gc_e)
    kb = kn * beta_e
    vb = v * beta_e
    kbg = kb * egc
    qg = qn * egc
    kn_w[...] = kn.astype(BF16)
    qg_w[...] = qg
    for j, gi in chains:
        rs, gs = rsl(j), gsl(gi)
        kbq_w[j, gi] = jnp.concatenate([kb[rs, gs], qn[rs, gs]], axis=0).astype(BF16)
        vbk_w[j, gi] = jnp.concatenate([_block_diag(vb[rs, gs].astype(BF16), lo_half),
                                        _block_diag(kbg[rs, gs].astype(BF16), lo_half)], axis=1)
    ride(2)
    xinv = level(xinv, 2)

    decay, kdec, s_decay = [], [], []
    for j in range(cps):
        gce = gc_e[rsl(j)]
        gc_t = jnp.sum(gce * eye, axis=0, keepdims=True)
        decay.append(jnp.exp(jnp.where(tril > 0.5, gce - gc_t, -jnp.inf)))
        g_last = gce[CHUNK - 1:CHUNK]
        kdec.append(kn[rsl(j)] * jnp.exp(g_last - gce))
        s_decay.append(jnp.exp(g_last))
        decay_w[rsl(j)] = decay[j]
        kdec_w[rsl(j)] = kdec[j]
        for gi in range(N_GROUPS):
            sdec1_w[j, gi] = jnp.broadcast_to(s_decay[j][:, gsl(gi)], (8, GROUP_W))
    ride(3)
    xinv = level(xinv, 3)
    for i in range(4, cps + 1):
        ride(i)
    ride(cps)

    for gi in range(N_GROUPS):
        state_ref[gi] = states[gi]

    @pl.when(s == pl.num_programs(0) - 1)
    def _():
        sout_ref[...] = state_ref[...]

    o = jnp.concatenate([jnp.concatenate(r, axis=1) for r in o_rows], axis=0)
    ms = _head_sums(o * o, bd_ones_ref[...]) * (1.0 / GDN_DV)
    o = o * lax.rsqrt(ms + EPS) * gnorm_ref[...] * _silu(z_ref[...])
    o_ref[...] = o.astype(BF16)

    for li in range(4, len(LEVELS)):
        xinv = level(xinv, li)

    uw = [_dot(x.astype(BF16), vbk_r[j, gi]) for (j, gi), x in zip(chains, xinv)]
    kwu = [_diag_blocks(_dot(kdec_r[rsl(j), gsl(gi)].T.astype(BF16), m.astype(BF16)), lo_half)
           for (j, gi), m in zip(chains, uw)]

    for ci, (j, gi) in enumerate(chains):
        rs, gs = rsl(j), gsl(gi)
        u, wmat = uw[ci][:, :GROUP_W], uw[ci][:, GROUP_W:]
        ku, kw = kwu[ci][:, :GROUP_W], kwu[ci][:, GROUP_W:]
        lhs_ref[j, gi] = jnp.concatenate([kw, wmat, qg_r[rs, gs]], axis=0).astype(BF16)
        amat_ref[j, gi] = amat[ci].astype(BF16)
        u_ref[j, gi] = u
        ku_ref[j, gi] = ku
        sdec2_ref[j, gi] = sdec1_r[j, gi]


def _gdn(gqkv, gba, z, state0, params, batch, seq, cps, front_pad):
    rows = cps * CHUNK
    blocks_per_seq = seq // rows
    n_blocks = batch * blocks_per_seq
    consts = _gdn_constants(cps)
    cur = lambda w: pl.BlockSpec((rows, w), lambda s: (jnp.minimum(s, n_blocks - 1), 0))
    done = lambda w: pl.BlockSpec((rows, w), lambda s: (jnp.maximum(s - 2, 0), 0))
    state_shape = (N_GROUPS, GDN_DK, GROUP_W)
    per_chain = lambda r, w, dt: pltpu.VMEM((cps, N_GROUPS, r, w), dt)
    per_row = lambda dt: pltpu.VMEM((rows, GDN_W), dt)
    two = lambda v: pltpu.VMEM((2,) + tuple(v.shape), v.dtype)
    return pl.pallas_call(
        functools.partial(_gdn_kernel, cps=cps, front_pad=front_pad, blocks_per_seq=blocks_per_seq),
        grid=(n_blocks + 2,),
        in_specs=[cur(3 * GDN_W), cur(LANES), done(GDN_W), _const_spec(state_shape)]
                 + [_const_spec(a.shape) for a in params]
                 + [_const_spec(a.shape) for a in consts],
        out_specs=[done(GDN_W), pl.BlockSpec(state_shape, lambda s: (0, 0, 0))],
        out_shape=[jax.ShapeDtypeStruct((batch * seq, GDN_W), BF16),
                   jax.ShapeDtypeStruct(state_shape, F32)],
        scratch_shapes=[pltpu.VMEM(state_shape, F32),
                        two(per_row(BF16)),
                        two(per_chain(2 * CHUNK, GROUP_W, BF16)),
                        two(per_chain(GROUP_W, 2 * GROUP_W, BF16)),
                        two(per_row(F32)),
                        two(per_row(F32)),
                        two(per_row(F32)),
                        two(per_chain(8, GROUP_W, F32)),
                        per_chain(3 * CHUNK, GROUP_W, BF16),
                        per_chain(CHUNK, GROUP_W, BF16),
                        per_chain(CHUNK, GROUP_W, F32),
                        per_chain(CHUNK, GROUP_W, F32),
                        per_chain(8, GROUP_W, F32)],
        compiler_params=pltpu.CompilerParams(dimension_semantics=("arbitrary",),
                                             vmem_limit_bytes=VMEM_LIMIT),
        name="gdn_chunk",
    )(gqkv, gba, z, state0, *params, *consts)


def _merge_ffn_kernel(h1_ref, om_ref, og_ref, nm_ref, wgate_ref, wmo_ref, wgo_ref, wout_ref,
                      n2_ref, wg_ref, wu_ref, wd_ref, nf_ref, out_ref):
    n_chains = max(1, h1_ref.shape[0] // MERGE_CHAIN_ROWS)
    rows = [slice(c * MERGE_CHAIN_ROWS, (c + 1) * MERGE_CHAIN_ROWS) if n_chains > 1 else slice(None)
            for c in range(n_chains)]
    h1 = [h1_ref[r] for r in rows]
    un = [_rms(h, nm_ref[...]).astype(BF16) for h in h1]
    gates = [jax.nn.sigmoid(_dot_nt(u, wgate_ref[...])) for u in un]
    merged = [(g[:, :D_MODEL] * _dot(om_ref[r], wmo_ref[...])
               + g[:, D_MODEL:] * _dot(og_ref[r], wgo_ref[...])).astype(BF16) for g, r in zip(gates, rows)]
    h2 = [h + _dot(m, wout_ref[...]) for h, m in zip(h1, merged)]
    xn = [_rms(h, n2_ref[...]).astype(BF16) for h in h2]
    acts = [[] for _ in rows]
    for c in range(D_FF // FFN_CHUNK):
        cols = slice(c * FFN_CHUNK, (c + 1) * FFN_CHUNK)
        for a, x in zip(acts, xn):
            a.append((_silu(_dot(x, wg_ref[:, cols])) * _dot(x, wu_ref[:, cols])).astype(BF16))
    h3 = [h + 0.5 * _dot(jnp.concatenate(a, axis=1), wd_ref[...]) for h, a in zip(h2, acts)]
    for r, h in zip(rows, h3):
        out_ref[r] = _rms(h, nf_ref[...])


def _merge_ffn(h1, o_mla, o_gdn, weights, tm):
    n = h1.shape[0]
    row = lambda w: pl.BlockSpec((tm, w), lambda i: (i, 0))
    return pl.pallas_call(
        _merge_ffn_kernel,
        grid=(n // tm,),
        in_specs=[row(D_MODEL), row(V_W), row(GDN_W)] + [_const_spec(w.shape) for w in weights],
        out_specs=row(D_MODEL),
        out_shape=jax.ShapeDtypeStruct((n, D_MODEL), F32),
        compiler_params=pltpu.CompilerParams(dimension_semantics=("arbitrary",),
                                             vmem_limit_bytes=VMEM_LIMIT),
        name="merge_ffn",
    )(h1, o_mla, o_gdn, *weights)


def _rope_tables(first_pos, n):
    pos = np.arange(first_pos, first_pos + n, dtype=np.float64)
    inv = ROPE_THETA ** (-np.arange(0, MLA_ROPE, 2, dtype=np.float64) / MLA_ROPE)
    ang = pos[:, None] * inv[None, :]
    cos2 = np.tile(np.cos(ang), (1, 2))
    sin2 = np.tile(np.sin(ang), (1, 2))
    scale = (MLA_NOPE + MLA_ROPE) ** -0.5 * math.log2(math.e)
    pad = np.zeros((n, HEAD_PAD - MLA_NOPE - MLA_ROPE))
    tab_c = np.concatenate([np.ones((n, MLA_NOPE)), cos2, pad], axis=1) * scale
    tab_s = np.concatenate([np.zeros((n, MLA_NOPE)), sin2, pad], axis=1) * scale
    kpad = np.zeros((n, LANES - MLA_ROPE))
    tab_kc = np.concatenate([cos2, kpad], axis=1)
    tab_ks = np.concatenate([sin2, kpad], axis=1)
    return tuple(jnp.asarray(t, F32) for t in (tab_c, tab_s, tab_kc, tab_ks))


def _rot(w):
    half = MLA_ROPE // 2
    return jnp.concatenate([-w[..., half:], w[..., :half]], axis=-1)


def kernel(x, meta_tokens, ffn1_norm, ffn1_w_gate, ffn1_w_up, ffn1_w_down, mix_norm, w_in, q_norm, w_uq,
           kv_norm, w_ukv, w_mla_o, conv_w, a_log, dt_bias, gdn_norm, w_gdn_o, w_out, ffn2_norm,
           ffn2_w_gate, ffn2_w_up, ffn2_w_down, final_norm):
    assert ffn1_norm.shape[0] == 1, "single-layer block"
    batch, seq, d = x.shape
    assert d == D_MODEL and seq % CHUNK == 0
    tm = min(256, seq)
    tq = min(512, seq)
    assert seq % tm == 0 and seq % tq == 0

    assert w_in.shape[2] == D_IN
    w_in_raw = w_in[0].T

    wq = w_uq[0].reshape(MLA_Q_RANK, MLA_HEADS, MLA_NOPE + MLA_ROPE)
    wq_nope, wq_rope = wq[..., :MLA_NOPE], wq[..., MLA_NOPE:]
    zq = lambda n: jnp.zeros((MLA_Q_RANK, MLA_HEADS, n), F32)
    tail = HEAD_PAD - MLA_NOPE - MLA_ROPE
    wqa = jnp.concatenate([wq_nope, wq_rope, zq(tail)], axis=-1).reshape(MLA_Q_RANK, QK_W).astype(BF16)
    wqb = jnp.concatenate([zq(MLA_NOPE), _rot(wq_rope), zq(tail)], axis=-1).reshape(MLA_Q_RANK, QK_W).astype(BF16)
    wkv = w_ukv[0].reshape(MLA_KV_RANK, MLA_HEADS, MLA_NOPE + MLA_V)
    wk = jnp.concatenate([wkv[..., :MLA_NOPE], jnp.zeros((MLA_KV_RANK, MLA_HEADS, HEAD_PAD - MLA_NOPE), F32)],
                         axis=-1).reshape(MLA_KV_RANK, QK_W).astype(BF16)
    wvt = wkv[..., MLA_NOPE:].reshape(MLA_KV_RANK, V_W).T.astype(BF16)
    e_np = np.zeros((2 * LANES, QK_W), np.float32)
    half = MLA_ROPE // 2
    for h in range(MLA_HEADS):
        base = h * HEAD_PAD + MLA_NOPE
        for j in range(MLA_ROPE):
            e_np[j, base + j] = 1.0
        for j in range(half):
            e_np[LANES + half + j, base + j] = -1.0
            e_np[LANES + j, base + half + j] = 1.0
    e_mat = jnp.asarray(e_np, BF16)

    g = np.arange(GROUP_W)
    bd_ones = jnp.asarray((g[:, None] // GDN_DK == g[None, :] // GDN_DK).astype(np.float32), BF16)
    proj_weights = [
        ffn1_norm[0][None], ffn1_w_gate[0].astype(BF16), ffn1_w_up[0].astype(BF16),
        ffn1_w_down[0].astype(BF16), mix_norm[0][None], w_in_raw, q_norm[0][None], wqa, wqb,
        kv_norm[0][None], wk, wvt, e_mat, conv_w[0].astype(F32), bd_ones]

    zero_carry = jnp.zeros((8, 3 * GDN_W), F32)
    _, _, k_m, vt_m, gqkv_m, gba_m, _, meta_tail, win_a, wgate = _token_proj(
        meta_tokens.astype(F32), _rope_tables(0, N_META), zero_carry, proj_weights, N_META, 1, N_META)
    proj_weights = [win_a if w is w_in_raw else w for w in proj_weights]
    tp = 2 * tm if seq % (2 * tm) == 0 else tm
    h1, q, k, vt, gqkv, gba, z, _ = _token_proj(
        x.reshape(batch * seq, d), _rope_tables(N_META, seq), meta_tail, proj_weights, tp, seq // tp, tm)

    pad_rows = lambda a, n, front: jnp.pad(a, ((n - a.shape[0], 0) if front else (0, n - a.shape[0]), (0, 0)))
    o_mla = _mla_attn(q, k, vt, k_m, vt_m[0], batch, seq, tq, tm)

    hpad = lambda a: jnp.zeros((1, LANES), F32).at[0, GDN_HEADS:2 * GDN_HEADS].set(a)
    arate = hpad(-jnp.exp(a_log[0].astype(F32)))
    dtb = hpad(dt_bias[0].astype(F32))
    gnorm = jnp.tile(gdn_norm[0].astype(F32), GDN_HEADS)[None]
    gdn_params = (arate, dtb, gnorm)
    cps = 4 if seq % (4 * CHUNK) == 0 else 1
    _, state_meta = _gdn(pad_rows(gqkv_m, CHUNK, True), pad_rows(gba_m, CHUNK, True),
                         jnp.zeros((CHUNK, GDN_W), F32), jnp.zeros((N_GROUPS, GDN_DK, GROUP_W), F32),
                         gdn_params, 1, CHUNK, 1, CHUNK - N_META)
    o_gdn, _ = _gdn(gqkv, gba, z, state_meta, gdn_params, batch, seq, cps, 0)

    merge_weights = [
        mix_norm[0][None], wgate, w_mla_o[0].astype(BF16), w_gdn_o[0].astype(BF16), w_out[0].astype(BF16),
        ffn2_norm[0][None], ffn2_w_gate[0].astype(BF16), ffn2_w_up[0].astype(BF16),
        ffn2_w_down[0].astype(BF16), final_norm[None]]
    merge_rows = 2 * MERGE_CHAIN_ROWS if (batch * seq) % (2 * MERGE_CHAIN_ROWS) == 0 else tm
    out = _merge_ffn(h1, o_mla, o_gdn, merge_weights, merge_rows)
    return out.reshape(batch, seq, d)
```

```python
import functools
import math

import jax
import jax.numpy as jnp
import numpy as np
from jax import lax
from jax.experimental import pallas as pl
from jax.experimental.pallas import tpu as pltpu

F32 = jnp.float32
BF16 = jnp.bfloat16

D_MODEL = 1024
N_META = 16
EPS = 1e-6
D_FF = 2816
MLA_HEADS = 8
MLA_Q_RANK = 256
MLA_KV_RANK = 128
MLA_NOPE = 64
MLA_ROPE = 32
MLA_V = 64
ROPE_THETA = 10000.0
GDN_HEADS = 8
GDN_DK = 64
GDN_DV = 64
CONV_K = 4
CHUNK = 64

LANES = 128
HEAD_PAD = 128
QK_W = MLA_HEADS * HEAD_PAD
V_W = MLA_HEADS * MLA_V
GDN_W = GDN_HEADS * GDN_DK
GROUP_HEADS = 4
GROUP_W = GROUP_HEADS * GDN_DK
N_GROUPS = GDN_HEADS // GROUP_HEADS
FFN_CHUNK = 256
MERGE_CHAIN_ROWS = 256
PROJ_W = MLA_Q_RANK + MLA_KV_RANK + LANES + 3 * GDN_W + LANES + GDN_W
W_IN_SPLITS = (MLA_Q_RANK, MLA_KV_RANK, MLA_ROPE, GDN_W, GDN_W, GDN_W, GDN_HEADS, GDN_HEADS, GDN_W,
               D_MODEL, D_MODEL)
(OFF_CQ, OFF_CKV, OFF_KR, OFF_GQ, OFF_GK, OFF_GV, OFF_GB, OFF_GA, OFF_GZ, OFF_GATES, _OFF_GATE_GDN,
 D_IN) = (int(v) for v in np.concatenate([[0], np.cumsum(W_IN_SPLITS)]))
W_TOKEN_COLS = -(-OFF_GATES // LANES) * LANES
W_GATE_START = OFF_GATES // LANES * LANES
VMEM_LIMIT = 60 * 1024 * 1024
NEG_BIG = -1e30


def _const_spec(shape):
    zeros = (0,) * len(shape)
    return pl.BlockSpec(shape, lambda *_: zeros, pipeline_mode=pl.Buffered(1))


def _rms(x, w):
    return x * lax.rsqrt(jnp.mean(x * x, axis=-1, keepdims=True) + EPS) * w


def _dot(a, b):
    return jnp.dot(a, b, preferred_element_type=F32)


def _dot_nt(a, b):
    return lax.dot_general(a, b, (((1,), (1,)), ((), ())), preferred_element_type=F32)


def _silu(x):
    return x * jax.nn.sigmoid(x)


def _split(x, n):
    pieces = []
    for _ in range(n - 1):
        hi = x.astype(BF16)
        pieces.append(hi)
        x = x - hi.astype(F32)
    pieces.append(x.astype(BF16))
    return pieces


def _dot_exact_rhs(x, rhs, n):
    return sum(_dot(p, rhs) for p in _split(x, n))


def _dot_exact_lhs(lhs, x, n):
    return sum(_dot(lhs, p) for p in _split(x, n))


def _head_sums(x, bd_ones):
    return jnp.concatenate([_dot(x[:, g * GROUP_W:(g + 1) * GROUP_W].astype(BF16), bd_ones)
                            for g in range(N_GROUPS)], axis=1)


def _token_proj_kernel(*refs, tiles_per_seq, regroup):
    last = pl.num_programs(0) - 1

    @pl.when(pl.program_id(0) < last)
    def _():
        _token_proj_step(*refs, tiles_per_seq=tiles_per_seq, regroup=regroup, with_matmuls=True)

    @pl.when(pl.program_id(0) == last)
    def _():
        _token_proj_step(*refs, tiles_per_seq=tiles_per_seq, regroup=regroup, with_matmuls=False)


def _token_proj_step(*refs, tiles_per_seq, regroup, with_matmuls):
    it = iter(refs)
    take = lambda n: [next(it) for _ in range(n)]
    x_ref, tc_ref, ts_ref, tkc_ref, tks_ref, c0_ref, n1_ref, wg_ref, wu_ref, wd_ref, nm_ref = take(11)
    if regroup:
        wraw_ref, wrawg_ref = take(2)
    else:
        win_ref, = take(1)
    qn_ref, wqa_ref, wqb_ref, kvn_ref, wk_ref, wvt_ref, e_ref, convw_ref, bd_ones_ref = take(9)
    h1_ref, q_ref, k_ref, vt_ref, gqkv_ref, gba_ref, z_ref, tail_ref = take(8)
    if regroup:
        win_ref, wgate_ref = take(2)
    xe_ref, = take(1)
    tm = x_ref.shape[0]
    step = pl.program_id(0)

    @pl.when(step == 0)
    def _():
        xe_ref[...] = jnp.zeros_like(xe_ref)
        if regroup:
            o = OFF_GQ + LANES - MLA_ROPE
            win_ref[:, :o] = wraw_ref[:o].T.astype(BF16)
            win_ref[:, o:o + 3 * GDN_W] = wraw_ref[OFF_GQ:OFF_GB].T.astype(BF16)
            o += 3 * GDN_W
            row = lax.broadcasted_iota(jnp.int32, (LANES, 1), 0)
            ba_rows = jnp.where(row < 2 * GDN_HEADS, wraw_ref[OFF_GB:OFF_GB + LANES], 0)
            win_ref[:, o:o + LANES] = ba_rows.T.astype(BF16)
            win_ref[:, o + LANES:] = wraw_ref[OFF_GZ:OFF_GATES].T.astype(BF16)
            g0 = OFF_GATES - W_GATE_START
            wgate_ref[...] = wrawg_ref[g0:g0 + 2 * D_MODEL].T.astype(BF16)

    n_chains = vt_ref.shape[0]
    cr = tm // n_chains
    rows = [slice(c * cr, (c + 1) * cr) for c in range(n_chains)]
    x = [x_ref[r] for r in rows] if with_matmuls else []
    xn = [_rms(v, n1_ref[...]).astype(BF16) for v in x]

    n_ffn = D_FF // FFN_CHUNK
    n_conv = 3 * GDN_W // LANES
    acts = [[] for _ in rows]
    for c in range(max(n_ffn, n_conv)):
        if c < n_ffn and with_matmuls:
            cols = slice(c * FFN_CHUNK, (c + 1) * FFN_CHUNK)
            for a, v in zip(acts, xn):
                a.append((_silu(_dot(v, wg_ref[:, cols])) * _dot(v, wu_ref[:, cols])).astype(BF16))
        if c < n_conv:
            lanes = slice(c * LANES, (c + 1) * LANES)
            y = sum(convw_ref[t:t + 1, lanes] * xe_ref[pl.ds(8 - (CONV_K - 1) + t, tm), lanes]
                    for t in range(CONV_K))
            gqkv_ref[:, lanes] = _silu(y)

    def l2norm_qk():
        bd_ones = bd_ones_ref[...]
        gq = gqkv_ref[:, :GDN_W]
        gk = gqkv_ref[:, GDN_W:2 * GDN_W]
        gqkv_ref[:, :GDN_W] = gq * lax.rsqrt(_head_sums(gq * gq, bd_ones) + EPS) * (GDN_DK ** -0.5)
        gqkv_ref[:, GDN_W:2 * GDN_W] = gk * lax.rsqrt(_head_sums(gk * gk, bd_ones) + EPS)

    if not with_matmuls:
        l2norm_qk()
        return
    h1 = [v + 0.5 * _dot(jnp.concatenate(a, axis=1), wd_ref[...]) for v, a in zip(x, acts)]
    for r, v in zip(rows, h1):
        h1_ref[r] = v

    un = [_rms(v, nm_ref[...]).astype(BF16) for v in h1]
    proj = [_dot(v, win_ref[...]) for v in un]

    l2norm_qk()
    xe_ref[0:8] = jnp.where(step % tiles_per_seq == 0, c0_ref[...], xe_ref[tm:tm + 8])
    o_raw = MLA_Q_RANK + MLA_KV_RANK + LANES
    for r, p in zip(rows, proj):
        xe_ref[8 + r.start:8 + r.stop] = p[:, o_raw:o_raw + 3 * GDN_W]
        gba_ref[r] = p[:, o_raw + 3 * GDN_W:o_raw + 3 * GDN_W + LANES]
        z_ref[r] = p[:, o_raw + 3 * GDN_W + LANES:]
    tail_ref[...] = proj[-1][cr - 8:, o_raw:o_raw + 3 * GDN_W]

    cqn = [_rms(p[:, :MLA_Q_RANK], qn_ref[...]).astype(BF16) for p in proj]
    qa = [_dot(v, wqa_ref[...]) for v in cqn]
    qb = [_dot(v, wqb_ref[...]) for v in cqn]
    for r, a, b in zip(rows, qa, qb):
        tc = jnp.concatenate([tc_ref[r]] * MLA_HEADS, axis=1)
        ts = jnp.concatenate([ts_ref[r]] * MLA_HEADS, axis=1)
        q_ref[r] = (a * tc + b * ts).astype(BF16)

    ckvn = [_rms(p[:, MLA_Q_RANK:MLA_Q_RANK + MLA_KV_RANK], kvn_ref[...]).astype(BF16) for p in proj]
    for c, (r, p, v) in enumerate(zip(rows, proj, ckvn)):
        kr = p[:, MLA_Q_RANK + MLA_KV_RANK:o_raw]
        kr_terms = jnp.concatenate([kr * tkc_ref[r], kr * tks_ref[r]], axis=1).astype(BF16)
        k_ref[r] = (_dot(v, wk_ref[...]) + _dot(kr_terms, e_ref[...])).astype(BF16)
        vt_ref[c] = _dot_nt(wvt_ref[...], v).astype(BF16)


def _token_proj(x2d, tabs, conv_carry, weights, tm, tiles_per_seq, chain_rows):
    n = x2d.shape[0]
    assert n % tm == 0
    last = n // tm - 1
    tile = lambda i: jnp.minimum(i, last)
    row = lambda w: pl.BlockSpec((tm, w), lambda i: (tile(i), 0))
    tab = pl.BlockSpec((tm, LANES), lambda i: (tile(i) % tiles_per_seq, 0))
    out_widths = (D_MODEL, QK_W, QK_W, None, 3 * GDN_W, LANES, GDN_W)
    out_dtypes = (F32, BF16, BF16, BF16, F32, F32, F32)
    assert tm % chain_rows == 0
    vt_spec = pl.BlockSpec((tm // chain_rows, V_W, chain_rows), lambda i: (tile(i), 0, 0))
    conv_spec = pl.BlockSpec((tm, 3 * GDN_W), lambda i: (jnp.maximum(i - 1, 0), 0))
    tail_shape = (8, 3 * GDN_W)
    out_specs = [vt_spec if w is None else row(w) for w in out_widths]
    out_specs[4] = conv_spec
    out_specs.append(pl.BlockSpec(tail_shape, lambda i: (0, 0)))
    out_shape = [jax.ShapeDtypeStruct((n // chain_rows, V_W, chain_rows) if w is None else (n, w), d)
                 for w, d in zip(out_widths, out_dtypes)] + [jax.ShapeDtypeStruct(tail_shape, F32)]

    regroup = any(w.shape == (D_IN, D_MODEL) for w in weights)
    operands, w_specs = [], []
    for w in weights:
        if w.shape == (D_IN, D_MODEL):
            assert 2 * W_GATE_START >= D_IN
            operands += [w, w]
            w_specs += [_const_spec((W_TOKEN_COLS, D_MODEL)),
                        pl.BlockSpec((W_GATE_START, D_MODEL), lambda i: (1, 0), pipeline_mode=pl.Buffered(1))]
        else:
            operands.append(w)
            w_specs.append(_const_spec(w.shape))
    if regroup:
        for shape in ((D_MODEL, PROJ_W), (D_MODEL, 2 * D_MODEL)):
            out_specs.append(pl.BlockSpec(shape, lambda i: (0, 0)))
            out_shape.append(jax.ShapeDtypeStruct(shape, BF16))
    return pl.pallas_call(
        functools.partial(_token_proj_kernel, tiles_per_seq=tiles_per_seq, regroup=regroup),
        grid=(n // tm + 1,),
        in_specs=[row(D_MODEL)] + [tab] * len(tabs) + [_const_spec(tail_shape)] + w_specs,
        out_specs=out_specs,
        out_shape=out_shape,
        scratch_shapes=[pltpu.VMEM((8 + tm, 3 * GDN_W), F32)],
        compiler_params=pltpu.CompilerParams(dimension_semantics=("arbitrary",),
                                             vmem_limit_bytes=VMEM_LIMIT),
        name="token_proj",
    )(x2d, *tabs, conv_carry, *operands)


def _mla_kernel(q_ref, k_ref, vt_ref, km_ref, vmt_ref, o_ref, st_ref, *, tq, tk):
    def tile(qi, carry):
        rows = pl.ds(pl.multiple_of(qi * tq, tq), tq)
        _mla_tile(qi, q_ref.at[rows], k_ref, vt_ref, km_ref, vmt_ref, o_ref.at[rows], st_ref, tq=tq, tk=tk)
        return carry

    lax.fori_loop(0, q_ref.shape[0] // tq, tile, 0)


def _mla_tile(qi, q_ref, k_ref, vt_ref, km_ref, vmt_ref, o_ref, st_ref, *, tq, tk):
    heads = range(2)
    hs = [slice(h * HEAD_PAD, (h + 1) * HEAD_PAD) for h in heads]
    vs = [slice(h * MLA_V, (h + 1) * MLA_V) for h in heads]
    q = [q_ref[:, s] for s in hs]
    colmax = lambda s: jnp.max(s, axis=0, keepdims=True)

    def with_ones(vt):
        return jnp.concatenate([vt, jnp.ones((8, vt.shape[1]), BF16)], axis=0)

    def scores(ki, slot, q_from=0):
        rows = pl.ds(pl.multiple_of(ki * tk, tk), tk)
        block_max = []
        for h in heads:
            s = _dot_nt(k_ref[rows, hs[h]], q[h][q_from:])
            st_ref[slot, h, :, q_from:] = s
            block_max.append(colmax(s))
        return block_max

    def update(ki, slot, m, acc, block_max, mask, q_from=0):
        st = [st_ref[slot, h, :, q_from:] for h in heads]
        if mask is not None:
            st = [jnp.where(mask, s, NEG_BIG) for s in st]
            block_max = [colmax(s) for s in st]
        m_old = [x[:, q_from:] for x in m]
        m_new = [jnp.maximum(m_old[h], block_max[h]) for h in heads]
        p = [jnp.exp2(st[h] - m_new[h]).astype(BF16) for h in heads]
        vt = vt_ref[ki]
        acc_new = [jnp.exp2(m_old[h] - m_new[h]) * acc[h][:, q_from:] + _dot(with_ones(vt[vs[h]]), p[h])
                   for h in heads]
        if q_from:
            m_new = [jnp.concatenate([m[h][:, :q_from], m_new[h]], axis=1) for h in heads]
            acc_new = [jnp.concatenate([acc[h][:, :q_from], acc_new[h]], axis=1) for h in heads]
        return m_new, acc_new

    bm_a = scores(0, 0)
    bm_b = scores(1, 1)
    meta_valid = lax.broadcasted_iota(jnp.int32, (km_ref.shape[0], tq), 0) < N_META
    st = [jnp.where(meta_valid, _dot_nt(km_ref[:, hs[h]], q[h]), NEG_BIG) for h in heads]
    m = [colmax(s) for s in st]
    acc = [_dot(with_ones(vmt_ref[vs[h], :]), jnp.exp2(st[h] - m[h]).astype(BF16)) for h in heads]

    def stage(ka, carry, cur, nxt):
        m, acc, bm_a, bm_b = carry
        bm_a2 = scores(ka + 2, nxt[0])
        m, acc = update(ka, cur[0], m, acc, bm_a, None)
        bm_b2 = scores(ka + 3, nxt[1])
        m, acc = update(ka + 1, cur[1], m, acc, bm_b, None)
        return m, acc, bm_a2, bm_b2

    def body(i, carry):
        return stage(4 * i + 2, stage(4 * i, carry, (0, 1), (2, 3)), (2, 3), (0, 1))

    causal = (lax.broadcasted_iota(jnp.int32, (tk, tq), 1) >= lax.broadcasted_iota(jnp.int32, (tk, tq), 0))
    n_full = 2 * qi

    def finish(carry, cur):
        m, acc, _, _ = carry
        m, acc = update(n_full, cur[0], m, acc, None, causal)
        m, acc = update(n_full + 1, cur[1], m, acc, None, causal[:, :tk], q_from=tk)
        return acc

    carry = lax.fori_loop(0, qi // 2, body, (m, acc, bm_a, bm_b))
    acc = lax.cond(qi % 2 == 1,
                   lambda c: finish(stage(n_full - 2, c, (0, 1), (2, 3)), (2, 3)),
                   lambda c: finish(c, (0, 1)), carry)
    o_ref[...] = jnp.concatenate([(a[:MLA_V] * (1.0 / a[MLA_V:MLA_V + 1])).T for a in acc],
                                 axis=1).astype(BF16)


def _mla_attn(q, k, vt, k_meta, vt_meta, batch, seq, tq, tk):
    nq = seq // tq
    assert vt.shape[2] == tk and tq == 2 * tk
    kern = functools.partial(_mla_kernel, tq=tq, tk=tk)
    return pl.pallas_call(
        kern,
        grid=(batch, MLA_HEADS // 2),
        in_specs=[
            pl.BlockSpec((seq, 2 * HEAD_PAD), lambda b, hp: (b, hp)),
            pl.BlockSpec((seq, 2 * HEAD_PAD), lambda b, hp: (b, hp)),
            pl.BlockSpec((seq // tk, 2 * MLA_V, tk), lambda b, hp: (b, hp, 0)),
            pl.BlockSpec((k_meta.shape[0], 2 * HEAD_PAD), lambda b, hp: (0, hp)),
            pl.BlockSpec((2 * MLA_V, vt_meta.shape[1]), lambda b, hp: (hp, 0)),
        ],
        out_specs=pl.BlockSpec((seq, 2 * MLA_V), lambda b, hp: (b, hp)),
        out_shape=jax.ShapeDtypeStruct((batch * seq, V_W), BF16),
        scratch_shapes=[pltpu.VMEM((4, 2, tk, tq), F32)],
        compiler_params=pltpu.CompilerParams(
            dimension_semantics=("arbitrary", "arbitrary"),
            vmem_limit_bytes=VMEM_LIMIT),
        name="mla_attn",
    )(q, k, vt, k_meta, vt_meta)


LEVELS = (1, 2, 4, 8, 16, 32)
GDN_CONST_NAMES = ("expand_b", "expand_a", "ltri", "eye_t", "tril_t", "stril_t", "level_masks", "bd_ones")


def _gdn_constants(cps):
    i = np.arange(CHUNK)[:, None]
    lane = np.arange(GDN_W)[None, :]
    j = lane % GDN_DK
    c = {}
    r = np.arange(LANES)[:, None]
    c["expand_b"] = (r == lane // GDN_DK)
    c["expand_a"] = (r == GDN_HEADS + lane // GDN_DK)
    t = np.arange(cps * CHUNK)
    c["ltri"] = (t[:, None] >= t[None, :]) & (t[:, None] // CHUNK == t[None, :] // CHUNK)
    c["eye_t"] = (i == j)
    c["tril_t"] = (i >= j)
    c["stril_t"] = (i > j)
    jg = j[:, :GROUP_W]
    c["level_masks"] = np.stack([
        ((i // (2 * s) == jg // (2 * s)) & ((i // s) % 2 == 1) & ((jg // s) % 2 == 0))
        for s in LEVELS])
    g = np.arange(GROUP_W)
    c["bd_ones"] = (g[:, None] // GDN_DK == g[None, :] // GDN_DK)
    bf = ("bd_ones", "expand_b", "expand_a", "ltri")
    return [jnp.asarray(c[k].astype(np.float32), BF16 if k in bf else F32) for k in GDN_CONST_NAMES]


def _block_diag(y, lo_half):
    zeros = jnp.zeros((GDN_DK, LANES), y.dtype)
    blocks = []
    for h in range(GROUP_HEADS):
        t = h // 2
        tile = y[:, t * LANES:(t + 1) * LANES]
        piece = jnp.where(lo_half, tile, 0) if h % 2 == 0 else jnp.where(lo_half, 0, tile)
        blocks.append(jnp.concatenate([piece, zeros] if t == 0 else [zeros, piece], axis=1))
    return jnp.concatenate(blocks, axis=0)


def _head_matmul(x, y, lo_half):
    return _dot(x.astype(BF16), _block_diag(y.astype(BF16), lo_half))


def _diag_blocks(m, lo_half):
    tiles = []
    for t in range(m.shape[1] // LANES):
        h = 2 * (t % 2)
        cols = slice(t * LANES, (t + 1) * LANES)
        tiles.append(jnp.where(lo_half, m[h * GDN_DK:(h + 1) * GDN_DK, cols],
                               m[(h + 1) * GDN_DK:(h + 2) * GDN_DK, cols]))
    return jnp.concatenate(tiles, axis=1)


def _gdn_kernel(gx_ref, ba_ref, z_ref, s0_ref, arate_ref, dtb_ref, gnorm_ref,
                expand_b_ref, expand_a_ref, ltri_ref, eye_t_ref, tril_t_ref, stril_t_ref,
                level_masks_ref, bd_ones_ref,
                o_ref, sout_ref,
                state_ref, kn_ref, kbq_ref, vbk_ref, kdec_ref, qg_ref, decay_ref, sdec1_ref,
                lhs_ref, amat_ref, u_ref, ku_ref, sdec2_ref, *, cps, front_pad, blocks_per_seq):
    s = pl.program_id(0)
    rows = cps * CHUNK
    stage1 = (kn_ref, kbq_ref, vbk_ref, kdec_ref, qg_ref, decay_ref, sdec1_ref)
    stage2 = (lhs_ref, amat_ref, u_ref, ku_ref, sdec2_ref)

    @pl.when(s == 0)
    def _():
        for r in stage1 + stage2 + (state_ref,):
            r[...] = jnp.zeros_like(r)

    kn_r, kbq_r, vbk_r, kdec_r, qg_r, decay_r, sdec1_r = (r.at[1 - s % 2] for r in stage1)
    kn_w, kbq_w, vbk_w, kdec_w, qg_w, decay_w, sdec1_w = (r.at[s % 2] for r in stage1)

    lo_half = lax.broadcasted_iota(jnp.int32, (1, LANES), 1) < GDN_DK
    eye, tril, stril = eye_t_ref[...], tril_t_ref[...], stril_t_ref[...]
    chains = [(j, gi) for j in range(cps) for gi in range(N_GROUPS)]
    rsl = lambda j: slice(j * CHUNK, (j + 1) * CHUNK)
    gsl = lambda gi: slice(gi * GROUP_W, (gi + 1) * GROUP_W)

    first_of_seq = (s - 2) % blocks_per_seq == 0
    states = [jnp.where(first_of_seq, s0_ref[gi], state_ref[gi]) for gi in range(N_GROUPS)]
    o_rows = [[None] * N_GROUPS for _ in range(cps)]
    in_flight = {}

    def ride(i):
        if i - 1 in in_flight:
            for gi, big in enumerate(in_flight.pop(i - 1)):
                v_new = u_ref[i - 1, gi] - big[CHUNK:2 * CHUNK]
                o_rows[i - 1][gi] = big[2 * CHUNK:] + _dot(amat_ref[i - 1, gi],
                                                           _block_diag(v_new.astype(BF16), lo_half))
        if i < cps:
            bigs = []
            for gi in range(N_GROUPS):
                state = states[gi]
                big = _dot(lhs_ref[i, gi], _block_diag(state.astype(BF16), lo_half))
                states[gi] = sdec2_ref[i, gi][0:1] * state - big[:CHUNK] + ku_ref[i, gi]
                bigs.append(big)
            in_flight[i] = bigs

    ba = ba_ref[...]
    beta = jax.nn.sigmoid(ba)
    sp_in = ba + dtb_ref[...]
    g = arate_ref[...] * (jnp.maximum(sp_in, 0.0) + jnp.log1p(jnp.exp(-jnp.abs(sp_in))))
    if front_pad:
        valid = lax.broadcasted_iota(jnp.int32, (rows, 1), 0) >= front_pad
        beta = jnp.where(valid, beta, 0.0)
        g = jnp.where(valid, g, 0.0)
    beta_e = _dot_exact_rhs(beta, expand_b_ref[...], 2)
    gc = _dot_exact_lhs(ltri_ref[...], g, 3)
    ride(0)

    lmat, amat = [], []
    for j, gi in chains:
        rs, gs = rsl(j), gsl(gi)
        kstack = _block_diag(kn_r[rs, gs], lo_half)
        sc = _dot_nt(kbq_r[j, gi], kstack)
        lmat.append(sc[:CHUNK] * decay_r[rs, gs] * stril[:, gs])
        amat.append(sc[CHUNK:] * decay_r[rs, gs])

    gc_e = _dot_exact_rhs(gc, expand_a_ref[...], 3)
    ride(1)

    def level(xinv, li):
        xo = [_head_matmul(x, lm * level_masks_ref[li], lo_half) for x, lm in zip(xinv, lmat)]
        return [x - _head_matmul(y, x, lo_half) for x, y in zip(xinv, xo)]

    xinv = [eye[:, gsl(gi)] - lm * level_masks_ref[0] for (j, gi), lm in zip(chains, lmat)]
    xinv = level(xinv, 1)

    qn = gx_ref[:, :GDN_W]
    kn = gx_ref[:, GDN_W:2 * GDN_W]
    v = gx_ref[:, 2 * GDN_W:]
    egc = jnp.exp(gc_e)
    kb = kn * beta_e
    vb = v * beta_e
    kbg = kb * egc
    qg = qn * egc
    kn_w[...] = kn.astype(BF16)
    qg_w[...] = qg
    for j, gi in chains:
        rs, gs = rsl(j), gsl(gi)
        kbq_w[j, gi] = jnp.concatenate([kb[rs, gs], qn[rs, gs]], axis=0).astype(BF16)
        vbk_w[j, gi] = jnp.concatenate([_block_diag(vb[rs, gs].astype(BF16), lo_half),
                                        _block_diag(kbg[rs, gs].astype(BF16), lo_half)], axis=1)
    ride(2)
    xinv = level(xinv, 2)

    decay, kdec, s_decay = [], [], []
    for j in range(cps):
        gce = gc_e[rsl(j)]
        gc_t = jnp.sum(gce * eye, axis=0, keepdims=True)
        decay.append(jnp.exp(jnp.where(tril > 0.5, gce - gc_t, -jnp.inf)))
        g_last = gce[CHUNK - 1:CHUNK]
        kdec.append(kn[rsl(j)] * jnp.exp(g_last - gce))
        s_decay.append(jnp.exp(g_last))
        decay_w[rsl(j)] = decay[j]
        kdec_w[rsl(j)] = kdec[j]
        for gi in range(N_GROUPS):
            sdec1_w[j, gi] = jnp.broadcast_to(s_decay[j][:, gsl(gi)], (8, GROUP_W))
    ride(3)
    xinv = level(xinv, 3)
    for i in range(4, cps + 1):
        ride(i)
    ride(cps)

    for gi in range(N_GROUPS):
        state_ref[gi] = states[gi]

    @pl.when(s == pl.num_programs(0) - 1)
    def _():
        sout_ref[...] = state_ref[...]

    o = jnp.concatenate([jnp.concatenate(r, axis=1) for r in o_rows], axis=0)
    ms = _head_sums(o * o, bd_ones_ref[...]) * (1.0 / GDN_DV)
    o = o * lax.rsqrt(ms + EPS) * gnorm_ref[...] * _silu(z_ref[...])
    o_ref[...] = o.astype(BF16)

    for li in range(4, len(LEVELS)):
        xinv = level(xinv, li)

    uw = [_dot(x.astype(BF16), vbk_r[j, gi]) for (j, gi), x in zip(chains, xinv)]
    kwu = [_diag_blocks(_dot(kdec_r[rsl(j), gsl(gi)].T.astype(BF16), m.astype(BF16)), lo_half)
           for (j, gi), m in zip(chains, uw)]

    for ci, (j, gi) in enumerate(chains):
        rs, gs = rsl(j), gsl(gi)
        u, wmat = uw[ci][:, :GROUP_W], uw[ci][:, GROUP_W:]
        ku, kw = kwu[ci][:, :GROUP_W], kwu[ci][:, GROUP_W:]
        lhs_ref[j, gi] = jnp.concatenate([kw, wmat, qg_r[rs, gs]], axis=0).astype(BF16)
        amat_ref[j, gi] = amat[ci].astype(BF16)
        u_ref[j, gi] = u
        ku_ref[j, gi] = ku
        sdec2_ref[j, gi] = sdec1_r[j, gi]


def _gdn(gqkv, gba, z, state0, params, batch, seq, cps, front_pad):
    rows = cps * CHUNK
    blocks_per_seq = seq // rows
    n_blocks = batch * blocks_per_seq
    consts = _gdn_constants(cps)
    cur = lambda w: pl.BlockSpec((rows, w), lambda s: (jnp.minimum(s, n_blocks - 1), 0))
    done = lambda w: pl.BlockSpec((rows, w), lambda s: (jnp.maximum(s - 2, 0), 0))
    state_shape = (N_GROUPS, GDN_DK, GROUP_W)
    per_chain = lambda r, w, dt: pltpu.VMEM((cps, N_GROUPS, r, w), dt)
    per_row = lambda dt: pltpu.VMEM((rows, GDN_W), dt)
    two = lambda v: pltpu.VMEM((2,) + tuple(v.shape), v.dtype)
    return pl.pallas_call(
        functools.partial(_gdn_kernel, cps=cps, front_pad=front_pad, blocks_per_seq=blocks_per_seq),
        grid=(n_blocks + 2,),
        in_specs=[cur(3 * GDN_W), cur(LANES), done(GDN_W), _const_spec(state_shape)]
                 + [_const_spec(a.shape) for a in params]
                 + [_const_spec(a.shape) for a in consts],
        out_specs=[done(GDN_W), pl.BlockSpec(state_shape, lambda s: (0, 0, 0))],
        out_shape=[jax.ShapeDtypeStruct((batch * seq, GDN_W), BF16),
                   jax.ShapeDtypeStruct(state_shape, F32)],
        scratch_shapes=[pltpu.VMEM(state_shape, F32),
                        two(per_row(BF16)),
                        two(per_chain(2 * CHUNK, GROUP_W, BF16)),
                        two(per_chain(GROUP_W, 2 * GROUP_W, BF16)),
                        two(per_row(F32)),
                        two(per_row(F32)),
                        two(per_row(F32)),
                        two(per_chain(8, GROUP_W, F32)),
                        per_chain(3 * CHUNK, GROUP_W, BF16),
                        per_chain(CHUNK, GROUP_W, BF16),
                        per_chain(CHUNK, GROUP_W, F32),
                        per_chain(CHUNK, GROUP_W, F32),
                        per_chain(8, GROUP_W, F32)],
        compiler_params=pltpu.CompilerParams(dimension_semantics=("arbitrary",),
                                             vmem_limit_bytes=VMEM_LIMIT),
        name="gdn_chunk",
    )(gqkv, gba, z, state0, *params, *consts)


def _merge_ffn_kernel(h1_ref, om_ref, og_ref, nm_ref, wgate_ref, wmo_ref, wgo_ref, wout_ref,
                      n2_ref, wg_ref, wu_ref, wd_ref, nf_ref, out_ref):
    n_chains = max(1, h1_ref.shape[0] // MERGE_CHAIN_ROWS)
    rows = [slice(c * MERGE_CHAIN_ROWS, (c + 1) * MERGE_CHAIN_ROWS) if n_chains > 1 else slice(None)
            for c in range(n_chains)]
    h1 = [h1_ref[r] for r in rows]
    un = [_rms(h, nm_ref[...]).astype(BF16) for h in h1]
    gates = [jax.nn.sigmoid(_dot(u, wgate_ref[...])) for u in un]
    merged = [(g[:, :D_MODEL] * _dot(om_ref[r], wmo_ref[...])
               + g[:, D_MODEL:] * _dot(og_ref[r], wgo_ref[...])).astype(BF16) for g, r in zip(gates, rows)]
    h2 = [h + _dot(m, wout_ref[...]) for h, m in zip(h1, merged)]
    xn = [_rms(h, n2_ref[...]).astype(BF16) for h in h2]
    acts = [[] for _ in rows]
    for c in range(D_FF // FFN_CHUNK):
        cols = slice(c * FFN_CHUNK, (c + 1) * FFN_CHUNK)
        for a, x in zip(acts, xn):
            a.append((_silu(_dot(x, wg_ref[:, cols])) * _dot(x, wu_ref[:, cols])).astype(BF16))
    h3 = [h + 0.5 * _dot(jnp.concatenate(a, axis=1), wd_ref[...]) for h, a in zip(h2, acts)]
    for r, h in zip(rows, h3):
        out_ref[r] = _rms(h, nf_ref[...])


def _merge_ffn(h1, o_mla, o_gdn, weights, tm):
    n = h1.shape[0]
    row = lambda w: pl.BlockSpec((tm, w), lambda i: (i, 0))
    return pl.pallas_call(
        _merge_ffn_kernel,
        grid=(n // tm,),
        in_specs=[row(D_MODEL), row(V_W), row(GDN_W)] + [_const_spec(w.shape) for w in weights],
        out_specs=row(D_MODEL),
        out_shape=jax.ShapeDtypeStruct((n, D_MODEL), F32),
        compiler_params=pltpu.CompilerParams(dimension_semantics=("arbitrary",),
                                             vmem_limit_bytes=VMEM_LIMIT),
        name="merge_ffn",
    )(h1, o_mla, o_gdn, *weights)


def _rope_tables(first_pos, n):
    pos = np.arange(first_pos, first_pos + n, dtype=np.float64)
    inv = ROPE_THETA ** (-np.arange(0, MLA_ROPE, 2, dtype=np.float64) / MLA_ROPE)
    ang = pos[:, None] * inv[None, :]
    cos2 = np.tile(np.cos(ang), (1, 2))
    sin2 = np.tile(np.sin(ang), (1, 2))
    scale = (MLA_NOPE + MLA_ROPE) ** -0.5 * math.log2(math.e)
    pad = np.zeros((n, HEAD_PAD - MLA_NOPE - MLA_ROPE))
    tab_c = np.concatenate([np.ones((n, MLA_NOPE)), cos2, pad], axis=1) * scale
    tab_s = np.concatenate([np.zeros((n, MLA_NOPE)), sin2, pad], axis=1) * scale
    kpad = np.zeros((n, LANES - MLA_ROPE))
    tab_kc = np.concatenate([cos2, kpad], axis=1)
    tab_ks = np.concatenate([sin2, kpad], axis=1)
    return tuple(jnp.asarray(t, F32) for t in (tab_c, tab_s, tab_kc, tab_ks))


def _rot(w):
    half = MLA_ROPE // 2
    return jnp.concatenate([-w[..., half:], w[..., :half]], axis=-1)


def kernel(x, meta_tokens, ffn1_norm, ffn1_w_gate, ffn1_w_up, ffn1_w_down, mix_norm, w_in, q_norm, w_uq,
           kv_norm, w_ukv, w_mla_o, conv_w, a_log, dt_bias, gdn_norm, w_gdn_o, w_out, ffn2_norm,
           ffn2_w_gate, ffn2_w_up, ffn2_w_down, final_norm):
    assert ffn1_norm.shape[0] == 1, "single-layer block"
    batch, seq, d = x.shape
    assert d == D_MODEL and seq % CHUNK == 0
    tm = min(256, seq)
    tq = min(512, seq)
    assert seq % tm == 0 and seq % tq == 0

    assert w_in.shape[2] == D_IN
    w_in_raw = w_in[0].T

    wq = w_uq[0].reshape(MLA_Q_RANK, MLA_HEADS, MLA_NOPE + MLA_ROPE)
    wq_nope, wq_rope = wq[..., :MLA_NOPE], wq[..., MLA_NOPE:]
    zq = lambda n: jnp.zeros((MLA_Q_RANK, MLA_HEADS, n), F32)
    tail = HEAD_PAD - MLA_NOPE - MLA_ROPE
    wqa = jnp.concatenate([wq_nope, wq_rope, zq(tail)], axis=-1).reshape(MLA_Q_RANK, QK_W).astype(BF16)
    wqb = jnp.concatenate([zq(MLA_NOPE), _rot(wq_rope), zq(tail)], axis=-1).reshape(MLA_Q_RANK, QK_W).astype(BF16)
    wkv = w_ukv[0].reshape(MLA_KV_RANK, MLA_HEADS, MLA_NOPE + MLA_V)
    wk = jnp.concatenate([wkv[..., :MLA_NOPE], jnp.zeros((MLA_KV_RANK, MLA_HEADS, HEAD_PAD - MLA_NOPE), F32)],
                         axis=-1).reshape(MLA_KV_RANK, QK_W).astype(BF16)
    wvt = wkv[..., MLA_NOPE:].reshape(MLA_KV_RANK, V_W).T.astype(BF16)
    e_np = np.zeros((2 * LANES, QK_W), np.float32)
    half = MLA_ROPE // 2
    for h in range(MLA_HEADS):
        base = h * HEAD_PAD + MLA_NOPE
        for j in range(MLA_ROPE):
            e_np[j, base + j] = 1.0
        for j in range(half):
            e_np[LANES + half + j, base + j] = -1.0
            e_np[LANES + j, base + half + j] = 1.0
    e_mat = jnp.asarray(e_np, BF16)

    g = np.arange(GROUP_W)
    bd_ones = jnp.asarray((g[:, None] // GDN_DK == g[None, :] // GDN_DK).astype(np.float32), BF16)
    proj_weights = [
        ffn1_norm[0][None], ffn1_w_gate[0].astype(BF16), ffn1_w_up[0].astype(BF16),
        ffn1_w_down[0].astype(BF16), mix_norm[0][None], w_in_raw, q_norm[0][None], wqa, wqb,
        kv_norm[0][None], wk, wvt, e_mat, conv_w[0].astype(F32), bd_ones]

    zero_carry = jnp.zeros((8, 3 * GDN_W), F32)
    _, _, k_m, vt_m, gqkv_m, gba_m, _, meta_tail, win_a, wgate = _token_proj(
        meta_tokens.astype(F32), _rope_tables(0, N_META), zero_carry, proj_weights, N_META, 1, N_META)
    proj_weights = [win_a if w is w_in_raw else w for w in proj_weights]
    tp = 2 * tm if seq % (2 * tm) == 0 else tm
    h1, q, k, vt, gqkv, gba, z, _ = _token_proj(
        x.reshape(batch * seq, d), _rope_tables(N_META, seq), meta_tail, proj_weights, tp, seq // tp, tm)

    pad_rows = lambda a, n, front: jnp.pad(a, ((n - a.shape[0], 0) if front else (0, n - a.shape[0]), (0, 0)))
    o_mla = _mla_attn(q, k, vt, k_m, vt_m[0], batch, seq, tq, tm)

    hpad = lambda a: jnp.zeros((1, LANES), F32).at[0, GDN_HEADS:2 * GDN_HEADS].set(a)
    arate = hpad(-jnp.exp(a_log[0].astype(F32)))
    dtb = hpad(dt_bias[0].astype(F32))
    gnorm = jnp.tile(gdn_norm[0].astype(F32), GDN_HEADS)[None]
    gdn_params = (arate, dtb, gnorm)
    cps = 4 if seq % (4 * CHUNK) == 0 else 1
    _, state_meta = _gdn(pad_rows(gqkv_m, CHUNK, True), pad_rows(gba_m, CHUNK, True),
                         jnp.zeros((CHUNK, GDN_W), F32), jnp.zeros((N_GROUPS, GDN_DK, GROUP_W), F32),
                         gdn_params, 1, CHUNK, 1, CHUNK - N_META)
    o_gdn, _ = _gdn(gqkv, gba, z, state_meta, gdn_params, batch, seq, cps, 0)

    merge_weights = [
        mix_norm[0][None], wgate, w_mla_o[0].astype(BF16), w_gdn_o[0].astype(BF16), w_out[0].astype(BF16),
        ffn2_norm[0][None], ffn2_w_gate[0].astype(BF16), ffn2_w_up[0].astype(BF16),
        ffn2_w_down[0].astype(BF16), final_norm[None]]
    merge_rows = 2 * MERGE_CHAIN_ROWS if (batch * seq) % (2 * MERGE_CHAIN_ROWS) == 0 else tm
    out = _merge_ffn(h1, o_mla, o_gdn, merge_weights, merge_rows)
    return out.reshape(batch, seq, d)
```

```python
import functools
import math

import jax
import jax.numpy as jnp
import numpy as np
from jax import lax
from jax.experimental import pallas as pl
from jax.experimental.pallas import tpu as pltpu

F32 = jnp.float32
BF16 = jnp.bfloat16

D_MODEL = 1024
N_META = 16
EPS = 1e-6
D_FF = 2816
MLA_HEADS = 8
MLA_Q_RANK = 256
MLA_KV_RANK = 128
MLA_NOPE = 64
MLA_ROPE = 32
MLA_V = 64
ROPE_THETA = 10000.0
GDN_HEADS = 8
GDN_DK = 64
GDN_DV = 64
CONV_K = 4
CHUNK = 64

LANES = 128
HEAD_PAD = 128
QK_W = MLA_HEADS * HEAD_PAD
V_W = MLA_HEADS * MLA_V
GDN_W = GDN_HEADS * GDN_DK
GROUP_HEADS = 4
GROUP_W = GROUP_HEADS * GDN_DK
N_GROUPS = GDN_HEADS // GROUP_HEADS
FFN_CHUNK = 256
MERGE_CHAIN_ROWS = 256
PROJ_W = MLA_Q_RANK + MLA_KV_RANK + LANES + 3 * GDN_W + LANES + GDN_W
W_IN_SPLITS = (MLA_Q_RANK, MLA_KV_RANK, MLA_ROPE, GDN_W, GDN_W, GDN_W, GDN_HEADS, GDN_HEADS, GDN_W,
               D_MODEL, D_MODEL)
(OFF_CQ, OFF_CKV, OFF_KR, OFF_GQ, OFF_GK, OFF_GV, OFF_GB, OFF_GA, OFF_GZ, OFF_GATES, _OFF_GATE_GDN,
 D_IN) = (int(v) for v in np.concatenate([[0], np.cumsum(W_IN_SPLITS)]))
W_TOKEN_COLS = -(-OFF_GATES // LANES) * LANES
W_GATE_START = OFF_GATES // LANES * LANES
VMEM_LIMIT = 60 * 1024 * 1024
NEG_BIG = -1e30


def _const_spec(shape):
    zeros = (0,) * len(shape)
    return pl.BlockSpec(shape, lambda *_: zeros, pipeline_mode=pl.Buffered(1))


def _rms(x, w):
    return x * lax.rsqrt(jnp.mean(x * x, axis=-1, keepdims=True) + EPS) * w


def _dot(a, b):
    return jnp.dot(a, b, preferred_element_type=F32)


def _dot_nt(a, b):
    return lax.dot_general(a, b, (((1,), (1,)), ((), ())), preferred_element_type=F32)


def _silu(x):
    return x * jax.nn.sigmoid(x)


def _split(x, n):
    pieces = []
    for _ in range(n - 1):
        hi = x.astype(BF16)
        pieces.append(hi)
        x = x - hi.astype(F32)
    pieces.append(x.astype(BF16))
    return pieces


def _dot_exact_rhs(x, rhs, n):
    return sum(_dot(p, rhs) for p in _split(x, n))


def _dot_exact_lhs(lhs, x, n):
    return sum(_dot(lhs, p) for p in _split(x, n))


def _head_sums(x, bd_ones):
    return jnp.concatenate([_dot(x[:, g * GROUP_W:(g + 1) * GROUP_W].astype(BF16), bd_ones)
                            for g in range(N_GROUPS)], axis=1)


def _token_proj_kernel(*refs, tiles_per_seq, regroup):
    last = pl.num_programs(0) - 1

    @pl.when(pl.program_id(0) < last)
    def _():
        _token_proj_step(*refs, tiles_per_seq=tiles_per_seq, regroup=regroup, with_matmuls=True)

    @pl.when(pl.program_id(0) == last)
    def _():
        _token_proj_step(*refs, tiles_per_seq=tiles_per_seq, regroup=regroup, with_matmuls=False)


def _token_proj_step(*refs, tiles_per_seq, regroup, with_matmuls):
    it = iter(refs)
    take = lambda n: [next(it) for _ in range(n)]
    x_ref, tc_ref, ts_ref, tkc_ref, tks_ref, c0_ref, n1_ref, wg_ref, wu_ref, wd_ref, nm_ref = take(11)
    if regroup:
        wraw_ref, wrawg_ref = take(2)
    else:
        win_ref, = take(1)
    qn_ref, wqa_ref, kvn_ref, wk_ref, wvt_ref, convw_ref, bd_ones_ref = take(7)
    h1_ref, q_ref, k_ref, vt_ref, gqkv_ref, gba_ref, z_ref, tail_ref = take(8)
    if regroup:
        win_ref, wgate_ref = take(2)
    xe_ref, = take(1)
    tm = x_ref.shape[0]
    step = pl.program_id(0)

    @pl.when(step == 0)
    def _():
        xe_ref[...] = jnp.zeros_like(xe_ref)
        if regroup:
            o = OFF_GQ + LANES - MLA_ROPE
            win_ref[:, :o] = wraw_ref[:o].T.astype(BF16)
            win_ref[:, o:o + 3 * GDN_W] = wraw_ref[OFF_GQ:OFF_GB].T.astype(BF16)
            o += 3 * GDN_W
            row = lax.broadcasted_iota(jnp.int32, (LANES, 1), 0)
            ba_rows = jnp.where(row < 2 * GDN_HEADS, wraw_ref[OFF_GB:OFF_GB + LANES], 0)
            win_ref[:, o:o + LANES] = ba_rows.T.astype(BF16)
            win_ref[:, o + LANES:] = wraw_ref[OFF_GZ:OFF_GATES].T.astype(BF16)
            g0 = OFF_GATES - W_GATE_START
            wgate_ref[...] = wrawg_ref[g0:g0 + 2 * D_MODEL].T.astype(BF16)

    n_chains = vt_ref.shape[0]
    cr = tm // n_chains
    rows = [slice(c * cr, (c + 1) * cr) for c in range(n_chains)]
    x = [x_ref[r] for r in rows] if with_matmuls else []
    xn = [_rms(v, n1_ref[...]).astype(BF16) for v in x]

    n_ffn = D_FF // FFN_CHUNK
    n_conv = 3 * GDN_W // LANES
    acts = [[] for _ in rows]
    for c in range(max(n_ffn, n_conv)):
        if c < n_ffn and with_matmuls:
            cols = slice(c * FFN_CHUNK, (c + 1) * FFN_CHUNK)
            for a, v in zip(acts, xn):
                a.append((_silu(_dot(v, wg_ref[:, cols])) * _dot(v, wu_ref[:, cols])).astype(BF16))
        if c < n_conv:
            lanes = slice(c * LANES, (c + 1) * LANES)
            y = sum(convw_ref[t:t + 1, lanes] * xe_ref[pl.ds(8 - (CONV_K - 1) + t, tm), lanes]
                    for t in range(CONV_K))
            gqkv_ref[:, lanes] = _silu(y)

    def l2norm_qk():
        bd_ones = bd_ones_ref[...]
        gq = gqkv_ref[:, :GDN_W]
        gk = gqkv_ref[:, GDN_W:2 * GDN_W]
        gqkv_ref[:, :GDN_W] = gq * lax.rsqrt(_head_sums(gq * gq, bd_ones) + EPS) * (GDN_DK ** -0.5)
        gqkv_ref[:, GDN_W:2 * GDN_W] = gk * lax.rsqrt(_head_sums(gk * gk, bd_ones) + EPS)

    if not with_matmuls:
        l2norm_qk()
        return
    h1 = [v + 0.5 * _dot(jnp.concatenate(a, axis=1), wd_ref[...]) for v, a in zip(x, acts)]
    for r, v in zip(rows, h1):
        h1_ref[r] = v

    un = [_rms(v, nm_ref[...]).astype(BF16) for v in h1]
    proj = [_dot(v, win_ref[...]) for v in un]

    l2norm_qk()
    xe_ref[0:8] = jnp.where(step % tiles_per_seq == 0, c0_ref[...], xe_ref[tm:tm + 8])
    o_raw = MLA_Q_RANK + MLA_KV_RANK + LANES
    for r, p in zip(rows, proj):
        xe_ref[8 + r.start:8 + r.stop] = p[:, o_raw:o_raw + 3 * GDN_W]
        gba_ref[r] = p[:, o_raw + 3 * GDN_W:o_raw + 3 * GDN_W + LANES]
        z_ref[r] = p[:, o_raw + 3 * GDN_W + LANES:]
    tail_ref[...] = proj[-1][cr - 8:, o_raw:o_raw + 3 * GDN_W]

    cqn = [_rms(p[:, :MLA_Q_RANK], qn_ref[...]).astype(BF16) for p in proj]
    qa = [_dot(v, wqa_ref[...]) for v in cqn]
    half = MLA_ROPE // 2
    first_half = lax.broadcasted_iota(jnp.int32, (1, QK_W), 1) % HEAD_PAD < MLA_NOPE + half
    for r, a in zip(rows, qa):
        tc = jnp.concatenate([tc_ref[r]] * MLA_HEADS, axis=1)
        ts = jnp.concatenate([ts_ref[r]] * MLA_HEADS, axis=1)
        rot = jnp.where(first_half, -pltpu.roll(a, QK_W - half, axis=1), pltpu.roll(a, half, axis=1))
        q_ref[r] = (a * tc + rot * ts).astype(BF16)

    ckvn = [_rms(p[:, MLA_Q_RANK:MLA_Q_RANK + MLA_KV_RANK], kvn_ref[...]).astype(BF16) for p in proj]
    low = lax.broadcasted_iota(jnp.int32, (1, LANES), 1) < half
    for c, (r, p, v) in enumerate(zip(rows, proj, ckvn)):
        kr = p[:, MLA_Q_RANK + MLA_KV_RANK:o_raw]
        rot = jnp.where(low, -pltpu.roll(kr, LANES - half, axis=1), pltpu.roll(kr, half, axis=1))
        roped = pltpu.roll(kr * tkc_ref[r] + rot * tks_ref[r], MLA_NOPE, axis=1)
        k_ref[r] = (_dot(v, wk_ref[...]) + jnp.concatenate([roped] * MLA_HEADS, axis=1)).astype(BF16)
        vt_ref[c] = _dot_nt(wvt_ref[...], v).astype(BF16)


def _token_proj(x2d, tabs, conv_carry, weights, tm, tiles_per_seq, chain_rows):
    n = x2d.shape[0]
    assert n % tm == 0
    last = n // tm - 1
    tile = lambda i: jnp.minimum(i, last)
    row = lambda w: pl.BlockSpec((tm, w), lambda i: (tile(i), 0))
    tab = pl.BlockSpec((tm, LANES), lambda i: (tile(i) % tiles_per_seq, 0))
    out_widths = (D_MODEL, QK_W, QK_W, None, 3 * GDN_W, LANES, GDN_W)
    out_dtypes = (F32, BF16, BF16, BF16, F32, F32, F32)
    assert tm % chain_rows == 0
    vt_spec = pl.BlockSpec((tm // chain_rows, V_W, chain_rows), lambda i: (tile(i), 0, 0))
    conv_spec = pl.BlockSpec((tm, 3 * GDN_W), lambda i: (jnp.maximum(i - 1, 0), 0))
    tail_shape = (8, 3 * GDN_W)
    out_specs = [vt_spec if w is None else row(w) for w in out_widths]
    out_specs[4] = conv_spec
    out_specs.append(pl.BlockSpec(tail_shape, lambda i: (0, 0)))
    out_shape = [jax.ShapeDtypeStruct((n // chain_rows, V_W, chain_rows) if w is None else (n, w), d)
                 for w, d in zip(out_widths, out_dtypes)] + [jax.ShapeDtypeStruct(tail_shape, F32)]

    regroup = any(w.shape == (D_IN, D_MODEL) for w in weights)
    operands, w_specs = [], []
    for w in weights:
        if w.shape == (D_IN, D_MODEL):
            assert 2 * W_GATE_START >= D_IN
            operands += [w, w]
            w_specs += [_const_spec((W_TOKEN_COLS, D_MODEL)),
                        pl.BlockSpec((W_GATE_START, D_MODEL), lambda i: (1, 0), pipeline_mode=pl.Buffered(1))]
        else:
            operands.append(w)
            w_specs.append(_const_spec(w.shape))
    if regroup:
        for shape in ((D_MODEL, PROJ_W), (D_MODEL, 2 * D_MODEL)):
            out_specs.append(pl.BlockSpec(shape, lambda i: (0, 0)))
            out_shape.append(jax.ShapeDtypeStruct(shape, BF16))
    return pl.pallas_call(
        functools.partial(_token_proj_kernel, tiles_per_seq=tiles_per_seq, regroup=regroup),
        grid=(n // tm + 1,),
        in_specs=[row(D_MODEL)] + [tab] * len(tabs) + [_const_spec(tail_shape)] + w_specs,
        out_specs=out_specs,
        out_shape=out_shape,
        scratch_shapes=[pltpu.VMEM((8 + tm, 3 * GDN_W), F32)],
        compiler_params=pltpu.CompilerParams(dimension_semantics=("arbitrary",),
                                             vmem_limit_bytes=VMEM_LIMIT),
        name="token_proj",
    )(x2d, *tabs, conv_carry, *operands)


def _mla_kernel(q_ref, k_ref, vt_ref, km_ref, vmt_ref, o_ref, st_ref, *, tq, tk):
    def tile(qi, carry):
        rows = pl.ds(pl.multiple_of(qi * tq, tq), tq)
        _mla_tile(qi, q_ref.at[rows], k_ref, vt_ref, km_ref, vmt_ref, o_ref.at[rows], st_ref, tq=tq, tk=tk)
        return carry

    lax.fori_loop(0, q_ref.shape[0] // tq, tile, 0)


def _mla_tile(qi, q_ref, k_ref, vt_ref, km_ref, vmt_ref, o_ref, st_ref, *, tq, tk):
    heads = range(2)
    hs = [slice(h * HEAD_PAD, (h + 1) * HEAD_PAD) for h in heads]
    vs = [slice(h * MLA_V, (h + 1) * MLA_V) for h in heads]
    q = [q_ref[:, s] for s in hs]
    colmax = lambda s: jnp.max(s, axis=0, keepdims=True)

    def with_ones(vt):
        return jnp.concatenate([vt, jnp.ones((8, vt.shape[1]), BF16)], axis=0)

    def scores(ki, slot, q_from=0):
        rows = pl.ds(pl.multiple_of(ki * tk, tk), tk)
        block_max = []
        for h in heads:
            s = _dot_nt(k_ref[rows, hs[h]], q[h][q_from:])
            st_ref[slot, h, :, q_from:] = s
            block_max.append(colmax(s))
        return block_max

    def update(ki, slot, m, acc, block_max, mask, q_from=0):
        st = [st_ref[slot, h, :, q_from:] for h in heads]
        if mask is not None:
            st = [jnp.where(mask, s, NEG_BIG) for s in st]
            block_max = [colmax(s) for s in st]
        m_old = [x[:, q_from:] for x in m]
        m_new = [jnp.maximum(m_old[h], block_max[h]) for h in heads]
        p = [jnp.exp2(st[h] - m_new[h]).astype(BF16) for h in heads]
        vt = vt_ref[ki]
        acc_new = [jnp.exp2(m_old[h] - m_new[h]) * acc[h][:, q_from:] + _dot(with_ones(vt[vs[h]]), p[h])
                   for h in heads]
        if q_from:
            m_new = [jnp.concatenate([m[h][:, :q_from], m_new[h]], axis=1) for h in heads]
            acc_new = [jnp.concatenate([acc[h][:, :q_from], acc_new[h]], axis=1) for h in heads]
        return m_new, acc_new

    bm_a = scores(0, 0)
    bm_b = scores(1, 1)
    meta_valid = lax.broadcasted_iota(jnp.int32, (km_ref.shape[0], tq), 0) < N_META
    st = [jnp.where(meta_valid, _dot_nt(km_ref[:, hs[h]], q[h]), NEG_BIG) for h in heads]
    m = [colmax(s) for s in st]
    acc = [_dot(with_ones(vmt_ref[vs[h], :]), jnp.exp2(st[h] - m[h]).astype(BF16)) for h in heads]

    def stage(ka, carry, cur, nxt):
        m, acc, bm_a, bm_b = carry
        bm_a2 = scores(ka + 2, nxt[0])
        m, acc = update(ka, cur[0], m, acc, bm_a, None)
        bm_b2 = scores(ka + 3, nxt[1])
        m, acc = update(ka + 1, cur[1], m, acc, bm_b, None)
        return m, acc, bm_a2, bm_b2

    def body(i, carry):
        return stage(4 * i + 2, stage(4 * i, carry, (0, 1), (2, 3)), (2, 3), (0, 1))

    causal = (lax.broadcasted_iota(jnp.int32, (tk, tq), 1) >= lax.broadcasted_iota(jnp.int32, (tk, tq), 0))
    n_full = 2 * qi

    def finish(carry, cur):
        m, acc, _, _ = carry
        m, acc = update(n_full, cur[0], m, acc, None, causal)
        m, acc = update(n_full + 1, cur[1], m, acc, None, causal[:, :tk], q_from=tk)
        return acc

    carry = lax.fori_loop(0, qi // 2, body, (m, acc, bm_a, bm_b))
    acc = lax.cond(qi % 2 == 1,
                   lambda c: finish(stage(n_full - 2, c, (0, 1), (2, 3)), (2, 3)),
                   lambda c: finish(c, (0, 1)), carry)
    o_ref[...] = jnp.concatenate([(a[:MLA_V] * (1.0 / a[MLA_V:MLA_V + 1])).T for a in acc],
                                 axis=1).astype(BF16)


def _mla_attn(q, k, vt, k_meta, vt_meta, batch, seq, tq, tk):
    nq = seq // tq
    assert vt.shape[2] == tk and tq == 2 * tk
    kern = functools.partial(_mla_kernel, tq=tq, tk=tk)
    return pl.pallas_call(
        kern,
        grid=(batch, MLA_HEADS // 2),
        in_specs=[
            pl.BlockSpec((seq, 2 * HEAD_PAD), lambda b, hp: (b, hp)),
            pl.BlockSpec((seq, 2 * HEAD_PAD), lambda b, hp: (b, hp)),
            pl.BlockSpec((seq // tk, 2 * MLA_V, tk), lambda b, hp: (b, hp, 0)),
            pl.BlockSpec((k_meta.shape[0], 2 * HEAD_PAD), lambda b, hp: (0, hp)),
            pl.BlockSpec((2 * MLA_V, vt_meta.shape[1]), lambda b, hp: (hp, 0)),
        ],
        out_specs=pl.BlockSpec((seq, 2 * MLA_V), lambda b, hp: (b, hp)),
        out_shape=jax.ShapeDtypeStruct((batch * seq, V_W), BF16),
        scratch_shapes=[pltpu.VMEM((4, 2, tk, tq), F32)],
        compiler_params=pltpu.CompilerParams(
            dimension_semantics=("arbitrary", "arbitrary"),
            vmem_limit_bytes=VMEM_LIMIT),
        name="mla_attn",
    )(q, k, vt, k_meta, vt_meta)


LEVELS = (1, 2, 4, 8, 16, 32)
GDN_CONST_NAMES = ("expand_b", "expand_a", "ltri", "eye_t", "tril_t", "stril_t", "level_masks", "bd_ones")


def _gdn_constants(cps):
    i = np.arange(CHUNK)[:, None]
    lane = np.arange(GDN_W)[None, :]
    j = lane % GDN_DK
    c = {}
    r = np.arange(LANES)[:, None]
    c["expand_b"] = (r == lane // GDN_DK)
    c["expand_a"] = (r == GDN_HEADS + lane // GDN_DK)
    t = np.arange(cps * CHUNK)
    c["ltri"] = (t[:, None] >= t[None, :]) & (t[:, None] // CHUNK == t[None, :] // CHUNK)
    c["eye_t"] = (i == j)
    c["tril_t"] = (i >= j)
    c["stril_t"] = (i > j)
    jg = j[:, :GROUP_W]
    c["level_masks"] = np.stack([
        ((i // (2 * s) == jg // (2 * s)) & ((i // s) % 2 == 1) & ((jg // s) % 2 == 0))
        for s in LEVELS])
    g = np.arange(GROUP_W)
    c["bd_ones"] = (g[:, None] // GDN_DK == g[None, :] // GDN_DK)
    bf = ("bd_ones", "expand_b", "expand_a", "ltri")
    return [jnp.asarray(c[k].astype(np.float32), BF16 if k in bf else F32) for k in GDN_CONST_NAMES]


def _block_diag(y, lo_half):
    zeros = jnp.zeros((GDN_DK, LANES), y.dtype)
    blocks = []
    for h in range(GROUP_HEADS):
        t = h // 2
        tile = y[:, t * LANES:(t + 1) * LANES]
        piece = jnp.where(lo_half, tile, 0) if h % 2 == 0 else jnp.where(lo_half, 0, tile)
        blocks.append(jnp.concatenate([piece, zeros] if t == 0 else [zeros, piece], axis=1))
    return jnp.concatenate(blocks, axis=0)


def _head_matmul(x, y, lo_half):
    return _dot(x.astype(BF16), _block_diag(y.astype(BF16), lo_half))


def _diag_blocks(m, lo_half):
    tiles = []
    for t in range(m.shape[1] // LANES):
        h = 2 * (t % 2)
        cols = slice(t * LANES, (t + 1) * LANES)
        tiles.append(jnp.where(lo_half, m[h * GDN_DK:(h + 1) * GDN_DK, cols],
                               m[(h + 1) * GDN_DK:(h + 2) * GDN_DK, cols]))
    return jnp.concatenate(tiles, axis=1)


def _gdn_kernel(gx_ref, ba_ref, z_ref, s0_ref, arate_ref, dtb_ref, gnorm_ref,
                expand_b_ref, expand_a_ref, ltri_ref, eye_t_ref, tril_t_ref, stril_t_ref,
                level_masks_ref, bd_ones_ref,
                o_ref, sout_ref,
                state_ref, kn_ref, kbq_ref, vbk_ref, kdec_ref, qg_ref, decay_ref, sdec1_ref,
                lhs_ref, amat_ref, u_ref, ku_ref, sdec2_ref, *, cps, front_pad, blocks_per_seq):
    s = pl.program_id(0)
    rows = cps * CHUNK
    stage1 = (kn_ref, kbq_ref, vbk_ref, kdec_ref, qg_ref, decay_ref, sdec1_ref)
    stage2 = (lhs_ref, amat_ref, u_ref, ku_ref, sdec2_ref)

    @pl.when(s == 0)
    def _():
        for r in stage1 + stage2 + (state_ref,):
            r[...] = jnp.zeros_like(r)

    kn_r, kbq_r, vbk_r, kdec_r, qg_r, decay_r, sdec1_r = (r.at[1 - s % 2] for r in stage1)
    kn_w, kbq_w, vbk_w, kdec_w, qg_w, decay_w, sdec1_w = (r.at[s % 2] for r in stage1)

    lo_half = lax.broadcasted_iota(jnp.int32, (1, LANES), 1) < GDN_DK
    eye, tril, stril = eye_t_ref[...], tril_t_ref[...], stril_t_ref[...]
    chains = [(j, gi) for j in range(cps) for gi in range(N_GROUPS)]
    rsl = lambda j: slice(j * CHUNK, (j + 1) * CHUNK)
    gsl = lambda gi: slice(gi * GROUP_W, (gi + 1) * GROUP_W)

    first_of_seq = (s - 2) % blocks_per_seq == 0
    states = [jnp.where(first_of_seq, s0_ref[gi], state_ref[gi]) for gi in range(N_GROUPS)]
    o_rows = [[None] * N_GROUPS for _ in range(cps)]
    in_flight = {}

    def ride(i):
        if i - 1 in in_flight:
            for gi, big in enumerate(in_flight.pop(i - 1)):
                v_new = u_ref[i - 1, gi] - big[CHUNK:2 * CHUNK]
                o_rows[i - 1][gi] = big[2 * CHUNK:] + _dot(amat_ref[i - 1, gi],
                                                           _block_diag(v_new.astype(BF16), lo_half))
        if i < cps:
            bigs = []
            for gi in range(N_GROUPS):
                state = states[gi]
                big = _dot(lhs_ref[i, gi], _block_diag(state.astype(BF16), lo_half))
                states[gi] = sdec2_ref[i, gi][0:1] * state - big[:CHUNK] + ku_ref[i, gi]
                bigs.append(big)
            in_flight[i] = bigs

    ba = ba_ref[...]
    beta = jax.nn.sigmoid(ba)
    sp_in = ba + dtb_ref[...]
    g = arate_ref[...] * (jnp.maximum(sp_in, 0.0) + jnp.log1p(jnp.exp(-jnp.abs(sp_in))))
    if front_pad:
        valid = lax.broadcasted_iota(jnp.int32, (rows, 1), 0) >= front_pad
        beta = jnp.where(valid, beta, 0.0)
        g = jnp.where(valid, g, 0.0)
    beta_e = _dot_exact_rhs(beta, expand_b_ref[...], 2)
    gc = _dot_exact_lhs(ltri_ref[...], g, 3)
    ride(0)

    lmat, amat = [], []
    for j, gi in chains:
        rs, gs = rsl(j), gsl(gi)
        kstack = _block_diag(kn_r[rs, gs], lo_half)
        sc = _dot_nt(kbq_r[j, gi], kstack)
        lmat.append(sc[:CHUNK] * decay_r[rs, gs] * stril[:, gs])
        amat.append(sc[CHUNK:] * decay_r[rs, gs])

    gc_e = _dot_exact_rhs(gc, expand_a_ref[...], 3)
    ride(1)

    def level(xinv, li):
        xo = [_head_matmul(x, lm * level_masks_ref[li], lo_half) for x, lm in zip(xinv, lmat)]
        return [x - _head_matmul(y, x, lo_half) for x, y in zip(xinv, xo)]

    xinv = [eye[:, gsl(gi)] - lm * level_masks_ref[0] for (j, gi), lm in zip(chains, lmat)]
    xinv = level(xinv, 1)

    qn = gx_ref[:, :GDN_W]
    kn = gx_ref[:, GDN_W:2 * GDN_W]
    v = gx_ref[:, 2 * GDN_W:]
    egc = jnp.exp(gc_e)
    kb = kn * beta_e
    vb = v * beta_e
    kbg = kb * egc
    qg = qn * egc
    kn_w[...] = kn.astype(BF16)
    qg_w[...] = qg
    for j, gi in chains:
        rs, gs = rsl(j), gsl(gi)
        kbq_w[j, gi] = jnp.concatenate([kb[rs, gs], qn[rs, gs]], axis=0).astype(BF16)
        vbk_w[j, gi] = jnp.concatenate([_block_diag(vb[rs, gs].astype(BF16), lo_half),
                                        _block_diag(kbg[rs, gs].astype(BF16), lo_half)], axis=1)
    ride(2)
    xinv = level(xinv, 2)

    decay, kdec, s_decay = [], [], []
    for j in range(cps):
        gce = gc_e[rsl(j)]
        gc_t = jnp.sum(gce * eye, axis=0, keepdims=True)
        decay.append(jnp.exp(jnp.where(tril > 0.5, gce - gc_t, -jnp.inf)))
        g_last = gce[CHUNK - 1:CHUNK]
        kdec.append(kn[rsl(j)] * jnp.exp(g_last - gce))
        s_decay.append(jnp.exp(g_last))
        decay_w[rsl(j)] = decay[j]
        kdec_w[rsl(j)] = kdec[j]
        for gi in range(N_GROUPS):
            sdec1_w[j, gi] = jnp.broadcast_to(s_decay[j][:, gsl(gi)], (8, GROUP_W))
    ride(3)
    xinv = level(xinv, 3)
    for i in range(4, cps + 1):
        ride(i)
    ride(cps)

    for gi in range(N_GROUPS):
        state_ref[gi] = states[gi]

    @pl.when(s == pl.num_programs(0) - 1)
    def _():
        sout_ref[...] = state_ref[...]

    o = jnp.concatenate([jnp.concatenate(r, axis=1) for r in o_rows], axis=0)
    ms = _head_sums(o * o, bd_ones_ref[...]) * (1.0 / GDN_DV)
    o = o * lax.rsqrt(ms + EPS) * gnorm_ref[...] * _silu(z_ref[...])
    o_ref[...] = o.astype(BF16)

    for li in range(4, len(LEVELS)):
        xinv = level(xinv, li)

    uw = [_dot(x.astype(BF16), vbk_r[j, gi]) for (j, gi), x in zip(chains, xinv)]
    kwu = [_diag_blocks(_dot(kdec_r[rsl(j), gsl(gi)].T.astype(BF16), m.astype(BF16)), lo_half)
           for (j, gi), m in zip(chains, uw)]

    for ci, (j, gi) in enumerate(chains):
        rs, gs = rsl(j), gsl(gi)
        u, wmat = uw[ci][:, :GROUP_W], uw[ci][:, GROUP_W:]
        ku, kw = kwu[ci][:, :GROUP_W], kwu[ci][:, GROUP_W:]
        lhs_ref[j, gi] = jnp.concatenate([kw, wmat, qg_r[rs, gs]], axis=0).astype(BF16)
        amat_ref[j, gi] = amat[ci].astype(BF16)
        u_ref[j, gi] = u
        ku_ref[j, gi] = ku
        sdec2_ref[j, gi] = sdec1_r[j, gi]


def _gdn(gqkv, gba, z, state0, params, batch, seq, cps, front_pad):
    rows = cps * CHUNK
    blocks_per_seq = seq // rows
    n_blocks = batch * blocks_per_seq
    consts = _gdn_constants(cps)
    cur = lambda w: pl.BlockSpec((rows, w), lambda s: (jnp.minimum(s, n_blocks - 1), 0))
    done = lambda w: pl.BlockSpec((rows, w), lambda s: (jnp.maximum(s - 2, 0), 0))
    state_shape = (N_GROUPS, GDN_DK, GROUP_W)
    per_chain = lambda r, w, dt: pltpu.VMEM((cps, N_GROUPS, r, w), dt)
    per_row = lambda dt: pltpu.VMEM((rows, GDN_W), dt)
    two = lambda v: pltpu.VMEM((2,) + tuple(v.shape), v.dtype)
    return pl.pallas_call(
        functools.partial(_gdn_kernel, cps=cps, front_pad=front_pad, blocks_per_seq=blocks_per_seq),
        grid=(n_blocks + 2,),
        in_specs=[cur(3 * GDN_W), cur(LANES), done(GDN_W), _const_spec(state_shape)]
                 + [_const_spec(a.shape) for a in params]
                 + [_const_spec(a.shape) for a in consts],
        out_specs=[done(GDN_W), pl.BlockSpec(state_shape, lambda s: (0, 0, 0))],
        out_shape=[jax.ShapeDtypeStruct((batch * seq, GDN_W), BF16),
                   jax.ShapeDtypeStruct(state_shape, F32)],
        scratch_shapes=[pltpu.VMEM(state_shape, F32),
                        two(per_row(BF16)),
                        two(per_chain(2 * CHUNK, GROUP_W, BF16)),
                        two(per_chain(GROUP_W, 2 * GROUP_W, BF16)),
                        two(per_row(F32)),
                        two(per_row(F32)),
                        two(per_row(F32)),
                        two(per_chain(8, GROUP_W, F32)),
                        per_chain(3 * CHUNK, GROUP_W, BF16),
                        per_chain(CHUNK, GROUP_W, BF16),
                        per_chain(CHUNK, GROUP_W, F32),
                        per_chain(CHUNK, GROUP_W, F32),
                        per_chain(8, GROUP_W, F32)],
        compiler_params=pltpu.CompilerParams(dimension_semantics=("arbitrary",),
                                             vmem_limit_bytes=VMEM_LIMIT),
        name="gdn_chunk",
    )(gqkv, gba, z, state0, *params, *consts)


def _merge_ffn_kernel(h1_ref, om_ref, og_ref, nm_ref, wgate_ref, wmo_ref, wgo_ref, wout_ref,
                      n2_ref, wg_ref, wu_ref, wd_ref, nf_ref, out_ref):
    n_chains = max(1, h1_ref.shape[0] // MERGE_CHAIN_ROWS)
    rows = [slice(c * MERGE_CHAIN_ROWS, (c + 1) * MERGE_CHAIN_ROWS) if n_chains > 1 else slice(None)
            for c in range(n_chains)]
    h1 = [h1_ref[r] for r in rows]
    un = [_rms(h, nm_ref[...]).astype(BF16) for h in h1]
    gates = [jax.nn.sigmoid(_dot(u, wgate_ref[...])) for u in un]
    merged = [(g[:, :D_MODEL] * _dot(om_ref[r], wmo_ref[...])
               + g[:, D_MODEL:] * _dot(og_ref[r], wgo_ref[...])).astype(BF16) for g, r in zip(gates, rows)]
    h2 = [h + _dot(m, wout_ref[...]) for h, m in zip(h1, merged)]
    xn = [_rms(h, n2_ref[...]).astype(BF16) for h in h2]
    acts = [[] for _ in rows]
    for c in range(D_FF // FFN_CHUNK):
        cols = slice(c * FFN_CHUNK, (c + 1) * FFN_CHUNK)
        for a, x in zip(acts, xn):
            a.append((_silu(_dot(x, wg_ref[:, cols])) * _dot(x, wu_ref[:, cols])).astype(BF16))
    h3 = [h + 0.5 * _dot(jnp.concatenate(a, axis=1), wd_ref[...]) for h, a in zip(h2, acts)]
    for r, h in zip(rows, h3):
        out_ref[r] = _rms(h, nf_ref[...])


def _merge_ffn(h1, o_mla, o_gdn, weights, tm):
    n = h1.shape[0]
    row = lambda w: pl.BlockSpec((tm, w), lambda i: (i, 0))
    return pl.pallas_call(
        _merge_ffn_kernel,
        grid=(n // tm,),
        in_specs=[row(D_MODEL), row(V_W), row(GDN_W)] + [_const_spec(w.shape) for w in weights],
        out_specs=row(D_MODEL),
        out_shape=jax.ShapeDtypeStruct((n, D_MODEL), F32),
        compiler_params=pltpu.CompilerParams(dimension_semantics=("arbitrary",),
                                             vmem_limit_bytes=VMEM_LIMIT),
        name="merge_ffn",
    )(h1, o_mla, o_gdn, *weights)


def _rope_tables(first_pos, n):
    pos = np.arange(first_pos, first_pos + n, dtype=np.float64)
    inv = ROPE_THETA ** (-np.arange(0, MLA_ROPE, 2, dtype=np.float64) / MLA_ROPE)
    ang = pos[:, None] * inv[None, :]
    cos2 = np.tile(np.cos(ang), (1, 2))
    sin2 = np.tile(np.sin(ang), (1, 2))
    scale = (MLA_NOPE + MLA_ROPE) ** -0.5 * math.log2(math.e)
    pad = np.zeros((n, HEAD_PAD - MLA_NOPE - MLA_ROPE))
    tab_c = np.concatenate([np.ones((n, MLA_NOPE)), cos2, pad], axis=1) * scale
    tab_s = np.concatenate([np.zeros((n, MLA_NOPE)), sin2, pad], axis=1) * scale
    kpad = np.zeros((n, LANES - MLA_ROPE))
    tab_kc = np.concatenate([cos2, kpad], axis=1)
    tab_ks = np.concatenate([sin2, kpad], axis=1)
    return tuple(jnp.asarray(t, F32) for t in (tab_c, tab_s, tab_kc, tab_ks))


def kernel(x, meta_tokens, ffn1_norm, ffn1_w_gate, ffn1_w_up, ffn1_w_down, mix_norm, w_in, q_norm, w_uq,
           kv_norm, w_ukv, w_mla_o, conv_w, a_log, dt_bias, gdn_norm, w_gdn_o, w_out, ffn2_norm,
           ffn2_w_gate, ffn2_w_up, ffn2_w_down, final_norm):
    assert ffn1_norm.shape[0] == 1, "single-layer block"
    batch, seq, d = x.shape
    assert d == D_MODEL and seq % CHUNK == 0
    tm = min(256, seq)
    tq = min(512, seq)
    assert seq % tm == 0 and seq % tq == 0

    assert w_in.shape[2] == D_IN
    w_in_raw = w_in[0].T

    wq = w_uq[0].reshape(MLA_Q_RANK, MLA_HEADS, MLA_NOPE + MLA_ROPE)
    wq_nope, wq_rope = wq[..., :MLA_NOPE], wq[..., MLA_NOPE:]
    zq = lambda n: jnp.zeros((MLA_Q_RANK, MLA_HEADS, n), F32)
    tail = HEAD_PAD - MLA_NOPE - MLA_ROPE
    wqa = jnp.concatenate([wq_nope, wq_rope, zq(tail)], axis=-1).reshape(MLA_Q_RANK, QK_W).astype(BF16)
    wkv = w_ukv[0].reshape(MLA_KV_RANK, MLA_HEADS, MLA_NOPE + MLA_V)
    wk = jnp.concatenate([wkv[..., :MLA_NOPE], jnp.zeros((MLA_KV_RANK, MLA_HEADS, HEAD_PAD - MLA_NOPE), F32)],
                         axis=-1).reshape(MLA_KV_RANK, QK_W).astype(BF16)
    wvt = wkv[..., MLA_NOPE:].reshape(MLA_KV_RANK, V_W).T.astype(BF16)

    g = np.arange(GROUP_W)
    bd_ones = jnp.asarray((g[:, None] // GDN_DK == g[None, :] // GDN_DK).astype(np.float32), BF16)
    proj_weights = [
        ffn1_norm[0][None], ffn1_w_gate[0].astype(BF16), ffn1_w_up[0].astype(BF16),
        ffn1_w_down[0].astype(BF16), mix_norm[0][None], w_in_raw, q_norm[0][None], wqa,
        kv_norm[0][None], wk, wvt, conv_w[0].astype(F32), bd_ones]

    zero_carry = jnp.zeros((8, 3 * GDN_W), F32)
    _, _, k_m, vt_m, gqkv_m, gba_m, _, meta_tail, win_a, wgate = _token_proj(
        meta_tokens.astype(F32), _rope_tables(0, N_META), zero_carry, proj_weights, N_META, 1, N_META)
    proj_weights = [win_a if w is w_in_raw else w for w in proj_weights]
    tp = 2 * tm if seq % (2 * tm) == 0 else tm
    h1, q, k, vt, gqkv, gba, z, _ = _token_proj(
        x.reshape(batch * seq, d), _rope_tables(N_META, seq), meta_tail, proj_weights, tp, seq // tp, tm)

    pad_rows = lambda a, n, front: jnp.pad(a, ((n - a.shape[0], 0) if front else (0, n - a.shape[0]), (0, 0)))
    o_mla = _mla_attn(q, k, vt, k_m, vt_m[0], batch, seq, tq, tm)

    hpad = lambda a: jnp.zeros((1, LANES), F32).at[0, GDN_HEADS:2 * GDN_HEADS].set(a)
    arate = hpad(-jnp.exp(a_log[0].astype(F32)))
    dtb = hpad(dt_bias[0].astype(F32))
    gnorm = jnp.tile(gdn_norm[0].astype(F32), GDN_HEADS)[None]
    gdn_params = (arate, dtb, gnorm)
    cps = 4 if seq % (4 * CHUNK) == 0 else 1
    _, state_meta = _gdn(pad_rows(gqkv_m, CHUNK, True), pad_rows(gba_m, CHUNK, True),
                         jnp.zeros((CHUNK, GDN_W), F32), jnp.zeros((N_GROUPS, GDN_DK, GROUP_W), F32),
                         gdn_params, 1, CHUNK, 1, CHUNK - N_META)
    o_gdn, _ = _gdn(gqkv, gba, z, state_meta, gdn_params, batch, seq, cps, 0)

    merge_weights = [
        mix_norm[0][None], wgate, w_mla_o[0].astype(BF16), w_gdn_o[0].astype(BF16), w_out[0].astype(BF16),
        ffn2_norm[0][None], ffn2_w_gate[0].astype(BF16), ffn2_w_up[0].astype(BF16),
        ffn2_w_down[0].astype(BF16), final_norm[None]]
    merge_rows = 2 * MERGE_CHAIN_ROWS if (batch * seq) % (2 * MERGE_CHAIN_ROWS) == 0 else tm
    out = _merge_ffn(h1, o_mla, o_gdn, merge_weights, merge_rows)
    return out.reshape(batch, seq, d)
```

```python
import functools
import math

import jax
import jax.numpy as jnp
import numpy as np
from jax import lax
from jax.experimental import pallas as pl
from jax.experimental.pallas import tpu as pltpu

F32 = jnp.float32
BF16 = jnp.bfloat16

D_MODEL = 1024
N_META = 16
EPS = 1e-6
D_FF = 2816
MLA_HEADS = 8
MLA_Q_RANK = 256
MLA_KV_RANK = 128
MLA_NOPE = 64
MLA_ROPE = 32
MLA_V = 64
ROPE_THETA = 10000.0
GDN_HEADS = 8
GDN_DK = 64
GDN_DV = 64
CONV_K = 4
CHUNK = 64

LANES = 128
HEAD_PAD = 128
QK_W = MLA_HEADS * HEAD_PAD
V_W = MLA_HEADS * MLA_V
GDN_W = GDN_HEADS * GDN_DK
GROUP_HEADS = 4
GROUP_W = GROUP_HEADS * GDN_DK
N_GROUPS = GDN_HEADS // GROUP_HEADS
FFN_CHUNK = 256
MERGE_CHAIN_ROWS = 256
PROJ_W = MLA_Q_RANK + MLA_KV_RANK + LANES + 3 * GDN_W + LANES + GDN_W
W_IN_SPLITS = (MLA_Q_RANK, MLA_KV_RANK, MLA_ROPE, GDN_W, GDN_W, GDN_W, GDN_HEADS, GDN_HEADS, GDN_W,
               D_MODEL, D_MODEL)
(OFF_CQ, OFF_CKV, OFF_KR, OFF_GQ, OFF_GK, OFF_GV, OFF_GB, OFF_GA, OFF_GZ, OFF_GATES, _OFF_GATE_GDN,
 D_IN) = (int(v) for v in np.concatenate([[0], np.cumsum(W_IN_SPLITS)]))
W_TOKEN_COLS = -(-OFF_GATES // LANES) * LANES
W_GATE_START = OFF_GATES // LANES * LANES
VMEM_LIMIT = 60 * 1024 * 1024
NEG_BIG = -1e30


def _const_spec(shape):
    zeros = (0,) * len(shape)
    return pl.BlockSpec(shape, lambda *_: zeros, pipeline_mode=pl.Buffered(1))


def _rms(x, w):
    return x * lax.rsqrt(jnp.mean(x * x, axis=-1, keepdims=True) + EPS) * w


def _dot(a, b):
    return jnp.dot(a, b, preferred_element_type=F32)


def _dot_nt(a, b):
    return lax.dot_general(a, b, (((1,), (1,)), ((), ())), preferred_element_type=F32)


def _silu(x):
    return x * jax.nn.sigmoid(x)


def _split(x, n):
    pieces = []
    for _ in range(n - 1):
        hi = x.astype(BF16)
        pieces.append(hi)
        x = x - hi.astype(F32)
    pieces.append(x.astype(BF16))
    return pieces


def _dot_exact_rhs(x, rhs, n):
    return sum(_dot(p, rhs) for p in _split(x, n))


def _dot_exact_lhs(lhs, x, n):
    return sum(_dot(lhs, p) for p in _split(x, n))


def _head_sums(x, bd_ones):
    return jnp.concatenate([_dot(x[:, g * GROUP_W:(g + 1) * GROUP_W].astype(BF16), bd_ones)
                            for g in range(N_GROUPS)], axis=1)


def _token_proj_kernel(*refs, tiles_per_seq, regroup):
    last = pl.num_programs(0) - 1

    @pl.when(pl.program_id(0) < last)
    def _():
        _token_proj_step(*refs, tiles_per_seq=tiles_per_seq, regroup=regroup, with_matmuls=True)

    @pl.when(pl.program_id(0) == last)
    def _():
        _token_proj_step(*refs, tiles_per_seq=tiles_per_seq, regroup=regroup, with_matmuls=False)


def _token_proj_step(*refs, tiles_per_seq, regroup, with_matmuls):
    it = iter(refs)
    take = lambda n: [next(it) for _ in range(n)]
    x_ref, tc_ref, ts_ref, tkc_ref, tks_ref, c0_ref, n1_ref, wg_ref, wu_ref, wd_ref, nm_ref = take(11)
    if regroup:
        wraw_ref, wrawg_ref = take(2)
    else:
        win_ref, = take(1)
    qn_ref, wqa_ref, wqb_ref, kvn_ref, wk_ref, wvt_ref, convw_ref, bd_ones_ref = take(8)
    h1_ref, q_ref, k_ref, vt_ref, gqkv_ref, gba_ref, z_ref, tail_ref = take(8)
    if regroup:
        win_ref, wgate_ref = take(2)
    xe_ref, = take(1)
    tm = x_ref.shape[0]
    step = pl.program_id(0)

    @pl.when(step == 0)
    def _():
        xe_ref[...] = jnp.zeros_like(xe_ref)
        if regroup:
            o = OFF_GQ + LANES - MLA_ROPE
            win_ref[:, :o] = wraw_ref[:o].T.astype(BF16)
            win_ref[:, o:o + 3 * GDN_W] = wraw_ref[OFF_GQ:OFF_GB].T.astype(BF16)
            o += 3 * GDN_W
            row = lax.broadcasted_iota(jnp.int32, (LANES, 1), 0)
            ba_rows = jnp.where(row < 2 * GDN_HEADS, wraw_ref[OFF_GB:OFF_GB + LANES], 0)
            win_ref[:, o:o + LANES] = ba_rows.T.astype(BF16)
            win_ref[:, o + LANES:] = wraw_ref[OFF_GZ:OFF_GATES].T.astype(BF16)
            g0 = OFF_GATES - W_GATE_START
            wgate_ref[...] = wrawg_ref[g0:g0 + 2 * D_MODEL].T.astype(BF16)

    n_chains = vt_ref.shape[0]
    cr = tm // n_chains
    rows = [slice(c * cr, (c + 1) * cr) for c in range(n_chains)]
    x = [x_ref[r] for r in rows] if with_matmuls else []
    xn = [_rms(v, n1_ref[...]).astype(BF16) for v in x]

    n_ffn = D_FF // FFN_CHUNK
    n_conv = 3 * GDN_W // LANES
    acts = [[] for _ in rows]
    for c in range(max(n_ffn, n_conv)):
        if c < n_ffn and with_matmuls:
            cols = slice(c * FFN_CHUNK, (c + 1) * FFN_CHUNK)
            for a, v in zip(acts, xn):
                a.append((_silu(_dot(v, wg_ref[:, cols])) * _dot(v, wu_ref[:, cols])).astype(BF16))
        if c < n_conv:
            lanes = slice(c * LANES, (c + 1) * LANES)
            y = sum(convw_ref[t:t + 1, lanes] * xe_ref[pl.ds(8 - (CONV_K - 1) + t, tm), lanes]
                    for t in range(CONV_K))
            gqkv_ref[:, lanes] = _silu(y)

    def l2norm_qk():
        bd_ones = bd_ones_ref[...]
        gq = gqkv_ref[:, :GDN_W]
        gk = gqkv_ref[:, GDN_W:2 * GDN_W]
        gqkv_ref[:, :GDN_W] = gq * lax.rsqrt(_head_sums(gq * gq, bd_ones) + EPS) * (GDN_DK ** -0.5)
        gqkv_ref[:, GDN_W:2 * GDN_W] = gk * lax.rsqrt(_head_sums(gk * gk, bd_ones) + EPS)

    if not with_matmuls:
        l2norm_qk()
        return
    h1 = [v + 0.5 * _dot(jnp.concatenate(a, axis=1), wd_ref[...]) for v, a in zip(x, acts)]
    for r, v in zip(rows, h1):
        h1_ref[r] = v

    un = [_rms(v, nm_ref[...]).astype(BF16) for v in h1]
    proj = [_dot(v, win_ref[...]) for v in un]

    l2norm_qk()
    xe_ref[0:8] = jnp.where(step % tiles_per_seq == 0, c0_ref[...], xe_ref[tm:tm + 8])
    o_raw = MLA_Q_RANK + MLA_KV_RANK + LANES
    for r, p in zip(rows, proj):
        xe_ref[8 + r.start:8 + r.stop] = p[:, o_raw:o_raw + 3 * GDN_W]
        gba_ref[r] = p[:, o_raw + 3 * GDN_W:o_raw + 3 * GDN_W + LANES]
        z_ref[r] = p[:, o_raw + 3 * GDN_W + LANES:]
    tail_ref[...] = proj[-1][cr - 8:, o_raw:o_raw + 3 * GDN_W]

    cqn = [_rms(p[:, :MLA_Q_RANK], qn_ref[...]).astype(BF16) for p in proj]
    qa = [_dot(v, wqa_ref[...]) for v in cqn]
    qb = [_dot(v, wqb_ref[...]) for v in cqn]
    for r, a, b in zip(rows, qa, qb):
        tc = jnp.concatenate([tc_ref[r]] * MLA_HEADS, axis=1)
        ts = jnp.concatenate([ts_ref[r]] * MLA_HEADS, axis=1)
        q_ref[r] = (a * tc + b * ts).astype(BF16)

    ckvn = [_rms(p[:, MLA_Q_RANK:MLA_Q_RANK + MLA_KV_RANK], kvn_ref[...]).astype(BF16) for p in proj]
    half = MLA_ROPE // 2
    low = lax.broadcasted_iota(jnp.int32, (1, LANES), 1) < half
    for c, (r, p, v) in enumerate(zip(rows, proj, ckvn)):
        kr = p[:, MLA_Q_RANK + MLA_KV_RANK:o_raw]
        rot = jnp.where(low, -pltpu.roll(kr, LANES - half, axis=1), pltpu.roll(kr, half, axis=1))
        roped = pltpu.roll(kr * tkc_ref[r] + rot * tks_ref[r], MLA_NOPE, axis=1)
        k_ref[r] = (_dot(v, wk_ref[...]) + jnp.concatenate([roped] * MLA_HEADS, axis=1)).astype(BF16)
        vt_ref[c] = _dot_nt(wvt_ref[...], v).astype(BF16)


def _token_proj(x2d, tabs, conv_carry, weights, tm, tiles_per_seq, chain_rows):
    n = x2d.shape[0]
    assert n % tm == 0
    last = n // tm - 1
    tile = lambda i: jnp.minimum(i, last)
    row = lambda w: pl.BlockSpec((tm, w), lambda i: (tile(i), 0))
    tab = pl.BlockSpec((tm, LANES), lambda i: (tile(i) % tiles_per_seq, 0))
    out_widths = (D_MODEL, QK_W, QK_W, None, 3 * GDN_W, LANES, GDN_W)
    out_dtypes = (F32, BF16, BF16, BF16, F32, F32, F32)
    assert tm % chain_rows == 0
    vt_spec = pl.BlockSpec((tm // chain_rows, V_W, chain_rows), lambda i: (tile(i), 0, 0))
    conv_spec = pl.BlockSpec((tm, 3 * GDN_W), lambda i: (jnp.maximum(i - 1, 0), 0))
    tail_shape = (8, 3 * GDN_W)
    out_specs = [vt_spec if w is None else row(w) for w in out_widths]
    out_specs[4] = conv_spec
    out_specs.append(pl.BlockSpec(tail_shape, lambda i: (0, 0)))
    out_shape = [jax.ShapeDtypeStruct((n // chain_rows, V_W, chain_rows) if w is None else (n, w), d)
                 for w, d in zip(out_widths, out_dtypes)] + [jax.ShapeDtypeStruct(tail_shape, F32)]

    regroup = any(w.shape == (D_IN, D_MODEL) for w in weights)
    operands, w_specs = [], []
    for w in weights:
        if w.shape == (D_IN, D_MODEL):
            assert 2 * W_GATE_START >= D_IN
            operands += [w, w]
            w_specs += [_const_spec((W_TOKEN_COLS, D_MODEL)),
                        pl.BlockSpec((W_GATE_START, D_MODEL), lambda i: (1, 0), pipeline_mode=pl.Buffered(1))]
        else:
            operands.append(w)
            w_specs.append(_const_spec(w.shape))
    if regroup:
        for shape in ((D_MODEL, PROJ_W), (D_MODEL, 2 * D_MODEL)):
            out_specs.append(pl.BlockSpec(shape, lambda i: (0, 0)))
            out_shape.append(jax.ShapeDtypeStruct(shape, BF16))
    return pl.pallas_call(
        functools.partial(_token_proj_kernel, tiles_per_seq=tiles_per_seq, regroup=regroup),
        grid=(n // tm + 1,),
        in_specs=[row(D_MODEL)] + [tab] * len(tabs) + [_const_spec(tail_shape)] + w_specs,
        out_specs=out_specs,
        out_shape=out_shape,
        scratch_shapes=[pltpu.VMEM((8 + tm, 3 * GDN_W), F32)],
        compiler_params=pltpu.CompilerParams(dimension_semantics=("arbitrary",),
                                             vmem_limit_bytes=VMEM_LIMIT),
        name="token_proj",
    )(x2d, *tabs, conv_carry, *operands)


def _mla_kernel(q_ref, k_ref, vt_ref, km_ref, vmt_ref, o_ref, st_ref, *, tq, tk):
    def tile(qi, carry):
        rows = pl.ds(pl.multiple_of(qi * tq, tq), tq)
        _mla_tile(qi, q_ref.at[rows], k_ref, vt_ref, km_ref, vmt_ref, o_ref.at[rows], st_ref, tq=tq, tk=tk)
        return carry

    lax.fori_loop(0, q_ref.shape[0] // tq, tile, 0)


def _mla_tile(qi, q_ref, k_ref, vt_ref, km_ref, vmt_ref, o_ref, st_ref, *, tq, tk):
    heads = range(2)
    hs = [slice(h * HEAD_PAD, (h + 1) * HEAD_PAD) for h in heads]
    vs = [slice(h * MLA_V, (h + 1) * MLA_V) for h in heads]
    q = [q_ref[:, s] for s in hs]
    colmax = lambda s: jnp.max(s, axis=0, keepdims=True)

    def with_ones(vt):
        return jnp.concatenate([vt, jnp.ones((8, vt.shape[1]), BF16)], axis=0)

    def scores(ki, slot, q_from=0):
        rows = pl.ds(pl.multiple_of(ki * tk, tk), tk)
        block_max = []
        for h in heads:
            s = _dot_nt(k_ref[rows, hs[h]], q[h][q_from:])
            st_ref[slot, h, :, q_from:] = s
            block_max.append(colmax(s))
        return block_max

    def update(ki, slot, m, acc, block_max, mask, q_from=0):
        st = [st_ref[slot, h, :, q_from:] for h in heads]
        if mask is not None:
            st = [jnp.where(mask, s, NEG_BIG) for s in st]
            block_max = [colmax(s) for s in st]
        m_old = [x[:, q_from:] for x in m]
        m_new = [jnp.maximum(m_old[h], block_max[h]) for h in heads]
        p = [jnp.exp2(st[h] - m_new[h]).astype(BF16) for h in heads]
        vt = vt_ref[ki]
        acc_new = [jnp.exp2(m_old[h] - m_new[h]) * acc[h][:, q_from:] + _dot(with_ones(vt[vs[h]]), p[h])
                   for h in heads]
        if q_from:
            m_new = [jnp.concatenate([m[h][:, :q_from], m_new[h]], axis=1) for h in heads]
            acc_new = [jnp.concatenate([acc[h][:, :q_from], acc_new[h]], axis=1) for h in heads]
        return m_new, acc_new

    bm_a = scores(0, 0)
    bm_b = scores(1, 1)
    meta_valid = lax.broadcasted_iota(jnp.int32, (km_ref.shape[0], tq), 0) < N_META
    st = [jnp.where(meta_valid, _dot_nt(km_ref[:, hs[h]], q[h]), NEG_BIG) for h in heads]
    m = [colmax(s) for s in st]
    acc = [_dot(with_ones(vmt_ref[vs[h], :]), jnp.exp2(st[h] - m[h]).astype(BF16)) for h in heads]

    def stage(ka, carry, cur, nxt):
        m, acc, bm_a, bm_b = carry
        bm_a2 = scores(ka + 2, nxt[0])
        m, acc = update(ka, cur[0], m, acc, bm_a, None)
        bm_b2 = scores(ka + 3, nxt[1])
        m, acc = update(ka + 1, cur[1], m, acc, bm_b, None)
        return m, acc, bm_a2, bm_b2

    def body(i, carry):
        return stage(4 * i + 2, stage(4 * i, carry, (0, 1), (2, 3)), (2, 3), (0, 1))

    causal = (lax.broadcasted_iota(jnp.int32, (tk, tq), 1) >= lax.broadcasted_iota(jnp.int32, (tk, tq), 0))
    n_full = 2 * qi

    def finish(carry, cur):
        m, acc, _, _ = carry
        m, acc = update(n_full, cur[0], m, acc, None, causal)
        m, acc = update(n_full + 1, cur[1], m, acc, None, causal[:, :tk], q_from=tk)
        return acc

    carry = lax.fori_loop(0, qi // 2, body, (m, acc, bm_a, bm_b))
    acc = lax.cond(qi % 2 == 1,
                   lambda c: finish(stage(n_full - 2, c, (0, 1), (2, 3)), (2, 3)),
                   lambda c: finish(c, (0, 1)), carry)
    o_ref[...] = jnp.concatenate([(a[:MLA_V] * (1.0 / a[MLA_V:MLA_V + 1])).T for a in acc],
                                 axis=1).astype(BF16)


def _mla_attn(q, k, vt, k_meta, vt_meta, batch, seq, tq, tk):
    nq = seq // tq
    assert vt.shape[2] == tk and tq == 2 * tk
    kern = functools.partial(_mla_kernel, tq=tq, tk=tk)
    return pl.pallas_call(
        kern,
        grid=(batch, MLA_HEADS // 2),
        in_specs=[
            pl.BlockSpec((seq, 2 * HEAD_PAD), lambda b, hp: (b, hp)),
            pl.BlockSpec((seq, 2 * HEAD_PAD), lambda b, hp: (b, hp)),
            pl.BlockSpec((seq // tk, 2 * MLA_V, tk), lambda b, hp: (b, hp, 0)),
            pl.BlockSpec((k_meta.shape[0], 2 * HEAD_PAD), lambda b, hp: (0, hp)),
            pl.BlockSpec((2 * MLA_V, vt_meta.shape[1]), lambda b, hp: (hp, 0)),
        ],
        out_specs=pl.BlockSpec((seq, 2 * MLA_V), lambda b, hp: (b, hp)),
        out_shape=jax.ShapeDtypeStruct((batch * seq, V_W), BF16),
        scratch_shapes=[pltpu.VMEM((4, 2, tk, tq), F32)],
        compiler_params=pltpu.CompilerParams(
            dimension_semantics=("arbitrary", "arbitrary"),
            vmem_limit_bytes=VMEM_LIMIT),
        name="mla_attn",
    )(q, k, vt, k_meta, vt_meta)


LEVELS = (1, 2, 4, 8, 16, 32)
GDN_CONST_NAMES = ("expand_b", "expand_a", "ltri", "eye_t", "tril_t", "stril_t", "level_masks", "bd_ones")


def _gdn_constants(cps):
    i = np.arange(CHUNK)[:, None]
    lane = np.arange(GDN_W)[None, :]
    j = lane % GDN_DK
    c = {}
    r = np.arange(LANES)[:, None]
    c["expand_b"] = (r == lane // GDN_DK)
    c["expand_a"] = (r == GDN_HEADS + lane // GDN_DK)
    t = np.arange(cps * CHUNK)
    c["ltri"] = (t[:, None] >= t[None, :]) & (t[:, None] // CHUNK == t[None, :] // CHUNK)
    c["eye_t"] = (i == j)
    c["tril_t"] = (i >= j)
    c["stril_t"] = (i > j)
    jg = j[:, :GROUP_W]
    c["level_masks"] = np.stack([
        ((i // (2 * s) == jg // (2 * s)) & ((i // s) % 2 == 1) & ((jg // s) % 2 == 0))
        for s in LEVELS])
    g = np.arange(GROUP_W)
    c["bd_ones"] = (g[:, None] // GDN_DK == g[None, :] // GDN_DK)
    bf = ("bd_ones", "expand_b", "expand_a", "ltri")
    return [jnp.asarray(c[k].astype(np.float32), BF16 if k in bf else F32) for k in GDN_CONST_NAMES]


def _block_diag(y, lo_half):
    zeros = jnp.zeros((GDN_DK, LANES), y.dtype)
    blocks = []
    for h in range(GROUP_HEADS):
        t = h // 2
        tile = y[:, t * LANES:(t + 1) * LANES]
        piece = jnp.where(lo_half, tile, 0) if h % 2 == 0 else jnp.where(lo_half, 0, tile)
        blocks.append(jnp.concatenate([piece, zeros] if t == 0 else [zeros, piece], axis=1))
    return jnp.concatenate(blocks, axis=0)


def _head_matmul(x, y, lo_half):
    return _dot(x.astype(BF16), _block_diag(y.astype(BF16), lo_half))


def _diag_blocks(m, lo_half):
    tiles = []
    for t in range(m.shape[1] // LANES):
        h = 2 * (t % 2)
        cols = slice(t * LANES, (t + 1) * LANES)
        tiles.append(jnp.where(lo_half, m[h * GDN_DK:(h + 1) * GDN_DK, cols],
                               m[(h + 1) * GDN_DK:(h + 2) * GDN_DK, cols]))
    return jnp.concatenate(tiles, axis=1)


def _gdn_kernel(gx_ref, ba_ref, z_ref, s0_ref, arate_ref, dtb_ref, gnorm_ref,
                expand_b_ref, expand_a_ref, ltri_ref, eye_t_ref, tril_t_ref, stril_t_ref,
                level_masks_ref, bd_ones_ref,
                o_ref, sout_ref,
                state_ref, kn_ref, kbq_ref, vbk_ref, kdec_ref, qg_ref, decay_ref, sdec1_ref,
                lhs_ref, amat_ref, u_ref, ku_ref, sdec2_ref, *, cps, front_pad, blocks_per_seq):
    s = pl.program_id(0)
    rows = cps * CHUNK
    stage1 = (kn_ref, kbq_ref, vbk_ref, kdec_ref, qg_ref, decay_ref, sdec1_ref)
    stage2 = (lhs_ref, amat_ref, u_ref, ku_ref, sdec2_ref)

    @pl.when(s == 0)
    def _():
        for r in stage1 + stage2 + (state_ref,):
            r[...] = jnp.zeros_like(r)

    kn_r, kbq_r, vbk_r, kdec_r, qg_r, decay_r, sdec1_r = (r.at[1 - s % 2] for r in stage1)
    kn_w, kbq_w, vbk_w, kdec_w, qg_w, decay_w, sdec1_w = (r.at[s % 2] for r in stage1)

    lo_half = lax.broadcasted_iota(jnp.int32, (1, LANES), 1) < GDN_DK
    eye, tril, stril = eye_t_ref[...], tril_t_ref[...], stril_t_ref[...]
    chains = [(j, gi) for j in range(cps) for gi in range(N_GROUPS)]
    rsl = lambda j: slice(j * CHUNK, (j + 1) * CHUNK)
    gsl = lambda gi: slice(gi * GROUP_W, (gi + 1) * GROUP_W)

    first_of_seq = (s - 2) % blocks_per_seq == 0
    states = [jnp.where(first_of_seq, s0_ref[gi], state_ref[gi]) for gi in range(N_GROUPS)]
    o_rows = [[None] * N_GROUPS for _ in range(cps)]
    in_flight = {}

    def ride(i):
        if i - 1 in in_flight:
            for gi, big in enumerate(in_flight.pop(i - 1)):
                v_new = u_ref[i - 1, gi] - big[CHUNK:2 * CHUNK]
                o_rows[i - 1][gi] = big[2 * CHUNK:] + _dot(amat_ref[i - 1, gi],
                                                           _block_diag(v_new.astype(BF16), lo_half))
        if i < cps:
            bigs = []
            for gi in range(N_GROUPS):
                state = states[gi]
                big = _dot(lhs_ref[i, gi], _block_diag(state.astype(BF16), lo_half))
                states[gi] = sdec2_ref[i, gi][0:1] * state - big[:CHUNK] + ku_ref[i, gi]
                bigs.append(big)
            in_flight[i] = bigs

    ba = ba_ref[...]
    beta = jax.nn.sigmoid(ba)
    sp_in = ba + dtb_ref[...]
    g = arate_ref[...] * (jnp.maximum(sp_in, 0.0) + jnp.log1p(jnp.exp(-jnp.abs(sp_in))))
    if front_pad:
        valid = lax.broadcasted_iota(jnp.int32, (rows, 1), 0) >= front_pad
        beta = jnp.where(valid, beta, 0.0)
        g = jnp.where(valid, g, 0.0)
    beta_e = _dot_exact_rhs(beta, expand_b_ref[...], 2)
    gc = _dot_exact_lhs(ltri_ref[...], g, 3)
    ride(0)

    lmat, amat = [], []
    for j, gi in chains:
        rs, gs = rsl(j), gsl(gi)
        kstack = _block_diag(kn_r[rs, gs], lo_half)
        sc = _dot_nt(kbq_r[j, gi], kstack)
        lmat.append(sc[:CHUNK] * decay_r[rs, gs] * stril[:, gs])
        amat.append(sc[CHUNK:] * decay_r[rs, gs])

    gc_e = _dot_exact_rhs(gc, expand_a_ref[...], 3)
    ride(1)

    def level(xinv, li):
        xo = [_head_matmul(x, lm * level_masks_ref[li], lo_half) for x, lm in zip(xinv, lmat)]
        return [x - _head_matmul(y, x, lo_half) for x, y in zip(xinv, xo)]

    xinv = [eye[:, gsl(gi)] - lm * level_masks_ref[0] for (j, gi), lm in zip(chains, lmat)]
    xinv = level(xinv, 1)

    qn = gx_ref[:, :GDN_W]
    kn = gx_ref[:, GDN_W:2 * GDN_W]
    v = gx_ref[:, 2 * GDN_W:]
    egc = jnp.exp(gc_e)
    kb = kn * beta_e
    vb = v * beta_e
    kbg = kb * egc
    qg = qn * egc
    kn_w[...] = kn.astype(BF16)
    qg_w[...] = qg
    for j, gi in chains:
        rs, gs = rsl(j), gsl(gi)
        kbq_w[j, gi] = jnp.concatenate([kb[rs, gs], qn[rs, gs]], axis=0).astype(BF16)
        vbk_w[j, gi] = jnp.concatenate([_block_diag(vb[rs, gs].astype(BF16), lo_half),
                                        _block_diag(kbg[rs, gs].astype(BF16), lo_half)], axis=1)
    ride(2)
    xinv = level(xinv, 2)

    decay, kdec, s_decay = [], [], []
    for j in range(cps):
        gce = gc_e[rsl(j)]
        gc_t = jnp.sum(gce * eye, axis=0, keepdims=True)
        decay.append(jnp.exp(jnp.where(tril > 0.5, gce - gc_t, -jnp.inf)))
        g_last = gce[CHUNK - 1:CHUNK]
        kdec.append(kn[rsl(j)] * jnp.exp(g_last - gce))
        s_decay.append(jnp.exp(g_last))
        decay_w[rsl(j)] = decay[j]
        kdec_w[rsl(j)] = kdec[j]
        for gi in range(N_GROUPS):
            sdec1_w[j, gi] = jnp.broadcast_to(s_decay[j][:, gsl(gi)], (8, GROUP_W))
    ride(3)
    xinv = level(xinv, 3)
    for i in range(4, cps + 1):
        ride(i)
    ride(cps)

    for gi in range(N_GROUPS):
        state_ref[gi] = states[gi]

    @pl.when(s == pl.num_programs(0) - 1)
    def _():
        sout_ref[...] = state_ref[...]

    o = jnp.concatenate([jnp.concatenate(r, axis=1) for r in o_rows], axis=0)
    ms = _head_sums(o * o, bd_ones_ref[...]) * (1.0 / GDN_DV)
    o = o * lax.rsqrt(ms + EPS) * gnorm_ref[...] * _silu(z_ref[...])
    o_ref[...] = o.astype(BF16)

    for li in range(4, len(LEVELS)):
        xinv = level(xinv, li)

    uw = [_dot(x.astype(BF16), vbk_r[j, gi]) for (j, gi), x in zip(chains, xinv)]
    kwu = [_diag_blocks(_dot(kdec_r[rsl(j), gsl(gi)].T.astype(BF16), m.astype(BF16)), lo_half)
           for (j, gi), m in zip(chains, uw)]

    for ci, (j, gi) in enumerate(chains):
        rs, gs = rsl(j), gsl(gi)
        u, wmat = uw[ci][:, :GROUP_W], uw[ci][:, GROUP_W:]
        ku, kw = kwu[ci][:, :GROUP_W], kwu[ci][:, GROUP_W:]
        lhs_ref[j, gi] = jnp.concatenate([kw, wmat, qg_r[rs, gs]], axis=0).astype(BF16)
        amat_ref[j, gi] = amat[ci].astype(BF16)
        u_ref[j, gi] = u
        ku_ref[j, gi] = ku
        sdec2_ref[j, gi] = sdec1_r[j, gi]


def _gdn(gqkv, gba, z, state0, params, batch, seq, cps, front_pad):
    rows = cps * CHUNK
    blocks_per_seq = seq // rows
    n_blocks = batch * blocks_per_seq
    consts = _gdn_constants(cps)
    cur = lambda w: pl.BlockSpec((rows, w), lambda s: (jnp.minimum(s, n_blocks - 1), 0))
    done = lambda w: pl.BlockSpec((rows, w), lambda s: (jnp.maximum(s - 2, 0), 0))
    state_shape = (N_GROUPS, GDN_DK, GROUP_W)
    per_chain = lambda r, w, dt: pltpu.VMEM((cps, N_GROUPS, r, w), dt)
    per_row = lambda dt: pltpu.VMEM((rows, GDN_W), dt)
    two = lambda v: pltpu.VMEM((2,) + tuple(v.shape), v.dtype)
    return pl.pallas_call(
        functools.partial(_gdn_kernel, cps=cps, front_pad=front_pad, blocks_per_seq=blocks_per_seq),
        grid=(n_blocks + 2,),
        in_specs=[cur(3 * GDN_W), cur(LANES), done(GDN_W), _const_spec(state_shape)]
                 + [_const_spec(a.shape) for a in params]
                 + [_const_spec(a.shape) for a in consts],
        out_specs=[done(GDN_W), pl.BlockSpec(state_shape, lambda s: (0, 0, 0))],
        out_shape=[jax.ShapeDtypeStruct((batch * seq, GDN_W), BF16),
                   jax.ShapeDtypeStruct(state_shape, F32)],
        scratch_shapes=[pltpu.VMEM(state_shape, F32),
                        two(per_row(BF16)),
                        two(per_chain(2 * CHUNK, GROUP_W, BF16)),
                        two(per_chain(GROUP_W, 2 * GROUP_W, BF16)),
                        two(per_row(F32)),
                        two(per_row(F32)),
                        two(per_row(F32)),
                        two(per_chain(8, GROUP_W, F32)),
                        per_chain(3 * CHUNK, GROUP_W, BF16),
                        per_chain(CHUNK, GROUP_W, BF16),
                        per_chain(CHUNK, GROUP_W, F32),
                        per_chain(CHUNK, GROUP_W, F32),
                        per_chain(8, GROUP_W, F32)],
        compiler_params=pltpu.CompilerParams(dimension_semantics=("arbitrary",),
                                             vmem_limit_bytes=VMEM_LIMIT),
        name="gdn_chunk",
    )(gqkv, gba, z, state0, *params, *consts)


def _merge_ffn_kernel(h1_ref, om_ref, og_ref, nm_ref, wgate_ref, wmo_ref, wgo_ref, wout_ref,
                      n2_ref, wg_ref, wu_ref, wd_ref, nf_ref, out_ref):
    n_chains = max(1, h1_ref.shape[0] // MERGE_CHAIN_ROWS)
    rows = [slice(c * MERGE_CHAIN_ROWS, (c + 1) * MERGE_CHAIN_ROWS) if n_chains > 1 else slice(None)
            for c in range(n_chains)]
    h1 = [h1_ref[r] for r in rows]
    un = [_rms(h, nm_ref[...]).astype(BF16) for h in h1]
    gates = [jax.nn.sigmoid(_dot(u, wgate_ref[...])) for u in un]
    merged = [(g[:, :D_MODEL] * _dot(om_ref[r], wmo_ref[...])
               + g[:, D_MODEL:] * _dot(og_ref[r], wgo_ref[...])).astype(BF16) for g, r in zip(gates, rows)]
    h2 = [h + _dot(m, wout_ref[...]) for h, m in zip(h1, merged)]
    xn = [_rms(h, n2_ref[...]).astype(BF16) for h in h2]
    acts = [[] for _ in rows]
    for c in range(D_FF // FFN_CHUNK):
        cols = slice(c * FFN_CHUNK, (c + 1) * FFN_CHUNK)
        for a, x in zip(acts, xn):
            a.append((_silu(_dot(x, wg_ref[:, cols])) * _dot(x, wu_ref[:, cols])).astype(BF16))
    h3 = [h + 0.5 * _dot(jnp.concatenate(a, axis=1), wd_ref[...]) for h, a in zip(h2, acts)]
    for r, h in zip(rows, h3):
        out_ref[r] = _rms(h, nf_ref[...])


def _merge_ffn(h1, o_mla, o_gdn, weights, tm):
    n = h1.shape[0]
    row = lambda w: pl.BlockSpec((tm, w), lambda i: (i, 0))
    return pl.pallas_call(
        _merge_ffn_kernel,
        grid=(n // tm,),
        in_specs=[row(D_MODEL), row(V_W), row(GDN_W)] + [_const_spec(w.shape) for w in weights],
        out_specs=row(D_MODEL),
        out_shape=jax.ShapeDtypeStruct((n, D_MODEL), F32),
        compiler_params=pltpu.CompilerParams(dimension_semantics=("arbitrary",),
                                             vmem_limit_bytes=VMEM_LIMIT),
        name="merge_ffn",
    )(h1, o_mla, o_gdn, *weights)


def _rope_tables(first_pos, n):
    pos = np.arange(first_pos, first_pos + n, dtype=np.float64)
    inv = ROPE_THETA ** (-np.arange(0, MLA_ROPE, 2, dtype=np.float64) / MLA_ROPE)
    ang = pos[:, None] * inv[None, :]
    cos2 = np.tile(np.cos(ang), (1, 2))
    sin2 = np.tile(np.sin(ang), (1, 2))
    scale = (MLA_NOPE + MLA_ROPE) ** -0.5 * math.log2(math.e)
    pad = np.zeros((n, HEAD_PAD - MLA_NOPE - MLA_ROPE))
    tab_c = np.concatenate([np.ones((n, MLA_NOPE)), cos2, pad], axis=1) * scale
    tab_s = np.concatenate([np.zeros((n, MLA_NOPE)), sin2, pad], axis=1) * scale
    kpad = np.zeros((n, LANES - MLA_ROPE))
    tab_kc = np.concatenate([cos2, kpad], axis=1)
    tab_ks = np.concatenate([sin2, kpad], axis=1)
    return tuple(jnp.asarray(t, F32) for t in (tab_c, tab_s, tab_kc, tab_ks))


def _rot(w):
    half = MLA_ROPE // 2
    return jnp.concatenate([-w[..., half:], w[..., :half]], axis=-1)


def kernel(x, meta_tokens, ffn1_norm, ffn1_w_gate, ffn1_w_up, ffn1_w_down, mix_norm, w_in, q_norm, w_uq,
           kv_norm, w_ukv, w_mla_o, conv_w, a_log, dt_bias, gdn_norm, w_gdn_o, w_out, ffn2_norm,
           ffn2_w_gate, ffn2_w_up, ffn2_w_down, final_norm):
    assert ffn1_norm.shape[0] == 1, "single-layer block"
    batch, seq, d = x.shape
    assert d == D_MODEL and seq % CHUNK == 0
    tm = min(256, seq)
    tq = min(512, seq)
    assert seq % tm == 0 and seq % tq == 0

    assert w_in.shape[2] == D_IN
    w_in_raw = w_in[0].T

    wq = w_uq[0].reshape(MLA_Q_RANK, MLA_HEADS, MLA_NOPE + MLA_ROPE)
    wq_nope, wq_rope = wq[..., :MLA_NOPE], wq[..., MLA_NOPE:]
    zq = lambda n: jnp.zeros((MLA_Q_RANK, MLA_HEADS, n), F32)
    tail = HEAD_PAD - MLA_NOPE - MLA_ROPE
    wqa = jnp.concatenate([wq_nope, wq_rope, zq(tail)], axis=-1).reshape(MLA_Q_RANK, QK_W).astype(BF16)
    wqb = jnp.concatenate([zq(MLA_NOPE), _rot(wq_rope), zq(tail)], axis=-1).reshape(MLA_Q_RANK, QK_W).astype(BF16)
    wkv = w_ukv[0].reshape(MLA_KV_RANK, MLA_HEADS, MLA_NOPE + MLA_V)
    wk = jnp.concatenate([wkv[..., :MLA_NOPE], jnp.zeros((MLA_KV_RANK, MLA_HEADS, HEAD_PAD - MLA_NOPE), F32)],
                         axis=-1).reshape(MLA_KV_RANK, QK_W).astype(BF16)
    wvt = wkv[..., MLA_NOPE:].reshape(MLA_KV_RANK, V_W).T.astype(BF16)

    g = np.arange(GROUP_W)
    bd_ones = jnp.asarray((g[:, None] // GDN_DK == g[None, :] // GDN_DK).astype(np.float32), BF16)
    proj_weights = [
        ffn1_norm[0][None], ffn1_w_gate[0].astype(BF16), ffn1_w_up[0].astype(BF16),
        ffn1_w_down[0].astype(BF16), mix_norm[0][None], w_in_raw, q_norm[0][None], wqa, wqb,
        kv_norm[0][None], wk, wvt, conv_w[0].astype(F32), bd_ones]

    zero_carry = jnp.zeros((8, 3 * GDN_W), F32)
    _, _, k_m, vt_m, gqkv_m, gba_m, _, meta_tail, win_a, wgate = _token_proj(
        meta_tokens.astype(F32), _rope_tables(0, N_META), zero_carry, proj_weights, N_META, 1, N_META)
    proj_weights = [win_a if w is w_in_raw else w for w in proj_weights]
    tp = 2 * tm if seq % (2 * tm) == 0 else tm
    h1, q, k, vt, gqkv, gba, z, _ = _token_proj(
        x.reshape(batch * seq, d), _rope_tables(N_META, seq), meta_tail, proj_weights, tp, seq // tp, tm)

    pad_rows = lambda a, n, front: jnp.pad(a, ((n - a.shape[0], 0) if front else (0, n - a.shape[0]), (0, 0)))
    o_mla = _mla_attn(q, k, vt, k_m, vt_m[0], batch, seq, tq, tm)

    hpad = lambda a: jnp.zeros((1, LANES), F32).at[0, GDN_HEADS:2 * GDN_HEADS].set(a)
    arate = hpad(-jnp.exp(a_log[0].astype(F32)))
    dtb = hpad(dt_bias[0].astype(F32))
    gnorm = jnp.tile(gdn_norm[0].astype(F32), GDN_HEADS)[None]
    gdn_params = (arate, dtb, gnorm)
    cps = 4 if seq % (4 * CHUNK) == 0 else 1
    _, state_meta = _gdn(pad_rows(gqkv_m, CHUNK, True), pad_rows(gba_m, CHUNK, True),
                         jnp.zeros((CHUNK, GDN_W), F32), jnp.zeros((N_GROUPS, GDN_DK, GROUP_W), F32),
                         gdn_params, 1, CHUNK, 1, CHUNK - N_META)
    o_gdn, _ = _gdn(gqkv, gba, z, state_meta, gdn_params, batch, seq, cps, 0)

    merge_weights = [
        mix_norm[0][None], wgate, w_mla_o[0].astype(BF16), w_gdn_o[0].astype(BF16), w_out[0].astype(BF16),
        ffn2_norm[0][None], ffn2_w_gate[0].astype(BF16), ffn2_w_up[0].astype(BF16),
        ffn2_w_down[0].astype(BF16), final_norm[None]]
    merge_rows = 2 * MERGE_CHAIN_ROWS if (batch * seq) % (2 * MERGE_CHAIN_ROWS) == 0 else tm
    out = _merge_ffn(h1, o_mla, o_gdn, merge_weights, merge_rows)
    return out.reshape(batch, seq, d)
```

```python
import functools
import math

import jax
import jax.numpy as jnp
import numpy as np
from jax import lax
from jax.experimental import pallas as pl
from jax.experimental.pallas import tpu as pltpu

F32 = jnp.float32
BF16 = jnp.bfloat16

D_MODEL = 1024
N_META = 16
EPS = 1e-6
D_FF = 2816
MLA_HEADS = 8
MLA_Q_RANK = 256
MLA_KV_RANK = 128
MLA_NOPE = 64
MLA_ROPE = 32
MLA_V = 64
ROPE_THETA = 10000.0
GDN_HEADS = 8
GDN_DK = 64
GDN_DV = 64
CONV_K = 4
CHUNK = 64

LANES = 128
HEAD_PAD = 128
QK_W = MLA_HEADS * HEAD_PAD
V_W = MLA_HEADS * MLA_V
GDN_W = GDN_HEADS * GDN_DK
GROUP_HEADS = 4
GROUP_W = GROUP_HEADS * GDN_DK
N_GROUPS = GDN_HEADS // GROUP_HEADS
FFN_CHUNK = 256
MERGE_CHAIN_ROWS = 256
PROJ_W = MLA_Q_RANK + MLA_KV_RANK + LANES + 3 * GDN_W + LANES + GDN_W
W_IN_SPLITS = (MLA_Q_RANK, MLA_KV_RANK, MLA_ROPE, GDN_W, GDN_W, GDN_W, GDN_HEADS, GDN_HEADS, GDN_W,
               D_MODEL, D_MODEL)
(OFF_CQ, OFF_CKV, OFF_KR, OFF_GQ, OFF_GK, OFF_GV, OFF_GB, OFF_GA, OFF_GZ, OFF_GATES, _OFF_GATE_GDN,
 D_IN) = (int(v) for v in np.concatenate([[0], np.cumsum(W_IN_SPLITS)]))
W_TOKEN_COLS = -(-OFF_GATES // LANES) * LANES
W_GATE_START = OFF_GATES // LANES * LANES
VMEM_LIMIT = 60 * 1024 * 1024
NEG_BIG = -1e30


def _const_spec(shape):
    zeros = (0,) * len(shape)
    return pl.BlockSpec(shape, lambda *_: zeros, pipeline_mode=pl.Buffered(1))


def _rms(x, w):
    return x * lax.rsqrt(jnp.mean(x * x, axis=-1, keepdims=True) + EPS) * w


def _dot(a, b):
    return jnp.dot(a, b, preferred_element_type=F32)


def _dot_nt(a, b):
    return lax.dot_general(a, b, (((1,), (1,)), ((), ())), preferred_element_type=F32)


def _silu(x):
    return x * jax.nn.sigmoid(x)


def _split(x, n):
    pieces = []
    for _ in range(n - 1):
        hi = x.astype(BF16)
        pieces.append(hi)
        x = x - hi.astype(F32)
    pieces.append(x.astype(BF16))
    return pieces


def _dot_exact_rhs(x, rhs, n):
    return sum(_dot(p, rhs) for p in _split(x, n))


def _dot_exact_lhs(lhs, x, n):
    return sum(_dot(lhs, p) for p in _split(x, n))


def _head_sums(x, bd_ones):
    return jnp.concatenate([_dot(x[:, g * GROUP_W:(g + 1) * GROUP_W].astype(BF16), bd_ones)
                            for g in range(N_GROUPS)], axis=1)


def _token_proj_kernel(*refs, tiles_per_seq, regroup):
    last = pl.num_programs(0) - 1

    @pl.when(pl.program_id(0) < last)
    def _():
        _token_proj_step(*refs, tiles_per_seq=tiles_per_seq, regroup=regroup, with_matmuls=True)

    @pl.when(pl.program_id(0) == last)
    def _():
        _token_proj_step(*refs, tiles_per_seq=tiles_per_seq, regroup=regroup, with_matmuls=False)


def _token_proj_step(*refs, tiles_per_seq, regroup, with_matmuls):
    it = iter(refs)
    take = lambda n: [next(it) for _ in range(n)]
    x_ref, tc_ref, ts_ref, tkc_ref, tks_ref, c0_ref, n1_ref, wg_ref, wu_ref, wd_ref, nm_ref = take(11)
    if regroup:
        wraw_ref, wrawg_ref = take(2)
    else:
        win_ref, = take(1)
    qn_ref, wqa_ref, wqb_ref, kvn_ref, wk_ref, wvt_ref, convw_ref, bd_ones_ref = take(8)
    h1_ref, q_ref, k_ref, vt_ref, gqkv_ref, gba_ref, z_ref, tail_ref = take(8)
    if regroup:
        win_ref, wgate_ref = take(2)
    xe_ref, = take(1)
    tm = x_ref.shape[0]
    step = pl.program_id(0)

    @pl.when(step == 0)
    def _():
        xe_ref[...] = jnp.zeros_like(xe_ref)
        if regroup:
            o = OFF_GQ + LANES - MLA_ROPE
            win_ref[:, :o] = wraw_ref[:o].T.astype(BF16)
            win_ref[:, o:o + 3 * GDN_W] = wraw_ref[OFF_GQ:OFF_GB].T.astype(BF16)
            o += 3 * GDN_W
            row = lax.broadcasted_iota(jnp.int32, (LANES, 1), 0)
            ba_rows = jnp.where(row < 2 * GDN_HEADS, wraw_ref[OFF_GB:OFF_GB + LANES], 0)
            win_ref[:, o:o + LANES] = ba_rows.T.astype(BF16)
            win_ref[:, o + LANES:] = wraw_ref[OFF_GZ:OFF_GATES].T.astype(BF16)
            g0 = OFF_GATES - W_GATE_START
            wgate_ref[...] = wrawg_ref[g0:g0 + 2 * D_MODEL].T.astype(BF16)

    n_chains = vt_ref.shape[0]
    cr = tm // n_chains
    rows = [slice(c * cr, (c + 1) * cr) for c in range(n_chains)]
    x = [x_ref[r] for r in rows] if with_matmuls else []
    xn = [_rms(v, n1_ref[...]).astype(BF16) for v in x]

    n_ffn = D_FF // FFN_CHUNK
    n_conv = 3 * GDN_W // LANES
    acts = [[] for _ in rows]
    for c in range(max(n_ffn, n_conv)):
        if c < n_ffn and with_matmuls:
            cols = slice(c * FFN_CHUNK, (c + 1) * FFN_CHUNK)
            for a, v in zip(acts, xn):
                a.append((_silu(_dot(v, wg_ref[:, cols])) * _dot(v, wu_ref[:, cols])).astype(BF16))
        if c < n_conv:
            lanes = slice(c * LANES, (c + 1) * LANES)
            y = sum(convw_ref[t:t + 1, lanes] * xe_ref[pl.ds(8 - (CONV_K - 1) + t, tm), lanes]
                    for t in range(CONV_K))
            gqkv_ref[:, lanes] = _silu(y)

    def l2norm_qk():
        bd_ones = bd_ones_ref[...]
        gq = gqkv_ref[:, :GDN_W]
        gk = gqkv_ref[:, GDN_W:2 * GDN_W]
        gqkv_ref[:, :GDN_W] = gq * lax.rsqrt(_head_sums(gq * gq, bd_ones) + EPS) * (GDN_DK ** -0.5)
        gqkv_ref[:, GDN_W:2 * GDN_W] = gk * lax.rsqrt(_head_sums(gk * gk, bd_ones) + EPS)

    if not with_matmuls:
        l2norm_qk()
        return
    h1 = [v + 0.5 * _dot(jnp.concatenate(a, axis=1), wd_ref[...]) for v, a in zip(x, acts)]
    for r, v in zip(rows, h1):
        h1_ref[r] = v

    un = [_rms(v, nm_ref[...]).astype(BF16) for v in h1]
    proj = [_dot(v, win_ref[...]) for v in un]

    l2norm_qk()
    xe_ref[0:8] = jnp.where(step % tiles_per_seq == 0, c0_ref[...], xe_ref[tm:tm + 8])
    o_raw = MLA_Q_RANK + MLA_KV_RANK + LANES
    for r, p in zip(rows, proj):
        xe_ref[8 + r.start:8 + r.stop] = p[:, o_raw:o_raw + 3 * GDN_W]
        gba_ref[r] = p[:, o_raw + 3 * GDN_W:o_raw + 3 * GDN_W + LANES]
        z_ref[r] = p[:, o_raw + 3 * GDN_W + LANES:]
    tail_ref[...] = proj[-1][cr - 8:, o_raw:o_raw + 3 * GDN_W]

    cqn = [_rms(p[:, :MLA_Q_RANK], qn_ref[...]).astype(BF16) for p in proj]
    qa = [_dot(v, wqa_ref[...]) for v in cqn]
    qb = [_dot(v, wqb_ref[...]) for v in cqn]
    for r, a, b in zip(rows, qa, qb):
        tc = jnp.concatenate([tc_ref[r]] * MLA_HEADS, axis=1)
        ts = jnp.concatenate([ts_ref[r]] * MLA_HEADS, axis=1)
        q_ref[r] = (a * tc + b * ts).astype(BF16)

    ckvn = [_rms(p[:, MLA_Q_RANK:MLA_Q_RANK + MLA_KV_RANK], kvn_ref[...]).astype(BF16) for p in proj]
    half = MLA_ROPE // 2
    low = lax.broadcasted_iota(jnp.int32, (1, LANES), 1) < half
    for c, (r, p, v) in enumerate(zip(rows, proj, ckvn)):
        kr = p[:, MLA_Q_RANK + MLA_KV_RANK:o_raw]
        rot = jnp.where(low, -pltpu.roll(kr, LANES - half, axis=1), pltpu.roll(kr, half, axis=1))
        roped = pltpu.roll(kr * tkc_ref[r] + rot * tks_ref[r], MLA_NOPE, axis=1)
        k_ref[r] = (_dot(v, wk_ref[...]) + jnp.concatenate([roped] * MLA_HEADS, axis=1)).astype(BF16)
        vt_ref[c] = _dot_nt(wvt_ref[...], v).astype(BF16)


def _token_proj(x2d, tabs, conv_carry, weights, tm, tiles_per_seq, chain_rows):
    n = x2d.shape[0]
    assert n % tm == 0
    last = n // tm - 1
    tile = lambda i: jnp.minimum(i, last)
    row = lambda w: pl.BlockSpec((tm, w), lambda i: (tile(i), 0))
    tab = pl.BlockSpec((tm, LANES), lambda i: (tile(i) % tiles_per_seq, 0))
    out_widths = (D_MODEL, QK_W, QK_W, None, 3 * GDN_W, LANES, GDN_W)
    out_dtypes = (F32, BF16, BF16, BF16, F32, F32, F32)
    assert tm % chain_rows == 0
    vt_spec = pl.BlockSpec((tm // chain_rows, V_W, chain_rows), lambda i: (tile(i), 0, 0))
    conv_spec = pl.BlockSpec((tm, 3 * GDN_W), lambda i: (jnp.maximum(i - 1, 0), 0))
    tail_shape = (8, 3 * GDN_W)
    out_specs = [vt_spec if w is None else row(w) for w in out_widths]
    out_specs[4] = conv_spec
    out_specs.append(pl.BlockSpec(tail_shape, lambda i: (0, 0)))
    out_shape = [jax.ShapeDtypeStruct((n // chain_rows, V_W, chain_rows) if w is None else (n, w), d)
                 for w, d in zip(out_widths, out_dtypes)] + [jax.ShapeDtypeStruct(tail_shape, F32)]

    regroup = any(w.shape == (D_IN, D_MODEL) for w in weights)
    operands, w_specs = [], []
    for w in weights:
        if w.shape == (D_IN, D_MODEL):
            assert 2 * W_GATE_START >= D_IN
            operands += [w, w]
            w_specs += [_const_spec((W_TOKEN_COLS, D_MODEL)),
                        pl.BlockSpec((W_GATE_START, D_MODEL), lambda i: (1, 0), pipeline_mode=pl.Buffered(1))]
        else:
            operands.append(w)
            w_specs.append(_const_spec(w.shape))
    if regroup:
        for shape in ((D_MODEL, PROJ_W), (D_MODEL, 2 * D_MODEL)):
            out_specs.append(pl.BlockSpec(shape, lambda i: (0, 0)))
            out_shape.append(jax.ShapeDtypeStruct(shape, BF16))
    return pl.pallas_call(
        functools.partial(_token_proj_kernel, tiles_per_seq=tiles_per_seq, regroup=regroup),
        grid=(n // tm + 1,),
        in_specs=[row(D_MODEL)] + [tab] * len(tabs) + [_const_spec(tail_shape)] + w_specs,
        out_specs=out_specs,
        out_shape=out_shape,
        scratch_shapes=[pltpu.VMEM((8 + tm, 3 * GDN_W), F32)],
        compiler_params=pltpu.CompilerParams(dimension_semantics=("arbitrary",),
                                             vmem_limit_bytes=VMEM_LIMIT),
        name="token_proj",
    )(x2d, *tabs, conv_carry, *operands)


def _mla_kernel(q_ref, k_ref, vt_ref, km_ref, vmt_ref, o_ref, st_ref, *, tq, tk):
    def tile(qi, carry):
        rows = pl.ds(pl.multiple_of(qi * tq, tq), tq)
        _mla_tile(qi, q_ref.at[rows], k_ref, vt_ref, km_ref, vmt_ref, o_ref.at[rows], st_ref, tq=tq, tk=tk)
        return carry

    lax.fori_loop(0, q_ref.shape[0] // tq, tile, 0)


def _mla_tile(qi, q_ref, k_ref, vt_ref, km_ref, vmt_ref, o_ref, st_ref, *, tq, tk):
    heads = range(2)
    hs = [slice(h * HEAD_PAD, (h + 1) * HEAD_PAD) for h in heads]
    vs = [slice(h * MLA_V, (h + 1) * MLA_V) for h in heads]
    q = [q_ref[:, s] for s in hs]
    colmax = lambda s: jnp.max(s, axis=0, keepdims=True)

    def with_ones(vt):
        return jnp.concatenate([vt, jnp.ones((8, vt.shape[1]), BF16)], axis=0)

    def scores(ki, slot, q_from=0):
        rows = pl.ds(pl.multiple_of(ki * tk, tk), tk)
        block_max = []
        for h in heads:
            s = _dot_nt(k_ref[rows, hs[h]], q[h][q_from:])
            st_ref[slot, h, :, q_from:] = s
            block_max.append(colmax(s))
        return block_max

    def update(ki, slot, m, acc, block_max, mask, q_from=0):
        st = [st_ref[slot, h, :, q_from:] for h in heads]
        if mask is not None:
            st = [jnp.where(mask, s, NEG_BIG) for s in st]
            block_max = [colmax(s) for s in st]
        m_old = [x[:, q_from:] for x in m]
        m_new = [jnp.maximum(m_old[h], block_max[h]) for h in heads]
        p = [jnp.exp2(st[h] - m_new[h]).astype(BF16) for h in heads]
        vt = vt_ref[ki]
        acc_new = [jnp.exp2(m_old[h] - m_new[h]) * acc[h][:, q_from:] + _dot(with_ones(vt[vs[h]]), p[h])
                   for h in heads]
        if q_from:
            m_new = [jnp.concatenate([m[h][:, :q_from], m_new[h]], axis=1) for h in heads]
            acc_new = [jnp.concatenate([acc[h][:, :q_from], acc_new[h]], axis=1) for h in heads]
        return m_new, acc_new

    bm_a = scores(0, 0)
    bm_b = scores(1, 1)
    meta_valid = lax.broadcasted_iota(jnp.int32, (km_ref.shape[0], tq), 0) < N_META
    st = [jnp.where(meta_valid, _dot_nt(km_ref[:, hs[h]], q[h]), NEG_BIG) for h in heads]
    m = [colmax(s) for s in st]
    acc = [_dot(with_ones(vmt_ref[vs[h], :]), jnp.exp2(st[h] - m[h]).astype(BF16)) for h in heads]

    def stage(ka, carry, cur, nxt):
        m, acc, bm_a, bm_b = carry
        bm_a2 = scores(ka + 2, nxt[0])
        m, acc = update(ka, cur[0], m, acc, bm_a, None)
        bm_b2 = scores(ka + 3, nxt[1])
        m, acc = update(ka + 1, cur[1], m, acc, bm_b, None)
        return m, acc, bm_a2, bm_b2

    def body(i, carry):
        return stage(4 * i + 2, stage(4 * i, carry, (0, 1), (2, 3)), (2, 3), (0, 1))

    causal = (lax.broadcasted_iota(jnp.int32, (tk, tq), 1) >= lax.broadcasted_iota(jnp.int32, (tk, tq), 0))
    n_full = 2 * qi

    def finish(carry, cur):
        m, acc, _, _ = carry
        m, acc = update(n_full, cur[0], m, acc, None, causal)
        m, acc = update(n_full + 1, cur[1], m, acc, None, causal[:, :tk], q_from=tk)
        return acc

    carry = lax.fori_loop(0, qi // 2, body, (m, acc, bm_a, bm_b))
    acc = lax.cond(qi % 2 == 1,
                   lambda c: finish(stage(n_full - 2, c, (0, 1), (2, 3)), (2, 3)),
                   lambda c: finish(c, (0, 1)), carry)
    o_ref[...] = jnp.concatenate([(a[:MLA_V] * (1.0 / a[MLA_V:MLA_V + 1])).T for a in acc],
                                 axis=1).astype(BF16)


def _mla_attn(q, k, vt, k_meta, vt_meta, batch, seq, tq, tk):
    nq = seq // tq
    assert vt.shape[2] == tk and tq == 2 * tk
    kern = functools.partial(_mla_kernel, tq=tq, tk=tk)
    return pl.pallas_call(
        kern,
        grid=(batch, MLA_HEADS // 2),
        in_specs=[
            pl.BlockSpec((seq, 2 * HEAD_PAD), lambda b, hp: (b, hp)),
            pl.BlockSpec((seq, 2 * HEAD_PAD), lambda b, hp: (b, hp)),
            pl.BlockSpec((seq // tk, 2 * MLA_V, tk), lambda b, hp: (b, hp, 0)),
            pl.BlockSpec((k_meta.shape[0], 2 * HEAD_PAD), lambda b, hp: (0, hp)),
            pl.BlockSpec((2 * MLA_V, vt_meta.shape[1]), lambda b, hp: (hp, 0)),
        ],
        out_specs=pl.BlockSpec((seq, 2 * MLA_V), lambda b, hp: (b, hp)),
        out_shape=jax.ShapeDtypeStruct((batch * seq, V_W), BF16),
        scratch_shapes=[pltpu.VMEM((4, 2, tk, tq), F32)],
        compiler_params=pltpu.CompilerParams(
            dimension_semantics=("arbitrary", "arbitrary"),
            vmem_limit_bytes=VMEM_LIMIT),
        name="mla_attn",
    )(q, k, vt, k_meta, vt_meta)


LEVELS = (1, 2, 4, 8, 16, 32)
GDN_CONST_NAMES = ("expand_b", "expand_a", "ltri", "eye_t", "tril_t", "stril_t", "level_masks", "bd_ones")


def _gdn_constants(cps):
    i = np.arange(CHUNK)[:, None]
    lane = np.arange(GDN_W)[None, :]
    j = lane % GDN_DK
    c = {}
    r = np.arange(LANES)[:, None]
    c["expand_b"] = (r == lane // GDN_DK)
    c["expand_a"] = (r == GDN_HEADS + lane // GDN_DK)
    t = np.arange(cps * CHUNK)
    c["ltri"] = (t[:, None] >= t[None, :]) & (t[:, None] // CHUNK == t[None, :] // CHUNK)
    c["eye_t"] = (i == j)
    c["tril_t"] = (i >= j)
    c["stril_t"] = (i > j)
    jg = j[:, :GROUP_W]
    c["level_masks"] = np.stack([
        ((i // (2 * s) == jg // (2 * s)) & ((i // s) % 2 == 1) & ((jg // s) % 2 == 0))
        for s in LEVELS])
    g = np.arange(GROUP_W)
    c["bd_ones"] = (g[:, None] // GDN_DK == g[None, :] // GDN_DK)
    bf = ("bd_ones", "expand_b", "expand_a", "ltri")
    return [jnp.asarray(c[k].astype(np.float32), BF16 if k in bf else F32) for k in GDN_CONST_NAMES]


def _block_diag(y, lo_half):
    zeros = jnp.zeros((GDN_DK, LANES), y.dtype)
    blocks = []
    for h in range(GROUP_HEADS):
        t = h // 2
        tile = y[:, t * LANES:(t + 1) * LANES]
        piece = jnp.where(lo_half, tile, 0) if h % 2 == 0 else jnp.where(lo_half, 0, tile)
        blocks.append(jnp.concatenate([piece, zeros] if t == 0 else [zeros, piece], axis=1))
    return jnp.concatenate(blocks, axis=0)


def _head_matmul(x, y, lo_half):
    return _dot(x.astype(BF16), _block_diag(y.astype(BF16), lo_half))


def _diag_blocks(m, lo_half):
    tiles = []
    for t in range(m.shape[1] // LANES):
        h = 2 * (t % 2)
        cols = slice(t * LANES, (t + 1) * LANES)
        tiles.append(jnp.where(lo_half, m[h * GDN_DK:(h + 1) * GDN_DK, cols],
                               m[(h + 1) * GDN_DK:(h + 2) * GDN_DK, cols]))
    return jnp.concatenate(tiles, axis=1)


def _gdn_kernel(gx_ref, ba_ref, z_ref, s0_ref, arate_ref, dtb_ref, gnorm_ref,
                expand_b_ref, expand_a_ref, ltri_ref, eye_t_ref, tril_t_ref, stril_t_ref,
                level_masks_ref, bd_ones_ref,
                o_ref, sout_ref,
                state_ref, kn_ref, kbq_ref, vbk_ref, kdec_ref, qg_ref, decay_ref, sdec1_ref,
                lhs_ref, amat_ref, u_ref, ku_ref, sdec2_ref, *, cps, front_pad, blocks_per_seq):
    s = pl.program_id(0)
    rows = cps * CHUNK
    stage1 = (kn_ref, kbq_ref, vbk_ref, kdec_ref, qg_ref, decay_ref, sdec1_ref)
    stage2 = (lhs_ref, amat_ref, u_ref, ku_ref, sdec2_ref)

    @pl.when(s == 0)
    def _():
        for r in stage1 + stage2 + (state_ref,):
            r[...] = jnp.zeros_like(r)

    kn_r, kbq_r, vbk_r, kdec_r, qg_r, decay_r, sdec1_r = (r.at[1 - s % 2] for r in stage1)
    kn_w, kbq_w, vbk_w, kdec_w, qg_w, decay_w, sdec1_w = (r.at[s % 2] for r in stage1)

    lo_half = lax.broadcasted_iota(jnp.int32, (1, LANES), 1) < GDN_DK
    eye, tril, stril = eye_t_ref[...], tril_t_ref[...], stril_t_ref[...]
    chains = [(j, gi) for j in range(cps) for gi in range(N_GROUPS)]
    rsl = lambda j: slice(j * CHUNK, (j + 1) * CHUNK)
    gsl = lambda gi: slice(gi * GROUP_W, (gi + 1) * GROUP_W)

    first_of_seq = (s - 2) % blocks_per_seq == 0
    states = [jnp.where(first_of_seq, s0_ref[gi], state_ref[gi]) for gi in range(N_GROUPS)]
    o_rows = [[None] * N_GROUPS for _ in range(cps)]
    in_flight = {}

    def ride(i):
        if i - 1 in in_flight:
            for gi, big in enumerate(in_flight.pop(i - 1)):
                v_new = u_ref[i - 1, gi] - big[CHUNK:2 * CHUNK]
                o_rows[i - 1][gi] = big[2 * CHUNK:] + _dot(amat_ref[i - 1, gi],
                                                           _block_diag(v_new.astype(BF16), lo_half))
        if i < cps:
            bigs = []
            for gi in range(N_GROUPS):
                state = states[gi]
                big = _dot(lhs_ref[i, gi], _block_diag(state.astype(BF16), lo_half))
                states[gi] = sdec2_ref[i, gi][0:1] * state - big[:CHUNK] + ku_ref[i, gi]
                bigs.append(big)
            in_flight[i] = bigs

    ba = ba_ref[...]
    beta = jax.nn.sigmoid(ba)
    sp_in = ba + dtb_ref[...]
    g = arate_ref[...] * (jnp.maximum(sp_in, 0.0) + jnp.log1p(jnp.exp(-jnp.abs(sp_in))))
    if front_pad:
        valid = lax.broadcasted_iota(jnp.int32, (rows, 1), 0) >= front_pad
        beta = jnp.where(valid, beta, 0.0)
        g = jnp.where(valid, g, 0.0)
    beta_e = _dot_exact_rhs(beta, expand_b_ref[...], 2)
    gc = _dot_exact_lhs(ltri_ref[...], g, 3)
    ride(0)

    lmat, amat = [], []
    for j, gi in chains:
        rs, gs = rsl(j), gsl(gi)
        kstack = _block_diag(kn_r[rs, gs], lo_half)
        sc = _dot_nt(kbq_r[j, gi], kstack)
        lmat.append(sc[:CHUNK] * decay_r[rs, gs] * stril[:, gs])
        amat.append(sc[CHUNK:] * decay_r[rs, gs])

    gc_e = _dot_exact_rhs(gc, expand_a_ref[...], 3)
    ride(1)

    def level(xinv, li):
        xo = [_head_matmul(x, lm * level_masks_ref[li], lo_half) for x, lm in zip(xinv, lmat)]
        return [x - _head_matmul(y, x, lo_half) for x, y in zip(xinv, xo)]

    xinv = [eye[:, gsl(gi)] - lm * level_masks_ref[0] for (j, gi), lm in zip(chains, lmat)]
    xinv = level(xinv, 1)

    qn = gx_ref[:, :GDN_W]
    kn = gx_ref[:, GDN_W:2 * GDN_W]
    v = gx_ref[:, 2 * GDN_W:]
    egc = jnp.exp(gc_e)
    kb = kn * beta_e
    vb = v * beta_e
    kbg = kb * egc
    qg = qn * egc
    kn_w[...] = kn.astype(BF16)
    qg_w[...] = qg
    for j, gi in chains:
        rs, gs = rsl(j), gsl(gi)
        kbq_w[j, gi] = jnp.concatenate([kb[rs, gs], qn[rs, gs]], axis=0).astype(BF16)
        vbk_w[j, gi] = jnp.concatenate([_block_diag(vb[rs, gs].astype(BF16), lo_half),
                                        _block_diag(kbg[rs, gs].astype(BF16), lo_half)], axis=1)
    ride(2)
    xinv = level(xinv, 2)

    decay, kdec, s_decay = [], [], []
    for j in range(cps):
        gce = gc_e[rsl(j)]
        gc_t = jnp.sum(gce * eye, axis=0, keepdims=True)
        decay.append(jnp.exp(jnp.where(tril > 0.5, gce - gc_t, -jnp.inf)))
        g_last = gce[CHUNK - 1:CHUNK]
        kdec.append(kn[rsl(j)] * jnp.exp(g_last - gce))
        s_decay.append(jnp.exp(g_last))
        decay_w[rsl(j)] = decay[j]
        kdec_w[rsl(j)] = kdec[j]
        for gi in range(N_GROUPS):
            sdec1_w[j, gi] = jnp.broadcast_to(s_decay[j][:, gsl(gi)], (8, GROUP_W))
    ride(3)
    xinv = level(xinv, 3)
    for i in range(4, cps + 1):
        ride(i)
    ride(cps)

    for gi in range(N_GROUPS):
        state_ref[gi] = states[gi]

    @pl.when(s == pl.num_programs(0) - 1)
    def _():
        sout_ref[...] = state_ref[...]

    o = jnp.concatenate([jnp.concatenate(r, axis=1) for r in o_rows], axis=0)
    ms = _head_sums(o * o, bd_ones_ref[...]) * (1.0 / GDN_DV)
    o = o * lax.rsqrt(ms + EPS) * gnorm_ref[...] * _silu(z_ref[...])
    o_ref[...] = o.astype(BF16)

    for li in range(4, len(LEVELS)):
        xinv = level(xinv, li)

    uw = [_dot(x.astype(BF16), vbk_r[j, gi]) for (j, gi), x in zip(chains, xinv)]
    kwu = [_diag_blocks(_dot(kdec_r[rsl(j), gsl(gi)].T.astype(BF16), m.astype(BF16)), lo_half)
           for (j, gi), m in zip(chains, uw)]

    for ci, (j, gi) in enumerate(chains):
        rs, gs = rsl(j), gsl(gi)
        u, wmat = uw[ci][:, :GROUP_W], uw[ci][:, GROUP_W:]
        ku, kw = kwu[ci][:, :GROUP_W], kwu[ci][:, GROUP_W:]
        lhs_ref[j, gi] = jnp.concatenate([kw, wmat, qg_r[rs, gs]], axis=0).astype(BF16)
        amat_ref[j, gi] = amat[ci].astype(BF16)
        u_ref[j, gi] = u
        ku_ref[j, gi] = ku
        sdec2_ref[j, gi] = sdec1_r[j, gi]


def _gdn(gqkv, gba, z, state0, params, batch, seq, cps, front_pad):
    rows = cps * CHUNK
    blocks_per_seq = seq // rows
    n_blocks = batch * blocks_per_seq
    consts = _gdn_constants(cps)
    cur = lambda w: pl.BlockSpec((rows, w), lambda s: (jnp.minimum(s, n_blocks - 1), 0))
    done = lambda w: pl.BlockSpec((rows, w), lambda s: (jnp.maximum(s - 2, 0), 0))
    state_shape = (N_GROUPS, GDN_DK, GROUP_W)
    per_chain = lambda r, w, dt: pltpu.VMEM((cps, N_GROUPS, r, w), dt)
    per_row = lambda dt: pltpu.VMEM((rows, GDN_W), dt)
    two = lambda v: pltpu.VMEM((2,) + tuple(v.shape), v.dtype)
    return pl.pallas_call(
        functools.partial(_gdn_kernel, cps=cps, front_pad=front_pad, blocks_per_seq=blocks_per_seq),
        grid=(n_blocks + 2,),
        in_specs=[cur(3 * GDN_W), cur(LANES), done(GDN_W), _const_spec(state_shape)]
                 + [_const_spec(a.shape) for a in params]
                 + [_const_spec(a.shape) for a in consts],
        out_specs=[done(GDN_W), pl.BlockSpec(state_shape, lambda s: (0, 0, 0))],
        out_shape=[jax.ShapeDtypeStruct((batch * seq, GDN_W), BF16),
                   jax.ShapeDtypeStruct(state_shape, F32)],
        scratch_shapes=[pltpu.VMEM(state_shape, F32),
                        two(per_row(BF16)),
                        two(per_chain(2 * CHUNK, GROUP_W, BF16)),
                        two(per_chain(GROUP_W, 2 * GROUP_W, BF16)),
                        two(per_row(F32)),
                        two(per_row(F32)),
                        two(per_row(F32)),
                        two(per_chain(8, GROUP_W, F32)),
                        per_chain(3 * CHUNK, GROUP_W, BF16),
                        per_chain(CHUNK, GROUP_W, BF16),
                        per_chain(CHUNK, GROUP_W, F32),
                        per_chain(CHUNK, GROUP_W, F32),
                        per_chain(8, GROUP_W, F32)],
        compiler_params=pltpu.CompilerParams(dimension_semantics=("arbitrary",),
                                             vmem_limit_bytes=VMEM_LIMIT),
        name="gdn_chunk",
    )(gqkv, gba, z, state0, *params, *consts)


WEIGHT_CHUNK_ROWS = 256


def _fetch_rounded(src_hbm, dst_ref, stage_ref, sem):
    n = src_hbm.shape[0] // WEIGHT_CHUNK_ROWS
    chunk = lambda i: pl.ds(i * WEIGHT_CHUNK_ROWS, WEIGHT_CHUNK_ROWS)
    copy = lambda i: pltpu.make_async_copy(src_hbm.at[chunk(i)], stage_ref.at[i % 2], sem.at[i % 2])
    copy(0).start()
    for i in range(n):
        if i + 1 < n:
            copy(i + 1).start()
        copy(i).wait()
        dst_ref[chunk(i)] = stage_ref[i % 2].astype(BF16)


def _merge_ffn_kernel(h1_ref, om_ref, og_ref, nm_ref, wgate_ref, wmo_ref, wgo_ref, wout_ref,
                      n2_ref, wg_hbm, wu_hbm, wd_hbm, nf_ref, out_ref,
                      wg_ref, wu_ref, wd_ref, stage_wide_ref, stage_narrow_ref, sem):
    @pl.when(pl.program_id(0) == 0)
    def _():
        _fetch_rounded(wg_hbm, wg_ref, stage_wide_ref, sem)
        _fetch_rounded(wu_hbm, wu_ref, stage_wide_ref, sem)
        _fetch_rounded(wd_hbm, wd_ref, stage_narrow_ref, sem)

    n_chains = max(1, h1_ref.shape[0] // MERGE_CHAIN_ROWS)
    rows = [slice(c * MERGE_CHAIN_ROWS, (c + 1) * MERGE_CHAIN_ROWS) if n_chains > 1 else slice(None)
            for c in range(n_chains)]
    h1 = [h1_ref[r] for r in rows]
    un = [_rms(h, nm_ref[...]).astype(BF16) for h in h1]
    gates = [jax.nn.sigmoid(_dot(u, wgate_ref[...])) for u in un]
    merged = [(g[:, :D_MODEL] * _dot(om_ref[r], wmo_ref[...])
               + g[:, D_MODEL:] * _dot(og_ref[r], wgo_ref[...])).astype(BF16) for g, r in zip(gates, rows)]
    h2 = [h + _dot(m, wout_ref[...]) for h, m in zip(h1, merged)]
    xn = [_rms(h, n2_ref[...]).astype(BF16) for h in h2]
    acts = [[] for _ in rows]
    for c in range(D_FF // FFN_CHUNK):
        cols = slice(c * FFN_CHUNK, (c + 1) * FFN_CHUNK)
        for a, x in zip(acts, xn):
            a.append((_silu(_dot(x, wg_ref[:, cols])) * _dot(x, wu_ref[:, cols])).astype(BF16))
    h3 = [h + 0.5 * _dot(jnp.concatenate(a, axis=1), wd_ref[...]) for h, a in zip(h2, acts)]
    for r, h in zip(rows, h3):
        out_ref[r] = _rms(h, nf_ref[...])


def _merge_ffn(h1, o_mla, o_gdn, weights, tm):
    n = h1.shape[0]
    row = lambda w: pl.BlockSpec((tm, w), lambda i: (i, 0))
    return pl.pallas_call(
        _merge_ffn_kernel,
        grid=(n // tm,),
        in_specs=[row(D_MODEL), row(V_W), row(GDN_W)]
                 + [pl.BlockSpec(memory_space=pl.ANY) if w.dtype == F32 and w.ndim == 2 and w.shape[0] > 1
                    else _const_spec(w.shape) for w in weights],
        out_specs=row(D_MODEL),
        out_shape=jax.ShapeDtypeStruct((n, D_MODEL), F32),
        scratch_shapes=[pltpu.VMEM((D_MODEL, D_FF), BF16), pltpu.VMEM((D_MODEL, D_FF), BF16),
                        pltpu.VMEM((D_FF, D_MODEL), BF16),
                        pltpu.VMEM((2, WEIGHT_CHUNK_ROWS, D_FF), F32),
                        pltpu.VMEM((2, WEIGHT_CHUNK_ROWS, D_MODEL), F32),
                        pltpu.SemaphoreType.DMA((2,))],
        compiler_params=pltpu.CompilerParams(dimension_semantics=("arbitrary",),
                                             vmem_limit_bytes=VMEM_LIMIT),
        name="merge_ffn",
    )(h1, o_mla, o_gdn, *weights)


def _rope_tables(first_pos, n):
    pos = np.arange(first_pos, first_pos + n, dtype=np.float64)
    inv = ROPE_THETA ** (-np.arange(0, MLA_ROPE, 2, dtype=np.float64) / MLA_ROPE)
    ang = pos[:, None] * inv[None, :]
    cos2 = np.tile(np.cos(ang), (1, 2))
    sin2 = np.tile(np.sin(ang), (1, 2))
    scale = (MLA_NOPE + MLA_ROPE) ** -0.5 * math.log2(math.e)
    pad = np.zeros((n, HEAD_PAD - MLA_NOPE - MLA_ROPE))
    tab_c = np.concatenate([np.ones((n, MLA_NOPE)), cos2, pad], axis=1) * scale
    tab_s = np.concatenate([np.zeros((n, MLA_NOPE)), sin2, pad], axis=1) * scale
    kpad = np.zeros((n, LANES - MLA_ROPE))
    tab_kc = np.concatenate([cos2, kpad], axis=1)
    tab_ks = np.concatenate([sin2, kpad], axis=1)
    return tuple(jnp.asarray(t, F32) for t in (tab_c, tab_s, tab_kc, tab_ks))


def _rot(w):
    half = MLA_ROPE // 2
    return jnp.concatenate([-w[..., half:], w[..., :half]], axis=-1)


def kernel(x, meta_tokens, ffn1_norm, ffn1_w_gate, ffn1_w_up, ffn1_w_down, mix_norm, w_in, q_norm, w_uq,
           kv_norm, w_ukv, w_mla_o, conv_w, a_log, dt_bias, gdn_norm, w_gdn_o, w_out, ffn2_norm,
           ffn2_w_gate, ffn2_w_up, ffn2_w_down, final_norm):
    assert ffn1_norm.shape[0] == 1, "single-layer block"
    batch, seq, d = x.shape
    assert d == D_MODEL and seq % CHUNK == 0
    tm = min(256, seq)
    tq = min(512, seq)
    assert seq % tm == 0 and seq % tq == 0

    assert w_in.shape[2] == D_IN
    w_in_raw = w_in[0].T

    wq = w_uq[0].reshape(MLA_Q_RANK, MLA_HEADS, MLA_NOPE + MLA_ROPE)
    wq_nope, wq_rope = wq[..., :MLA_NOPE], wq[..., MLA_NOPE:]
    zq = lambda n: jnp.zeros((MLA_Q_RANK, MLA_HEADS, n), F32)
    tail = HEAD_PAD - MLA_NOPE - MLA_ROPE
    wqa = jnp.concatenate([wq_nope, wq_rope, zq(tail)], axis=-1).reshape(MLA_Q_RANK, QK_W).astype(BF16)
    wqb = jnp.concatenate([zq(MLA_NOPE), _rot(wq_rope), zq(tail)], axis=-1).reshape(MLA_Q_RANK, QK_W).astype(BF16)
    wkv = w_ukv[0].reshape(MLA_KV_RANK, MLA_HEADS, MLA_NOPE + MLA_V)
    wk = jnp.concatenate([wkv[..., :MLA_NOPE], jnp.zeros((MLA_KV_RANK, MLA_HEADS, HEAD_PAD - MLA_NOPE), F32)],
                         axis=-1).reshape(MLA_KV_RANK, QK_W).astype(BF16)
    wvt = wkv[..., MLA_NOPE:].reshape(MLA_KV_RANK, V_W).T.astype(BF16)

    g = np.arange(GROUP_W)
    bd_ones = jnp.asarray((g[:, None] // GDN_DK == g[None, :] // GDN_DK).astype(np.float32), BF16)
    proj_weights = [
        ffn1_norm[0][None], ffn1_w_gate[0].astype(BF16), ffn1_w_up[0].astype(BF16),
        ffn1_w_down[0].astype(BF16), mix_norm[0][None], w_in_raw, q_norm[0][None], wqa, wqb,
        kv_norm[0][None], wk, wvt, conv_w[0].astype(F32), bd_ones]

    zero_carry = jnp.zeros((8, 3 * GDN_W), F32)
    _, _, k_m, vt_m, gqkv_m, gba_m, _, meta_tail, win_a, wgate = _token_proj(
        meta_tokens.astype(F32), _rope_tables(0, N_META), zero_carry, proj_weights, N_META, 1, N_META)
    proj_weights = [win_a if w is w_in_raw else w for w in proj_weights]
    tp = 2 * tm if seq % (2 * tm) == 0 else tm
    h1, q, k, vt, gqkv, gba, z, _ = _token_proj(
        x.reshape(batch * seq, d), _rope_tables(N_META, seq), meta_tail, proj_weights, tp, seq // tp, tm)

    pad_rows = lambda a, n, front: jnp.pad(a, ((n - a.shape[0], 0) if front else (0, n - a.shape[0]), (0, 0)))
    o_mla = _mla_attn(q, k, vt, k_m, vt_m[0], batch, seq, tq, tm)

    hpad = lambda a: jnp.zeros((1, LANES), F32).at[0, GDN_HEADS:2 * GDN_HEADS].set(a)
    arate = hpad(-jnp.exp(a_log[0].astype(F32)))
    dtb = hpad(dt_bias[0].astype(F32))
    gnorm = jnp.tile(gdn_norm[0].astype(F32), GDN_HEADS)[None]
    gdn_params = (arate, dtb, gnorm)
    cps = 4 if seq % (4 * CHUNK) == 0 else 1
    _, state_meta = _gdn(pad_rows(gqkv_m, CHUNK, True), pad_rows(gba_m, CHUNK, True),
                         jnp.zeros((CHUNK, GDN_W), F32), jnp.zeros((N_GROUPS, GDN_DK, GROUP_W), F32),
                         gdn_params, 1, CHUNK, 1, CHUNK - N_META)
    o_gdn, _ = _gdn(gqkv, gba, z, state_meta, gdn_params, batch, seq, cps, 0)

    merge_weights = [
        mix_norm[0][None], wgate, w_mla_o[0].astype(BF16), w_gdn_o[0].astype(BF16), w_out[0].astype(BF16),
        ffn2_norm[0][None], ffn2_w_gate[0].astype(F32), ffn2_w_up[0].astype(F32),
        ffn2_w_down[0].astype(F32), final_norm[None]]
    merge_rows = 2 * MERGE_CHAIN_ROWS if (batch * seq) % (2 * MERGE_CHAIN_ROWS) == 0 else tm
    out = _merge_ffn(h1, o_mla, o_gdn, merge_weights, merge_rows)
    return out.reshape(batch, seq, d)
```
